```python
import jax, jax.numpy as jnp
from jax import lax
import numpy as np

D_MODEL = 4096
BATCH = 1
SEQ = 8192
DEPTH = 1

MIX_WIDTH = D_MODEL
ATTN_WIDTH = MIX_WIDTH // 2
POOL_WIDTH = MIX_WIDTH - ATTN_WIDTH
HEAD_DIM = 128
N_HEADS = ATTN_WIDTH // HEAD_DIM
N_KV_HEADS = 4
GROUP = N_HEADS // N_KV_HEADS
KV_WIDTH = N_KV_HEADS * HEAD_DIM
IN_WIDTH = POOL_WIDTH + ATTN_WIDTH + 2 * KV_WIDTH
POOL_WINDOWS = (2, 4, 8, 16)
N_POOL_GROUPS = len(POOL_WINDOWS)
POOL_GROUP_WIDTH = POOL_WIDTH // N_POOL_GROUPS
GRID_W = 64
ROPE_THETA = 10000.0
ROPE_AXIS_DIM = HEAD_DIM // 2
Q_BLOCK = 128
N_EXPERTS = 32
TOP_K = 4
D_FF = D_MODEL // 4
SWIGLU_ALPHA = 1.702
SWIGLU_LIMIT = 7.0
MOE_BLOCK = 128
N_MOD = 6
EPS = 1e-6

kernel_name = "hybrid_pool_axialgqa_moe_block"


def _rmsnorm(x, g):
    xf = x.astype(jnp.float32)
    y = xf * lax.rsqrt(jnp.mean(xf * xf, axis=-1, keepdims=True) + EPS)
    return (y * g.astype(jnp.float32)).astype(x.dtype)


def _axial_rope_tables(seq):
    n_rows = seq // GRID_W
    row = jnp.repeat(jnp.arange(n_rows, dtype=jnp.float32), GRID_W)
    col = jnp.tile(jnp.arange(GRID_W, dtype=jnp.float32), n_rows)
    inv_freq = ROPE_THETA ** (-jnp.arange(0, ROPE_AXIS_DIM, 2, dtype=jnp.float32) / ROPE_AXIS_DIM)
    ang = jnp.stack([row, col], axis=-1)[:, :, None] * inv_freq
    return jnp.cos(ang), jnp.sin(ang)


def _apply_axial_rope(x, cos, sin):
    b, s, h, _ = x.shape
    xr = x.astype(jnp.float32).reshape(b, s, h, 2, 2, ROPE_AXIS_DIM // 2)
    x1, x2 = xr[..., 0, :], xr[..., 1, :]
    cs = cos[None, :, None]
    sn = sin[None, :, None]
    out = jnp.stack([x1 * cs - x2 * sn, x1 * sn + x2 * cs], axis=-2)
    return out.reshape(b, s, h, HEAD_DIM).astype(x.dtype)


def _attention(q, k, v, q_norm_g, k_norm_g, cos, sin):
    b, s, _ = q.shape
    q = _apply_axial_rope(_rmsnorm(q.reshape(b, s, N_HEADS, HEAD_DIM), q_norm_g), cos, sin)
    k = _apply_axial_rope(_rmsnorm(k.reshape(b, s, N_KV_HEADS, HEAD_DIM), k_norm_g), cos, sin)
    v = v.reshape(b, s, N_KV_HEADS, HEAD_DIM)
    nb = s // Q_BLOCK
    qb = q.reshape(b, nb, Q_BLOCK, N_KV_HEADS, GROUP, HEAD_DIM).transpose(1, 0, 3, 4, 2, 5)
    kt = k.transpose(0, 2, 1, 3)
    vt = v.transpose(0, 2, 1, 3)
    scale = HEAD_DIM ** -0.5

    def attend_block(q_blk):
        sc = jnp.einsum('bkgqd,bksd->bkgqs', q_blk, kt, preferred_element_type=jnp.float32) * scale
        p = jax.nn.softmax(sc, axis=-1)
        return jnp.einsum('bkgqs,bksd->bkgqd', p.astype(vt.dtype), vt)

    o = lax.map(attend_block, qb)
    return o.transpose(1, 0, 4, 2, 3, 5).reshape(b, s, ATTN_WIDTH)


def _multiscale_pool(p, w_pool, pool_scale):
    b, s, _ = p.shape
    pf = p.astype(jnp.float32)
    cs = jnp.concatenate([jnp.zeros((b, 1, POOL_WIDTH), jnp.float32), jnp.cumsum(pf, axis=1)], axis=1)
    t = jnp.arange(s)
    outs = []
    for gi, w in enumerate(POOL_WINDOWS):
        lo = jnp.clip(t - w // 2, 0, s - 1)
        hi = jnp.clip(t + w // 2 - 1, 0, s - 1)
        sl = slice(gi * POOL_GROUP_WIDTH, (gi + 1) * POOL_GROUP_WIDTH)
        csg = cs[:, :, sl]
        win_sum = csg[:, hi + 1] - csg[:, lo]
        cnt = (hi - lo + 1).astype(jnp.float32)[None, :, None]
        outs.append(win_sum / cnt - pf[:, :, sl])
    d = jnp.stack(outs, axis=2).astype(p.dtype)
    y = jnp.einsum('bsgc,gcd->bsgd', d, w_pool)
    return y.reshape(b, s, POOL_WIDTH) * pool_scale


def _moe(h, w_router, b_router, w1, b1, w2, b2):
    b, s, d = h.shape
    n = b * s
    xt = h.reshape(n, d)
    logits = (xt @ w_router + b_router).astype(jnp.float32)
    top_vals, top_idx = lax.top_k(logits, TOP_K)
    gates = jax.nn.softmax(top_vals, axis=-1)
    nk = n * TOP_K
    e_flat = top_idx.reshape(-1)
    tok_flat = jnp.arange(nk, dtype=jnp.int32) // TOP_K
    g_flat = gates.reshape(-1)
    order = jnp.argsort(e_flat, stable=True)
    e_sorted = e_flat[order]
    tok_sorted = tok_flat[order]
    g_sorted = g_flat[order]
    sizes = jnp.bincount(e_flat, length=N_EXPERTS)
    padded = ((sizes + MOE_BLOCK - 1) // MOE_BLOCK) * MOE_BLOCK
    start = jnp.cumsum(sizes) - sizes
    pend = jnp.cumsum(padded)
    pstart = pend - padded
    dest = pstart[e_sorted] + (jnp.arange(nk) - start[e_sorted])
    n_blocks = nk // MOE_BLOCK + N_EXPERTS
    n_rows = n_blocks * MOE_BLOCK
    row_tok = jnp.zeros((n_rows,), jnp.int32).at[dest].set(tok_sorted.astype(jnp.int32))
    row_gate = jnp.zeros((n_rows,), jnp.float32).at[dest].set(g_sorted)
    blk_start = jnp.arange(n_blocks) * MOE_BLOCK
    block_expert = jnp.clip(jnp.searchsorted(pend, blk_start, side='right'), 0, N_EXPERTS - 1)

    def step(out, blk):
        tok, gte, e = blk
        xb = xt[tok]
        hb = xb @ w1[e] + b1[e]
        glu = jnp.minimum(hb[:, :D_FF], SWIGLU_LIMIT)
        lin = jnp.clip(hb[:, D_FF:], -SWIGLU_LIMIT, SWIGLU_LIMIT)
        act = glu * jax.nn.sigmoid(SWIGLU_ALPHA * glu) * (lin + 1)
        yb = act @ w2[e] + b2[e]
        out = out.at[tok].add(yb.astype(jnp.float32) * gte[:, None])
        return out, None

    out, _ = lax.scan(step, jnp.zeros((n, d), jnp.float32),
                      (row_tok.reshape(n_blocks, MOE_BLOCK), row_gate.reshape(n_blocks, MOE_BLOCK), block_expert))
    return out.reshape(b, s, d).astype(h.dtype)


def setup_inputs(seed: int = 0) -> dict:
    key = jax.random.key(seed)
    ks = jax.random.split(key, 18)

    def nrm(k, shape, scale):
        return jax.random.normal(k, shape, jnp.float32) * scale

    return {
        "x": nrm(ks[0], (BATCH, SEQ, D_MODEL), 1.0),
        "c": nrm(ks[1], (BATCH, D_MODEL), 1.0),
        "w_mod": nrm(ks[2], (DEPTH, D_MODEL, N_MOD * D_MODEL), 0.5 * D_MODEL ** -0.5),
        "b_mod": nrm(ks[3], (DEPTH, N_MOD * D_MODEL), 0.02),
        "norm1_g": 1.0 + nrm(ks[4], (DEPTH, D_MODEL), 0.05),
        "w_in": nrm(ks[5], (DEPTH, D_MODEL, IN_WIDTH), D_MODEL ** -0.5),
        "q_norm_g": 1.0 + nrm(ks[6], (DEPTH, HEAD_DIM), 0.05),
        "k_norm_g": 1.0 + nrm(ks[7], (DEPTH, HEAD_DIM), 0.05),
        "w_pool": nrm(ks[8], (DEPTH, N_POOL_GROUPS, POOL_GROUP_WIDTH, POOL_GROUP_WIDTH), POOL_GROUP_WIDTH ** -0.5),
        "pool_scale": 1.0 + nrm(ks[9], (DEPTH, POOL_WIDTH), 0.1),
        "w_out": nrm(ks[10], (DEPTH, MIX_WIDTH, D_MODEL), MIX_WIDTH ** -0.5),
        "norm2_g": 1.0 + nrm(ks[11], (DEPTH, D_MODEL), 0.05),
        "w_router": nrm(ks[12], (DEPTH, D_MODEL, N_EXPERTS), D_MODEL ** -0.5),
        "b_router": nrm(ks[13], (DEPTH, N_EXPERTS), 0.01),
        "w1": nrm(ks[14], (DEPTH, N_EXPERTS, D_MODEL, 2 * D_FF), D_MODEL ** -0.5),
        "b1": nrm(ks[15], (DEPTH, N_EXPERTS, 2 * D_FF), 0.01),
        "w2": nrm(ks[16], (DEPTH, N_EXPERTS, D_FF, D_MODEL), D_FF ** -0.5),
        "b2": nrm(ks[17], (DEPTH, N_EXPERTS, D_MODEL), 0.01),
    }


def reference(x, c, w_mod, b_mod, norm1_g, w_in, q_norm_g, k_norm_g, w_pool, pool_scale,
              w_out, norm2_g, w_router, b_router, w1, b1, w2, b2):
    b, s, d = x.shape
    cos, sin = _axial_rope_tables(s)
    c_act = jax.nn.silu(c)
    for l in range(DEPTH):
        mod = c_act @ w_mod[l] + b_mod[l]
        sh1, sc1, g1, sh2, sc2, g2 = [m[:, None, :] for m in jnp.split(mod, N_MOD, axis=-1)]
        h = _rmsnorm(x, norm1_g[l]) * (1 + sc1) + sh1
        proj = h @ w_in[l]
        pool_in, q, k, v = jnp.split(
            proj, [POOL_WIDTH, POOL_WIDTH + ATTN_WIDTH, POOL_WIDTH + ATTN_WIDTH + KV_WIDTH], axis=-1)
        attn_out = _attention(q, k, v, q_norm_g[l], k_norm_g[l], cos, sin)
        pool_out = _multiscale_pool(pool_in, w_pool[l], pool_scale[l])
        mix = jnp.concatenate([attn_out, pool_out], axis=-1)
        x = x + g1 * (mix @ w_out[l])
        h2 = _rmsnorm(x, norm2_g[l]) * (1 + sc2) + sh2
        x = x + g2 * _moe(h2, w_router[l], b_router[l], w1[l], b1[l], w2[l], b2[l])
    return x
```

```python
import functools
import math

import jax
import jax.numpy as jnp
from jax import lax
from jax.experimental import pallas as pl
from jax.experimental.pallas import tpu as pltpu

F32 = jnp.float32
BF16 = jnp.bfloat16
I32 = jnp.int32

D_MODEL = 4096
SEQ = 8192
POOL_WIDTH = 2048
ATTN_WIDTH = 2048
HEAD_DIM = 128
N_HEADS = 16
N_KV_HEADS = 4
GROUP = N_HEADS // N_KV_HEADS
KV_WIDTH = N_KV_HEADS * HEAD_DIM
IN_WIDTH = POOL_WIDTH + ATTN_WIDTH + 2 * KV_WIDTH
POOL_WINDOWS = (2, 4, 8, 16)
POOL_GROUP_WIDTH = POOL_WIDTH // len(POOL_WINDOWS)
GRID_W = 64
ROPE_THETA = 10000.0
ROPE_AXIS_DIM = HEAD_DIM // 2
N_EXPERTS = 32
TOP_K = 4
D_FF = D_MODEL // 4
SWIGLU_ALPHA = 1.702
SWIGLU_LIMIT = 7.0
N_MOD = 6
EPS = 1e-6

LANES = 128
VMEM_LIMIT = 56 * 1024 * 1024

Q_SCALE = (HEAD_DIM ** -0.5) * math.log2(math.e)

MOE_TM = 256
MOE_NB = SEQ * TOP_K // MOE_TM + N_EXPERTS
MOE_ROWS = MOE_NB * MOE_TM


def _params(sem, vmem=VMEM_LIMIT):
    return pltpu.CompilerParams(dimension_semantics=sem, vmem_limit_bytes=vmem)


MOD_TN = 512
MOD_KC = 256


def _mod_kernel(c_ref, w_ref, b_ref, o_ref):
    def body(k, acc):
        r = pl.multiple_of(k * MOD_KC, MOD_KC)
        ck = c_ref[pl.ds(r, MOD_KC), :]
        ck = ck * jax.nn.sigmoid(ck)
        p = w_ref[pl.ds(r, MOD_KC), :] * ck
        return acc + p.reshape(MOD_KC // 8, 8, MOD_TN).sum(axis=0)

    acc = lax.fori_loop(0, D_MODEL // MOD_KC, body, jnp.zeros((8, MOD_TN), F32))
    o_ref[...] = acc.sum(axis=0, keepdims=True) + b_ref[...]


def _mod(c_col, w_mod, b_mod):
    n = N_MOD * D_MODEL
    return pl.pallas_call(
        _mod_kernel,
        out_shape=jax.ShapeDtypeStruct((1, n), F32),
        grid=(n // MOD_TN,),
        in_specs=[
            pl.BlockSpec((D_MODEL, 1), lambda j: (0, 0)),
            pl.BlockSpec((D_MODEL, MOD_TN), lambda j: (0, j)),
            pl.BlockSpec((1, MOD_TN), lambda j: (0, j)),
        ],
        out_specs=pl.BlockSpec((1, MOD_TN), lambda j: (0, j)),
        compiler_params=_params(("arbitrary",)),
        name="mod",
    )(c_col, w_mod, b_mod)


IP_TM = 512
IP_TN = 512
IP_NJ = IN_WIDTH // IP_TN
NORM_ROWS = 16


def _modulated_rmsnorm_rows(x, g, sc, sh):
    ms = jnp.mean(x * x, axis=-1, keepdims=True)
    return x * lax.rsqrt(ms + EPS) * g * (1.0 + sc) + sh


def _norm_rope(xh, g, cos, sin_signed):
    ms = jnp.mean(xh * xh, axis=-1, keepdims=True)
    y = xh * lax.rsqrt(ms + EPS) * g
    lane = lax.broadcasted_iota(I32, y.shape, 1)
    partner = jnp.where((lane % 64) < 32, pltpu.roll(y, 96, 1), pltpu.roll(y, 32, 1))
    return y * cos + partner * sin_signed


def _inproj_kernel(x_ref, g_ref, sc_ref, sh_ref, w_ref, cos_ref, sin_ref, qg_ref, kg_ref,
                   pool_ref, qt_ref, k_ref, vt_ref, h_scr):
    j = pl.program_id(1)

    @pl.when(j == 0)
    def _():
        def body(r, _):
            r0 = pl.multiple_of(r * NORM_ROWS, NORM_ROWS)
            h = _modulated_rmsnorm_rows(x_ref[pl.ds(r0, NORM_ROWS), :], g_ref[...], sc_ref[...], sh_ref[...])
            h_scr[pl.ds(r0, NORM_ROWS), :] = h.astype(BF16)
            return 0
        lax.fori_loop(0, IP_TM // NORM_ROWS, body, 0)

    acc = jnp.dot(h_scr[...], w_ref[...], preferred_element_type=F32)

    @pl.when(j < 4)
    def _():
        pool_ref[...] = acc.astype(BF16)

    @pl.when((j >= 4) & (j < 8))
    def _():
        for hh in range(IP_TN // HEAD_DIM):
            sl = slice(hh * HEAD_DIM, (hh + 1) * HEAD_DIM)
            r = _norm_rope(acc[:, sl], qg_ref[...], cos_ref[...], sin_ref[...]) * Q_SCALE
            qt_ref[sl, :] = r.T.astype(BF16)

    @pl.when(j == 8)
    def _():
        for hh in range(IP_TN // HEAD_DIM):
            sl = slice(hh * HEAD_DIM, (hh + 1) * HEAD_DIM)
            k_ref[:, sl] = _norm_rope(acc[:, sl], kg_ref[...], cos_ref[...], sin_ref[...]).astype(BF16)

    @pl.when(j == 9)
    def _():
        vt_ref[...] = acc.T.astype(BF16)


def _inproj(x2, mod, norm1_g, w_in_b, cos_t, sin_t, qg, kg):
    row = lambda n: pl.BlockSpec((1, D_MODEL), lambda i, j, n=n: (0, n))
    return pl.pallas_call(
        _inproj_kernel,
        out_shape=(
            jax.ShapeDtypeStruct((SEQ, POOL_WIDTH), BF16),
            jax.ShapeDtypeStruct((ATTN_WIDTH, SEQ), BF16),
            jax.ShapeDtypeStruct((SEQ, KV_WIDTH), BF16),
            jax.ShapeDtypeStruct((KV_WIDTH, SEQ), BF16),
        ),
        grid=(SEQ // IP_TM, IP_NJ),
        in_specs=[
            pl.BlockSpec((IP_TM, D_MODEL), lambda i, j: (i, 0)),
            pl.BlockSpec((1, D_MODEL), lambda i, j: (0, 0)),
            row(1), row(0),
            pl.BlockSpec((D_MODEL, IP_TN), lambda i, j: (0, j)),
            pl.BlockSpec((IP_TM, HEAD_DIM), lambda i, j: (i, 0)),
            pl.BlockSpec((IP_TM, HEAD_DIM), lambda i, j: (i, 0)),
            pl.BlockSpec((1, HEAD_DIM), lambda i, j: (0, 0)),
            pl.BlockSpec((1, HEAD_DIM), lambda i, j: (0, 0)),
        ],
        out_specs=(
            pl.BlockSpec((IP_TM, IP_TN), lambda i, j: (i, jnp.minimum(j, 3))),
            pl.BlockSpec((IP_TN, IP_TM), lambda i, j: (jnp.clip(j - 4, 0, 3), i)),
            pl.BlockSpec((IP_TM, KV_WIDTH), lambda i, j: (i, 0)),
            pl.BlockSpec((KV_WIDTH, IP_TM), lambda i, j: (0, i)),
        ),
        scratch_shapes=[pltpu.VMEM((IP_TM, D_MODEL), BF16)],
        compiler_params=_params(("arbitrary", "arbitrary")),
        name="inproj",
    )(x2, norm1_g, mod, mod, w_in_b, cos_t, sin_t, qg, kg)


AT_TQ = 512
AT_TK = 512


def _attn_kernel(qt_ref, k_ref, vt_ref, o_ref):
    qt = qt_ref[...]

    def chunk(c, carry):
        m, l, acc = carry
        c0 = pl.multiple_of(c * AT_TK, AT_TK)
        s = jnp.dot(k_ref[pl.ds(c0, AT_TK), :], qt, preferred_element_type=F32)
        m_new = jnp.maximum(m, s.max(axis=0, keepdims=True))
        alpha = jnp.exp2(m - m_new)
        p = jnp.exp2(s - m_new)
        l = alpha * l + p.sum(axis=0, keepdims=True)
        pv = jnp.dot(vt_ref[:, pl.ds(c0, AT_TK)], p.astype(BF16), preferred_element_type=F32)
        return m_new, l, alpha * acc + pv

    init = (jnp.full((1, AT_TQ), -jnp.inf, F32), jnp.zeros((1, AT_TQ), F32), jnp.zeros((HEAD_DIM, AT_TQ), F32))
    _, l, acc = lax.fori_loop(0, SEQ // AT_TK, chunk, init)
    o_ref[...] = (acc / l).T.astype(BF16)


def _attention(qt, k, vt):
    return pl.pallas_call(
        _attn_kernel,
        out_shape=jax.ShapeDtypeStruct((SEQ, ATTN_WIDTH), BF16),
        grid=(N_HEADS, SEQ // AT_TQ),
        in_specs=[
            pl.BlockSpec((HEAD_DIM, AT_TQ), lambda h, i: (h, i)),
            pl.BlockSpec((SEQ, HEAD_DIM), lambda h, i: (0, h // GROUP)),
            pl.BlockSpec((HEAD_DIM, SEQ), lambda h, i: (h // GROUP, 0)),
        ],
        out_specs=pl.BlockSpec((AT_TQ, HEAD_DIM), lambda h, i: (i, h)),
        compiler_params=_params(("arbitrary", "arbitrary")),
        name="attn",
    )(qt, k, vt)


PL_TM = 256
PL_HALO = 16


def _pool_kernel(prev_ref, main_ref, next_ref, wp_ref, scale_ref, o_ref, buf):
    i = pl.program_id(0)
    last = pl.num_programs(0) - 1
    buf[0:PL_HALO, :] = jnp.where(i == 0, 0.0, prev_ref[...].astype(F32))
    buf[PL_HALO:PL_HALO + PL_TM, :] = main_ref[...].astype(F32)
    buf[PL_HALO + PL_TM:, :] = jnp.where(i == last, 0.0, next_ref[...].astype(F32))
    t = i * PL_TM + lax.broadcasted_iota(I32, (PL_TM, 1), 0)
    for gi, w in enumerate(POOL_WINDOWS):
        cols = slice(gi * POOL_GROUP_WIDTH, (gi + 1) * POOL_GROUP_WIDTH)
        win = buf[PL_HALO - w // 2:PL_HALO - w // 2 + PL_TM, cols]
        for d in range(-w // 2 + 1, w // 2):
            win = win + buf[PL_HALO + d:PL_HALO + d + PL_TM, cols]
        lo = jnp.maximum(t - w // 2, 0)
        hi = jnp.minimum(t + w // 2 - 1, SEQ - 1)
        cnt = (hi - lo + 1).astype(F32)
        dlt = win / cnt - buf[PL_HALO:PL_HALO + PL_TM, cols]
        y = jnp.dot(dlt.astype(BF16), wp_ref[gi], preferred_element_type=F32)
        o_ref[:, cols] = (y * scale_ref[:, cols]).astype(BF16)


def _pool(pool_in, w_pool_b, pool_scale):
    nh = PL_TM // PL_HALO
    n_halo_blocks = SEQ // PL_HALO
    return pl.pallas_call(
        _pool_kernel,
        out_shape=jax.ShapeDtypeStruct((SEQ, POOL_WIDTH), BF16),
        grid=(SEQ // PL_TM,),
        in_specs=[
            pl.BlockSpec((PL_HALO, POOL_WIDTH), lambda i: (jnp.maximum(i * nh - 1, 0), 0)),
            pl.BlockSpec((PL_TM, POOL_WIDTH), lambda i: (i, 0)),
            pl.BlockSpec((PL_HALO, POOL_WIDTH), lambda i: (jnp.minimum((i + 1) * nh, n_halo_blocks - 1), 0)),
            pl.BlockSpec((len(POOL_WINDOWS), POOL_GROUP_WIDTH, POOL_GROUP_WIDTH), lambda i: (0, 0, 0)),
            pl.BlockSpec((1, POOL_WIDTH), lambda i: (0, 0)),
        ],
        out_specs=pl.BlockSpec((PL_TM, POOL_WIDTH), lambda i: (i, 0)),
        scratch_shapes=[pltpu.VMEM((PL_TM + 2 * PL_HALO, POOL_WIDTH), F32)],
        compiler_params=_params(("arbitrary",)),
        name="pool",
    )(pool_in, pool_in, pool_in, w_pool_b, pool_scale)


OP_TM = 1024
OP_TN = 512


def _outproj_kernel(a_ref, p_ref, wa_ref, wp_ref, x_ref, g_ref, o_ref):
    acc = jnp.dot(a_ref[...], wa_ref[...], preferred_element_type=F32)
    acc = acc + jnp.dot(p_ref[...], wp_ref[...], preferred_element_type=F32)
    o_ref[...] = x_ref[...] + g_ref[...] * acc


def _outproj(attn, pool, w_out_b, x2, mod):
    return pl.pallas_call(
        _outproj_kernel,
        out_shape=jax.ShapeDtypeStruct((SEQ, D_MODEL), F32),
        grid=(SEQ // OP_TM, D_MODEL // OP_TN),
        in_specs=[
            pl.BlockSpec((OP_TM, ATTN_WIDTH), lambda i, j: (i, 0)),
            pl.BlockSpec((OP_TM, POOL_WIDTH), lambda i, j: (i, 0)),
            pl.BlockSpec((ATTN_WIDTH, OP_TN), lambda i, j: (0, j)),
            pl.BlockSpec((POOL_WIDTH, OP_TN), lambda i, j: (1, j)),
            pl.BlockSpec((OP_TM, OP_TN), lambda i, j: (i, j)),
            pl.BlockSpec((1, OP_TN), lambda i, j: (0, 2 * (D_MODEL // OP_TN) + j)),
        ],
        out_specs=pl.BlockSpec((OP_TM, OP_TN), lambda i, j: (i, j)),
        compiler_params=_params(("arbitrary", "arbitrary")),
        name="outproj",
    )(attn, pool, w_out_b, w_out_b, x2, mod)


N2_TM = 256
RT_PAD = LANES


def _norm2_kernel(x_ref, g_ref, sc_ref, sh_ref, wcat_ref, whi_ref, b_ref, h_ref, lt_ref, hi_scr, lo_scr):
    def body(r, _):
        r0 = pl.multiple_of(r * NORM_ROWS, NORM_ROWS)
        h = _modulated_rmsnorm_rows(x_ref[pl.ds(r0, NORM_ROWS), :], g_ref[...], sc_ref[...], sh_ref[...])
        h_ref[pl.ds(r0, NORM_ROWS), :] = h
        hi = h.astype(BF16)
        hi_scr[pl.ds(r0, NORM_ROWS), :] = hi
        lo_scr[pl.ds(r0, NORM_ROWS), :] = (h - hi.astype(F32)).astype(BF16)
        return 0
    lax.fori_loop(0, N2_TM // NORM_ROWS, body, 0)
    a = jnp.dot(hi_scr[...], wcat_ref[...], preferred_element_type=F32)
    b = jnp.dot(lo_scr[...], whi_ref[...], preferred_element_type=F32)
    logits = a[:, :RT_PAD] + a[:, RT_PAD:] + b + b_ref[...]
    lt_ref[...] = logits.T[:N_EXPERTS, :]


def _norm2(x1, norm2_g, mod, wcat, whi, b_pad):
    row = lambda n: pl.BlockSpec((1, D_MODEL), lambda i, n=n: (0, n))
    return pl.pallas_call(
        _norm2_kernel,
        out_shape=(
            jax.ShapeDtypeStruct((SEQ, D_MODEL), F32),
            jax.ShapeDtypeStruct((N_EXPERTS, SEQ), F32),
        ),
        grid=(SEQ // N2_TM,),
        in_specs=[
            pl.BlockSpec((N2_TM, D_MODEL), lambda i: (i, 0)),
            pl.BlockSpec((1, D_MODEL), lambda i: (0, 0)),
            row(4), row(3),
            pl.BlockSpec((D_MODEL, 2 * RT_PAD), lambda i: (0, 0)),
            pl.BlockSpec((D_MODEL, RT_PAD), lambda i: (0, 0)),
            pl.BlockSpec((1, RT_PAD), lambda i: (0, 0)),
        ],
        out_specs=(
            pl.BlockSpec((N2_TM, D_MODEL), lambda i: (i, 0)),
            pl.BlockSpec((N_EXPERTS, N2_TM), lambda i: (0, i)),
        ),
        scratch_shapes=[pltpu.VMEM((N2_TM, D_MODEL), BF16), pltpu.VMEM((N2_TM, D_MODEL), BF16)],
        compiler_params=_params(("arbitrary",)),
        name="norm2",
    )(x1, norm2_g, mod, mod, wcat, whi, b_pad)


RT_CH = 1024
RT_SB = 256


def _route_kernel(lt_ref, dest_ref, gate_ref, nblk_ref, idx_scr, rank_scr):
    e_col = lax.broadcasted_iota(I32, (N_EXPERTS, RT_CH), 0).astype(F32)
    tri = (lax.broadcasted_iota(I32, (RT_SB, RT_SB), 0) < lax.broadcasted_iota(I32, (RT_SB, RT_SB), 1)).astype(BF16)
    carry = jnp.zeros((N_EXPERTS, 1), F32)
    for c in range(SEQ // RT_CH):
        cs = slice(c * RT_CH, (c + 1) * RT_CH)
        work = lt_ref[:, cs]
        vals = []
        mask = jnp.zeros((N_EXPERTS, RT_CH), F32)
        for k in range(TOP_K):
            m = work.max(axis=0, keepdims=True)
            idx = jnp.where(work == m, e_col, float(N_EXPERTS)).min(axis=0, keepdims=True)
            sel = e_col == idx
            vals.append(m)
            idx_scr[k:k + 1, cs] = idx
            mask = jnp.where(sel, 1.0, mask)
            work = jnp.where(sel, -jnp.inf, work)
        ex = [jnp.exp(v - vals[0]) for v in vals]
        den = ex[0] + ex[1] + ex[2] + ex[3]
        for k in range(TOP_K):
            gate_ref[k:k + 1, cs] = ex[k] / den
        for b in range(RT_CH // RT_SB):
            blk = mask[:, b * RT_SB:(b + 1) * RT_SB]
            pref = jnp.dot(blk.astype(BF16), tri, preferred_element_type=F32)
            rank_scr[:, c * RT_CH + b * RT_SB:c * RT_CH + (b + 1) * RT_SB] = pref + carry
            carry = carry + blk.sum(axis=1, keepdims=True)
    nblk = jnp.floor((carry + (MOE_TM - 1)) * (1.0 / MOE_TM))
    nblk_b = jnp.broadcast_to(nblk, (N_EXPERTS, LANES))
    lower = (lax.broadcasted_iota(I32, (N_EXPERTS, N_EXPERTS), 1) < lax.broadcasted_iota(I32, (N_EXPERTS, N_EXPERTS), 0)).astype(BF16)
    start_blk = jnp.dot(lower, nblk_b.astype(BF16), preferred_element_type=F32)
    nblk_ref[...] = nblk_b.astype(I32)
    start = start_blk[:, 0:1] * float(MOE_TM)
    for c in range(SEQ // RT_CH):
        cs = slice(c * RT_CH, (c + 1) * RT_CH)
        slot = rank_scr[:, cs] + start
        for k in range(TOP_K):
            sel = e_col == idx_scr[k:k + 1, cs]
            dest_ref[k:k + 1, cs] = jnp.where(sel, slot, 0.0).sum(axis=0, keepdims=True).astype(I32)


def _route(logits_t):
    return pl.pallas_call(
        _route_kernel,
        out_shape=(
            jax.ShapeDtypeStruct((TOP_K, SEQ), I32),
            jax.ShapeDtypeStruct((TOP_K, SEQ), F32),
            jax.ShapeDtypeStruct((N_EXPERTS, LANES), I32),
        ),
        scratch_shapes=[pltpu.VMEM((8, SEQ), F32), pltpu.VMEM((N_EXPERTS, SEQ), F32)],
        compiler_params=pltpu.CompilerParams(vmem_limit_bytes=VMEM_LIMIT),
        name="route",
    )(logits_t)


DP_CH = 64


def _dispatch_kernel(dest_ref, lastblk_ref, h_hbm, xs_hbm, zero_buf, zsem, sem):
    zero_buf[...] = jnp.zeros_like(zero_buf)

    def zcopy(e):
        b = jnp.maximum(lastblk_ref[e], 0)
        return pltpu.make_async_copy(zero_buf, xs_hbm.at[pl.ds(pl.multiple_of(b * MOE_TM, MOE_TM), MOE_TM)], zsem)

    def zstart(e, _):
        @pl.when(lastblk_ref[e] >= 0)
        def _():
            zcopy(e).start()
        return 0

    def zwait(e, _):
        @pl.when(lastblk_ref[e] >= 0)
        def _():
            zcopy(e).wait()
        return 0

    lax.fori_loop(0, N_EXPERTS, zstart, 0)
    lax.fori_loop(0, N_EXPERTS, zwait, 0)

    def row_copy(t, k):
        d = dest_ref[k * SEQ + t]
        return pltpu.make_async_copy(h_hbm.at[pl.ds(t, 1)], xs_hbm.at[pl.ds(d, 1)], sem)

    def issue(c):
        def body(u, _):
            for k in range(TOP_K):
                row_copy(c * DP_CH + u, k).start()
            return 0
        lax.fori_loop(0, DP_CH, body, 0)

    def drain(c):
        def body(u, _):
            for k in range(TOP_K):
                row_copy(c * DP_CH + u, k).wait()
            return 0
        lax.fori_loop(0, DP_CH, body, 0)

    n_chunks = SEQ // DP_CH
    issue(0)

    def step(c, _):
        issue(c)
        drain(c - 1)
        return 0

    lax.fori_loop(1, n_chunks, step, 0)
    drain(n_chunks - 1)


def _dispatch(dest_flat, lastblk, h2):
    return pl.pallas_call(
        _dispatch_kernel,
        out_shape=jax.ShapeDtypeStruct((MOE_ROWS, D_MODEL), F32),
        grid_spec=pltpu.PrefetchScalarGridSpec(
            num_scalar_prefetch=2,
            grid=(1,),
            in_specs=[pl.BlockSpec(memory_space=pl.ANY)],
            out_specs=pl.BlockSpec(memory_space=pl.ANY),
            scratch_shapes=[
                pltpu.VMEM((MOE_TM, D_MODEL), F32),
                pltpu.SemaphoreType.DMA,
                pltpu.SemaphoreType.DMA,
            ],
        ),
        compiler_params=_params(("arbitrary",)),
        name="dispatch",
    )(dest_flat, lastblk, h2)


F1_TF = 512
F2_TN = 2048


def _ffn1_kernel(be_ref, nu_ref, x_ref, wg_ref, wl_ref, bg_ref, bl_ref, o_ref):
    @pl.when(pl.program_id(1) < nu_ref[0])
    def _():
        x = x_ref[...].astype(BF16)
        glu = jnp.dot(x, wg_ref[...], preferred_element_type=F32) + bg_ref[...]
        lin = jnp.dot(x, wl_ref[...], preferred_element_type=F32) + bl_ref[...]
        glu = jnp.minimum(glu, SWIGLU_LIMIT)
        lin = jnp.clip(lin, -SWIGLU_LIMIT, SWIGLU_LIMIT)
        o_ref[...] = (glu * jax.nn.sigmoid(SWIGLU_ALPHA * glu) * (lin + 1.0)).astype(BF16)


def _ffn1(block_expert, n_used, xs, w1_b, b1_3):
    nj = D_FF // F1_TF
    blk = lambda i, nu: jnp.minimum(i, nu[0] - 1)
    return pl.pallas_call(
        _ffn1_kernel,
        out_shape=jax.ShapeDtypeStruct((MOE_ROWS, D_FF), BF16),
        grid_spec=pltpu.PrefetchScalarGridSpec(
            num_scalar_prefetch=2,
            grid=(nj, MOE_NB),
            in_specs=[
                pl.BlockSpec((MOE_TM, D_MODEL), lambda j, i, be, nu: (blk(i, nu), 0)),
                pl.BlockSpec((None, D_MODEL, F1_TF), lambda j, i, be, nu: (be[blk(i, nu)], 0, j)),
                pl.BlockSpec((None, D_MODEL, F1_TF), lambda j, i, be, nu: (be[blk(i, nu)], 0, nj + j)),
                pl.BlockSpec((None, 1, F1_TF), lambda j, i, be, nu: (be[blk(i, nu)], 0, j)),
                pl.BlockSpec((None, 1, F1_TF), lambda j, i, be, nu: (be[blk(i, nu)], 0, nj + j)),
            ],
            out_specs=pl.BlockSpec((MOE_TM, F1_TF), lambda j, i, be, nu: (blk(i, nu), j)),
        ),
        compiler_params=_params(("arbitrary", "arbitrary")),
        name="ffn1",
    )(block_expert, n_used, xs, w1_b, w1_b, b1_3, b1_3)


def _ffn2_kernel(be_ref, nu_ref, a_ref, w_ref, b_ref, o_ref):
    @pl.when(pl.program_id(1) < nu_ref[0])
    def _():
        o_ref[...] = jnp.dot(a_ref[...], w_ref[...], preferred_element_type=F32) + b_ref[...]


def _ffn2(block_expert, n_used, act, w2_b, b2_3):
    blk = lambda i, nu: jnp.minimum(i, nu[0] - 1)
    return pl.pallas_call(
        _ffn2_kernel,
        out_shape=jax.ShapeDtypeStruct((MOE_ROWS, D_MODEL), F32),
        grid_spec=pltpu.PrefetchScalarGridSpec(
            num_scalar_prefetch=2,
            grid=(D_MODEL // F2_TN, MOE_NB),
            in_specs=[
                pl.BlockSpec((MOE_TM, D_FF), lambda j, i, be, nu: (blk(i, nu), 0)),
                pl.BlockSpec((None, D_FF, F2_TN), lambda j, i, be, nu: (be[blk(i, nu)], 0, j)),
                pl.BlockSpec((None, 1, F2_TN), lambda j, i, be, nu: (be[blk(i, nu)], 0, j)),
            ],
            out_specs=pl.BlockSpec((MOE_TM, F2_TN), lambda j, i, be, nu: (blk(i, nu), j)),
        ),
        compiler_params=_params(("arbitrary", "arbitrary")),
        name="ffn2",
    )(block_expert, n_used, act, w2_b, b2_3)


CB_TM = 128


def _combine_kernel(dest_ref, x_ref, gate_ref, g2_ref, ys_hbm, o_ref, buf, sem):
    i = pl.program_id(0)

    def row_copy(u, k):
        d = dest_ref[k * SEQ + i * CB_TM + u]
        return pltpu.make_async_copy(ys_hbm.at[pl.ds(d, 1)], buf.at[k, pl.ds(u, 1)], sem)

    def issue(u, _):
        for k in range(TOP_K):
            row_copy(u, k).start()
        return 0

    def drain(u, _):
        for k in range(TOP_K):
            row_copy(u, k).wait()
        return 0

    lax.fori_loop(0, CB_TM, issue, 0)
    lax.fori_loop(0, CB_TM, drain, 0)
    gate = gate_ref[...]
    y = buf[0] * gate[:, 0:1]
    for k in range(1, TOP_K):
        y = y + buf[k] * gate[:, k:k + 1]
    o_ref[...] = x_ref[...] + g2_ref[...] * y


def _combine(dest_flat, x1, gates_t, mod, ys):
    return pl.pallas_call(
        _combine_kernel,
        out_shape=jax.ShapeDtypeStruct((SEQ, D_MODEL), F32),
        grid_spec=pltpu.PrefetchScalarGridSpec(
            num_scalar_prefetch=1,
            grid=(SEQ // CB_TM,),
            in_specs=[
                pl.BlockSpec((CB_TM, D_MODEL), lambda i, d: (i, 0)),
                pl.BlockSpec((CB_TM, TOP_K), lambda i, d: (i, 0)),
                pl.BlockSpec((1, D_MODEL), lambda i, d: (0, 5)),
                pl.BlockSpec(memory_space=pl.ANY),
            ],
            out_specs=pl.BlockSpec((CB_TM, D_MODEL), lambda i, d: (i, 0)),
            scratch_shapes=[
                pltpu.VMEM((TOP_K, CB_TM, D_MODEL), F32),
                pltpu.SemaphoreType.DMA,
            ],
        ),
        compiler_params=_params(("arbitrary",)),
        name="combine",
    )(dest_flat, x1, gates_t, mod, ys)


def _rope_tables():
    t = jnp.arange(SEQ, dtype=I32)
    row = (t // GRID_W).astype(F32)
    col = (t % GRID_W).astype(F32)
    inv_freq = ROPE_THETA ** (-jnp.arange(0, ROPE_AXIS_DIM, 2, dtype=F32) / ROPE_AXIS_DIM)
    ang_r = row[:, None] * inv_freq
    ang_c = col[:, None] * inv_freq
    cos_t = jnp.concatenate([jnp.cos(ang_r), jnp.cos(ang_r), jnp.cos(ang_c), jnp.cos(ang_c)], axis=-1)
    sin_t = jnp.concatenate([-jnp.sin(ang_r), jnp.sin(ang_r), -jnp.sin(ang_c), jnp.sin(ang_c)], axis=-1)
    return cos_t, sin_t


def kernel(x, c, w_mod, b_mod, norm1_g, w_in, q_norm_g, k_norm_g, w_pool, pool_scale, w_out, norm2_g,
           w_router, b_router, w1, b1, w2, b2):
    assert x.shape == (1, SEQ, D_MODEL) and w_mod.shape[0] == 1
    x2 = x[0]
    cos_t, sin_t = _rope_tables()

    mod = _mod(c.reshape(D_MODEL, 1), w_mod[0], b_mod)

    pool_in, qt, k, vt = _inproj(x2, mod, norm1_g, w_in[0].astype(BF16), cos_t, sin_t, q_norm_g, k_norm_g)
    attn = _attention(qt, k, vt)
    pool = _pool(pool_in, w_pool[0].astype(BF16), pool_scale)
    x1 = _outproj(attn, pool, w_out[0].astype(BF16), x2, mod)

    wr = w_router[0]
    wr_hi = wr.astype(BF16)
    wr_lo = (wr - wr_hi.astype(F32)).astype(BF16)
    pad = lambda a: jnp.pad(a, ((0, 0), (0, RT_PAD - N_EXPERTS)))
    wcat = jnp.concatenate([pad(wr_hi), pad(wr_lo)], axis=1)
    h2, logits_t = _norm2(x1, norm2_g, mod, wcat, pad(wr_hi), pad(b_router))

    dest, gates, nblk = _route(logits_t)
    nblk = nblk[:, 0]
    end_blk = jnp.cumsum(nblk)
    n_used = end_blk[-1:].astype(I32)
    block_expert = jnp.clip(jnp.searchsorted(end_blk, jnp.arange(MOE_NB, dtype=I32), side="right"),
                            0, N_EXPERTS - 1).astype(I32)
    lastblk = jnp.where(nblk > 0, end_blk - 1, -1).astype(I32)
    dest_flat = dest.reshape(TOP_K * SEQ)

    xs = _dispatch(dest_flat, lastblk, h2)
    act = _ffn1(block_expert, n_used, xs, w1[0].astype(BF16), b1[0].reshape(N_EXPERTS, 1, 2 * D_FF))
    ys = _ffn2(block_expert, n_used, act, w2[0].astype(BF16), b2[0].reshape(N_EXPERTS, 1, D_MODEL))
    out = _combine(dest_flat, x1, gates.T, mod, ys)
    return out[None]
```

```python
import functools
import math

import jax
import jax.numpy as jnp
from jax import lax
from jax.experimental import pallas as pl
from jax.experimental.pallas import tpu as pltpu

F32 = jnp.float32
BF16 = jnp.bfloat16
I32 = jnp.int32

D_MODEL = 4096
SEQ = 8192
POOL_WIDTH = 2048
ATTN_WIDTH = 2048
HEAD_DIM = 128
N_HEADS = 16
N_KV_HEADS = 4
GROUP = N_HEADS // N_KV_HEADS
KV_WIDTH = N_KV_HEADS * HEAD_DIM
IN_WIDTH = POOL_WIDTH + ATTN_WIDTH + 2 * KV_WIDTH
POOL_WINDOWS = (2, 4, 8, 16)
POOL_GROUP_WIDTH = POOL_WIDTH // len(POOL_WINDOWS)
GRID_W = 64
ROPE_THETA = 10000.0
ROPE_AXIS_DIM = HEAD_DIM // 2
N_EXPERTS = 32
TOP_K = 4
D_FF = D_MODEL // 4
SWIGLU_ALPHA = 1.702
SWIGLU_LIMIT = 7.0
N_MOD = 6
EPS = 1e-6

LANES = 128
VMEM_LIMIT = 56 * 1024 * 1024

Q_SCALE = (HEAD_DIM ** -0.5) * math.log2(math.e)

MOE_TM = 256
MOE_NB = SEQ * TOP_K // MOE_TM + N_EXPERTS
MOE_ROWS = MOE_NB * MOE_TM


def _params(sem, vmem=VMEM_LIMIT):
    return pltpu.CompilerParams(dimension_semantics=sem, vmem_limit_bytes=vmem)


MOD_TN = 512
MOD_KC = 256


def _mod_kernel(c_ref, w_ref, b_ref, o_ref):
    def body(k, acc):
        r = pl.multiple_of(k * MOD_KC, MOD_KC)
        ck = c_ref[pl.ds(r, MOD_KC), :]
        ck = ck * jax.nn.sigmoid(ck)
        p = w_ref[pl.ds(r, MOD_KC), :] * ck
        return acc + p.reshape(MOD_KC // 8, 8, MOD_TN).sum(axis=0)

    acc = lax.fori_loop(0, D_MODEL // MOD_KC, body, jnp.zeros((8, MOD_TN), F32))
    o_ref[...] = acc.sum(axis=0, keepdims=True) + b_ref[...]


def _mod(c_col, w_mod, b_mod):
    n = N_MOD * D_MODEL
    return pl.pallas_call(
        _mod_kernel,
        out_shape=jax.ShapeDtypeStruct((1, n), F32),
        grid=(n // MOD_TN,),
        in_specs=[
            pl.BlockSpec((D_MODEL, 1), lambda j: (0, 0)),
            pl.BlockSpec((D_MODEL, MOD_TN), lambda j: (0, j)),
            pl.BlockSpec((1, MOD_TN), lambda j: (0, j)),
        ],
        out_specs=pl.BlockSpec((1, MOD_TN), lambda j: (0, j)),
        compiler_params=_params(("arbitrary",)),
        name="mod",
    )(c_col, w_mod, b_mod)


IP_TM = 512
IP_TN = 512
IP_NJ = IN_WIDTH // IP_TN
NORM_ROWS = 16
BF16_SUBLANES = 16
KA_W = 2 * HEAD_DIM
VA_H = HEAD_DIM + BF16_SUBLANES


def _modulated_rmsnorm_rows(x, g, sc, sh):
    ms = jnp.mean(x * x, axis=-1, keepdims=True)
    return x * lax.rsqrt(ms + EPS) * g * (1.0 + sc) + sh


def _norm_rope(xh, g, cos, sin_signed):
    ms = jnp.mean(xh * xh, axis=-1, keepdims=True)
    y = xh * lax.rsqrt(ms + EPS) * g
    lane = lax.broadcasted_iota(I32, y.shape, 1)
    partner = jnp.where((lane % 64) < 32, pltpu.roll(y, 96, 1), pltpu.roll(y, 32, 1))
    return y * cos + partner * sin_signed


def _inproj_kernel(x_ref, g_ref, sc_ref, sh_ref, w_ref, cos_ref, sin_ref, qg_ref, kg_ref,
                   pool_ref, qt_ref, k_ref, vt_ref, h_scr):
    j = pl.program_id(1)

    @pl.when(j == 0)
    def _():
        def body(r, _):
            r0 = pl.multiple_of(r * NORM_ROWS, NORM_ROWS)
            h = _modulated_rmsnorm_rows(x_ref[pl.ds(r0, NORM_ROWS), :], g_ref[...], sc_ref[...], sh_ref[...])
            h_scr[pl.ds(r0, NORM_ROWS), :] = h.astype(BF16)
            return 0
        lax.fori_loop(0, IP_TM // NORM_ROWS, body, 0)

    acc = jnp.dot(h_scr[...], w_ref[...], preferred_element_type=F32)

    @pl.when(j < 4)
    def _():
        pool_ref[...] = acc.astype(BF16)

    @pl.when((j >= 4) & (j < 8))
    def _():
        for hh in range(IP_TN // HEAD_DIM):
            sl = slice(hh * HEAD_DIM, (hh + 1) * HEAD_DIM)
            r = _norm_rope(acc[:, sl], qg_ref[...], cos_ref[...], sin_ref[...]) * Q_SCALE
            qt_ref[sl, :] = r.T.astype(BF16)

    @pl.when(j == 8)
    def _():
        lane = lax.broadcasted_iota(I32, (IP_TM, HEAD_DIM), 1)
        one_col = jnp.where(lane == 0, 1.0, 0.0).astype(BF16)
        for hh in range(N_KV_HEADS):
            sl = slice(hh * HEAD_DIM, (hh + 1) * HEAD_DIM)
            k_ref[:, hh * KA_W:hh * KA_W + HEAD_DIM] = _norm_rope(
                acc[:, sl], kg_ref[...], cos_ref[...], sin_ref[...]).astype(BF16)
            k_ref[:, hh * KA_W + HEAD_DIM:(hh + 1) * KA_W] = one_col

    @pl.when(j == 9)
    def _():
        for hh in range(N_KV_HEADS):
            sl = slice(hh * HEAD_DIM, (hh + 1) * HEAD_DIM)
            vt_ref[hh * VA_H:hh * VA_H + HEAD_DIM, :] = acc[:, sl].T.astype(BF16)
            vt_ref[hh * VA_H + HEAD_DIM:(hh + 1) * VA_H, :] = jnp.ones((VA_H - HEAD_DIM, IP_TM), BF16)


def _inproj(x2, mod, norm1_g, w_in_b, cos_t, sin_t, qg, kg):
    row = lambda n: pl.BlockSpec((1, D_MODEL), lambda i, j, n=n: (0, n))
    return pl.pallas_call(
        _inproj_kernel,
        out_shape=(
            jax.ShapeDtypeStruct((SEQ, POOL_WIDTH), BF16),
            jax.ShapeDtypeStruct((ATTN_WIDTH, SEQ), BF16),
            jax.ShapeDtypeStruct((SEQ, N_KV_HEADS * KA_W), BF16),
            jax.ShapeDtypeStruct((N_KV_HEADS * VA_H, SEQ), BF16),
        ),
        grid=(SEQ // IP_TM, IP_NJ),
        in_specs=[
            pl.BlockSpec((IP_TM, D_MODEL), lambda i, j: (i, 0)),
            pl.BlockSpec((1, D_MODEL), lambda i, j: (0, 0)),
            row(1), row(0),
            pl.BlockSpec((D_MODEL, IP_TN), lambda i, j: (0, j)),
            pl.BlockSpec((IP_TM, HEAD_DIM), lambda i, j: (i, 0)),
            pl.BlockSpec((IP_TM, HEAD_DIM), lambda i, j: (i, 0)),
            pl.BlockSpec((1, HEAD_DIM), lambda i, j: (0, 0)),
            pl.BlockSpec((1, HEAD_DIM), lambda i, j: (0, 0)),
        ],
        out_specs=(
            pl.BlockSpec((IP_TM, IP_TN), lambda i, j: (i, jnp.minimum(j, 3))),
            pl.BlockSpec((IP_TN, IP_TM), lambda i, j: (jnp.clip(j - 4, 0, 3), i)),
            pl.BlockSpec((IP_TM, N_KV_HEADS * KA_W), lambda i, j: (i, 0)),
            pl.BlockSpec((N_KV_HEADS * VA_H, IP_TM), lambda i, j: (0, i)),
        ),
        scratch_shapes=[pltpu.VMEM((IP_TM, D_MODEL), BF16)],
        compiler_params=_params(("arbitrary", "arbitrary")),
        name="inproj",
    )(x2, norm1_g, mod, mod, w_in_b, cos_t, sin_t, qg, kg)


AT_TQ = 512
AT_TK = 1024
AT_TK_ONLINE = 512
SHIFT_LIMIT = 60.0


def _attn_kernel(qt_ref, k_ref, vt_ref, o_ref, qa_scr, p_scr, kmax_scr):
    h = pl.program_id(0)
    i = pl.program_id(1)

    @pl.when((i == 0) & (h % GROUP == 0))
    def _():
        def body(c, mx):
            c0 = pl.multiple_of(c * AT_TK, AT_TK)
            kc = k_ref[pl.ds(c0, AT_TK), :HEAD_DIM].astype(F32)
            n2 = (kc * kc).sum(axis=1, keepdims=True)
            return jnp.maximum(mx, n2.max(axis=0, keepdims=True))
        mx = lax.fori_loop(0, SEQ // AT_TK, body, jnp.zeros((1, 1), F32))
        kmax_scr[...] = jnp.broadcast_to(jnp.sqrt(mx), kmax_scr.shape)

    q = qt_ref[...].astype(F32)
    bound = jnp.sqrt((q * q).sum(axis=0, keepdims=True)) * kmax_scr[0:1, 0:1] * 1.01
    fast = jnp.max(bound) <= SHIFT_LIMIT

    @pl.when(fast)
    def _():
        qa_scr[0:HEAD_DIM, :] = qt_ref[...]
        row = lax.broadcasted_iota(I32, (KA_W - HEAD_DIM, AT_TQ), 0)
        qa_scr[HEAD_DIM:, :] = jnp.where(row == 0, -bound, 0.0).astype(BF16)

        def body(c, _):
            c0 = pl.multiple_of(c * AT_TK, AT_TK)
            s = jnp.dot(k_ref[pl.ds(c0, AT_TK), :], qa_scr[...], preferred_element_type=F32)
            p_scr[pl.ds(c0, AT_TK), :] = jnp.exp2(s).astype(BF16)
            return 0

        lax.fori_loop(0, SEQ // AT_TK, body, 0)
        o = jnp.dot(vt_ref[...], p_scr[...], preferred_element_type=F32)
        o_ref[...] = (o[:HEAD_DIM] * (1.0 / o[HEAD_DIM:HEAD_DIM + 1])).T.astype(BF16)

    @pl.when(jnp.logical_not(fast))
    def _():
        qt = qt_ref[...]

        def chunk(c, carry):
            m, l, acc = carry
            c0 = pl.multiple_of(c * AT_TK_ONLINE, AT_TK_ONLINE)
            s = jnp.dot(k_ref[pl.ds(c0, AT_TK_ONLINE), :HEAD_DIM], qt, preferred_element_type=F32)
            m_new = jnp.maximum(m, s.max(axis=0, keepdims=True))
            alpha = jnp.exp2(m - m_new)
            p = jnp.exp2(s - m_new)
            l = alpha * l + p.sum(axis=0, keepdims=True)
            pv = jnp.dot(vt_ref[:HEAD_DIM, pl.ds(c0, AT_TK_ONLINE)], p.astype(BF16), preferred_element_type=F32)
            return m_new, l, alpha * acc + pv

        init = (jnp.full((1, AT_TQ), -jnp.inf, F32), jnp.zeros((1, AT_TQ), F32),
                jnp.zeros((HEAD_DIM, AT_TQ), F32))
        _, l, acc = lax.fori_loop(0, SEQ // AT_TK_ONLINE, chunk, init)
        o_ref[...] = (acc * (1.0 / l)).T.astype(BF16)


def _attention(qt, k, vt):
    return pl.pallas_call(
        _attn_kernel,
        out_shape=jax.ShapeDtypeStruct((SEQ, ATTN_WIDTH), BF16),
        grid=(N_HEADS, SEQ // AT_TQ),
        in_specs=[
            pl.BlockSpec((HEAD_DIM, AT_TQ), lambda h, i: (h, i)),
            pl.BlockSpec((SEQ, KA_W), lambda h, i: (0, h // GROUP)),
            pl.BlockSpec((VA_H, SEQ), lambda h, i: (h // GROUP, 0)),
        ],
        out_specs=pl.BlockSpec((AT_TQ, HEAD_DIM), lambda h, i: (i, h)),
        scratch_shapes=[
            pltpu.VMEM((KA_W, AT_TQ), BF16),
            pltpu.VMEM((SEQ, AT_TQ), BF16),
            pltpu.VMEM((8, LANES), F32),
        ],
        compiler_params=_params(("arbitrary", "arbitrary")),
        name="attn",
    )(qt, k, vt)


PL_TM = 256
PL_HALO = 16


def _pool_kernel(prev_ref, main_ref, next_ref, wp_ref, scale_ref, o_ref, buf):
    i = pl.program_id(0)
    last = pl.num_programs(0) - 1
    buf[0:PL_HALO, :] = jnp.where(i == 0, 0.0, prev_ref[...].astype(F32))
    buf[PL_HALO:PL_HALO + PL_TM, :] = main_ref[...].astype(F32)
    buf[PL_HALO + PL_TM:, :] = jnp.where(i == last, 0.0, next_ref[...].astype(F32))
    t = i * PL_TM + lax.broadcasted_iota(I32, (PL_TM, 1), 0)
    for gi, w in enumerate(POOL_WINDOWS):
        cols = slice(gi * POOL_GROUP_WIDTH, (gi + 1) * POOL_GROUP_WIDTH)
        win = buf[PL_HALO - w // 2:PL_HALO - w // 2 + PL_TM, cols]
        for d in range(-w // 2 + 1, w // 2):
            win = win + buf[PL_HALO + d:PL_HALO + d + PL_TM, cols]
        lo = jnp.maximum(t - w // 2, 0)
        hi = jnp.minimum(t + w // 2 - 1, SEQ - 1)
        cnt = (hi - lo + 1).astype(F32)
        dlt = win / cnt - buf[PL_HALO:PL_HALO + PL_TM, cols]
        y = jnp.dot(dlt.astype(BF16), wp_ref[gi], preferred_element_type=F32)
        o_ref[:, cols] = (y * scale_ref[:, cols]).astype(BF16)


def _pool(pool_in, w_pool_b, pool_scale):
    nh = PL_TM // PL_HALO
    n_halo_blocks = SEQ // PL_HALO
    return pl.pallas_call(
        _pool_kernel,
        out_shape=jax.ShapeDtypeStruct((SEQ, POOL_WIDTH), BF16),
        grid=(SEQ // PL_TM,),
        in_specs=[
            pl.BlockSpec((PL_HALO, POOL_WIDTH), lambda i: (jnp.maximum(i * nh - 1, 0), 0)),
            pl.BlockSpec((PL_TM, POOL_WIDTH), lambda i: (i, 0)),
            pl.BlockSpec((PL_HALO, POOL_WIDTH), lambda i: (jnp.minimum((i + 1) * nh, n_halo_blocks - 1), 0)),
            pl.BlockSpec((len(POOL_WINDOWS), POOL_GROUP_WIDTH, POOL_GROUP_WIDTH), lambda i: (0, 0, 0)),
            pl.BlockSpec((1, POOL_WIDTH), lambda i: (0, 0)),
        ],
        out_specs=pl.BlockSpec((PL_TM, POOL_WIDTH), lambda i: (i, 0)),
        scratch_shapes=[pltpu.VMEM((PL_TM + 2 * PL_HALO, POOL_WIDTH), F32)],
        compiler_params=_params(("arbitrary",)),
        name="pool",
    )(pool_in, pool_in, pool_in, w_pool_b, pool_scale)


OP_TM = 1024
OP_TN = 512


def _outproj_kernel(a_ref, p_ref, wa_ref, wp_ref, x_ref, g_ref, o_ref):
    acc = jnp.dot(a_ref[...], wa_ref[...], preferred_element_type=F32)
    acc = acc + jnp.dot(p_ref[...], wp_ref[...], preferred_element_type=F32)
    o_ref[...] = x_ref[...] + g_ref[...] * acc


def _outproj(attn, pool, w_out_b, x2, mod):
    return pl.pallas_call(
        _outproj_kernel,
        out_shape=jax.ShapeDtypeStruct((SEQ, D_MODEL), F32),
        grid=(SEQ // OP_TM, D_MODEL // OP_TN),
        in_specs=[
            pl.BlockSpec((OP_TM, ATTN_WIDTH), lambda i, j: (i, 0)),
            pl.BlockSpec((OP_TM, POOL_WIDTH), lambda i, j: (i, 0)),
            pl.BlockSpec((ATTN_WIDTH, OP_TN), lambda i, j: (0, j)),
            pl.BlockSpec((POOL_WIDTH, OP_TN), lambda i, j: (1, j)),
            pl.BlockSpec((OP_TM, OP_TN), lambda i, j: (i, j)),
            pl.BlockSpec((1, OP_TN), lambda i, j: (0, 2 * (D_MODEL // OP_TN) + j)),
        ],
        out_specs=pl.BlockSpec((OP_TM, OP_TN), lambda i, j: (i, j)),
        compiler_params=_params(("arbitrary", "arbitrary")),
        name="outproj",
    )(attn, pool, w_out_b, w_out_b, x2, mod)


N2_TM = 256
RT_PAD = LANES


def _norm2_kernel(x_ref, g_ref, sc_ref, sh_ref, wcat_ref, whi_ref, b_ref, lt_ref, hi_scr, lo_scr):
    def body(r, _):
        r0 = pl.multiple_of(r * NORM_ROWS, NORM_ROWS)
        h = _modulated_rmsnorm_rows(x_ref[pl.ds(r0, NORM_ROWS), :], g_ref[...], sc_ref[...], sh_ref[...])
        hi = h.astype(BF16)
        hi_scr[pl.ds(r0, NORM_ROWS), :] = hi
        lo_scr[pl.ds(r0, NORM_ROWS), :] = (h - hi.astype(F32)).astype(BF16)
        return 0
    lax.fori_loop(0, N2_TM // NORM_ROWS, body, 0)
    a = jnp.dot(hi_scr[...], wcat_ref[...], preferred_element_type=F32)
    b = jnp.dot(lo_scr[...], whi_ref[...], preferred_element_type=F32)
    logits = a[:, :RT_PAD] + a[:, RT_PAD:] + b + b_ref[...]
    lt_ref[...] = logits.T[:N_EXPERTS, :]


def _norm2(x1, norm2_g, mod, wcat, whi, b_pad):
    row = lambda n: pl.BlockSpec((1, D_MODEL), lambda i, n=n: (0, n))
    return pl.pallas_call(
        _norm2_kernel,
        out_shape=jax.ShapeDtypeStruct((N_EXPERTS, SEQ), F32),
        grid=(SEQ // N2_TM,),
        in_specs=[
            pl.BlockSpec((N2_TM, D_MODEL), lambda i: (i, 0)),
            pl.BlockSpec((1, D_MODEL), lambda i: (0, 0)),
            row(4), row(3),
            pl.BlockSpec((D_MODEL, 2 * RT_PAD), lambda i: (0, 0)),
            pl.BlockSpec((D_MODEL, RT_PAD), lambda i: (0, 0)),
            pl.BlockSpec((1, RT_PAD), lambda i: (0, 0)),
        ],
        out_specs=pl.BlockSpec((N_EXPERTS, N2_TM), lambda i: (0, i)),
        scratch_shapes=[pltpu.VMEM((N2_TM, D_MODEL), BF16), pltpu.VMEM((N2_TM, D_MODEL), BF16)],
        compiler_params=_params(("arbitrary",)),
        name="norm2",
    )(x1, norm2_g, mod, mod, wcat, whi, b_pad)


RT_CH = 1024
RT_SB = 256


def _route_kernel(lt_ref, dest_ref, gate_ref, nblk_ref, idx_scr, rank_scr):
    e_col = lax.broadcasted_iota(I32, (N_EXPERTS, RT_CH), 0).astype(F32)
    tri = (lax.broadcasted_iota(I32, (RT_SB, RT_SB), 0) < lax.broadcasted_iota(I32, (RT_SB, RT_SB), 1)).astype(BF16)
    carry = jnp.zeros((N_EXPERTS, 1), F32)
    for c in range(SEQ // RT_CH):
        cs = slice(c * RT_CH, (c + 1) * RT_CH)
        work = lt_ref[:, cs]
        vals = []
        mask = jnp.zeros((N_EXPERTS, RT_CH), F32)
        for k in range(TOP_K):
            m = work.max(axis=0, keepdims=True)
            idx = jnp.where(work == m, e_col, float(N_EXPERTS)).min(axis=0, keepdims=True)
            sel = e_col == idx
            vals.append(m)
            idx_scr[k:k + 1, cs] = idx
            mask = jnp.where(sel, 1.0, mask)
            work = jnp.where(sel, -jnp.inf, work)
        ex = [jnp.exp(v - vals[0]) for v in vals]
        den = ex[0] + ex[1] + ex[2] + ex[3]
        for k in range(TOP_K):
            gate_ref[k:k + 1, cs] = ex[k] / den
        for b in range(RT_CH // RT_SB):
            blk = mask[:, b * RT_SB:(b + 1) * RT_SB]
            pref = jnp.dot(blk.astype(BF16), tri, preferred_element_type=F32)
            rank_scr[:, c * RT_CH + b * RT_SB:c * RT_CH + (b + 1) * RT_SB] = pref + carry
            carry = carry + blk.sum(axis=1, keepdims=True)
    nblk = jnp.floor((carry + (MOE_TM - 1)) * (1.0 / MOE_TM))
    nblk_b = jnp.broadcast_to(nblk, (N_EXPERTS, LANES))
    lower = (lax.broadcasted_iota(I32, (N_EXPERTS, N_EXPERTS), 1) < lax.broadcasted_iota(I32, (N_EXPERTS, N_EXPERTS), 0)).astype(BF16)
    start_blk = jnp.dot(lower, nblk_b.astype(BF16), preferred_element_type=F32)
    nblk_ref[...] = nblk_b.astype(I32)
    start = start_blk[:, 0:1] * float(MOE_TM)
    for c in range(SEQ // RT_CH):
        cs = slice(c * RT_CH, (c + 1) * RT_CH)
        slot = rank_scr[:, cs] + start
        for k in range(TOP_K):
            sel = e_col == idx_scr[k:k + 1, cs]
            dest_ref[k:k + 1, cs] = jnp.where(sel, slot, 0.0).sum(axis=0, keepdims=True).astype(I32)


def _route(logits_t):
    return pl.pallas_call(
        _route_kernel,
        out_shape=(
            jax.ShapeDtypeStruct((TOP_K, SEQ), I32),
            jax.ShapeDtypeStruct((TOP_K, SEQ), F32),
            jax.ShapeDtypeStruct((N_EXPERTS, LANES), I32),
        ),
        scratch_shapes=[pltpu.VMEM((8, SEQ), F32), pltpu.VMEM((N_EXPERTS, SEQ), F32)],
        compiler_params=pltpu.CompilerParams(vmem_limit_bytes=VMEM_LIMIT),
        name="route",
    )(logits_t)


DP_TM = 256
XS_W = D_MODEL // 2
U32 = jnp.uint32


def _dispatch_kernel(dest_ref, lastblk_ref, x_ref, g_ref, sc_ref, sh_ref, xs_hbm, pk, zero_buf, zsem, sem):
    i = pl.program_id(0)
    par = i % 2

    @pl.when(i == 0)
    def _():
        zero_buf[...] = jnp.zeros_like(zero_buf)

        def zcopy(e):
            b = jnp.maximum(lastblk_ref[e], 0)
            return pltpu.make_async_copy(zero_buf, xs_hbm.at[pl.ds(pl.multiple_of(b * MOE_TM, MOE_TM), MOE_TM)], zsem)

        def zstart(e, _):
            @pl.when(lastblk_ref[e] >= 0)
            def _():
                zcopy(e).start()
            return 0

        def zwait(e, _):
            @pl.when(lastblk_ref[e] >= 0)
            def _():
                zcopy(e).wait()
            return 0

        lax.fori_loop(0, N_EXPERTS, zstart, 0)
        lax.fori_loop(0, N_EXPERTS, zwait, 0)

    def norm_body(r, _):
        r0 = pl.multiple_of(r * NORM_ROWS, NORM_ROWS)
        h = _modulated_rmsnorm_rows(x_ref[pl.ds(r0, NORM_ROWS), :], g_ref[...], sc_ref[...], sh_ref[...])
        lo = lax.bitcast_convert_type(h[:, :XS_W].astype(BF16).astype(F32), U32)
        hi = lax.bitcast_convert_type(h[:, XS_W:].astype(BF16).astype(F32), U32)
        pk[par, pl.ds(r0, NORM_ROWS), :] = (lo >> 16) | (hi & jnp.uint32(0xFFFF0000))
        return 0

    lax.fori_loop(0, DP_TM // NORM_ROWS, norm_body, 0)

    def issue(u, _):
        for k in range(TOP_K):
            d = dest_ref[k * SEQ + i * DP_TM + u]
            pltpu.make_async_copy(pk.at[par, pl.ds(u, 1)], xs_hbm.at[pl.ds(d, 1)], sem.at[par]).start()
        return 0

    lax.fori_loop(0, DP_TM, issue, 0)

    def drain(p):
        for _ in range(TOP_K):
            pltpu.make_async_copy(pk.at[p], xs_hbm.at[pl.ds(0, DP_TM)], sem.at[p]).wait()

    @pl.when(i > 0)
    def _():
        drain(1 - par)

    @pl.when(i == pl.num_programs(0) - 1)
    def _():
        drain(par)


def _dispatch(dest_flat, lastblk, x1, norm2_g, mod):
    row = lambda n: pl.BlockSpec((1, D_MODEL), lambda i, d, lb, n=n: (0, n))
    return pl.pallas_call(
        _dispatch_kernel,
        out_shape=jax.ShapeDtypeStruct((MOE_ROWS, XS_W), U32),
        grid_spec=pltpu.PrefetchScalarGridSpec(
            num_scalar_prefetch=2,
            grid=(SEQ // DP_TM,),
            in_specs=[
                pl.BlockSpec((DP_TM, D_MODEL), lambda i, d, lb: (i, 0)),
                pl.BlockSpec((1, D_MODEL), lambda i, d, lb: (0, 0)),
                row(4), row(3),
            ],
            out_specs=pl.BlockSpec(memory_space=pl.ANY),
            scratch_shapes=[
                pltpu.VMEM((2, DP_TM, XS_W), U32),
                pltpu.VMEM((MOE_TM, XS_W), U32),
                pltpu.SemaphoreType.DMA,
                pltpu.SemaphoreType.DMA((2,)),
            ],
        ),
        compiler_params=_params(("arbitrary",)),
        name="dispatch",
    )(dest_flat, lastblk, x1, norm2_g, mod, mod)


F1_TF = 512
F2_TN = 2048


def _ffn1_kernel(be_ref, nu_ref, x_ref, wg_ref, wl_ref, bg_ref, bl_ref, o_ref):
    @pl.when(pl.program_id(1) < nu_ref[0])
    def _():
        xp = x_ref[...]
        x_lo = lax.bitcast_convert_type(xp << 16, F32).astype(BF16)
        x_hi = lax.bitcast_convert_type(xp & jnp.uint32(0xFFFF0000), F32).astype(BF16)

        def proj(w_ref, b_ref):
            y = jnp.dot(x_lo, w_ref[:XS_W, :], preferred_element_type=F32)
            return y + jnp.dot(x_hi, w_ref[XS_W:, :], preferred_element_type=F32) + b_ref[...]

        glu = proj(wg_ref, bg_ref)
        lin = proj(wl_ref, bl_ref)
        glu = jnp.minimum(glu, SWIGLU_LIMIT)
        lin = jnp.clip(lin, -SWIGLU_LIMIT, SWIGLU_LIMIT)
        o_ref[...] = (glu * jax.nn.sigmoid(SWIGLU_ALPHA * glu) * (lin + 1.0)).astype(BF16)


def _ffn1(block_expert, n_used, xs, w1_b, b1_3):
    nj = D_FF // F1_TF
    blk = lambda i, nu: jnp.minimum(i, nu[0] - 1)
    return pl.pallas_call(
        _ffn1_kernel,
        out_shape=jax.ShapeDtypeStruct((MOE_ROWS, D_FF), BF16),
        grid_spec=pltpu.PrefetchScalarGridSpec(
            num_scalar_prefetch=2,
            grid=(nj, MOE_NB),
            in_specs=[
                pl.BlockSpec((MOE_TM, XS_W), lambda j, i, be, nu: (blk(i, nu), 0)),
                pl.BlockSpec((None, D_MODEL, F1_TF), lambda j, i, be, nu: (be[blk(i, nu)], 0, j)),
                pl.BlockSpec((None, D_MODEL, F1_TF), lambda j, i, be, nu: (be[blk(i, nu)], 0, nj + j)),
                pl.BlockSpec((None, 1, F1_TF), lambda j, i, be, nu: (be[blk(i, nu)], 0, j)),
                pl.BlockSpec((None, 1, F1_TF), lambda j, i, be, nu: (be[blk(i, nu)], 0, nj + j)),
            ],
            out_specs=pl.BlockSpec((MOE_TM, F1_TF), lambda j, i, be, nu: (blk(i, nu), j)),
        ),
        compiler_params=_params(("arbitrary", "arbitrary")),
        name="ffn1",
    )(block_expert, n_used, xs, w1_b, w1_b, b1_3, b1_3)


def _ffn2_kernel(be_ref, nu_ref, a_ref, w_ref, b_ref, o_ref):
    @pl.when(pl.program_id(1) < nu_ref[0])
    def _():
        o_ref[...] = jnp.dot(a_ref[...], w_ref[...], preferred_element_type=F32) + b_ref[...]


def _ffn2(block_expert, n_used, act, w2_b, b2_3):
    blk = lambda i, nu: jnp.minimum(i, nu[0] - 1)
    return pl.pallas_call(
        _ffn2_kernel,
        out_shape=jax.ShapeDtypeStruct((MOE_ROWS, D_MODEL), F32),
        grid_spec=pltpu.PrefetchScalarGridSpec(
            num_scalar_prefetch=2,
            grid=(D_MODEL // F2_TN, MOE_NB),
            in_specs=[
                pl.BlockSpec((MOE_TM, D_FF), lambda j, i, be, nu: (blk(i, nu), 0)),
                pl.BlockSpec((None, D_FF, F2_TN), lambda j, i, be, nu: (be[blk(i, nu)], 0, j)),
                pl.BlockSpec((None, 1, F2_TN), lambda j, i, be, nu: (be[blk(i, nu)], 0, j)),
            ],
            out_specs=pl.BlockSpec((MOE_TM, F2_TN), lambda j, i, be, nu: (blk(i, nu), j)),
        ),
        compiler_params=_params(("arbitrary", "arbitrary")),
        name="ffn2",
    )(block_expert, n_used, act, w2_b, b2_3)


CB_TM = 128


def _combine_kernel(dest_ref, x_ref, gate_ref, g2_ref, ys_hbm, o_ref, buf, sem):
    i = pl.program_id(0)

    def row_copy(u, k):
        d = dest_ref[k * SEQ + i * CB_TM + u]
        return pltpu.make_async_copy(ys_hbm.at[pl.ds(d, 1)], buf.at[k, pl.ds(u, 1)], sem)

    def issue(u, _):
        for k in range(TOP_K):
            row_copy(u, k).start()
        return 0

    def drain(u, _):
        for k in range(TOP_K):
            row_copy(u, k).wait()
        return 0

    lax.fori_loop(0, CB_TM, issue, 0)
    lax.fori_loop(0, CB_TM, drain, 0)
    gate = gate_ref[...]
    y = buf[0] * gate[:, 0:1]
    for k in range(1, TOP_K):
        y = y + buf[k] * gate[:, k:k + 1]
    o_ref[...] = x_ref[...] + g2_ref[...] * y


def _combine(dest_flat, x1, gates_t, mod, ys):
    return pl.pallas_call(
        _combine_kernel,
        out_shape=jax.ShapeDtypeStruct((SEQ, D_MODEL), F32),
        grid_spec=pltpu.PrefetchScalarGridSpec(
            num_scalar_prefetch=1,
            grid=(SEQ // CB_TM,),
            in_specs=[
                pl.BlockSpec((CB_TM, D_MODEL), lambda i, d: (i, 0)),
                pl.BlockSpec((CB_TM, TOP_K), lambda i, d: (i, 0)),
                pl.BlockSpec((1, D_MODEL), lambda i, d: (0, 5)),
                pl.BlockSpec(memory_space=pl.ANY),
            ],
            out_specs=pl.BlockSpec((CB_TM, D_MODEL), lambda i, d: (i, 0)),
            scratch_shapes=[
                pltpu.VMEM((TOP_K, CB_TM, D_MODEL), F32),
                pltpu.SemaphoreType.DMA,
            ],
        ),
        compiler_params=_params(("arbitrary",)),
        name="combine",
    )(dest_flat, x1, gates_t, mod, ys)


def _rope_tables():
    t = jnp.arange(SEQ, dtype=I32)
    row = (t // GRID_W).astype(F32)
    col = (t % GRID_W).astype(F32)
    inv_freq = ROPE_THETA ** (-jnp.arange(0, ROPE_AXIS_DIM, 2, dtype=F32) / ROPE_AXIS_DIM)
    ang_r = row[:, None] * inv_freq
    ang_c = col[:, None] * inv_freq
    cos_t = jnp.concatenate([jnp.cos(ang_r), jnp.cos(ang_r), jnp.cos(ang_c), jnp.cos(ang_c)], axis=-1)
    sin_t = jnp.concatenate([-jnp.sin(ang_r), jnp.sin(ang_r), -jnp.sin(ang_c), jnp.sin(ang_c)], axis=-1)
    return cos_t, sin_t


def kernel(x, c, w_mod, b_mod, norm1_g, w_in, q_norm_g, k_norm_g, w_pool, pool_scale, w_out, norm2_g,
           w_router, b_router, w1, b1, w2, b2):
    assert x.shape == (1, SEQ, D_MODEL) and w_mod.shape[0] == 1
    x2 = x[0]
    cos_t, sin_t = _rope_tables()

    mod = _mod(c.reshape(D_MODEL, 1), w_mod[0], b_mod)

    pool_in, qt, k, vt = _inproj(x2, mod, norm1_g, w_in[0].astype(BF16), cos_t, sin_t, q_norm_g, k_norm_g)
    attn = _attention(qt, k, vt)
    pool = _pool(pool_in, w_pool[0].astype(BF16), pool_scale)
    x1 = _outproj(attn, pool, w_out[0].astype(BF16), x2, mod)

    wr = w_router[0]
    wr_hi = wr.astype(BF16)
    wr_lo = (wr - wr_hi.astype(F32)).astype(BF16)
    pad = lambda a: jnp.pad(a, ((0, 0), (0, RT_PAD - N_EXPERTS)))
    wcat = jnp.concatenate([pad(wr_hi), pad(wr_lo)], axis=1)
    logits_t = _norm2(x1, norm2_g, mod, wcat, pad(wr_hi), pad(b_router))

    dest, gates, nblk = _route(logits_t)
    nblk = nblk[:, 0]
    end_blk = jnp.cumsum(nblk)
    n_used = end_blk[-1:].astype(I32)
    block_expert = jnp.minimum(jnp.sum(end_blk[None, :] <= jnp.arange(MOE_NB, dtype=I32)[:, None], axis=1),
                               N_EXPERTS - 1).astype(I32)
    lastblk = jnp.where(nblk > 0, end_blk - 1, -1).astype(I32)
    dest_flat = dest.reshape(TOP_K * SEQ)

    xs = _dispatch(dest_flat, lastblk, x1, norm2_g, mod)
    act = _ffn1(block_expert, n_used, xs, w1[0].astype(BF16), b1[0].reshape(N_EXPERTS, 1, 2 * D_FF))
    ys = _ffn2(block_expert, n_used, act, w2[0].astype(BF16), b2[0].reshape(N_EXPERTS, 1, D_MODEL))
    out = _combine(dest_flat, x1, gates.T, mod, ys)
    return out[None]
```

```python
import functools
import math

import jax
import jax.numpy as jnp
from jax import lax
from jax.experimental import pallas as pl
from jax.experimental.pallas import tpu as pltpu

F32 = jnp.float32
BF16 = jnp.bfloat16
I32 = jnp.int32

D_MODEL = 4096
SEQ = 8192
POOL_WIDTH = 2048
ATTN_WIDTH = 2048
HEAD_DIM = 128
N_HEADS = 16
N_KV_HEADS = 4
GROUP = N_HEADS // N_KV_HEADS
KV_WIDTH = N_KV_HEADS * HEAD_DIM
IN_WIDTH = POOL_WIDTH + ATTN_WIDTH + 2 * KV_WIDTH
POOL_WINDOWS = (2, 4, 8, 16)
POOL_GROUP_WIDTH = POOL_WIDTH // len(POOL_WINDOWS)
GRID_W = 64
ROPE_THETA = 10000.0
ROPE_AXIS_DIM = HEAD_DIM // 2
N_EXPERTS = 32
TOP_K = 4
D_FF = D_MODEL // 4
SWIGLU_ALPHA = 1.702
SWIGLU_LIMIT = 7.0
N_MOD = 6
EPS = 1e-6

LANES = 128
VMEM_LIMIT = 56 * 1024 * 1024

Q_SCALE = (HEAD_DIM ** -0.5) * math.log2(math.e)

MOE_TM = 256
MOE_NB = SEQ * TOP_K // MOE_TM + N_EXPERTS
MOE_ROWS = MOE_NB * MOE_TM


def _params(sem, vmem=VMEM_LIMIT):
    return pltpu.CompilerParams(dimension_semantics=sem, vmem_limit_bytes=vmem)


MOD_TN = 512
MOD_KC = 256


def _mod_kernel(c_ref, w_ref, b_ref, o_ref):
    def body(k, acc):
        r = pl.multiple_of(k * MOD_KC, MOD_KC)
        ck = c_ref[pl.ds(r, MOD_KC), :]
        ck = ck * jax.nn.sigmoid(ck)
        p = w_ref[pl.ds(r, MOD_KC), :] * ck
        return acc + p.reshape(MOD_KC // 8, 8, MOD_TN).sum(axis=0)

    acc = lax.fori_loop(0, D_MODEL // MOD_KC, body, jnp.zeros((8, MOD_TN), F32))
    o_ref[...] = acc.sum(axis=0, keepdims=True) + b_ref[...]


def _mod(c_col, w_mod, b_mod):
    n = N_MOD * D_MODEL
    return pl.pallas_call(
        _mod_kernel,
        out_shape=jax.ShapeDtypeStruct((1, n), F32),
        grid=(n // MOD_TN,),
        in_specs=[
            pl.BlockSpec((D_MODEL, 1), lambda j: (0, 0)),
            pl.BlockSpec((D_MODEL, MOD_TN), lambda j: (0, j)),
            pl.BlockSpec((1, MOD_TN), lambda j: (0, j)),
        ],
        out_specs=pl.BlockSpec((1, MOD_TN), lambda j: (0, j)),
        compiler_params=_params(("arbitrary",)),
        name="mod",
    )(c_col, w_mod, b_mod)


IP_TM = 512
IP_TN = 512
IP_NJ = IN_WIDTH // IP_TN
NORM_ROWS = 16
BF16_SUBLANES = 16
KA_W = 2 * HEAD_DIM
VA_H = HEAD_DIM + BF16_SUBLANES


def _prep_modulation(g_ref, sc_ref, sh_ref, a_scr, s_scr):
    a_scr[...] = jnp.broadcast_to(g_ref[...] * (1.0 + sc_ref[...]), a_scr.shape)
    s_scr[...] = jnp.broadcast_to(sh_ref[...], s_scr.shape)


def _row_rms(x_ref, rs_scr, n_rows):
    def body(r, _):
        r0 = pl.multiple_of(r * NORM_ROWS, NORM_ROWS)
        width = x_ref.shape[1]
        parts = []
        for c in range(width // LANES):
            xc = x_ref[pl.ds(r0, NORM_ROWS), c * LANES:(c + 1) * LANES]
            parts.append(xc * xc)
        while len(parts) > 1:
            parts = [parts[p] + parts[p + 1] for p in range(0, len(parts), 2)]
        rs_scr[pl.ds(r0, NORM_ROWS), :] = parts[0]
        return 0
    lax.fori_loop(0, n_rows // NORM_ROWS, body, 0)
    ms = jnp.sum(rs_scr[...], axis=-1, keepdims=True) * (1.0 / x_ref.shape[1])
    rs_scr[...] = jnp.broadcast_to(lax.rsqrt(ms + EPS), rs_scr.shape)


def _normed_tile(x_ref, rs_scr, a_scr, s_scr, r0, c):
    cs = slice(c * LANES, (c + 1) * LANES)
    return x_ref[pl.ds(r0, NORM_ROWS), cs] * rs_scr[pl.ds(r0, NORM_ROWS), :] * a_scr[:, cs] + s_scr[:, cs]


def _norm_rope_t(xt, g_col, cos_t, sin_t):
    ms = jnp.mean(xt * xt, axis=0, keepdims=True)
    y = xt * lax.rsqrt(ms + EPS) * g_col
    q = ROPE_AXIS_DIM // 2
    partner = jnp.concatenate([y[q:2 * q], y[0:q], y[3 * q:4 * q], y[2 * q:3 * q]], axis=0)
    return y * cos_t + partner * sin_t


def _inproj_kernel(x_ref, g_ref, sc_ref, sh_ref, w_ref, cos_ref, sin_ref, qg_ref, kg_ref,
                   pool_ref, qt_ref, k_ref, vt_ref, h_scr, rs_scr, a_scr, s_scr):
    j = pl.program_id(1)

    @pl.when(j == 0)
    def _():
        _prep_modulation(g_ref, sc_ref, sh_ref, a_scr, s_scr)
        _row_rms(x_ref, rs_scr, IP_TM)

        def body(r, _):
            r0 = pl.multiple_of(r * NORM_ROWS, NORM_ROWS)
            for c in range(D_MODEL // LANES):
                h_scr[pl.ds(r0, NORM_ROWS), c * LANES:(c + 1) * LANES] = _normed_tile(
                    x_ref, rs_scr, a_scr, s_scr, r0, c).astype(BF16)
            return 0
        lax.fori_loop(0, IP_TM // NORM_ROWS, body, 0)

    acc = jnp.dot(h_scr[...], w_ref[...], preferred_element_type=F32)

    @pl.when(j < 4)
    def _():
        pool_ref[...] = acc.astype(BF16)

    @pl.when((j >= 4) & (j < 8))
    def _():
        for hh in range(IP_TN // HEAD_DIM):
            sl = slice(hh * HEAD_DIM, (hh + 1) * HEAD_DIM)
            r = _norm_rope_t(acc[:, sl].T, qg_ref[...], cos_ref[...], sin_ref[...]) * Q_SCALE
            qt_ref[sl, :] = r.astype(BF16)

    @pl.when(j == 8)
    def _():
        lane = lax.broadcasted_iota(I32, (IP_TM, HEAD_DIM), 1)
        one_col = jnp.where(lane == 0, 1.0, 0.0).astype(BF16)
        for hh in range(N_KV_HEADS):
            sl = slice(hh * HEAD_DIM, (hh + 1) * HEAD_DIM)
            k_ref[:, hh * KA_W:hh * KA_W + HEAD_DIM] = _norm_rope_t(
                acc[:, sl].T, kg_ref[...], cos_ref[...], sin_ref[...]).T.astype(BF16)
            k_ref[:, hh * KA_W + HEAD_DIM:(hh + 1) * KA_W] = one_col

    @pl.when(j == 9)
    def _():
        for hh in range(N_KV_HEADS):
            sl = slice(hh * HEAD_DIM, (hh + 1) * HEAD_DIM)
            vt_ref[hh * VA_H:hh * VA_H + HEAD_DIM, :] = acc[:, sl].T.astype(BF16)
            vt_ref[hh * VA_H + HEAD_DIM:(hh + 1) * VA_H, :] = jnp.ones((VA_H - HEAD_DIM, IP_TM), BF16)


def _inproj(x2, mod, norm1_g, w_in_b, cos_t, sin_t, qg, kg):
    row = lambda n: pl.BlockSpec((1, D_MODEL), lambda i, j, n=n: (0, n))
    return pl.pallas_call(
        _inproj_kernel,
        out_shape=(
            jax.ShapeDtypeStruct((SEQ, POOL_WIDTH), BF16),
            jax.ShapeDtypeStruct((ATTN_WIDTH, SEQ), BF16),
            jax.ShapeDtypeStruct((SEQ, N_KV_HEADS * KA_W), BF16),
            jax.ShapeDtypeStruct((N_KV_HEADS * VA_H, SEQ), BF16),
        ),
        grid=(SEQ // IP_TM, IP_NJ),
        in_specs=[
            pl.BlockSpec((IP_TM, D_MODEL), lambda i, j: (i, 0)),
            pl.BlockSpec((1, D_MODEL), lambda i, j: (0, 0)),
            row(1), row(0),
            pl.BlockSpec((D_MODEL, IP_TN), lambda i, j: (0, j)),
            pl.BlockSpec((HEAD_DIM, IP_TM), lambda i, j: (0, i)),
            pl.BlockSpec((HEAD_DIM, IP_TM), lambda i, j: (0, i)),
            pl.BlockSpec((HEAD_DIM, 1), lambda i, j: (0, 0)),
            pl.BlockSpec((HEAD_DIM, 1), lambda i, j: (0, 0)),
        ],
        out_specs=(
            pl.BlockSpec((IP_TM, IP_TN), lambda i, j: (i, jnp.minimum(j, 3))),
            pl.BlockSpec((IP_TN, IP_TM), lambda i, j: (jnp.clip(j - 4, 0, 3), i)),
            pl.BlockSpec((IP_TM, N_KV_HEADS * KA_W), lambda i, j: (i, 0)),
            pl.BlockSpec((N_KV_HEADS * VA_H, IP_TM), lambda i, j: (0, i)),
        ),
        scratch_shapes=[
            pltpu.VMEM((IP_TM, D_MODEL), BF16),
            pltpu.VMEM((IP_TM, LANES), F32),
            pltpu.VMEM((NORM_ROWS, D_MODEL), F32),
            pltpu.VMEM((NORM_ROWS, D_MODEL), F32),
        ],
        compiler_params=_params(("arbitrary", "arbitrary")),
        name="inproj",
    )(x2, norm1_g, mod, mod, w_in_b, cos_t, sin_t, qg, kg)


AT_TQ = 512
AT_TK = 1024
AT_TK_ONLINE = 512
SHIFT_LIMIT = 60.0


def _attn_kernel(qt_ref, k_ref, vt_ref, o_ref, qa_scr, p_scr, kmax_scr):
    h = pl.program_id(0)
    i = pl.program_id(1)

    @pl.when((i == 0) & (h % GROUP == 0))
    def _():
        def body(c, mx):
            c0 = pl.multiple_of(c * AT_TK, AT_TK)
            kc = k_ref[pl.ds(c0, AT_TK), :HEAD_DIM].astype(F32)
            n2 = (kc * kc).sum(axis=1, keepdims=True)
            return jnp.maximum(mx, n2.max(axis=0, keepdims=True))
        mx = lax.fori_loop(0, SEQ // AT_TK, body, jnp.zeros((1, 1), F32))
        kmax_scr[...] = jnp.broadcast_to(jnp.sqrt(mx), kmax_scr.shape)

    q = qt_ref[...].astype(F32)
    bound = jnp.sqrt((q * q).sum(axis=0, keepdims=True)) * kmax_scr[0:1, 0:1] * 1.01
    fast = jnp.max(bound) <= SHIFT_LIMIT

    @pl.when(fast)
    def _():
        qa_scr[0:HEAD_DIM, :] = qt_ref[...]
        row = lax.broadcasted_iota(I32, (KA_W - HEAD_DIM, AT_TQ), 0)
        qa_scr[HEAD_DIM:, :] = jnp.where(row == 0, -bound, 0.0).astype(BF16)

        def body(c, _):
            c0 = pl.multiple_of(c * AT_TK, AT_TK)
            s = jnp.dot(k_ref[pl.ds(c0, AT_TK), :], qa_scr[...], preferred_element_type=F32)
            p_scr[pl.ds(c0, AT_TK), :] = jnp.exp2(s).astype(BF16)
            return 0

        lax.fori_loop(0, SEQ // AT_TK, body, 0)
        o = jnp.dot(vt_ref[...], p_scr[...], preferred_element_type=F32)
        o_ref[...] = (o[:HEAD_DIM] * (1.0 / o[HEAD_DIM:HEAD_DIM + 1])).T.astype(BF16)

    @pl.when(jnp.logical_not(fast))
    def _():
        qt = qt_ref[...]

        def chunk(c, carry):
            m, l, acc = carry
            c0 = pl.multiple_of(c * AT_TK_ONLINE, AT_TK_ONLINE)
            s = jnp.dot(k_ref[pl.ds(c0, AT_TK_ONLINE), :HEAD_DIM], qt, preferred_element_type=F32)
            m_new = jnp.maximum(m, s.max(axis=0, keepdims=True))
            alpha = jnp.exp2(m - m_new)
            p = jnp.exp2(s - m_new)
            l = alpha * l + p.sum(axis=0, keepdims=True)
            pv = jnp.dot(vt_ref[:HEAD_DIM, pl.ds(c0, AT_TK_ONLINE)], p.astype(BF16), preferred_element_type=F32)
            return m_new, l, alpha * acc + pv

        init = (jnp.full((1, AT_TQ), -jnp.inf, F32), jnp.zeros((1, AT_TQ), F32),
                jnp.zeros((HEAD_DIM, AT_TQ), F32))
        _, l, acc = lax.fori_loop(0, SEQ // AT_TK_ONLINE, chunk, init)
        o_ref[...] = (acc * (1.0 / l)).T.astype(BF16)


def _attention(qt, k, vt):
    return pl.pallas_call(
        _attn_kernel,
        out_shape=jax.ShapeDtypeStruct((SEQ, ATTN_WIDTH), BF16),
        grid=(N_HEADS, SEQ // AT_TQ),
        in_specs=[
            pl.BlockSpec((HEAD_DIM, AT_TQ), lambda h, i: (h, i)),
            pl.BlockSpec((SEQ, KA_W), lambda h, i: (0, h // GROUP)),
            pl.BlockSpec((VA_H, SEQ), lambda h, i: (h // GROUP, 0)),
        ],
        out_specs=pl.BlockSpec((AT_TQ, HEAD_DIM), lambda h, i: (i, h)),
        scratch_shapes=[
            pltpu.VMEM((KA_W, AT_TQ), BF16),
            pltpu.VMEM((SEQ, AT_TQ), BF16),
            pltpu.VMEM((8, LANES), F32),
        ],
        compiler_params=_params(("arbitrary", "arbitrary")),
        name="attn",
    )(qt, k, vt)


PL_TM = 256
PL_HALO = 16


def _pool_kernel(prev_ref, main_ref, next_ref, wp_ref, scale_ref, o_ref, buf):
    i = pl.program_id(0)
    last = pl.num_programs(0) - 1
    buf[0:PL_HALO, :] = jnp.where(i == 0, 0.0, prev_ref[...].astype(F32))
    buf[PL_HALO:PL_HALO + PL_TM, :] = main_ref[...].astype(F32)
    buf[PL_HALO + PL_TM:, :] = jnp.where(i == last, 0.0, next_ref[...].astype(F32))
    t = i * PL_TM + lax.broadcasted_iota(I32, (PL_TM, 1), 0)
    for gi, w in enumerate(POOL_WINDOWS):
        cols = slice(gi * POOL_GROUP_WIDTH, (gi + 1) * POOL_GROUP_WIDTH)
        win = buf[PL_HALO - w // 2:PL_HALO - w // 2 + PL_TM, cols]
        for d in range(-w // 2 + 1, w // 2):
            win = win + buf[PL_HALO + d:PL_HALO + d + PL_TM, cols]
        lo = jnp.maximum(t - w // 2, 0)
        hi = jnp.minimum(t + w // 2 - 1, SEQ - 1)
        cnt = (hi - lo + 1).astype(F32)
        dlt = win / cnt - buf[PL_HALO:PL_HALO + PL_TM, cols]
        y = jnp.dot(dlt.astype(BF16), wp_ref[gi], preferred_element_type=F32)
        o_ref[:, cols] = (y * scale_ref[:, cols]).astype(BF16)


def _pool(pool_in, w_pool_b, pool_scale):
    nh = PL_TM // PL_HALO
    n_halo_blocks = SEQ // PL_HALO
    return pl.pallas_call(
        _pool_kernel,
        out_shape=jax.ShapeDtypeStruct((SEQ, POOL_WIDTH), BF16),
        grid=(SEQ // PL_TM,),
        in_specs=[
            pl.BlockSpec((PL_HALO, POOL_WIDTH), lambda i: (jnp.maximum(i * nh - 1, 0), 0)),
            pl.BlockSpec((PL_TM, POOL_WIDTH), lambda i: (i, 0)),
            pl.BlockSpec((PL_HALO, POOL_WIDTH), lambda i: (jnp.minimum((i + 1) * nh, n_halo_blocks - 1), 0)),
            pl.BlockSpec((len(POOL_WINDOWS), POOL_GROUP_WIDTH, POOL_GROUP_WIDTH), lambda i: (0, 0, 0)),
            pl.BlockSpec((1, POOL_WIDTH), lambda i: (0, 0)),
        ],
        out_specs=pl.BlockSpec((PL_TM, POOL_WIDTH), lambda i: (i, 0)),
        scratch_shapes=[pltpu.VMEM((PL_TM + 2 * PL_HALO, POOL_WIDTH), F32)],
        compiler_params=_params(("arbitrary",)),
        name="pool",
    )(pool_in, pool_in, pool_in, w_pool_b, pool_scale)


OP_TM = 1024
OP_TN = 512


def _outproj_kernel(a_ref, p_ref, wa_ref, wp_ref, x_ref, g_ref, o_ref):
    acc = jnp.dot(a_ref[...], wa_ref[...], preferred_element_type=F32)
    acc = acc + jnp.dot(p_ref[...], wp_ref[...], preferred_element_type=F32)
    o_ref[...] = x_ref[...] + g_ref[...] * acc


def _outproj(attn, pool, w_out_b, x2, mod):
    return pl.pallas_call(
        _outproj_kernel,
        out_shape=jax.ShapeDtypeStruct((SEQ, D_MODEL), F32),
        grid=(SEQ // OP_TM, D_MODEL // OP_TN),
        in_specs=[
            pl.BlockSpec((OP_TM, ATTN_WIDTH), lambda i, j: (i, 0)),
            pl.BlockSpec((OP_TM, POOL_WIDTH), lambda i, j: (i, 0)),
            pl.BlockSpec((ATTN_WIDTH, OP_TN), lambda i, j: (0, j)),
            pl.BlockSpec((POOL_WIDTH, OP_TN), lambda i, j: (1, j)),
            pl.BlockSpec((OP_TM, OP_TN), lambda i, j: (i, j)),
            pl.BlockSpec((1, OP_TN), lambda i, j: (0, 2 * (D_MODEL // OP_TN) + j)),
        ],
        out_specs=pl.BlockSpec((OP_TM, OP_TN), lambda i, j: (i, j)),
        compiler_params=_params(("arbitrary", "arbitrary")),
        name="outproj",
    )(attn, pool, w_out_b, w_out_b, x2, mod)


N2_TM = 256
RT_PAD = LANES


def _norm2_kernel(x_ref, g_ref, sc_ref, sh_ref, wcat_ref, whi_ref, b_ref, lt_ref, hi_scr, lo_scr,
                  rs_scr, a_scr, s_scr):
    @pl.when(pl.program_id(0) == 0)
    def _():
        _prep_modulation(g_ref, sc_ref, sh_ref, a_scr, s_scr)

    _row_rms(x_ref, rs_scr, N2_TM)

    def body(r, _):
        r0 = pl.multiple_of(r * NORM_ROWS, NORM_ROWS)
        for c in range(D_MODEL // LANES):
            cs = slice(c * LANES, (c + 1) * LANES)
            h = _normed_tile(x_ref, rs_scr, a_scr, s_scr, r0, c)
            hi = h.astype(BF16)
            hi_scr[pl.ds(r0, NORM_ROWS), cs] = hi
            lo_scr[pl.ds(r0, NORM_ROWS), cs] = (h - hi.astype(F32)).astype(BF16)
        return 0
    lax.fori_loop(0, N2_TM // NORM_ROWS, body, 0)
    a = jnp.dot(hi_scr[...], wcat_ref[...], preferred_element_type=F32)
    b = jnp.dot(lo_scr[...], whi_ref[...], preferred_element_type=F32)
    logits = a[:, :RT_PAD] + a[:, RT_PAD:] + b + b_ref[...]
    lt_ref[...] = logits.T[:N_EXPERTS, :]


def _norm2(x1, norm2_g, mod, wcat, whi, b_pad):
    row = lambda n: pl.BlockSpec((1, D_MODEL), lambda i, n=n: (0, n))
    return pl.pallas_call(
        _norm2_kernel,
        out_shape=jax.ShapeDtypeStruct((N_EXPERTS, SEQ), F32),
        grid=(SEQ // N2_TM,),
        in_specs=[
            pl.BlockSpec((N2_TM, D_MODEL), lambda i: (i, 0)),
            pl.BlockSpec((1, D_MODEL), lambda i: (0, 0)),
            row(4), row(3),
            pl.BlockSpec((D_MODEL, 2 * RT_PAD), lambda i: (0, 0)),
            pl.BlockSpec((D_MODEL, RT_PAD), lambda i: (0, 0)),
            pl.BlockSpec((1, RT_PAD), lambda i: (0, 0)),
        ],
        out_specs=pl.BlockSpec((N_EXPERTS, N2_TM), lambda i: (0, i)),
        scratch_shapes=[
            pltpu.VMEM((N2_TM, D_MODEL), BF16),
            pltpu.VMEM((N2_TM, D_MODEL), BF16),
            pltpu.VMEM((N2_TM, LANES), F32),
            pltpu.VMEM((NORM_ROWS, D_MODEL), F32),
            pltpu.VMEM((NORM_ROWS, D_MODEL), F32),
        ],
        compiler_params=_params(("arbitrary",)),
        name="norm2",
    )(x1, norm2_g, mod, mod, wcat, whi, b_pad)


RT_CH = 1024
RT_SB = 256


def _route_kernel(lt_ref, dest_ref, gate_ref, nblk_ref, idx_scr, rank_scr):
    e_col = lax.broadcasted_iota(I32, (N_EXPERTS, RT_CH), 0).astype(F32)
    tri = (lax.broadcasted_iota(I32, (RT_SB, RT_SB), 0) < lax.broadcasted_iota(I32, (RT_SB, RT_SB), 1)).astype(BF16)
    carry = jnp.zeros((N_EXPERTS, 1), F32)
    for c in range(SEQ // RT_CH):
        cs = slice(c * RT_CH, (c + 1) * RT_CH)
        work = lt_ref[:, cs]
        vals = []
        mask = jnp.zeros((N_EXPERTS, RT_CH), F32)
        for k in range(TOP_K):
            m = work.max(axis=0, keepdims=True)
            idx = jnp.where(work == m, e_col, float(N_EXPERTS)).min(axis=0, keepdims=True)
            sel = e_col == idx
            vals.append(m)
            idx_scr[k:k + 1, cs] = idx
            mask = jnp.where(sel, 1.0, mask)
            work = jnp.where(sel, -jnp.inf, work)
        ex = [jnp.exp(v - vals[0]) for v in vals]
        den = ex[0] + ex[1] + ex[2] + ex[3]
        for k in range(TOP_K):
            gate_ref[k:k + 1, cs] = ex[k] / den
        for b in range(RT_CH // RT_SB):
            blk = mask[:, b * RT_SB:(b + 1) * RT_SB]
            pref = jnp.dot(blk.astype(BF16), tri, preferred_element_type=F32)
            rank_scr[:, c * RT_CH + b * RT_SB:c * RT_CH + (b + 1) * RT_SB] = pref + carry
            carry = carry + blk.sum(axis=1, keepdims=True)
    nblk = jnp.floor((carry + (MOE_TM - 1)) * (1.0 / MOE_TM))
    nblk_b = jnp.broadcast_to(nblk, (N_EXPERTS, LANES))
    lower = (lax.broadcasted_iota(I32, (N_EXPERTS, N_EXPERTS), 1) < lax.broadcasted_iota(I32, (N_EXPERTS, N_EXPERTS), 0)).astype(BF16)
    start_blk = jnp.dot(lower, nblk_b.astype(BF16), preferred_element_type=F32)
    nblk_ref[...] = nblk_b.astype(I32)
    start = start_blk[:, 0:1] * float(MOE_TM)
    for c in range(SEQ // RT_CH):
        cs = slice(c * RT_CH, (c + 1) * RT_CH)
        slot = rank_scr[:, cs] + start
        for k in range(TOP_K):
            sel = e_col == idx_scr[k:k + 1, cs]
            dest_ref[k:k + 1, cs] = jnp.where(sel, slot, 0.0).sum(axis=0, keepdims=True).astype(I32)


def _route(logits_t):
    return pl.pallas_call(
        _route_kernel,
        out_shape=(
            jax.ShapeDtypeStruct((TOP_K, SEQ), I32),
            jax.ShapeDtypeStruct((TOP_K, SEQ), F32),
            jax.ShapeDtypeStruct((N_EXPERTS, LANES), I32),
        ),
        scratch_shapes=[pltpu.VMEM((8, SEQ), F32), pltpu.VMEM((N_EXPERTS, SEQ), F32)],
        compiler_params=pltpu.CompilerParams(vmem_limit_bytes=VMEM_LIMIT),
        name="route",
    )(logits_t)


DP_TM = 256
XS_W = D_MODEL // 2
U32 = jnp.uint32


def _dispatch_kernel(dest_ref, lastblk_ref, x_ref, g_ref, sc_ref, sh_ref, xs_hbm, pk, zero_buf, rs_scr, a_scr,
                     s_scr, zsem, sem):
    i = pl.program_id(0)
    par = i % 2

    @pl.when(i == 0)
    def _():
        _prep_modulation(g_ref, sc_ref, sh_ref, a_scr, s_scr)
        zero_buf[...] = jnp.zeros_like(zero_buf)

        def zcopy(e):
            b = jnp.maximum(lastblk_ref[e], 0)
            return pltpu.make_async_copy(zero_buf, xs_hbm.at[pl.ds(pl.multiple_of(b * MOE_TM, MOE_TM), MOE_TM)], zsem)

        def zstart(e, _):
            @pl.when(lastblk_ref[e] >= 0)
            def _():
                zcopy(e).start()
            return 0

        def zwait(e, _):
            @pl.when(lastblk_ref[e] >= 0)
            def _():
                zcopy(e).wait()
            return 0

        lax.fori_loop(0, N_EXPERTS, zstart, 0)
        lax.fori_loop(0, N_EXPERTS, zwait, 0)

    _row_rms(x_ref, rs_scr, DP_TM)

    def norm_body(r, _):
        r0 = pl.multiple_of(r * NORM_ROWS, NORM_ROWS)
        for c in range(XS_W // LANES):
            lo = _normed_tile(x_ref, rs_scr, a_scr, s_scr, r0, c)
            hi = _normed_tile(x_ref, rs_scr, a_scr, s_scr, r0, c + XS_W // LANES)
            lo = lax.bitcast_convert_type(lo.astype(BF16).astype(F32), U32)
            hi = lax.bitcast_convert_type(hi.astype(BF16).astype(F32), U32)
            pk[par, pl.ds(r0, NORM_ROWS), c * LANES:(c + 1) * LANES] = (lo >> 16) | (hi & jnp.uint32(0xFFFF0000))
        return 0

    lax.fori_loop(0, DP_TM // NORM_ROWS, norm_body, 0)

    def issue(u, _):
        for k in range(TOP_K):
            d = dest_ref[k * SEQ + i * DP_TM + u]
            pltpu.make_async_copy(pk.at[par, pl.ds(u, 1)], xs_hbm.at[pl.ds(d, 1)], sem.at[par]).start()
        return 0

    lax.fori_loop(0, DP_TM, issue, 0)

    def drain(p):
        for _ in range(TOP_K):
            pltpu.make_async_copy(pk.at[p], xs_hbm.at[pl.ds(0, DP_TM)], sem.at[p]).wait()

    @pl.when(i > 0)
    def _():
        drain(1 - par)

    @pl.when(i == pl.num_programs(0) - 1)
    def _():
        drain(par)


def _dispatch(dest_flat, lastblk, x1, norm2_g, mod):
    row = lambda n: pl.BlockSpec((1, D_MODEL), lambda i, d, lb, n=n: (0, n))
    return pl.pallas_call(
        _dispatch_kernel,
        out_shape=jax.ShapeDtypeStruct((MOE_ROWS, XS_W), U32),
        grid_spec=pltpu.PrefetchScalarGridSpec(
            num_scalar_prefetch=2,
            grid=(SEQ // DP_TM,),
            in_specs=[
                pl.BlockSpec((DP_TM, D_MODEL), lambda i, d, lb: (i, 0)),
                pl.BlockSpec((1, D_MODEL), lambda i, d, lb: (0, 0)),
                row(4), row(3),
            ],
            out_specs=pl.BlockSpec(memory_space=pl.ANY),
            scratch_shapes=[
                pltpu.VMEM((2, DP_TM, XS_W), U32),
                pltpu.VMEM((MOE_TM, XS_W), U32),
                pltpu.VMEM((DP_TM, LANES), F32),
                pltpu.VMEM((NORM_ROWS, D_MODEL), F32),
                pltpu.VMEM((NORM_ROWS, D_MODEL), F32),
                pltpu.SemaphoreType.DMA,
                pltpu.SemaphoreType.DMA((2,)),
            ],
        ),
        compiler_params=_params(("arbitrary",)),
        name="dispatch",
    )(dest_flat, lastblk, x1, norm2_g, mod, mod)


F1_TF = 512
F2_TN = 2048


CAST_ROWS = 128


def _expert_schedule(nblk):
    end_blk = jnp.cumsum(nblk)
    n_used = end_blk[-1]
    blocks = jnp.arange(MOE_NB, dtype=I32)
    be = jnp.minimum(jnp.sum(end_blk[None, :] <= blocks[:, None], axis=1), N_EXPERTS - 1).astype(I32)
    first = (blocks < n_used) & (be != jnp.concatenate([jnp.full((1,), -1, I32), be[:-1]]))
    seg = jnp.cumsum(first.astype(I32)) - 1
    seg_end = end_blk[be]
    nxt = jnp.where(seg_end < n_used, be[jnp.minimum(seg_end, MOE_NB - 1)], -1)
    meta = jnp.stack([n_used, jnp.sum(first.astype(I32))]).astype(I32)
    lastblk = jnp.where(nblk > 0, end_blk - 1, -1).astype(I32)
    return meta, be, first.astype(I32), seg.astype(I32), nxt.astype(I32), lastblk


def _stream_expert_weights(j, i, nj, meta_ref, be_ref, first_ref, seg_ref, nxt_ref, tile_copies, stg, wbuf):
    @pl.when(first_ref[i] == 1)
    def _():
        seq = j * meta_ref[1] + seg_ref[i]
        slot = seq % 2

        @pl.when(seq == 0)
        def _():
            for cp in tile_copies(be_ref[i], j, slot):
                cp.start()

        for cp in tile_copies(be_ref[i], j, slot):
            cp.wait()

        def cast(r, _):
            r0 = pl.multiple_of(r * CAST_ROWS, CAST_ROWS)
            wbuf[pl.ds(r0, CAST_ROWS), :] = stg[slot, pl.ds(r0, CAST_ROWS), :].astype(BF16)
            return 0

        lax.fori_loop(0, wbuf.shape[0] // CAST_ROWS, cast, 0)

        nxt = nxt_ref[i]

        @pl.when(nxt >= 0)
        def _():
            for cp in tile_copies(nxt, j, 1 - slot):
                cp.start()

        @pl.when((nxt < 0) & (j + 1 < nj))
        def _():
            for cp in tile_copies(be_ref[0], j + 1, 1 - slot):
                cp.start()


def _ffn1_kernel(meta_ref, be_ref, first_ref, seg_ref, nxt_ref, x_ref, bg_ref, bl_ref, w1_hbm, o_ref,
                 stg, wbuf, sem):
    j = pl.program_id(0)
    i = pl.program_id(1)
    nj = pl.num_programs(0)

    def tile_copies(e, jj, slot):
        cg = pl.multiple_of(jj * F1_TF, F1_TF)
        cl = pl.multiple_of(D_FF + jj * F1_TF, F1_TF)
        return (
            pltpu.make_async_copy(w1_hbm.at[e, :, pl.ds(cg, F1_TF)], stg.at[slot, :, 0:F1_TF], sem.at[slot]),
            pltpu.make_async_copy(w1_hbm.at[e, :, pl.ds(cl, F1_TF)], stg.at[slot, :, F1_TF:2 * F1_TF], sem.at[slot]),
        )

    @pl.when(i < meta_ref[0])
    def _():
        _stream_expert_weights(j, i, nj, meta_ref, be_ref, first_ref, seg_ref, nxt_ref, tile_copies, stg, wbuf)
        xp = x_ref[...]
        x_lo = lax.bitcast_convert_type(xp << 16, F32).astype(BF16)
        x_hi = lax.bitcast_convert_type(xp & jnp.uint32(0xFFFF0000), F32).astype(BF16)
        y = jnp.dot(x_lo, wbuf[:XS_W, :], preferred_element_type=F32)
        y = y + jnp.dot(x_hi, wbuf[XS_W:, :], preferred_element_type=F32)
        glu = jnp.minimum(y[:, :F1_TF] + bg_ref[...], SWIGLU_LIMIT)
        lin = jnp.clip(y[:, F1_TF:] + bl_ref[...], -SWIGLU_LIMIT, SWIGLU_LIMIT)
        o_ref[...] = (glu * jax.nn.sigmoid(SWIGLU_ALPHA * glu) * (lin + 1.0)).astype(BF16)


def _ffn1(sched, xs, w1, b1_3):
    meta, be, first, seg, nxt, _ = sched
    nj = D_FF // F1_TF
    blk = lambda i, meta: jnp.minimum(i, meta[0] - 1)
    return pl.pallas_call(
        _ffn1_kernel,
        out_shape=jax.ShapeDtypeStruct((MOE_ROWS, D_FF), BF16),
        grid_spec=pltpu.PrefetchScalarGridSpec(
            num_scalar_prefetch=5,
            grid=(nj, MOE_NB),
            in_specs=[
                pl.BlockSpec((MOE_TM, XS_W), lambda j, i, meta, be, *_: (blk(i, meta), 0)),
                pl.BlockSpec((None, 1, F1_TF), lambda j, i, meta, be, *_: (be[blk(i, meta)], 0, j)),
                pl.BlockSpec((None, 1, F1_TF), lambda j, i, meta, be, *_: (be[blk(i, meta)], 0, nj + j)),
                pl.BlockSpec(memory_space=pl.ANY),
            ],
            out_specs=pl.BlockSpec((MOE_TM, F1_TF), lambda j, i, meta, be, *_: (blk(i, meta), j)),
            scratch_shapes=[
                pltpu.VMEM((2, D_MODEL, 2 * F1_TF), F32),
                pltpu.VMEM((D_MODEL, 2 * F1_TF), BF16),
                pltpu.SemaphoreType.DMA((2,)),
            ],
        ),
        compiler_params=_params(("arbitrary", "arbitrary")),
        name="ffn1",
    )(meta, be, first, seg, nxt, xs, b1_3, b1_3, w1)


def _ffn2_kernel(meta_ref, be_ref, first_ref, seg_ref, nxt_ref, a_ref, b_ref, w2_hbm, o_ref, stg, wbuf, sem):
    j = pl.program_id(0)
    i = pl.program_id(1)
    nj = pl.num_programs(0)

    def tile_copies(e, jj, slot):
        c0 = pl.multiple_of(jj * F2_TN, F2_TN)
        return (pltpu.make_async_copy(w2_hbm.at[e, :, pl.ds(c0, F2_TN)], stg.at[slot], sem.at[slot]),)

    @pl.when(i < meta_ref[0])
    def _():
        _stream_expert_weights(j, i, nj, meta_ref, be_ref, first_ref, seg_ref, nxt_ref, tile_copies, stg, wbuf)
        o_ref[...] = jnp.dot(a_ref[...], wbuf[...], preferred_element_type=F32) + b_ref[...]


def _ffn2(sched, act, w2, b2_3):
    meta, be, first, seg, nxt, _ = sched
    blk = lambda i, meta: jnp.minimum(i, meta[0] - 1)
    return pl.pallas_call(
        _ffn2_kernel,
        out_shape=jax.ShapeDtypeStruct((MOE_ROWS, D_MODEL), F32),
        grid_spec=pltpu.PrefetchScalarGridSpec(
            num_scalar_prefetch=5,
            grid=(D_MODEL // F2_TN, MOE_NB),
            in_specs=[
                pl.BlockSpec((MOE_TM, D_FF), lambda j, i, meta, be, *_: (blk(i, meta), 0)),
                pl.BlockSpec((None, 1, F2_TN), lambda j, i, meta, be, *_: (be[blk(i, meta)], 0, j)),
                pl.BlockSpec(memory_space=pl.ANY),
            ],
            out_specs=pl.BlockSpec((MOE_TM, F2_TN), lambda j, i, meta, be, *_: (blk(i, meta), j)),
            scratch_shapes=[
                pltpu.VMEM((2, D_FF, F2_TN), F32),
                pltpu.VMEM((D_FF, F2_TN), BF16),
                pltpu.SemaphoreType.DMA((2,)),
            ],
        ),
        compiler_params=_params(("arbitrary", "arbitrary")),
        name="ffn2",
    )(meta, be, first, seg, nxt, act, b2_3, w2)


CB_TM = 128


def _combine_kernel(dest_ref, x_ref, gate_ref, g2_ref, ys_hbm, o_ref, buf, sem):
    i = pl.program_id(0)
    par = i % 2

    def issue(tile, p):
        def body(u, _):
            for k in range(TOP_K):
                d = dest_ref[k * SEQ + tile * CB_TM + u]
                pltpu.make_async_copy(ys_hbm.at[pl.ds(d, 1)], buf.at[p, k, pl.ds(u, 1)], sem.at[p]).start()
            return 0
        lax.fori_loop(0, CB_TM, body, 0)

    @pl.when(i == 0)
    def _():
        issue(0, 0)

    @pl.when(i + 1 < pl.num_programs(0))
    def _():
        issue(i + 1, 1 - par)

    for k in range(TOP_K):
        pltpu.make_async_copy(ys_hbm.at[pl.ds(0, CB_TM)], buf.at[par, k], sem.at[par]).wait()

    def body(r, _):
        r0 = pl.multiple_of(r * 8, 8)
        rows = pl.ds(r0, 8)
        y = buf[par, 0, rows, :] * gate_ref[rows, 0:1]
        for k in range(1, TOP_K):
            y = y + buf[par, k, rows, :] * gate_ref[rows, k:k + 1]
        o_ref[rows, :] = x_ref[rows, :] + g2_ref[...] * y
        return 0

    lax.fori_loop(0, CB_TM // 8, body, 0)


def _combine(dest_flat, x1, gates_t, mod, ys):
    return pl.pallas_call(
        _combine_kernel,
        out_shape=jax.ShapeDtypeStruct((SEQ, D_MODEL), F32),
        grid_spec=pltpu.PrefetchScalarGridSpec(
            num_scalar_prefetch=1,
            grid=(SEQ // CB_TM,),
            in_specs=[
                pl.BlockSpec((CB_TM, D_MODEL), lambda i, d: (i, 0)),
                pl.BlockSpec((CB_TM, TOP_K), lambda i, d: (i, 0)),
                pl.BlockSpec((1, D_MODEL), lambda i, d: (0, 5)),
                pl.BlockSpec(memory_space=pl.ANY),
            ],
            out_specs=pl.BlockSpec((CB_TM, D_MODEL), lambda i, d: (i, 0)),
            scratch_shapes=[
                pltpu.VMEM((2, TOP_K, CB_TM, D_MODEL), F32),
                pltpu.SemaphoreType.DMA((2,)),
            ],
        ),
        compiler_params=_params(("arbitrary",)),
        name="combine",
    )(dest_flat, x1, gates_t, mod, ys)


def _rope_tables():
    t = jnp.arange(SEQ, dtype=I32)
    row = (t // GRID_W).astype(F32)
    col = (t % GRID_W).astype(F32)
    inv_freq = ROPE_THETA ** (-jnp.arange(0, ROPE_AXIS_DIM, 2, dtype=F32) / ROPE_AXIS_DIM)
    ang_r = inv_freq[:, None] * row[None, :]
    ang_c = inv_freq[:, None] * col[None, :]
    cos_t = jnp.concatenate([jnp.cos(ang_r), jnp.cos(ang_r), jnp.cos(ang_c), jnp.cos(ang_c)], axis=0)
    sin_t = jnp.concatenate([-jnp.sin(ang_r), jnp.sin(ang_r), -jnp.sin(ang_c), jnp.sin(ang_c)], axis=0)
    return cos_t, sin_t


def kernel(x, c, w_mod, b_mod, norm1_g, w_in, q_norm_g, k_norm_g, w_pool, pool_scale, w_out, norm2_g,
           w_router, b_router, w1, b1, w2, b2):
    assert x.shape == (1, SEQ, D_MODEL) and w_mod.shape[0] == 1
    x2 = x[0]
    cos_t, sin_t = _rope_tables()

    mod = _mod(c.reshape(D_MODEL, 1), w_mod[0], b_mod)

    pool_in, qt, k, vt = _inproj(x2, mod, norm1_g, w_in[0].astype(BF16), cos_t, sin_t,
                                 q_norm_g.reshape(HEAD_DIM, 1), k_norm_g.reshape(HEAD_DIM, 1))
    attn = _attention(qt, k, vt)
    pool = _pool(pool_in, w_pool[0].astype(BF16), pool_scale)
    x1 = _outproj(attn, pool, w_out[0].astype(BF16), x2, mod)

    wr = w_router[0]
    wr_hi = wr.astype(BF16)
    wr_lo = (wr - wr_hi.astype(F32)).astype(BF16)
    pad = lambda a: jnp.pad(a, ((0, 0), (0, RT_PAD - N_EXPERTS)))
    wcat = jnp.concatenate([pad(wr_hi), pad(wr_lo)], axis=1)
    logits_t = _norm2(x1, norm2_g, mod, wcat, pad(wr_hi), pad(b_router))

    dest, gates, nblk = _route(logits_t)
    sched = _expert_schedule(nblk[:, 0])
    dest_flat = dest.reshape(TOP_K * SEQ)

    xs = _dispatch(dest_flat, sched[-1], x1, norm2_g, mod)
    act = _ffn1(sched, xs, w1[0], b1[0].reshape(N_EXPERTS, 1, 2 * D_FF))
    ys = _ffn2(sched, act, w2[0], b2[0].reshape(N_EXPERTS, 1, D_MODEL))
    out = _combine(dest_flat, x1, gates.T, mod, ys)
    return out[None]
```

```python
import functools
import math

import jax
import jax.numpy as jnp
from jax import lax
from jax.experimental import pallas as pl
from jax.experimental.pallas import tpu as pltpu

F32 = jnp.float32
BF16 = jnp.bfloat16
I32 = jnp.int32

D_MODEL = 4096
SEQ = 8192
POOL_WIDTH = 2048
ATTN_WIDTH = 2048
HEAD_DIM = 128
N_HEADS = 16
N_KV_HEADS = 4
GROUP = N_HEADS // N_KV_HEADS
KV_WIDTH = N_KV_HEADS * HEAD_DIM
IN_WIDTH = POOL_WIDTH + ATTN_WIDTH + 2 * KV_WIDTH
POOL_WINDOWS = (2, 4, 8, 16)
POOL_GROUP_WIDTH = POOL_WIDTH // len(POOL_WINDOWS)
GRID_W = 64
ROPE_THETA = 10000.0
ROPE_AXIS_DIM = HEAD_DIM // 2
N_EXPERTS = 32
TOP_K = 4
D_FF = D_MODEL // 4
SWIGLU_ALPHA = 1.702
SWIGLU_LIMIT = 7.0
N_MOD = 6
EPS = 1e-6

LANES = 128
VMEM_LIMIT = 56 * 1024 * 1024

Q_SCALE = (HEAD_DIM ** -0.5) * math.log2(math.e)

MOE_TM = 256
MOE_NB = SEQ * TOP_K // MOE_TM + N_EXPERTS
MOE_ROWS = MOE_NB * MOE_TM


def _params(sem, vmem=VMEM_LIMIT):
    return pltpu.CompilerParams(dimension_semantics=sem, vmem_limit_bytes=vmem)


MOD_TN = 1024
MOD_KC = 256


def _mod_kernel(c_ref, w_ref, b_ref, o_ref):
    def body(k, acc):
        r = pl.multiple_of(k * MOD_KC, MOD_KC)
        ck = c_ref[pl.ds(r, MOD_KC), :]
        ck = ck * jax.nn.sigmoid(ck)
        p = w_ref[pl.ds(r, MOD_KC), :] * ck
        return acc + p.reshape(MOD_KC // 8, 8, MOD_TN).sum(axis=0)

    acc = lax.fori_loop(0, D_MODEL // MOD_KC, body, jnp.zeros((8, MOD_TN), F32))
    o_ref[...] = acc.sum(axis=0, keepdims=True) + b_ref[...]


def _mod(c_col, w_mod, b_mod):
    n = N_MOD * D_MODEL
    return pl.pallas_call(
        _mod_kernel,
        out_shape=jax.ShapeDtypeStruct((1, n), F32),
        grid=(n // MOD_TN,),
        in_specs=[
            pl.BlockSpec((D_MODEL, 1), lambda j: (0, 0)),
            pl.BlockSpec((D_MODEL, MOD_TN), lambda j: (0, j)),
            pl.BlockSpec((1, MOD_TN), lambda j: (0, j)),
        ],
        out_specs=pl.BlockSpec((1, MOD_TN), lambda j: (0, j)),
        compiler_params=_params(("arbitrary",)),
        name="mod",
    )(c_col, w_mod, b_mod)


IP_TM = 512
IP_TN = 512
IP_NJ = IN_WIDTH // IP_TN
NORM_ROWS = 16
BF16_SUBLANES = 16
KA_W = 2 * HEAD_DIM
VA_H = HEAD_DIM + BF16_SUBLANES


def _prep_modulation(g_ref, sc_ref, sh_ref, a_scr, s_scr):
    a_scr[...] = jnp.broadcast_to(g_ref[...] * (1.0 + sc_ref[...]), a_scr.shape)
    s_scr[...] = jnp.broadcast_to(sh_ref[...], s_scr.shape)


def _row_rms(x_ref, rs_scr, n_rows):
    def body(r, _):
        r0 = pl.multiple_of(r * NORM_ROWS, NORM_ROWS)
        width = x_ref.shape[1]
        parts = []
        for c in range(width // LANES):
            xc = x_ref[pl.ds(r0, NORM_ROWS), c * LANES:(c + 1) * LANES]
            parts.append(xc * xc)
        while len(parts) > 1:
            parts = [parts[p] + parts[p + 1] for p in range(0, len(parts), 2)]
        rs_scr[pl.ds(r0, NORM_ROWS), :] = parts[0]
        return 0
    lax.fori_loop(0, n_rows // NORM_ROWS, body, 0)
    ms = jnp.sum(rs_scr[...], axis=-1, keepdims=True) * (1.0 / x_ref.shape[1])
    rs_scr[...] = jnp.broadcast_to(lax.rsqrt(ms + EPS), rs_scr.shape)


def _normed_tile(x_ref, rs_scr, a_scr, s_scr, r0, c):
    cs = slice(c * LANES, (c + 1) * LANES)
    return x_ref[pl.ds(r0, NORM_ROWS), cs] * rs_scr[pl.ds(r0, NORM_ROWS), :] * a_scr[:, cs] + s_scr[:, cs]


def _norm_rope_t(xt, g_col, cos_t, sin_t):
    ms = jnp.mean(xt * xt, axis=0, keepdims=True)
    y = xt * lax.rsqrt(ms + EPS) * g_col
    q = ROPE_AXIS_DIM // 2
    partner = jnp.concatenate([y[q:2 * q], y[0:q], y[3 * q:4 * q], y[2 * q:3 * q]], axis=0)
    return y * cos_t + partner * sin_t


def _inproj_kernel(x_ref, g_ref, sc_ref, sh_ref, w_ref, cos_ref, sin_ref, qg_ref, kg_ref,
                   pool_ref, qt_ref, k_ref, vt_ref, h_scr, rs_scr, a_scr, s_scr):
    j = pl.program_id(1)

    @pl.when(j == 0)
    def _():
        _prep_modulation(g_ref, sc_ref, sh_ref, a_scr, s_scr)
        _row_rms(x_ref, rs_scr, IP_TM)

        def body(r, _):
            r0 = pl.multiple_of(r * NORM_ROWS, NORM_ROWS)
            for c in range(D_MODEL // LANES):
                h_scr[pl.ds(r0, NORM_ROWS), c * LANES:(c + 1) * LANES] = _normed_tile(
                    x_ref, rs_scr, a_scr, s_scr, r0, c).astype(BF16)
            return 0
        lax.fori_loop(0, IP_TM // NORM_ROWS, body, 0)

    acc = jnp.dot(h_scr[...], w_ref[...], preferred_element_type=F32)

    @pl.when(j < 4)
    def _():
        pool_ref[...] = acc.astype(BF16)

    @pl.when((j >= 4) & (j < 8))
    def _():
        for hh in range(IP_TN // HEAD_DIM):
            sl = slice(hh * HEAD_DIM, (hh + 1) * HEAD_DIM)
            r = _norm_rope_t(acc[:, sl].T, qg_ref[...], cos_ref[...], sin_ref[...]) * Q_SCALE
            qt_ref[sl, :] = r.astype(BF16)

    @pl.when(j == 8)
    def _():
        lane = lax.broadcasted_iota(I32, (IP_TM, HEAD_DIM), 1)
        one_col = jnp.where(lane == 0, 1.0, 0.0).astype(BF16)
        for hh in range(N_KV_HEADS):
            sl = slice(hh * HEAD_DIM, (hh + 1) * HEAD_DIM)
            k_ref[:, hh * KA_W:hh * KA_W + HEAD_DIM] = _norm_rope_t(
                acc[:, sl].T, kg_ref[...], cos_ref[...], sin_ref[...]).T.astype(BF16)
            k_ref[:, hh * KA_W + HEAD_DIM:(hh + 1) * KA_W] = one_col

    @pl.when(j == 9)
    def _():
        for hh in range(N_KV_HEADS):
            sl = slice(hh * HEAD_DIM, (hh + 1) * HEAD_DIM)
            vt_ref[hh * VA_H:hh * VA_H + HEAD_DIM, :] = acc[:, sl].T.astype(BF16)
            vt_ref[hh * VA_H + HEAD_DIM:(hh + 1) * VA_H, :] = jnp.ones((VA_H - HEAD_DIM, IP_TM), BF16)


def _inproj(x2, mod, norm1_g, w_in_b, cos_t, sin_t, qg, kg):
    row = lambda n: pl.BlockSpec((1, D_MODEL), lambda i, j, n=n: (0, n))
    return pl.pallas_call(
        _inproj_kernel,
        out_shape=(
            jax.ShapeDtypeStruct((SEQ, POOL_WIDTH), BF16),
            jax.ShapeDtypeStruct((ATTN_WIDTH, SEQ), BF16),
            jax.ShapeDtypeStruct((SEQ, N_KV_HEADS * KA_W), BF16),
            jax.ShapeDtypeStruct((N_KV_HEADS * VA_H, SEQ), BF16),
        ),
        grid=(SEQ // IP_TM, IP_NJ),
        in_specs=[
            pl.BlockSpec((IP_TM, D_MODEL), lambda i, j: (i, 0)),
            pl.BlockSpec((1, D_MODEL), lambda i, j: (0, 0)),
            row(1), row(0),
            pl.BlockSpec((D_MODEL, IP_TN), lambda i, j: (0, j)),
            pl.BlockSpec((HEAD_DIM, IP_TM), lambda i, j: (0, i)),
            pl.BlockSpec((HEAD_DIM, IP_TM), lambda i, j: (0, i)),
            pl.BlockSpec((HEAD_DIM, 1), lambda i, j: (0, 0)),
            pl.BlockSpec((HEAD_DIM, 1), lambda i, j: (0, 0)),
        ],
        out_specs=(
            pl.BlockSpec((IP_TM, IP_TN), lambda i, j: (i, jnp.minimum(j, 3))),
            pl.BlockSpec((IP_TN, IP_TM), lambda i, j: (jnp.clip(j - 4, 0, 3), i)),
            pl.BlockSpec((IP_TM, N_KV_HEADS * KA_W), lambda i, j: (i, 0)),
            pl.BlockSpec((N_KV_HEADS * VA_H, IP_TM), lambda i, j: (0, i)),
        ),
        scratch_shapes=[
            pltpu.VMEM((IP_TM, D_MODEL), BF16),
            pltpu.VMEM((IP_TM, LANES), F32),
            pltpu.VMEM((NORM_ROWS, D_MODEL), F32),
            pltpu.VMEM((NORM_ROWS, D_MODEL), F32),
        ],
        compiler_params=_params(("arbitrary", "arbitrary")),
        name="inproj",
    )(x2, norm1_g, mod, mod, w_in_b, cos_t, sin_t, qg, kg)


AT_TQ = 512
AT_TK = 8192
AT_TK_ONLINE = 512
SHIFT_LIMIT = 60.0


def _attn_kernel(qt_ref, k_ref, vt_ref, o_ref, qa_scr, p_scr, kmax_scr):
    h = pl.program_id(0)
    i = pl.program_id(1)

    @pl.when((i == 0) & (h % GROUP == 0))
    def _():
        def body(c, mx):
            c0 = pl.multiple_of(c * AT_TK, AT_TK)
            kc = k_ref[pl.ds(c0, AT_TK), :HEAD_DIM].astype(F32)
            n2 = (kc * kc).sum(axis=1, keepdims=True)
            return jnp.maximum(mx, n2.max(axis=0, keepdims=True))
        mx = lax.fori_loop(0, SEQ // AT_TK, body, jnp.zeros((1, 1), F32))
        kmax_scr[...] = jnp.broadcast_to(jnp.sqrt(mx), kmax_scr.shape)

    q = qt_ref[...].astype(F32)
    bound = jnp.sqrt((q * q).sum(axis=0, keepdims=True)) * kmax_scr[0:1, 0:1] * 1.01
    fast = jnp.max(bound) <= SHIFT_LIMIT

    @pl.when(fast)
    def _():
        qa_scr[0:HEAD_DIM, :] = qt_ref[...]
        row = lax.broadcasted_iota(I32, (KA_W - HEAD_DIM, AT_TQ), 0)
        qa_scr[HEAD_DIM:, :] = jnp.where(row == 0, -bound, 0.0).astype(BF16)

        def body(c, _):
            c0 = pl.multiple_of(c * AT_TK, AT_TK)
            s = jnp.dot(k_ref[pl.ds(c0, AT_TK), :], qa_scr[...], preferred_element_type=F32)
            p_scr[pl.ds(c0, AT_TK), :] = jnp.exp2(s).astype(BF16)
            return 0

        lax.fori_loop(0, SEQ // AT_TK, body, 0)
        o = jnp.dot(vt_ref[...], p_scr[...], preferred_element_type=F32)
        o_ref[...] = (o[:HEAD_DIM] * (1.0 / o[HEAD_DIM:HEAD_DIM + 1])).T.astype(BF16)

    @pl.when(jnp.logical_not(fast))
    def _():
        qt = qt_ref[...]

        def chunk(c, carry):
            m, l, acc = carry
            c0 = pl.multiple_of(c * AT_TK_ONLINE, AT_TK_ONLINE)
            s = jnp.dot(k_ref[pl.ds(c0, AT_TK_ONLINE), :HEAD_DIM], qt, preferred_element_type=F32)
            m_new = jnp.maximum(m, s.max(axis=0, keepdims=True))
            alpha = jnp.exp2(m - m_new)
            p = jnp.exp2(s - m_new)
            l = alpha * l + p.sum(axis=0, keepdims=True)
            pv = jnp.dot(vt_ref[:HEAD_DIM, pl.ds(c0, AT_TK_ONLINE)], p.astype(BF16), preferred_element_type=F32)
            return m_new, l, alpha * acc + pv

        init = (jnp.full((1, AT_TQ), -jnp.inf, F32), jnp.zeros((1, AT_TQ), F32),
                jnp.zeros((HEAD_DIM, AT_TQ), F32))
        _, l, acc = lax.fori_loop(0, SEQ // AT_TK_ONLINE, chunk, init)
        o_ref[...] = (acc * (1.0 / l)).T.astype(BF16)


def _attention(qt, k, vt):
    return pl.pallas_call(
        _attn_kernel,
        out_shape=jax.ShapeDtypeStruct((SEQ, ATTN_WIDTH), BF16),
        grid=(N_HEADS, SEQ // AT_TQ),
        in_specs=[
            pl.BlockSpec((HEAD_DIM, AT_TQ), lambda h, i: (h, i)),
            pl.BlockSpec((SEQ, KA_W), lambda h, i: (0, h // GROUP)),
            pl.BlockSpec((VA_H, SEQ), lambda h, i: (h // GROUP, 0)),
        ],
        out_specs=pl.BlockSpec((AT_TQ, HEAD_DIM), lambda h, i: (i, h)),
        scratch_shapes=[
            pltpu.VMEM((KA_W, AT_TQ), BF16),
            pltpu.VMEM((SEQ, AT_TQ), BF16),
            pltpu.VMEM((8, LANES), F32),
        ],
        compiler_params=_params(("arbitrary", "arbitrary")),
        name="attn",
    )(qt, k, vt)


PL_TM = 256
PL_HALO = 16


def _pool_kernel(prev_ref, main_ref, next_ref, wp_ref, scale_ref, o_ref, buf):
    i = pl.program_id(0)
    last = pl.num_programs(0) - 1
    buf[0:PL_HALO, :] = jnp.where(i == 0, 0.0, prev_ref[...].astype(F32))
    buf[PL_HALO:PL_HALO + PL_TM, :] = main_ref[...].astype(F32)
    buf[PL_HALO + PL_TM:, :] = jnp.where(i == last, 0.0, next_ref[...].astype(F32))
    t = i * PL_TM + lax.broadcasted_iota(I32, (PL_TM, 1), 0)
    for gi, w in enumerate(POOL_WINDOWS):
        cols = slice(gi * POOL_GROUP_WIDTH, (gi + 1) * POOL_GROUP_WIDTH)
        win = buf[PL_HALO - w // 2:PL_HALO - w // 2 + PL_TM, cols]
        for d in range(-w // 2 + 1, w // 2):
            win = win + buf[PL_HALO + d:PL_HALO + d + PL_TM, cols]
        lo = jnp.maximum(t - w // 2, 0)
        hi = jnp.minimum(t + w // 2 - 1, SEQ - 1)
        cnt = (hi - lo + 1).astype(F32)
        dlt = win / cnt - buf[PL_HALO:PL_HALO + PL_TM, cols]
        y = jnp.dot(dlt.astype(BF16), wp_ref[gi], preferred_element_type=F32)
        o_ref[:, cols] = (y * scale_ref[:, cols]).astype(BF16)


def _pool(pool_in, w_pool_b, pool_scale):
    nh = PL_TM // PL_HALO
    n_halo_blocks = SEQ // PL_HALO
    return pl.pallas_call(
        _pool_kernel,
        out_shape=jax.ShapeDtypeStruct((SEQ, POOL_WIDTH), BF16),
        grid=(SEQ // PL_TM,),
        in_specs=[
            pl.BlockSpec((PL_HALO, POOL_WIDTH), lambda i: (jnp.maximum(i * nh - 1, 0), 0)),
            pl.BlockSpec((PL_TM, POOL_WIDTH), lambda i: (i, 0)),
            pl.BlockSpec((PL_HALO, POOL_WIDTH), lambda i: (jnp.minimum((i + 1) * nh, n_halo_blocks - 1), 0)),
            pl.BlockSpec((len(POOL_WINDOWS), POOL_GROUP_WIDTH, POOL_GROUP_WIDTH), lambda i: (0, 0, 0)),
            pl.BlockSpec((1, POOL_WIDTH), lambda i: (0, 0)),
        ],
        out_specs=pl.BlockSpec((PL_TM, POOL_WIDTH), lambda i: (i, 0)),
        scratch_shapes=[pltpu.VMEM((PL_TM + 2 * PL_HALO, POOL_WIDTH), F32)],
        compiler_params=_params(("arbitrary",)),
        name="pool",
    )(pool_in, pool_in, pool_in, w_pool_b, pool_scale)


OP_TM = 1024
OP_TN = 512


def _outproj_kernel(a_ref, p_ref, wa_ref, wp_ref, x_ref, g_ref, o_ref):
    acc = jnp.dot(a_ref[...], wa_ref[...], preferred_element_type=F32)
    acc = acc + jnp.dot(p_ref[...], wp_ref[...], preferred_element_type=F32)
    o_ref[...] = x_ref[...] + g_ref[...] * acc


def _outproj(attn, pool, w_out_b, x2, mod):
    return pl.pallas_call(
        _outproj_kernel,
        out_shape=jax.ShapeDtypeStruct((SEQ, D_MODEL), F32),
        grid=(SEQ // OP_TM, D_MODEL // OP_TN),
        in_specs=[
            pl.BlockSpec((OP_TM, ATTN_WIDTH), lambda i, j: (i, 0)),
            pl.BlockSpec((OP_TM, POOL_WIDTH), lambda i, j: (i, 0)),
            pl.BlockSpec((ATTN_WIDTH, OP_TN), lambda i, j: (0, j)),
            pl.BlockSpec((POOL_WIDTH, OP_TN), lambda i, j: (1, j)),
            pl.BlockSpec((OP_TM, OP_TN), lambda i, j: (i, j)),
            pl.BlockSpec((1, OP_TN), lambda i, j: (0, 2 * (D_MODEL // OP_TN) + j)),
        ],
        out_specs=pl.BlockSpec((OP_TM, OP_TN), lambda i, j: (i, j)),
        compiler_params=_params(("arbitrary", "arbitrary")),
        name="outproj",
    )(attn, pool, w_out_b, w_out_b, x2, mod)


N2_TM = 256
RT_PAD = LANES


def _norm2_kernel(x_ref, g_ref, sc_ref, sh_ref, wcat_ref, whi_ref, b_ref, lt_ref, hi_scr, lo_scr,
                  rs_scr, a_scr, s_scr):
    @pl.when(pl.program_id(0) == 0)
    def _():
        _prep_modulation(g_ref, sc_ref, sh_ref, a_scr, s_scr)

    _row_rms(x_ref, rs_scr, N2_TM)

    def body(r, _):
        r0 = pl.multiple_of(r * NORM_ROWS, NORM_ROWS)
        for c in range(D_MODEL // LANES):
            cs = slice(c * LANES, (c + 1) * LANES)
            h = _normed_tile(x_ref, rs_scr, a_scr, s_scr, r0, c)
            hi = h.astype(BF16)
            hi_scr[pl.ds(r0, NORM_ROWS), cs] = hi
            lo_scr[pl.ds(r0, NORM_ROWS), cs] = (h - hi.astype(F32)).astype(BF16)
        return 0
    lax.fori_loop(0, N2_TM // NORM_ROWS, body, 0)
    a = jnp.dot(hi_scr[...], wcat_ref[...], preferred_element_type=F32)
    b = jnp.dot(lo_scr[...], whi_ref[...], preferred_element_type=F32)
    logits = a[:, :RT_PAD] + a[:, RT_PAD:] + b + b_ref[...]
    lt_ref[...] = logits.T[:N_EXPERTS, :]


def _norm2(x1, norm2_g, mod, wcat, whi, b_pad):
    row = lambda n: pl.BlockSpec((1, D_MODEL), lambda i, n=n: (0, n))
    return pl.pallas_call(
        _norm2_kernel,
        out_shape=jax.ShapeDtypeStruct((N_EXPERTS, SEQ), F32),
        grid=(SEQ // N2_TM,),
        in_specs=[
            pl.BlockSpec((N2_TM, D_MODEL), lambda i: (i, 0)),
            pl.BlockSpec((1, D_MODEL), lambda i: (0, 0)),
            row(4), row(3),
            pl.BlockSpec((D_MODEL, 2 * RT_PAD), lambda i: (0, 0)),
            pl.BlockSpec((D_MODEL, RT_PAD), lambda i: (0, 0)),
            pl.BlockSpec((1, RT_PAD), lambda i: (0, 0)),
        ],
        out_specs=pl.BlockSpec((N_EXPERTS, N2_TM), lambda i: (0, i)),
        scratch_shapes=[
            pltpu.VMEM((N2_TM, D_MODEL), BF16),
            pltpu.VMEM((N2_TM, D_MODEL), BF16),
            pltpu.VMEM((N2_TM, LANES), F32),
            pltpu.VMEM((NORM_ROWS, D_MODEL), F32),
            pltpu.VMEM((NORM_ROWS, D_MODEL), F32),
        ],
        compiler_params=_params(("arbitrary",)),
        name="norm2",
    )(x1, norm2_g, mod, mod, wcat, whi, b_pad)


RT_CH = 1024
RT_SB = 256


def _route_kernel(lt_ref, dest_ref, gate_ref, nblk_ref, idx_scr, rank_scr):
    e_col = lax.broadcasted_iota(I32, (N_EXPERTS, RT_CH), 0).astype(F32)
    tri = (lax.broadcasted_iota(I32, (RT_SB, RT_SB), 0) < lax.broadcasted_iota(I32, (RT_SB, RT_SB), 1)).astype(BF16)
    carry = jnp.zeros((N_EXPERTS, 1), F32)
    for c in range(SEQ // RT_CH):
        cs = slice(c * RT_CH, (c + 1) * RT_CH)
        work = lt_ref[:, cs]
        vals = []
        mask = jnp.zeros((N_EXPERTS, RT_CH), F32)
        for k in range(TOP_K):
            m = work.max(axis=0, keepdims=True)
            idx = jnp.where(work == m, e_col, float(N_EXPERTS)).min(axis=0, keepdims=True)
            sel = e_col == idx
            vals.append(m)
            idx_scr[k:k + 1, cs] = idx
            mask = jnp.where(sel, 1.0, mask)
            work = jnp.where(sel, -jnp.inf, work)
        ex = [jnp.exp(v - vals[0]) for v in vals]
        den = ex[0] + ex[1] + ex[2] + ex[3]
        for k in range(TOP_K):
            gate_ref[k:k + 1, cs] = ex[k] / den
        for b in range(RT_CH // RT_SB):
            blk = mask[:, b * RT_SB:(b + 1) * RT_SB]
            pref = jnp.dot(blk.astype(BF16), tri, preferred_element_type=F32)
            rank_scr[:, c * RT_CH + b * RT_SB:c * RT_CH + (b + 1) * RT_SB] = pref + carry
            carry = carry + blk.sum(axis=1, keepdims=True)
    nblk = jnp.floor((carry + (MOE_TM - 1)) * (1.0 / MOE_TM))
    nblk_b = jnp.broadcast_to(nblk, (N_EXPERTS, LANES))
    lower = (lax.broadcasted_iota(I32, (N_EXPERTS, N_EXPERTS), 1) < lax.broadcasted_iota(I32, (N_EXPERTS, N_EXPERTS), 0)).astype(BF16)
    start_blk = jnp.dot(lower, nblk_b.astype(BF16), preferred_element_type=F32)
    nblk_ref[...] = nblk_b.astype(I32)
    start = start_blk[:, 0:1] * float(MOE_TM)
    for c in range(SEQ // RT_CH):
        cs = slice(c * RT_CH, (c + 1) * RT_CH)
        slot = rank_scr[:, cs] + start
        for k in range(TOP_K):
            sel = e_col == idx_scr[k:k + 1, cs]
            dest_ref[k:k + 1, cs] = jnp.where(sel, slot, 0.0).sum(axis=0, keepdims=True).astype(I32)


def _route(logits_t):
    return pl.pallas_call(
        _route_kernel,
        out_shape=(
            jax.ShapeDtypeStruct((TOP_K, SEQ), I32),
            jax.ShapeDtypeStruct((TOP_K, SEQ), F32),
            jax.ShapeDtypeStruct((N_EXPERTS, LANES), I32),
        ),
        scratch_shapes=[pltpu.VMEM((8, SEQ), F32), pltpu.VMEM((N_EXPERTS, SEQ), F32)],
        compiler_params=pltpu.CompilerParams(vmem_limit_bytes=VMEM_LIMIT),
        name="route",
    )(logits_t)


DP_TM = 256
XS_W = D_MODEL // 2
U32 = jnp.uint32


def _dispatch_kernel(dest_ref, lastblk_ref, x_ref, g_ref, sc_ref, sh_ref, xs_hbm, pk, zero_buf, rs_scr, a_scr,
                     s_scr, zsem, sem):
    i = pl.program_id(0)
    par = i % 2

    @pl.when(i == 0)
    def _():
        _prep_modulation(g_ref, sc_ref, sh_ref, a_scr, s_scr)
        zero_buf[...] = jnp.zeros_like(zero_buf)

        def zcopy(e):
            b = jnp.maximum(lastblk_ref[e], 0)
            return pltpu.make_async_copy(zero_buf, xs_hbm.at[pl.ds(pl.multiple_of(b * MOE_TM, MOE_TM), MOE_TM)], zsem)

        def zstart(e, _):
            @pl.when(lastblk_ref[e] >= 0)
            def _():
                zcopy(e).start()
            return 0

        def zwait(e, _):
            @pl.when(lastblk_ref[e] >= 0)
            def _():
                zcopy(e).wait()
            return 0

        lax.fori_loop(0, N_EXPERTS, zstart, 0)
        lax.fori_loop(0, N_EXPERTS, zwait, 0)

    _row_rms(x_ref, rs_scr, DP_TM)

    def norm_body(r, _):
        r0 = pl.multiple_of(r * NORM_ROWS, NORM_ROWS)
        for c in range(XS_W // LANES):
            lo = _normed_tile(x_ref, rs_scr, a_scr, s_scr, r0, c)
            hi = _normed_tile(x_ref, rs_scr, a_scr, s_scr, r0, c + XS_W // LANES)
            lo = lax.bitcast_convert_type(lo.astype(BF16).astype(F32), U32)
            hi = lax.bitcast_convert_type(hi.astype(BF16).astype(F32), U32)
            pk[par, pl.ds(r0, NORM_ROWS), c * LANES:(c + 1) * LANES] = (lo >> 16) | (hi & jnp.uint32(0xFFFF0000))
        return 0

    lax.fori_loop(0, DP_TM // NORM_ROWS, norm_body, 0)

    def issue(u, _):
        for k in range(TOP_K):
            d = dest_ref[k * SEQ + i * DP_TM + u]
            pltpu.make_async_copy(pk.at[par, pl.ds(u, 1)], xs_hbm.at[pl.ds(d, 1)], sem.at[par]).start()
        return 0

    lax.fori_loop(0, DP_TM, issue, 0)

    def drain(p):
        for _ in range(TOP_K):
            pltpu.make_async_copy(pk.at[p], xs_hbm.at[pl.ds(0, DP_TM)], sem.at[p]).wait()

    @pl.when(i > 0)
    def _():
        drain(1 - par)

    @pl.when(i == pl.num_programs(0) - 1)
    def _():
        drain(par)


def _dispatch(dest_flat, lastblk, x1, norm2_g, mod):
    row = lambda n: pl.BlockSpec((1, D_MODEL), lambda i, d, lb, n=n: (0, n))
    return pl.pallas_call(
        _dispatch_kernel,
        out_shape=jax.ShapeDtypeStruct((MOE_ROWS, XS_W), U32),
        grid_spec=pltpu.PrefetchScalarGridSpec(
            num_scalar_prefetch=2,
            grid=(SEQ // DP_TM,),
            in_specs=[
                pl.BlockSpec((DP_TM, D_MODEL), lambda i, d, lb: (i, 0)),
                pl.BlockSpec((1, D_MODEL), lambda i, d, lb: (0, 0)),
                row(4), row(3),
            ],
            out_specs=pl.BlockSpec(memory_space=pl.ANY),
            scratch_shapes=[
                pltpu.VMEM((2, DP_TM, XS_W), U32),
                pltpu.VMEM((MOE_TM, XS_W), U32),
                pltpu.VMEM((DP_TM, LANES), F32),
                pltpu.VMEM((NORM_ROWS, D_MODEL), F32),
                pltpu.VMEM((NORM_ROWS, D_MODEL), F32),
                pltpu.SemaphoreType.DMA,
                pltpu.SemaphoreType.DMA((2,)),
            ],
        ),
        compiler_params=_params(("arbitrary",)),
        name="dispatch",
    )(dest_flat, lastblk, x1, norm2_g, mod, mod)


F1_TF = 512
F2_TN = 2048


CAST_ROWS = 128


def _expert_schedule(nblk):
    end_blk = jnp.cumsum(nblk)
    n_used = end_blk[-1]
    blocks = jnp.arange(MOE_NB, dtype=I32)
    be = jnp.minimum(jnp.sum(end_blk[None, :] <= blocks[:, None], axis=1), N_EXPERTS - 1).astype(I32)
    first = (blocks < n_used) & (be != jnp.concatenate([jnp.full((1,), -1, I32), be[:-1]]))
    seg = jnp.cumsum(first.astype(I32)) - 1
    seg_end = end_blk[be]
    nxt = jnp.where(seg_end < n_used, be[jnp.minimum(seg_end, MOE_NB - 1)], -1)
    meta = jnp.stack([n_used, jnp.sum(first.astype(I32))]).astype(I32)
    lastblk = jnp.where(nblk > 0, end_blk - 1, -1).astype(I32)
    return meta, be, first.astype(I32), seg.astype(I32), nxt.astype(I32), lastblk


def _stream_expert_weights(j, i, nj, meta_ref, be_ref, first_ref, seg_ref, nxt_ref, tile_copies, stg, wbuf):
    @pl.when(first_ref[i] == 1)
    def _():
        seq = j * meta_ref[1] + seg_ref[i]
        slot = seq % 2

        @pl.when(seq == 0)
        def _():
            for cp in tile_copies(be_ref[i], j, slot):
                cp.start()

        for cp in tile_copies(be_ref[i], j, slot):
            cp.wait()

        def cast(r, _):
            r0 = pl.multiple_of(r * CAST_ROWS, CAST_ROWS)
            wbuf[pl.ds(r0, CAST_ROWS), :] = stg[slot, pl.ds(r0, CAST_ROWS), :].astype(BF16)
            return 0

        lax.fori_loop(0, wbuf.shape[0] // CAST_ROWS, cast, 0)

        nxt = nxt_ref[i]

        @pl.when(nxt >= 0)
        def _():
            for cp in tile_copies(nxt, j, 1 - slot):
                cp.start()

        @pl.when((nxt < 0) & (j + 1 < nj))
        def _():
            for cp in tile_copies(be_ref[0], j + 1, 1 - slot):
                cp.start()


def _ffn1_kernel(meta_ref, be_ref, first_ref, seg_ref, nxt_ref, x_ref, bg_ref, bl_ref, w1_hbm, o_ref,
                 stg, wbuf, sem):
    j = pl.program_id(0)
    i = pl.program_id(1)
    nj = pl.num_programs(0)

    def tile_copies(e, jj, slot):
        cg = pl.multiple_of(jj * F1_TF, F1_TF)
        cl = pl.multiple_of(D_FF + jj * F1_TF, F1_TF)
        return (
            pltpu.make_async_copy(w1_hbm.at[e, :, pl.ds(cg, F1_TF)], stg.at[slot, :, 0:F1_TF], sem.at[slot]),
            pltpu.make_async_copy(w1_hbm.at[e, :, pl.ds(cl, F1_TF)], stg.at[slot, :, F1_TF:2 * F1_TF], sem.at[slot]),
        )

    @pl.when(i < meta_ref[0])
    def _():
        _stream_expert_weights(j, i, nj, meta_ref, be_ref, first_ref, seg_ref, nxt_ref, tile_copies, stg, wbuf)
        xp = x_ref[...]
        x_lo = lax.bitcast_convert_type(xp << 16, F32).astype(BF16)
        x_hi = lax.bitcast_convert_type(xp & jnp.uint32(0xFFFF0000), F32).astype(BF16)
        y = jnp.dot(x_lo, wbuf[:XS_W, :], preferred_element_type=F32)
        y = y + jnp.dot(x_hi, wbuf[XS_W:, :], preferred_element_type=F32)
        glu = jnp.minimum(y[:, :F1_TF] + bg_ref[...], SWIGLU_LIMIT)
        lin = jnp.clip(y[:, F1_TF:] + bl_ref[...], -SWIGLU_LIMIT, SWIGLU_LIMIT)
        o_ref[...] = (glu * jax.nn.sigmoid(SWIGLU_ALPHA * glu) * (lin + 1.0)).astype(BF16)


def _ffn1(sched, xs, w1, b1_3):
    meta, be, first, seg, nxt, _ = sched
    nj = D_FF // F1_TF
    blk = lambda i, meta: jnp.minimum(i, meta[0] - 1)
    return pl.pallas_call(
        _ffn1_kernel,
        out_shape=jax.ShapeDtypeStruct((MOE_ROWS, D_FF), BF16),
        grid_spec=pltpu.PrefetchScalarGridSpec(
            num_scalar_prefetch=5,
            grid=(nj, MOE_NB),
            in_specs=[
                pl.BlockSpec((MOE_TM, XS_W), lambda j, i, meta, be, *_: (blk(i, meta), 0)),
                pl.BlockSpec((None, 1, F1_TF), lambda j, i, meta, be, *_: (be[blk(i, meta)], 0, j)),
                pl.BlockSpec((None, 1, F1_TF), lambda j, i, meta, be, *_: (be[blk(i, meta)], 0, nj + j)),
                pl.BlockSpec(memory_space=pl.ANY),
            ],
            out_specs=pl.BlockSpec((MOE_TM, F1_TF), lambda j, i, meta, be, *_: (blk(i, meta), j)),
            scratch_shapes=[
                pltpu.VMEM((2, D_MODEL, 2 * F1_TF), F32),
                pltpu.VMEM((D_MODEL, 2 * F1_TF), BF16),
                pltpu.SemaphoreType.DMA((2,)),
            ],
        ),
        compiler_params=_params(("arbitrary", "arbitrary")),
        name="ffn1",
    )(meta, be, first, seg, nxt, xs, b1_3, b1_3, w1)


def _ffn2_kernel(meta_ref, be_ref, first_ref, seg_ref, nxt_ref, a_ref, b_ref, w2_hbm, o_ref, stg, wbuf, sem):
    j = pl.program_id(0)
    i = pl.program_id(1)
    nj = pl.num_programs(0)

    def tile_copies(e, jj, slot):
        c0 = pl.multiple_of(jj * F2_TN, F2_TN)
        return (pltpu.make_async_copy(w2_hbm.at[e, :, pl.ds(c0, F2_TN)], stg.at[slot], sem.at[slot]),)

    @pl.when(i < meta_ref[0])
    def _():
        _stream_expert_weights(j, i, nj, meta_ref, be_ref, first_ref, seg_ref, nxt_ref, tile_copies, stg, wbuf)
        o_ref[...] = jnp.dot(a_ref[...], wbuf[...], preferred_element_type=F32) + b_ref[...]


def _ffn2(sched, act, w2, b2_3):
    meta, be, first, seg, nxt, _ = sched
    blk = lambda i, meta: jnp.minimum(i, meta[0] - 1)
    return pl.pallas_call(
        _ffn2_kernel,
        out_shape=jax.ShapeDtypeStruct((MOE_ROWS, D_MODEL), F32),
        grid_spec=pltpu.PrefetchScalarGridSpec(
            num_scalar_prefetch=5,
            grid=(D_MODEL // F2_TN, MOE_NB),
            in_specs=[
                pl.BlockSpec((MOE_TM, D_FF), lambda j, i, meta, be, *_: (blk(i, meta), 0)),
                pl.BlockSpec((None, 1, F2_TN), lambda j, i, meta, be, *_: (be[blk(i, meta)], 0, j)),
                pl.BlockSpec(memory_space=pl.ANY),
            ],
            out_specs=pl.BlockSpec((MOE_TM, F2_TN), lambda j, i, meta, be, *_: (blk(i, meta), j)),
            scratch_shapes=[
                pltpu.VMEM((2, D_FF, F2_TN), F32),
                pltpu.VMEM((D_FF, F2_TN), BF16),
                pltpu.SemaphoreType.DMA((2,)),
            ],
        ),
        compiler_params=_params(("arbitrary", "arbitrary")),
        name="ffn2",
    )(meta, be, first, seg, nxt, act, b2_3, w2)


CB_TM = 128


def _combine_kernel(dest_ref, x_ref, gate_ref, g2_ref, ys_hbm, o_ref, buf, gate_scr, g2_scr, sem):
    i = pl.program_id(0)
    par = i % 2

    def issue(tile, p):
        def body(u, _):
            for k in range(TOP_K):
                d = dest_ref[k * SEQ + tile * CB_TM + u]
                pltpu.make_async_copy(ys_hbm.at[pl.ds(d, 1)], buf.at[p, k, pl.ds(u, 1)], sem.at[p]).start()
            return 0
        lax.fori_loop(0, CB_TM, body, 0)

    @pl.when(i == 0)
    def _():
        issue(0, 0)

    @pl.when(i + 1 < pl.num_programs(0))
    def _():
        issue(i + 1, 1 - par)

    for k in range(TOP_K):
        pltpu.make_async_copy(ys_hbm.at[pl.ds(0, CB_TM)], buf.at[par, k], sem.at[par]).wait()

    for k in range(TOP_K):
        gate_scr[k] = jnp.broadcast_to(gate_ref[:, k:k + 1], (CB_TM, LANES))

    @pl.when(i == 0)
    def _():
        g2_scr[...] = jnp.broadcast_to(g2_ref[...], g2_scr.shape)

    def body(r, _):
        r0 = pl.multiple_of(r * 8, 8)
        rows = pl.ds(r0, 8)
        gk = [gate_scr[k, rows, :] for k in range(TOP_K)]
        for c in range(D_MODEL // LANES):
            cs = slice(c * LANES, (c + 1) * LANES)
            y = buf[par, 0, rows, cs] * gk[0]
            for k in range(1, TOP_K):
                y = y + buf[par, k, rows, cs] * gk[k]
            o_ref[rows, cs] = x_ref[rows, cs] + g2_scr[:, cs] * y
        return 0

    lax.fori_loop(0, CB_TM // 8, body, 0)


def _combine(dest_flat, x1, gates_t, mod, ys):
    return pl.pallas_call(
        _combine_kernel,
        out_shape=jax.ShapeDtypeStruct((SEQ, D_MODEL), F32),
        grid_spec=pltpu.PrefetchScalarGridSpec(
            num_scalar_prefetch=1,
            grid=(SEQ // CB_TM,),
            in_specs=[
                pl.BlockSpec((CB_TM, D_MODEL), lambda i, d: (i, 0)),
                pl.BlockSpec((CB_TM, TOP_K), lambda i, d: (i, 0)),
                pl.BlockSpec((1, D_MODEL), lambda i, d: (0, 5)),
                pl.BlockSpec(memory_space=pl.ANY),
            ],
            out_specs=pl.BlockSpec((CB_TM, D_MODEL), lambda i, d: (i, 0)),
            scratch_shapes=[
                pltpu.VMEM((2, TOP_K, CB_TM, D_MODEL), F32),
                pltpu.VMEM((TOP_K, CB_TM, LANES), F32),
                pltpu.VMEM((8, D_MODEL), F32),
                pltpu.SemaphoreType.DMA((2,)),
            ],
        ),
        compiler_params=_params(("arbitrary",)),
        name="combine",
    )(dest_flat, x1, gates_t, mod, ys)


def _rope_tables():
    t = jnp.arange(SEQ, dtype=I32)
    row = (t // GRID_W).astype(F32)
    col = (t % GRID_W).astype(F32)
    inv_freq = ROPE_THETA ** (-jnp.arange(0, ROPE_AXIS_DIM, 2, dtype=F32) / ROPE_AXIS_DIM)
    ang_r = inv_freq[:, None] * row[None, :]
    ang_c = inv_freq[:, None] * col[None, :]
    cos_t = jnp.concatenate([jnp.cos(ang_r), jnp.cos(ang_r), jnp.cos(ang_c), jnp.cos(ang_c)], axis=0)
    sin_t = jnp.concatenate([-jnp.sin(ang_r), jnp.sin(ang_r), -jnp.sin(ang_c), jnp.sin(ang_c)], axis=0)
    return cos_t, sin_t


def kernel(x, c, w_mod, b_mod, norm1_g, w_in, q_norm_g, k_norm_g, w_pool, pool_scale, w_out, norm2_g,
           w_router, b_router, w1, b1, w2, b2):
    assert x.shape == (1, SEQ, D_MODEL) and w_mod.shape[0] == 1
    x2 = x[0]
    cos_t, sin_t = _rope_tables()

    mod = _mod(c.reshape(D_MODEL, 1), w_mod[0], b_mod)

    pool_in, qt, k, vt = _inproj(x2, mod, norm1_g, w_in[0].astype(BF16), cos_t, sin_t,
                                 q_norm_g.reshape(HEAD_DIM, 1), k_norm_g.reshape(HEAD_DIM, 1))
    attn = _attention(qt, k, vt)
    pool = _pool(pool_in, w_pool[0].astype(BF16), pool_scale)
    x1 = _outproj(attn, pool, w_out[0].astype(BF16), x2, mod)

    wr = w_router[0]
    wr_hi = wr.astype(BF16)
    wr_lo = (wr - wr_hi.astype(F32)).astype(BF16)
    pad = lambda a: jnp.pad(a, ((0, 0), (0, RT_PAD - N_EXPERTS)))
    wcat = jnp.concatenate([pad(wr_hi), pad(wr_lo)], axis=1)
    logits_t = _norm2(x1, norm2_g, mod, wcat, pad(wr_hi), pad(b_router))

    dest, gates, nblk = _route(logits_t)
    sched = _expert_schedule(nblk[:, 0])
    dest_flat = dest.reshape(TOP_K * SEQ)

    xs = _dispatch(dest_flat, sched[-1], x1, norm2_g, mod)
    act = _ffn1(sched, xs, w1[0], b1[0].reshape(N_EXPERTS, 1, 2 * D_FF))
    ys = _ffn2(sched, act, w2[0], b2[0].reshape(N_EXPERTS, 1, D_MODEL))
    out = _combine(dest_flat, x1, gates.T, mod, ys)
    return out[None]
```

```python
import functools
import math

import jax
import jax.numpy as jnp
from jax import lax
from jax.experimental import pallas as pl
from jax.experimental.pallas import tpu as pltpu

F32 = jnp.float32
BF16 = jnp.bfloat16
I32 = jnp.int32

D_MODEL = 4096
SEQ = 8192
POOL_WIDTH = 2048
ATTN_WIDTH = 2048
HEAD_DIM = 128
N_HEADS = 16
N_KV_HEADS = 4
GROUP = N_HEADS // N_KV_HEADS
KV_WIDTH = N_KV_HEADS * HEAD_DIM
IN_WIDTH = POOL_WIDTH + ATTN_WIDTH + 2 * KV_WIDTH
POOL_WINDOWS = (2, 4, 8, 16)
POOL_GROUP_WIDTH = POOL_WIDTH // len(POOL_WINDOWS)
GRID_W = 64
ROPE_THETA = 10000.0
ROPE_AXIS_DIM = HEAD_DIM // 2
N_EXPERTS = 32
TOP_K = 4
D_FF = D_MODEL // 4
SWIGLU_ALPHA = 1.702
SWIGLU_LIMIT = 7.0
N_MOD = 6
EPS = 1e-6

LANES = 128
VMEM_LIMIT = 56 * 1024 * 1024

Q_SCALE = (HEAD_DIM ** -0.5) * math.log2(math.e)

MOE_TM = 256
MOE_NB = SEQ * TOP_K // MOE_TM + N_EXPERTS
MOE_ROWS = MOE_NB * MOE_TM


def _params(sem, vmem=VMEM_LIMIT):
    return pltpu.CompilerParams(dimension_semantics=sem, vmem_limit_bytes=vmem)


MOD_TN = 1024
MOD_KC = 256


def _mod_kernel(c_ref, w_ref, b_ref, o_ref):
    def body(k, acc):
        r = pl.multiple_of(k * MOD_KC, MOD_KC)
        ck = c_ref[pl.ds(r, MOD_KC), :]
        ck = ck * jax.nn.sigmoid(ck)
        p = w_ref[pl.ds(r, MOD_KC), :] * ck
        return acc + p.reshape(MOD_KC // 8, 8, MOD_TN).sum(axis=0)

    acc = lax.fori_loop(0, D_MODEL // MOD_KC, body, jnp.zeros((8, MOD_TN), F32))
    o_ref[...] = acc.sum(axis=0, keepdims=True) + b_ref[...]


def _mod(c_col, w_mod, b_mod):
    n = N_MOD * D_MODEL
    return pl.pallas_call(
        _mod_kernel,
        out_shape=jax.ShapeDtypeStruct((1, n), F32),
        grid=(n // MOD_TN,),
        in_specs=[
            pl.BlockSpec((D_MODEL, 1), lambda j: (0, 0)),
            pl.BlockSpec((D_MODEL, MOD_TN), lambda j: (0, j)),
            pl.BlockSpec((1, MOD_TN), lambda j: (0, j)),
        ],
        out_specs=pl.BlockSpec((1, MOD_TN), lambda j: (0, j)),
        compiler_params=_params(("arbitrary",)),
        name="mod",
    )(c_col, w_mod, b_mod)


IP_TM = 512
IP_TN = 512
IP_NJ = IN_WIDTH // IP_TN
NORM_ROWS = 16
BF16_SUBLANES = 16
KA_W = 2 * HEAD_DIM
VA_H = HEAD_DIM + BF16_SUBLANES


def _prep_modulation(g_ref, sc_ref, sh_ref, a_scr, s_scr):
    a_scr[...] = jnp.broadcast_to(g_ref[...] * (1.0 + sc_ref[...]), a_scr.shape)
    s_scr[...] = jnp.broadcast_to(sh_ref[...], s_scr.shape)


def _row_rms(x_ref, rs_scr, n_rows):
    def body(r, _):
        r0 = pl.multiple_of(r * NORM_ROWS, NORM_ROWS)
        width = x_ref.shape[1]
        parts = []
        for c in range(width // LANES):
            xc = x_ref[pl.ds(r0, NORM_ROWS), c * LANES:(c + 1) * LANES]
            parts.append(xc * xc)
        while len(parts) > 1:
            parts = [parts[p] + parts[p + 1] for p in range(0, len(parts), 2)]
        rs_scr[pl.ds(r0, NORM_ROWS), :] = parts[0]
        return 0
    lax.fori_loop(0, n_rows // NORM_ROWS, body, 0)
    ms = jnp.sum(rs_scr[...], axis=-1, keepdims=True) * (1.0 / x_ref.shape[1])
    rs_scr[...] = jnp.broadcast_to(lax.rsqrt(ms + EPS), rs_scr.shape)


def _normed_tile(x_ref, rs_scr, a_scr, s_scr, r0, c):
    cs = slice(c * LANES, (c + 1) * LANES)
    return x_ref[pl.ds(r0, NORM_ROWS), cs] * rs_scr[pl.ds(r0, NORM_ROWS), :] * a_scr[:, cs] + s_scr[:, cs]


def _norm_rope_t(xt, g_col, cos_t, sin_t):
    ms = jnp.mean(xt * xt, axis=0, keepdims=True)
    y = xt * lax.rsqrt(ms + EPS) * g_col
    q = ROPE_AXIS_DIM // 2
    partner = jnp.concatenate([y[q:2 * q], y[0:q], y[3 * q:4 * q], y[2 * q:3 * q]], axis=0)
    return y * cos_t + partner * sin_t


def _inproj_kernel(x_ref, g_ref, sc_ref, sh_ref, w_ref, cos_ref, sin_ref, qg_ref, kg_ref,
                   pool_ref, qt_ref, k_ref, vt_ref, h_scr, rs_scr, a_scr, s_scr):
    j = pl.program_id(1)

    @pl.when(j == 0)
    def _():
        _prep_modulation(g_ref, sc_ref, sh_ref, a_scr, s_scr)
        _row_rms(x_ref, rs_scr, IP_TM)

        def body(r, _):
            r0 = pl.multiple_of(r * NORM_ROWS, NORM_ROWS)
            for c in range(D_MODEL // LANES):
                h_scr[pl.ds(r0, NORM_ROWS), c * LANES:(c + 1) * LANES] = _normed_tile(
                    x_ref, rs_scr, a_scr, s_scr, r0, c).astype(BF16)
            return 0
        lax.fori_loop(0, IP_TM // NORM_ROWS, body, 0)

    acc = jnp.dot(h_scr[...], w_ref[...], preferred_element_type=F32)

    @pl.when(j < 4)
    def _():
        pool_ref[...] = acc.astype(BF16)

    @pl.when((j >= 4) & (j < 8))
    def _():
        for hh in range(IP_TN // HEAD_DIM):
            sl = slice(hh * HEAD_DIM, (hh + 1) * HEAD_DIM)
            r = _norm_rope_t(acc[:, sl].T, qg_ref[...], cos_ref[...], sin_ref[...]) * Q_SCALE
            qt_ref[sl, :] = r.astype(BF16)

    @pl.when(j == 8)
    def _():
        lane = lax.broadcasted_iota(I32, (IP_TM, HEAD_DIM), 1)
        one_col = jnp.where(lane == 0, 1.0, 0.0).astype(BF16)
        for hh in range(N_KV_HEADS):
            sl = slice(hh * HEAD_DIM, (hh + 1) * HEAD_DIM)
            k_ref[:, hh * KA_W:hh * KA_W + HEAD_DIM] = _norm_rope_t(
                acc[:, sl].T, kg_ref[...], cos_ref[...], sin_ref[...]).T.astype(BF16)
            k_ref[:, hh * KA_W + HEAD_DIM:(hh + 1) * KA_W] = one_col

    @pl.when(j == 9)
    def _():
        for hh in range(N_KV_HEADS):
            sl = slice(hh * HEAD_DIM, (hh + 1) * HEAD_DIM)
            vt_ref[hh * VA_H:hh * VA_H + HEAD_DIM, :] = acc[:, sl].T.astype(BF16)
            vt_ref[hh * VA_H + HEAD_DIM:(hh + 1) * VA_H, :] = jnp.ones((VA_H - HEAD_DIM, IP_TM), BF16)


def _inproj(x2, mod, norm1_g, w_in_b, cos_t, sin_t, qg, kg):
    row = lambda n: pl.BlockSpec((1, D_MODEL), lambda i, j, n=n: (0, n))
    return pl.pallas_call(
        _inproj_kernel,
        out_shape=(
            jax.ShapeDtypeStruct((SEQ, POOL_WIDTH), BF16),
            jax.ShapeDtypeStruct((ATTN_WIDTH, SEQ), BF16),
            jax.ShapeDtypeStruct((SEQ, N_KV_HEADS * KA_W), BF16),
            jax.ShapeDtypeStruct((N_KV_HEADS * VA_H, SEQ), BF16),
        ),
        grid=(SEQ // IP_TM, IP_NJ),
        in_specs=[
            pl.BlockSpec((IP_TM, D_MODEL), lambda i, j: (i, 0)),
            pl.BlockSpec((1, D_MODEL), lambda i, j: (0, 0)),
            row(1), row(0),
            pl.BlockSpec((D_MODEL, IP_TN), lambda i, j: (0, j)),
            pl.BlockSpec((HEAD_DIM, IP_TM), lambda i, j: (0, i)),
            pl.BlockSpec((HEAD_DIM, IP_TM), lambda i, j: (0, i)),
            pl.BlockSpec((HEAD_DIM, 1), lambda i, j: (0, 0)),
            pl.BlockSpec((HEAD_DIM, 1), lambda i, j: (0, 0)),
        ],
        out_specs=(
            pl.BlockSpec((IP_TM, IP_TN), lambda i, j: (i, jnp.minimum(j, 3))),
            pl.BlockSpec((IP_TN, IP_TM), lambda i, j: (jnp.clip(j - 4, 0, 3), i)),
            pl.BlockSpec((IP_TM, N_KV_HEADS * KA_W), lambda i, j: (i, 0)),
            pl.BlockSpec((N_KV_HEADS * VA_H, IP_TM), lambda i, j: (0, i)),
        ),
        scratch_shapes=[
            pltpu.VMEM((IP_TM, D_MODEL), BF16),
            pltpu.VMEM((IP_TM, LANES), F32),
            pltpu.VMEM((NORM_ROWS, D_MODEL), F32),
            pltpu.VMEM((NORM_ROWS, D_MODEL), F32),
        ],
        compiler_params=_params(("arbitrary", "arbitrary")),
        name="inproj",
    )(x2, norm1_g, mod, mod, w_in_b, cos_t, sin_t, qg, kg)


AT_TQ = 512
AT_TK = 8192
AT_TK_ONLINE = 512
SHIFT_LIMIT = 60.0


def _attn_kernel(qt_ref, k_ref, vt_ref, o_ref, qa_scr, p_scr, kmax_scr):
    h = pl.program_id(0)
    i = pl.program_id(1)

    @pl.when((i == 0) & (h % GROUP == 0))
    def _():
        def body(c, mx):
            c0 = pl.multiple_of(c * AT_TK, AT_TK)
            kc = k_ref[pl.ds(c0, AT_TK), :HEAD_DIM].astype(F32)
            n2 = (kc * kc).sum(axis=1, keepdims=True)
            return jnp.maximum(mx, n2.max(axis=0, keepdims=True))
        mx = lax.fori_loop(0, SEQ // AT_TK, body, jnp.zeros((1, 1), F32))
        kmax_scr[...] = jnp.broadcast_to(jnp.sqrt(mx), kmax_scr.shape)

    q = qt_ref[...].astype(F32)
    bound = jnp.sqrt((q * q).sum(axis=0, keepdims=True)) * kmax_scr[0:1, 0:1] * 1.01
    fast = jnp.max(bound) <= SHIFT_LIMIT

    @pl.when(fast)
    def _():
        qa_scr[0:HEAD_DIM, :] = qt_ref[...]
        row = lax.broadcasted_iota(I32, (KA_W - HEAD_DIM, AT_TQ), 0)
        qa_scr[HEAD_DIM:, :] = jnp.where(row == 0, -bound, 0.0).astype(BF16)

        def body(c, _):
            c0 = pl.multiple_of(c * AT_TK, AT_TK)
            s = jnp.dot(k_ref[pl.ds(c0, AT_TK), :], qa_scr[...], preferred_element_type=F32)
            p_scr[pl.ds(c0, AT_TK), :] = jnp.exp2(s).astype(BF16)
            return 0

        lax.fori_loop(0, SEQ // AT_TK, body, 0)
        o = jnp.dot(vt_ref[...], p_scr[...], preferred_element_type=F32)
        o_ref[...] = (o[:HEAD_DIM] * (1.0 / o[HEAD_DIM:HEAD_DIM + 1])).T.astype(BF16)

    @pl.when(jnp.logical_not(fast))
    def _():
        qt = qt_ref[...]

        def chunk(c, carry):
            m, l, acc = carry
            c0 = pl.multiple_of(c * AT_TK_ONLINE, AT_TK_ONLINE)
            s = jnp.dot(k_ref[pl.ds(c0, AT_TK_ONLINE), :HEAD_DIM], qt, preferred_element_type=F32)
            m_new = jnp.maximum(m, s.max(axis=0, keepdims=True))
            alpha = jnp.exp2(m - m_new)
            p = jnp.exp2(s - m_new)
            l = alpha * l + p.sum(axis=0, keepdims=True)
            pv = jnp.dot(vt_ref[:HEAD_DIM, pl.ds(c0, AT_TK_ONLINE)], p.astype(BF16), preferred_element_type=F32)
            return m_new, l, alpha * acc + pv

        init = (jnp.full((1, AT_TQ), -jnp.inf, F32), jnp.zeros((1, AT_TQ), F32),
                jnp.zeros((HEAD_DIM, AT_TQ), F32))
        _, l, acc = lax.fori_loop(0, SEQ // AT_TK_ONLINE, chunk, init)
        o_ref[...] = (acc * (1.0 / l)).T.astype(BF16)


def _attention(qt, k, vt):
    return pl.pallas_call(
        _attn_kernel,
        out_shape=jax.ShapeDtypeStruct((SEQ, ATTN_WIDTH), BF16),
        grid=(N_HEADS, SEQ // AT_TQ),
        in_specs=[
            pl.BlockSpec((HEAD_DIM, AT_TQ), lambda h, i: (h, i)),
            pl.BlockSpec((SEQ, KA_W), lambda h, i: (0, h // GROUP)),
            pl.BlockSpec((VA_H, SEQ), lambda h, i: (h // GROUP, 0)),
        ],
        out_specs=pl.BlockSpec((AT_TQ, HEAD_DIM), lambda h, i: (i, h)),
        scratch_shapes=[
            pltpu.VMEM((KA_W, AT_TQ), BF16),
            pltpu.VMEM((SEQ, AT_TQ), BF16),
            pltpu.VMEM((8, LANES), F32),
        ],
        compiler_params=_params(("arbitrary", "arbitrary")),
        name="attn",
    )(qt, k, vt)


PL_TM = 256
PL_HALO = 16


def _pool_kernel(prev_ref, main_ref, next_ref, wp_ref, scale_ref, o_ref, buf):
    i = pl.program_id(0)
    last = pl.num_programs(0) - 1
    buf[0:PL_HALO, :] = jnp.where(i == 0, 0.0, prev_ref[...].astype(F32))
    buf[PL_HALO:PL_HALO + PL_TM, :] = main_ref[...].astype(F32)
    buf[PL_HALO + PL_TM:, :] = jnp.where(i == last, 0.0, next_ref[...].astype(F32))
    t = i * PL_TM + lax.broadcasted_iota(I32, (PL_TM, 1), 0)
    for gi, w in enumerate(POOL_WINDOWS):
        cols = slice(gi * POOL_GROUP_WIDTH, (gi + 1) * POOL_GROUP_WIDTH)
        win = buf[PL_HALO - w // 2:PL_HALO - w // 2 + PL_TM, cols]
        for d in range(-w // 2 + 1, w // 2):
            win = win + buf[PL_HALO + d:PL_HALO + d + PL_TM, cols]
        lo = jnp.maximum(t - w // 2, 0)
        hi = jnp.minimum(t + w // 2 - 1, SEQ - 1)
        cnt = (hi - lo + 1).astype(F32)
        dlt = win / cnt - buf[PL_HALO:PL_HALO + PL_TM, cols]
        y = jnp.dot(dlt.astype(BF16), wp_ref[gi], preferred_element_type=F32)
        o_ref[:, cols] = (y * scale_ref[:, cols]).astype(BF16)


def _pool(pool_in, w_pool_b, pool_scale):
    nh = PL_TM // PL_HALO
    n_halo_blocks = SEQ // PL_HALO
    return pl.pallas_call(
        _pool_kernel,
        out_shape=jax.ShapeDtypeStruct((SEQ, POOL_WIDTH), BF16),
        grid=(SEQ // PL_TM,),
        in_specs=[
            pl.BlockSpec((PL_HALO, POOL_WIDTH), lambda i: (jnp.maximum(i * nh - 1, 0), 0)),
            pl.BlockSpec((PL_TM, POOL_WIDTH), lambda i: (i, 0)),
            pl.BlockSpec((PL_HALO, POOL_WIDTH), lambda i: (jnp.minimum((i + 1) * nh, n_halo_blocks - 1), 0)),
            pl.BlockSpec((len(POOL_WINDOWS), POOL_GROUP_WIDTH, POOL_GROUP_WIDTH), lambda i: (0, 0, 0)),
            pl.BlockSpec((1, POOL_WIDTH), lambda i: (0, 0)),
        ],
        out_specs=pl.BlockSpec((PL_TM, POOL_WIDTH), lambda i: (i, 0)),
        scratch_shapes=[pltpu.VMEM((PL_TM + 2 * PL_HALO, POOL_WIDTH), F32)],
        compiler_params=_params(("arbitrary",)),
        name="pool",
    )(pool_in, pool_in, pool_in, w_pool_b, pool_scale)


OP_TM = 1024
OP_TN = 512


def _outproj_kernel(a_ref, p_ref, wa_ref, wp_ref, x_ref, g_ref, o_ref):
    acc = jnp.dot(a_ref[...], wa_ref[...], preferred_element_type=F32)
    acc = acc + jnp.dot(p_ref[...], wp_ref[...], preferred_element_type=F32)
    o_ref[...] = x_ref[...] + g_ref[...] * acc


def _outproj(attn, pool, w_out_b, x2, mod):
    return pl.pallas_call(
        _outproj_kernel,
        out_shape=jax.ShapeDtypeStruct((SEQ, D_MODEL), F32),
        grid=(SEQ // OP_TM, D_MODEL // OP_TN),
        in_specs=[
            pl.BlockSpec((OP_TM, ATTN_WIDTH), lambda i, j: (i, 0)),
            pl.BlockSpec((OP_TM, POOL_WIDTH), lambda i, j: (i, 0)),
            pl.BlockSpec((ATTN_WIDTH, OP_TN), lambda i, j: (0, j)),
            pl.BlockSpec((POOL_WIDTH, OP_TN), lambda i, j: (1, j)),
            pl.BlockSpec((OP_TM, OP_TN), lambda i, j: (i, j)),
            pl.BlockSpec((1, OP_TN), lambda i, j: (0, 2 * (D_MODEL // OP_TN) + j)),
        ],
        out_specs=pl.BlockSpec((OP_TM, OP_TN), lambda i, j: (i, j)),
        compiler_params=_params(("arbitrary", "arbitrary")),
        name="outproj",
    )(attn, pool, w_out_b, w_out_b, x2, mod)


N2_TM = 256
RT_PAD = LANES


def _norm2_kernel(x_ref, g_ref, sc_ref, sh_ref, wcat_ref, whi_ref, b_ref, lt_ref, hi_scr, lo_scr,
                  rs_scr, a_scr, s_scr):
    @pl.when(pl.program_id(0) == 0)
    def _():
        _prep_modulation(g_ref, sc_ref, sh_ref, a_scr, s_scr)

    _row_rms(x_ref, rs_scr, N2_TM)

    def body(r, _):
        r0 = pl.multiple_of(r * NORM_ROWS, NORM_ROWS)
        for c in range(D_MODEL // LANES):
            cs = slice(c * LANES, (c + 1) * LANES)
            h = _normed_tile(x_ref, rs_scr, a_scr, s_scr, r0, c)
            hi = h.astype(BF16)
            hi_scr[pl.ds(r0, NORM_ROWS), cs] = hi
            lo_scr[pl.ds(r0, NORM_ROWS), cs] = (h - hi.astype(F32)).astype(BF16)
        return 0
    lax.fori_loop(0, N2_TM // NORM_ROWS, body, 0)
    a = jnp.dot(hi_scr[...], wcat_ref[...], preferred_element_type=F32)
    b = jnp.dot(lo_scr[...], whi_ref[...], preferred_element_type=F32)
    logits = a[:, :RT_PAD] + a[:, RT_PAD:] + b + b_ref[...]
    lt_ref[...] = logits.T[:N_EXPERTS, :]


def _norm2(x1, norm2_g, mod, wcat, whi, b_pad):
    row = lambda n: pl.BlockSpec((1, D_MODEL), lambda i, n=n: (0, n))
    return pl.pallas_call(
        _norm2_kernel,
        out_shape=jax.ShapeDtypeStruct((N_EXPERTS, SEQ), F32),
        grid=(SEQ // N2_TM,),
        in_specs=[
            pl.BlockSpec((N2_TM, D_MODEL), lambda i: (i, 0)),
            pl.BlockSpec((1, D_MODEL), lambda i: (0, 0)),
            row(4), row(3),
            pl.BlockSpec((D_MODEL, 2 * RT_PAD), lambda i: (0, 0)),
            pl.BlockSpec((D_MODEL, RT_PAD), lambda i: (0, 0)),
            pl.BlockSpec((1, RT_PAD), lambda i: (0, 0)),
        ],
        out_specs=pl.BlockSpec((N_EXPERTS, N2_TM), lambda i: (0, i)),
        scratch_shapes=[
            pltpu.VMEM((N2_TM, D_MODEL), BF16),
            pltpu.VMEM((N2_TM, D_MODEL), BF16),
            pltpu.VMEM((N2_TM, LANES), F32),
            pltpu.VMEM((NORM_ROWS, D_MODEL), F32),
            pltpu.VMEM((NORM_ROWS, D_MODEL), F32),
        ],
        compiler_params=_params(("arbitrary",)),
        name="norm2",
    )(x1, norm2_g, mod, mod, wcat, whi, b_pad)


RT_CH = 1024
RT_SB = 256


def _route_kernel(lt_ref, dest_ref, gate_ref, nblk_ref, idx_scr, rank_scr):
    e_col = lax.broadcasted_iota(I32, (N_EXPERTS, RT_CH), 0).astype(F32)
    tri = (lax.broadcasted_iota(I32, (RT_SB, RT_SB), 0) < lax.broadcasted_iota(I32, (RT_SB, RT_SB), 1)).astype(BF16)
    carry = jnp.zeros((N_EXPERTS, 1), F32)
    for c in range(SEQ // RT_CH):
        cs = slice(c * RT_CH, (c + 1) * RT_CH)
        work = lt_ref[:, cs]
        vals = []
        mask = jnp.zeros((N_EXPERTS, RT_CH), F32)
        for k in range(TOP_K):
            m = work.max(axis=0, keepdims=True)
            idx = jnp.where(work == m, e_col, float(N_EXPERTS)).min(axis=0, keepdims=True)
            sel = e_col == idx
            vals.append(m)
            idx_scr[k:k + 1, cs] = idx
            mask = jnp.where(sel, 1.0, mask)
            work = jnp.where(sel, -jnp.inf, work)
        ex = [jnp.exp(v - vals[0]) for v in vals]
        den = ex[0] + ex[1] + ex[2] + ex[3]
        for k in range(TOP_K):
            gate_ref[k:k + 1, cs] = ex[k] / den
        for b in range(RT_CH // RT_SB):
            blk = mask[:, b * RT_SB:(b + 1) * RT_SB]
            pref = jnp.dot(blk.astype(BF16), tri, preferred_element_type=F32)
            rank_scr[:, c * RT_CH + b * RT_SB:c * RT_CH + (b + 1) * RT_SB] = pref + carry
            carry = carry + blk.sum(axis=1, keepdims=True)
    nblk = jnp.floor((carry + (MOE_TM - 1)) * (1.0 / MOE_TM))
    nblk_b = jnp.broadcast_to(nblk, (N_EXPERTS, LANES))
    lower = (lax.broadcasted_iota(I32, (N_EXPERTS, N_EXPERTS), 1) < lax.broadcasted_iota(I32, (N_EXPERTS, N_EXPERTS), 0)).astype(BF16)
    start_blk = jnp.dot(lower, nblk_b.astype(BF16), preferred_element_type=F32)
    nblk_ref[...] = nblk_b.astype(I32)
    start = start_blk[:, 0:1] * float(MOE_TM)
    for c in range(SEQ // RT_CH):
        cs = slice(c * RT_CH, (c + 1) * RT_CH)
        slot = rank_scr[:, cs] + start
        for k in range(TOP_K):
            sel = e_col == idx_scr[k:k + 1, cs]
            dest_ref[k:k + 1, cs] = jnp.where(sel, slot, 0.0).sum(axis=0, keepdims=True).astype(I32)


def _route(logits_t):
    return pl.pallas_call(
        _route_kernel,
        out_shape=(
            jax.ShapeDtypeStruct((TOP_K, SEQ), I32),
            jax.ShapeDtypeStruct((TOP_K, SEQ), F32),
            jax.ShapeDtypeStruct((N_EXPERTS, LANES), I32),
        ),
        scratch_shapes=[pltpu.VMEM((8, SEQ), F32), pltpu.VMEM((N_EXPERTS, SEQ), F32)],
        compiler_params=pltpu.CompilerParams(vmem_limit_bytes=VMEM_LIMIT),
        name="route",
    )(logits_t)


DP_TM = 256
XS_W = D_MODEL // 2
U32 = jnp.uint32


def _dispatch_kernel(dest_ref, lastblk_ref, x_ref, g_ref, sc_ref, sh_ref, xs_hbm, pk, zero_buf, rs_scr, a_scr,
                     s_scr, zsem, sem):
    i = pl.program_id(0)
    par = i % 2

    @pl.when(i == 0)
    def _():
        _prep_modulation(g_ref, sc_ref, sh_ref, a_scr, s_scr)
        zero_buf[...] = jnp.zeros_like(zero_buf)

        def zcopy(e):
            b = jnp.maximum(lastblk_ref[e], 0)
            return pltpu.make_async_copy(zero_buf, xs_hbm.at[pl.ds(pl.multiple_of(b * MOE_TM, MOE_TM), MOE_TM)], zsem)

        def zstart(e, _):
            @pl.when(lastblk_ref[e] >= 0)
            def _():
                zcopy(e).start()
            return 0

        def zwait(e, _):
            @pl.when(lastblk_ref[e] >= 0)
            def _():
                zcopy(e).wait()
            return 0

        lax.fori_loop(0, N_EXPERTS, zstart, 0)
        lax.fori_loop(0, N_EXPERTS, zwait, 0)

    _row_rms(x_ref, rs_scr, DP_TM)

    def norm_body(r, _):
        r0 = pl.multiple_of(r * NORM_ROWS, NORM_ROWS)
        for c in range(XS_W // LANES):
            lo = _normed_tile(x_ref, rs_scr, a_scr, s_scr, r0, c)
            hi = _normed_tile(x_ref, rs_scr, a_scr, s_scr, r0, c + XS_W // LANES)
            lo = lax.bitcast_convert_type(lo.astype(BF16).astype(F32), U32)
            hi = lax.bitcast_convert_type(hi.astype(BF16).astype(F32), U32)
            pk[par, pl.ds(r0, NORM_ROWS), c * LANES:(c + 1) * LANES] = (lo >> 16) | (hi & jnp.uint32(0xFFFF0000))
        return 0

    lax.fori_loop(0, DP_TM // NORM_ROWS, norm_body, 0)

    def issue(u, _):
        for k in range(TOP_K):
            d = dest_ref[k * SEQ + i * DP_TM + u]
            pltpu.make_async_copy(pk.at[par, pl.ds(u, 1)], xs_hbm.at[pl.ds(d, 1)], sem.at[par]).start()
        return 0

    lax.fori_loop(0, DP_TM, issue, 0)

    def drain(p):
        for _ in range(TOP_K):
            pltpu.make_async_copy(pk.at[p], xs_hbm.at[pl.ds(0, DP_TM)], sem.at[p]).wait()

    @pl.when(i > 0)
    def _():
        drain(1 - par)

    @pl.when(i == pl.num_programs(0) - 1)
    def _():
        drain(par)


def _dispatch(dest_flat, lastblk, x1, norm2_g, mod):
    row = lambda n: pl.BlockSpec((1, D_MODEL), lambda i, d, lb, n=n: (0, n))
    return pl.pallas_call(
        _dispatch_kernel,
        out_shape=jax.ShapeDtypeStruct((MOE_ROWS, XS_W), U32),
        grid_spec=pltpu.PrefetchScalarGridSpec(
            num_scalar_prefetch=2,
            grid=(SEQ // DP_TM,),
            in_specs=[
                pl.BlockSpec((DP_TM, D_MODEL), lambda i, d, lb: (i, 0)),
                pl.BlockSpec((1, D_MODEL), lambda i, d, lb: (0, 0)),
                row(4), row(3),
            ],
            out_specs=pl.BlockSpec(memory_space=pl.ANY),
            scratch_shapes=[
                pltpu.VMEM((2, DP_TM, XS_W), U32),
                pltpu.VMEM((MOE_TM, XS_W), U32),
                pltpu.VMEM((DP_TM, LANES), F32),
                pltpu.VMEM((NORM_ROWS, D_MODEL), F32),
                pltpu.VMEM((NORM_ROWS, D_MODEL), F32),
                pltpu.SemaphoreType.DMA,
                pltpu.SemaphoreType.DMA((2,)),
            ],
        ),
        compiler_params=_params(("arbitrary",)),
        name="dispatch",
    )(dest_flat, lastblk, x1, norm2_g, mod, mod)


F1_TF = 512
F2_TN = 4096


CAST_ROWS = 128
WEIGHT_DMA_PRIORITY = 1


def _expert_schedule(nblk):
    end_blk = jnp.cumsum(nblk)
    n_used = end_blk[-1]
    blocks = jnp.arange(MOE_NB, dtype=I32)
    be = jnp.minimum(jnp.sum(end_blk[None, :] <= blocks[:, None], axis=1), N_EXPERTS - 1).astype(I32)
    first = (blocks < n_used) & (be != jnp.concatenate([jnp.full((1,), -1, I32), be[:-1]]))
    seg = jnp.cumsum(first.astype(I32)) - 1
    seg_end = end_blk[be]
    nxt = jnp.where(seg_end < n_used, be[jnp.minimum(seg_end, MOE_NB - 1)], -1)
    meta = jnp.stack([n_used, jnp.sum(first.astype(I32))]).astype(I32)
    lastblk = jnp.where(nblk > 0, end_blk - 1, -1).astype(I32)
    return meta, be, first.astype(I32), seg.astype(I32), nxt.astype(I32), lastblk


def _stream_expert_weights(j, i, nj, meta_ref, be_ref, first_ref, seg_ref, nxt_ref, tile_copies, stg, wbuf):
    @pl.when(first_ref[i] == 1)
    def _():
        seq = j * meta_ref[1] + seg_ref[i]
        slot = seq % 2

        @pl.when(seq == 0)
        def _():
            for cp in tile_copies(be_ref[i], j, slot):
                cp.start(priority=WEIGHT_DMA_PRIORITY)

        for cp in tile_copies(be_ref[i], j, slot):
            cp.wait()

        nxt = nxt_ref[i]

        @pl.when(nxt >= 0)
        def _():
            for cp in tile_copies(nxt, j, 1 - slot):
                cp.start(priority=WEIGHT_DMA_PRIORITY)

        @pl.when((nxt < 0) & (j + 1 < nj))
        def _():
            for cp in tile_copies(be_ref[0], j + 1, 1 - slot):
                cp.start(priority=WEIGHT_DMA_PRIORITY)

        def cast(r, _):
            r0 = pl.multiple_of(r * CAST_ROWS, CAST_ROWS)
            wbuf[pl.ds(r0, CAST_ROWS), :] = stg[slot, pl.ds(r0, CAST_ROWS), :].astype(BF16)
            return 0

        lax.fori_loop(0, wbuf.shape[0] // CAST_ROWS, cast, 0)


def _ffn1_kernel(meta_ref, be_ref, first_ref, seg_ref, nxt_ref, x_ref, bg_ref, bl_ref, w1_hbm, o_ref,
                 stg, wbuf, sem):
    j = pl.program_id(0)
    i = pl.program_id(1)
    nj = pl.num_programs(0)

    def tile_copies(e, jj, slot):
        cg = pl.multiple_of(jj * F1_TF, F1_TF)
        cl = pl.multiple_of(D_FF + jj * F1_TF, F1_TF)
        return (
            pltpu.make_async_copy(w1_hbm.at[e, :, pl.ds(cg, F1_TF)], stg.at[slot, :, 0:F1_TF], sem.at[slot]),
            pltpu.make_async_copy(w1_hbm.at[e, :, pl.ds(cl, F1_TF)], stg.at[slot, :, F1_TF:2 * F1_TF], sem.at[slot]),
        )

    @pl.when(i < meta_ref[0])
    def _():
        _stream_expert_weights(j, i, nj, meta_ref, be_ref, first_ref, seg_ref, nxt_ref, tile_copies, stg, wbuf)
        xp = x_ref[...]
        x_lo = lax.bitcast_convert_type(xp << 16, F32).astype(BF16)
        x_hi = lax.bitcast_convert_type(xp & jnp.uint32(0xFFFF0000), F32).astype(BF16)
        y = jnp.dot(x_lo, wbuf[:XS_W, :], preferred_element_type=F32)
        y = y + jnp.dot(x_hi, wbuf[XS_W:, :], preferred_element_type=F32)
        glu = jnp.minimum(y[:, :F1_TF] + bg_ref[...], SWIGLU_LIMIT)
        lin = jnp.clip(y[:, F1_TF:] + bl_ref[...], -SWIGLU_LIMIT, SWIGLU_LIMIT)
        o_ref[...] = (glu * jax.nn.sigmoid(SWIGLU_ALPHA * glu) * (lin + 1.0)).astype(BF16)


def _ffn1(sched, xs, w1, b1_3):
    meta, be, first, seg, nxt, _ = sched
    nj = D_FF // F1_TF
    blk = lambda i, meta: jnp.minimum(i, meta[0] - 1)
    return pl.pallas_call(
        _ffn1_kernel,
        out_shape=jax.ShapeDtypeStruct((MOE_ROWS, D_FF), BF16),
        grid_spec=pltpu.PrefetchScalarGridSpec(
            num_scalar_prefetch=5,
            grid=(nj, MOE_NB),
            in_specs=[
                pl.BlockSpec((MOE_TM, XS_W), lambda j, i, meta, be, *_: (blk(i, meta), 0)),
                pl.BlockSpec((None, 1, F1_TF), lambda j, i, meta, be, *_: (be[blk(i, meta)], 0, j)),
                pl.BlockSpec((None, 1, F1_TF), lambda j, i, meta, be, *_: (be[blk(i, meta)], 0, nj + j)),
                pl.BlockSpec(memory_space=pl.ANY),
            ],
            out_specs=pl.BlockSpec((MOE_TM, F1_TF), lambda j, i, meta, be, *_: (blk(i, meta), j)),
            scratch_shapes=[
                pltpu.VMEM((2, D_MODEL, 2 * F1_TF), F32),
                pltpu.VMEM((D_MODEL, 2 * F1_TF), BF16),
                pltpu.SemaphoreType.DMA((2,)),
            ],
        ),
        compiler_params=_params(("arbitrary", "arbitrary")),
        name="ffn1",
    )(meta, be, first, seg, nxt, xs, b1_3, b1_3, w1)


def _ffn2_kernel(meta_ref, be_ref, first_ref, seg_ref, nxt_ref, a_ref, b_ref, w2_hbm, o_ref, stg, wbuf, sem):
    j = pl.program_id(0)
    i = pl.program_id(1)
    nj = pl.num_programs(0)

    def tile_copies(e, jj, slot):
        c0 = pl.multiple_of(jj * F2_TN, F2_TN)
        return (pltpu.make_async_copy(w2_hbm.at[e, :, pl.ds(c0, F2_TN)], stg.at[slot], sem.at[slot]),)

    @pl.when(i < meta_ref[0])
    def _():
        _stream_expert_weights(j, i, nj, meta_ref, be_ref, first_ref, seg_ref, nxt_ref, tile_copies, stg, wbuf)
        o_ref[...] = jnp.dot(a_ref[...], wbuf[...], preferred_element_type=F32) + b_ref[...]


def _ffn2(sched, act, w2, b2_3):
    meta, be, first, seg, nxt, _ = sched
    blk = lambda i, meta: jnp.minimum(i, meta[0] - 1)
    return pl.pallas_call(
        _ffn2_kernel,
        out_shape=jax.ShapeDtypeStruct((MOE_ROWS, D_MODEL), F32),
        grid_spec=pltpu.PrefetchScalarGridSpec(
            num_scalar_prefetch=5,
            grid=(D_MODEL // F2_TN, MOE_NB),
            in_specs=[
                pl.BlockSpec((MOE_TM, D_FF), lambda j, i, meta, be, *_: (blk(i, meta), 0)),
                pl.BlockSpec((None, 1, F2_TN), lambda j, i, meta, be, *_: (be[blk(i, meta)], 0, j)),
                pl.BlockSpec(memory_space=pl.ANY),
            ],
            out_specs=pl.BlockSpec((MOE_TM, F2_TN), lambda j, i, meta, be, *_: (blk(i, meta), j)),
            scratch_shapes=[
                pltpu.VMEM((2, D_FF, F2_TN), F32),
                pltpu.VMEM((D_FF, F2_TN), BF16),
                pltpu.SemaphoreType.DMA((2,)),
            ],
        ),
        compiler_params=_params(("arbitrary", "arbitrary")),
        name="ffn2",
    )(meta, be, first, seg, nxt, act, b2_3, w2)


CB_TM = 128


def _combine_kernel(dest_ref, x_ref, gate_ref, g2_ref, ys_hbm, o_ref, buf, gate_scr, g2_scr, sem):
    i = pl.program_id(0)
    par = i % 2

    def issue(tile, p):
        def body(u, _):
            for k in range(TOP_K):
                d = dest_ref[k * SEQ + tile * CB_TM + u]
                pltpu.make_async_copy(ys_hbm.at[pl.ds(d, 1)], buf.at[p, k, pl.ds(u, 1)], sem.at[p]).start()
            return 0
        lax.fori_loop(0, CB_TM, body, 0)

    @pl.when(i == 0)
    def _():
        issue(0, 0)

    @pl.when(i + 1 < pl.num_programs(0))
    def _():
        issue(i + 1, 1 - par)

    for k in range(TOP_K):
        pltpu.make_async_copy(ys_hbm.at[pl.ds(0, CB_TM)], buf.at[par, k], sem.at[par]).wait()

    for k in range(TOP_K):
        gate_scr[k] = jnp.broadcast_to(gate_ref[:, k:k + 1], (CB_TM, LANES))

    @pl.when(i == 0)
    def _():
        g2_scr[...] = jnp.broadcast_to(g2_ref[...], g2_scr.shape)

    def body(r, _):
        r0 = pl.multiple_of(r * 8, 8)
        rows = pl.ds(r0, 8)
        gk = [gate_scr[k, rows, :] for k in range(TOP_K)]
        for c in range(D_MODEL // LANES):
            cs = slice(c * LANES, (c + 1) * LANES)
            y = buf[par, 0, rows, cs] * gk[0]
            for k in range(1, TOP_K):
                y = y + buf[par, k, rows, cs] * gk[k]
            o_ref[rows, cs] = x_ref[rows, cs] + g2_scr[:, cs] * y
        return 0

    lax.fori_loop(0, CB_TM // 8, body, 0)


def _combine(dest_flat, x1, gates_t, mod, ys):
    return pl.pallas_call(
        _combine_kernel,
        out_shape=jax.ShapeDtypeStruct((SEQ, D_MODEL), F32),
        grid_spec=pltpu.PrefetchScalarGridSpec(
            num_scalar_prefetch=1,
            grid=(SEQ // CB_TM,),
            in_specs=[
                pl.BlockSpec((CB_TM, D_MODEL), lambda i, d: (i, 0)),
                pl.BlockSpec((CB_TM, TOP_K), lambda i, d: (i, 0)),
                pl.BlockSpec((1, D_MODEL), lambda i, d: (0, 5)),
                pl.BlockSpec(memory_space=pl.ANY),
            ],
            out_specs=pl.BlockSpec((CB_TM, D_MODEL), lambda i, d: (i, 0)),
            scratch_shapes=[
                pltpu.VMEM((2, TOP_K, CB_TM, D_MODEL), F32),
                pltpu.VMEM((TOP_K, CB_TM, LANES), F32),
                pltpu.VMEM((8, D_MODEL), F32),
                pltpu.SemaphoreType.DMA((2,)),
            ],
        ),
        compiler_params=_params(("arbitrary",)),
        name="combine",
    )(dest_flat, x1, gates_t, mod, ys)


def _rope_tables():
    t = jnp.arange(SEQ, dtype=I32)
    row = (t // GRID_W).astype(F32)
    col = (t % GRID_W).astype(F32)
    inv_freq = ROPE_THETA ** (-jnp.arange(0, ROPE_AXIS_DIM, 2, dtype=F32) / ROPE_AXIS_DIM)
    ang_r = inv_freq[:, None] * row[None, :]
    ang_c = inv_freq[:, None] * col[None, :]
    cos_t = jnp.concatenate([jnp.cos(ang_r), jnp.cos(ang_r), jnp.cos(ang_c), jnp.cos(ang_c)], axis=0)
    sin_t = jnp.concatenate([-jnp.sin(ang_r), jnp.sin(ang_r), -jnp.sin(ang_c), jnp.sin(ang_c)], axis=0)
    return cos_t, sin_t


def kernel(x, c, w_mod, b_mod, norm1_g, w_in, q_norm_g, k_norm_g, w_pool, pool_scale, w_out, norm2_g,
           w_router, b_router, w1, b1, w2, b2):
    assert x.shape == (1, SEQ, D_MODEL) and w_mod.shape[0] == 1
    x2 = x[0]
    cos_t, sin_t = _rope_tables()

    mod = _mod(c.reshape(D_MODEL, 1), w_mod[0], b_mod)

    pool_in, qt, k, vt = _inproj(x2, mod, norm1_g, w_in[0].astype(BF16), cos_t, sin_t,
                                 q_norm_g.reshape(HEAD_DIM, 1), k_norm_g.reshape(HEAD_DIM, 1))
    attn = _attention(qt, k, vt)
    pool = _pool(pool_in, w_pool[0].astype(BF16), pool_scale)
    x1 = _outproj(attn, pool, w_out[0].astype(BF16), x2, mod)

    wr = w_router[0]
    wr_hi = wr.astype(BF16)
    wr_lo = (wr - wr_hi.astype(F32)).astype(BF16)
    pad = lambda a: jnp.pad(a, ((0, 0), (0, RT_PAD - N_EXPERTS)))
    wcat = jnp.concatenate([pad(wr_hi), pad(wr_lo)], axis=1)
    logits_t = _norm2(x1, norm2_g, mod, wcat, pad(wr_hi), pad(b_router))

    dest, gates, nblk = _route(logits_t)
    sched = _expert_schedule(nblk[:, 0])
    dest_flat = dest.reshape(TOP_K * SEQ)

    xs = _dispatch(dest_flat, sched[-1], x1, norm2_g, mod)
    act = _ffn1(sched, xs, w1[0], b1[0].reshape(N_EXPERTS, 1, 2 * D_FF))
    ys = _ffn2(sched, act, w2[0], b2[0].reshape(N_EXPERTS, 1, D_MODEL))
    out = _combine(dest_flat, x1, gates.T, mod, ys)
    return out[None]
```

```python
import functools
import math

import jax
import jax.numpy as jnp
from jax import lax
from jax.experimental import pallas as pl
from jax.experimental.pallas import tpu as pltpu

F32 = jnp.float32
BF16 = jnp.bfloat16
I32 = jnp.int32

D_MODEL = 4096
SEQ = 8192
POOL_WIDTH = 2048
ATTN_WIDTH = 2048
HEAD_DIM = 128
N_HEADS = 16
N_KV_HEADS = 4
GROUP = N_HEADS // N_KV_HEADS
KV_WIDTH = N_KV_HEADS * HEAD_DIM
IN_WIDTH = POOL_WIDTH + ATTN_WIDTH + 2 * KV_WIDTH
POOL_WINDOWS = (2, 4, 8, 16)
POOL_GROUP_WIDTH = POOL_WIDTH // len(POOL_WINDOWS)
GRID_W = 64
ROPE_THETA = 10000.0
ROPE_AXIS_DIM = HEAD_DIM // 2
N_EXPERTS = 32
TOP_K = 4
D_FF = D_MODEL // 4
SWIGLU_ALPHA = 1.702
SWIGLU_LIMIT = 7.0
N_MOD = 6
EPS = 1e-6

LANES = 128
VMEM_LIMIT = 56 * 1024 * 1024

Q_SCALE = (HEAD_DIM ** -0.5) * math.log2(math.e)

MOE_TM = 256
MOE_NB = SEQ * TOP_K // MOE_TM + N_EXPERTS
MOE_ROWS = MOE_NB * MOE_TM


def _params(sem, vmem=VMEM_LIMIT):
    return pltpu.CompilerParams(dimension_semantics=sem, vmem_limit_bytes=vmem)


MOD_TN = 1024
MOD_KC = 256


def _mod_kernel(c_ref, w_ref, b_ref, o_ref):
    def body(k, acc):
        r = pl.multiple_of(k * MOD_KC, MOD_KC)
        ck = c_ref[pl.ds(r, MOD_KC), :]
        ck = ck * jax.nn.sigmoid(ck)
        p = w_ref[pl.ds(r, MOD_KC), :] * ck
        return acc + p.reshape(MOD_KC // 8, 8, MOD_TN).sum(axis=0)

    acc = lax.fori_loop(0, D_MODEL // MOD_KC, body, jnp.zeros((8, MOD_TN), F32))
    o_ref[...] = acc.sum(axis=0, keepdims=True) + b_ref[...]


def _mod(c_col, w_mod, b_mod):
    n = N_MOD * D_MODEL
    return pl.pallas_call(
        _mod_kernel,
        out_shape=jax.ShapeDtypeStruct((1, n), F32),
        grid=(n // MOD_TN,),
        in_specs=[
            pl.BlockSpec((D_MODEL, 1), lambda j: (0, 0)),
            pl.BlockSpec((D_MODEL, MOD_TN), lambda j: (0, j)),
            pl.BlockSpec((1, MOD_TN), lambda j: (0, j)),
        ],
        out_specs=pl.BlockSpec((1, MOD_TN), lambda j: (0, j)),
        compiler_params=_params(("arbitrary",)),
        name="mod",
    )(c_col, w_mod, b_mod)


IP_TM = 512
IP_TN = 1024
IP_NJ = IN_WIDTH // IP_TN
IP_J_Q = POOL_WIDTH // IP_TN
IP_J_KV = IP_J_Q + ATTN_WIDTH // IP_TN
NORM_ROWS = 16
BF16_SUBLANES = 16
KA_W = 2 * HEAD_DIM
VA_H = HEAD_DIM + BF16_SUBLANES


def _prep_modulation(g_ref, sc_ref, sh_ref, a_scr, s_scr):
    a_scr[...] = jnp.broadcast_to(g_ref[...] * (1.0 + sc_ref[...]), a_scr.shape)
    s_scr[...] = jnp.broadcast_to(sh_ref[...], s_scr.shape)


def _row_rms(x_ref, rs_scr, n_rows):
    def body(r, _):
        r0 = pl.multiple_of(r * NORM_ROWS, NORM_ROWS)
        width = x_ref.shape[1]
        parts = []
        for c in range(width // LANES):
            xc = x_ref[pl.ds(r0, NORM_ROWS), c * LANES:(c + 1) * LANES]
            parts.append(xc * xc)
        while len(parts) > 1:
            parts = [parts[p] + parts[p + 1] for p in range(0, len(parts), 2)]
        rs_scr[pl.ds(r0, NORM_ROWS), :] = parts[0]
        return 0
    lax.fori_loop(0, n_rows // NORM_ROWS, body, 0)
    ms = jnp.sum(rs_scr[...], axis=-1, keepdims=True) * (1.0 / x_ref.shape[1])
    rs_scr[...] = jnp.broadcast_to(lax.rsqrt(ms + EPS), rs_scr.shape)


def _normed_tile(x_ref, rs_scr, a_scr, s_scr, r0, c):
    cs = slice(c * LANES, (c + 1) * LANES)
    return x_ref[pl.ds(r0, NORM_ROWS), cs] * rs_scr[pl.ds(r0, NORM_ROWS), :] * a_scr[:, cs] + s_scr[:, cs]


def _norm_rope_t(xt, g_col, cos_t, sin_t):
    ms = jnp.mean(xt * xt, axis=0, keepdims=True)
    y = xt * lax.rsqrt(ms + EPS) * g_col
    q = ROPE_AXIS_DIM // 2
    partner = jnp.concatenate([y[q:2 * q], y[0:q], y[3 * q:4 * q], y[2 * q:3 * q]], axis=0)
    return y * cos_t + partner * sin_t


def _inproj_kernel(x_ref, g_ref, sc_ref, sh_ref, w_ref, cos_ref, sin_ref, qg_ref, kg_ref,
                   pool_ref, qt_ref, k_ref, vt_ref, h_scr, rs_scr, a_scr, s_scr):
    j = pl.program_id(1)

    @pl.when(j == 0)
    def _():
        _prep_modulation(g_ref, sc_ref, sh_ref, a_scr, s_scr)
        _row_rms(x_ref, rs_scr, IP_TM)

        def body(r, _):
            r0 = pl.multiple_of(r * NORM_ROWS, NORM_ROWS)
            for c in range(D_MODEL // LANES):
                h_scr[pl.ds(r0, NORM_ROWS), c * LANES:(c + 1) * LANES] = _normed_tile(
                    x_ref, rs_scr, a_scr, s_scr, r0, c).astype(BF16)
            return 0
        lax.fori_loop(0, IP_TM // NORM_ROWS, body, 0)

    acc = jnp.dot(h_scr[...], w_ref[...], preferred_element_type=F32)

    @pl.when(j < IP_J_Q)
    def _():
        pool_ref[...] = acc.astype(BF16)

    @pl.when((j >= IP_J_Q) & (j < IP_J_KV))
    def _():
        for hh in range(IP_TN // HEAD_DIM):
            sl = slice(hh * HEAD_DIM, (hh + 1) * HEAD_DIM)
            r = _norm_rope_t(acc[:, sl].T, qg_ref[...], cos_ref[...], sin_ref[...]) * Q_SCALE
            qt_ref[sl, :] = r.astype(BF16)

    @pl.when(j == IP_J_KV)
    def _():
        lane = lax.broadcasted_iota(I32, (IP_TM, HEAD_DIM), 1)
        one_col = jnp.where(lane == 0, 1.0, 0.0).astype(BF16)
        for hh in range(N_KV_HEADS):
            sl = slice(hh * HEAD_DIM, (hh + 1) * HEAD_DIM)
            k_ref[:, hh * KA_W:hh * KA_W + HEAD_DIM] = _norm_rope_t(
                acc[:, sl].T, kg_ref[...], cos_ref[...], sin_ref[...]).T.astype(BF16)
            k_ref[:, hh * KA_W + HEAD_DIM:(hh + 1) * KA_W] = one_col
        for hh in range(N_KV_HEADS):
            sl = slice(KV_WIDTH + hh * HEAD_DIM, KV_WIDTH + (hh + 1) * HEAD_DIM)
            vt_ref[hh * VA_H:hh * VA_H + HEAD_DIM, :] = acc[:, sl].T.astype(BF16)
            vt_ref[hh * VA_H + HEAD_DIM:(hh + 1) * VA_H, :] = jnp.ones((VA_H - HEAD_DIM, IP_TM), BF16)


def _inproj(x2, mod, norm1_g, w_in_b, cos_t, sin_t, qg, kg):
    row = lambda n: pl.BlockSpec((1, D_MODEL), lambda i, j, n=n: (0, n))
    return pl.pallas_call(
        _inproj_kernel,
        out_shape=(
            jax.ShapeDtypeStruct((SEQ, POOL_WIDTH), BF16),
            jax.ShapeDtypeStruct((ATTN_WIDTH, SEQ), BF16),
            jax.ShapeDtypeStruct((SEQ, N_KV_HEADS * KA_W), BF16),
            jax.ShapeDtypeStruct((N_KV_HEADS * VA_H, SEQ), BF16),
        ),
        grid=(SEQ // IP_TM, IP_NJ),
        in_specs=[
            pl.BlockSpec((IP_TM, D_MODEL), lambda i, j: (i, 0)),
            pl.BlockSpec((1, D_MODEL), lambda i, j: (0, 0)),
            row(1), row(0),
            pl.BlockSpec((D_MODEL, IP_TN), lambda i, j: (0, j)),
            pl.BlockSpec((HEAD_DIM, IP_TM), lambda i, j: (0, i)),
            pl.BlockSpec((HEAD_DIM, IP_TM), lambda i, j: (0, i)),
            pl.BlockSpec((HEAD_DIM, 1), lambda i, j: (0, 0)),
            pl.BlockSpec((HEAD_DIM, 1), lambda i, j: (0, 0)),
        ],
        out_specs=(
            pl.BlockSpec((IP_TM, IP_TN), lambda i, j: (i, jnp.minimum(j, IP_J_Q - 1))),
            pl.BlockSpec((IP_TN, IP_TM), lambda i, j: (jnp.clip(j - IP_J_Q, 0, IP_J_KV - IP_J_Q - 1), i)),
            pl.BlockSpec((IP_TM, N_KV_HEADS * KA_W), lambda i, j: (i, 0)),
            pl.BlockSpec((N_KV_HEADS * VA_H, IP_TM), lambda i, j: (0, i)),
        ),
        scratch_shapes=[
            pltpu.VMEM((IP_TM, D_MODEL), BF16),
            pltpu.VMEM((IP_TM, LANES), F32),
            pltpu.VMEM((NORM_ROWS, D_MODEL), F32),
            pltpu.VMEM((NORM_ROWS, D_MODEL), F32),
        ],
        compiler_params=_params(("arbitrary", "arbitrary")),
        name="inproj",
    )(x2, norm1_g, mod, mod, w_in_b, cos_t, sin_t, qg, kg)


AT_TQ = 512
AT_TK = 8192
AT_TK_ONLINE = 512
SHIFT_LIMIT = 60.0


def _attn_kernel(qt_ref, k_ref, vt_ref, o_ref, qa_scr, p_scr, kmax_scr):
    h = pl.program_id(0)
    i = pl.program_id(1)

    @pl.when((i == 0) & (h % GROUP == 0))
    def _():
        def body(c, mx):
            c0 = pl.multiple_of(c * AT_TK, AT_TK)
            kc = k_ref[pl.ds(c0, AT_TK), :HEAD_DIM].astype(F32)
            n2 = (kc * kc).sum(axis=1, keepdims=True)
            return jnp.maximum(mx, n2.max(axis=0, keepdims=True))
        mx = lax.fori_loop(0, SEQ // AT_TK, body, jnp.zeros((1, 1), F32))
        kmax_scr[...] = jnp.broadcast_to(jnp.sqrt(mx), kmax_scr.shape)

    q = qt_ref[...].astype(F32)
    bound = jnp.sqrt((q * q).sum(axis=0, keepdims=True)) * kmax_scr[0:1, 0:1] * 1.01
    fast = jnp.max(bound) <= SHIFT_LIMIT

    @pl.when(fast)
    def _():
        qa_scr[0:HEAD_DIM, :] = qt_ref[...]
        row = lax.broadcasted_iota(I32, (KA_W - HEAD_DIM, AT_TQ), 0)
        qa_scr[HEAD_DIM:, :] = jnp.where(row == 0, -bound, 0.0).astype(BF16)

        def body(c, _):
            c0 = pl.multiple_of(c * AT_TK, AT_TK)
            s = jnp.dot(k_ref[pl.ds(c0, AT_TK), :], qa_scr[...], preferred_element_type=F32)
            p_scr[pl.ds(c0, AT_TK), :] = jnp.exp2(s).astype(BF16)
            return 0

        lax.fori_loop(0, SEQ // AT_TK, body, 0)
        o = jnp.dot(vt_ref[...], p_scr[...], preferred_element_type=F32)
        o_ref[...] = (o[:HEAD_DIM] * (1.0 / o[HEAD_DIM:HEAD_DIM + 1])).T.astype(BF16)

    @pl.when(jnp.logical_not(fast))
    def _():
        qt = qt_ref[...]

        def chunk(c, carry):
            m, l, acc = carry
            c0 = pl.multiple_of(c * AT_TK_ONLINE, AT_TK_ONLINE)
            s = jnp.dot(k_ref[pl.ds(c0, AT_TK_ONLINE), :HEAD_DIM], qt, preferred_element_type=F32)
            m_new = jnp.maximum(m, s.max(axis=0, keepdims=True))
            alpha = jnp.exp2(m - m_new)
            p = jnp.exp2(s - m_new)
            l = alpha * l + p.sum(axis=0, keepdims=True)
            pv = jnp.dot(vt_ref[:HEAD_DIM, pl.ds(c0, AT_TK_ONLINE)], p.astype(BF16), preferred_element_type=F32)
            return m_new, l, alpha * acc + pv

        init = (jnp.full((1, AT_TQ), -jnp.inf, F32), jnp.zeros((1, AT_TQ), F32),
                jnp.zeros((HEAD_DIM, AT_TQ), F32))
        _, l, acc = lax.fori_loop(0, SEQ // AT_TK_ONLINE, chunk, init)
        o_ref[...] = (acc * (1.0 / l)).T.astype(BF16)


def _attention(qt, k, vt):
    return pl.pallas_call(
        _attn_kernel,
        out_shape=jax.ShapeDtypeStruct((SEQ, ATTN_WIDTH), BF16),
        grid=(N_HEADS, SEQ // AT_TQ),
        in_specs=[
            pl.BlockSpec((HEAD_DIM, AT_TQ), lambda h, i: (h, i)),
            pl.BlockSpec((SEQ, KA_W), lambda h, i: (0, h // GROUP)),
            pl.BlockSpec((VA_H, SEQ), lambda h, i: (h // GROUP, 0)),
        ],
        out_specs=pl.BlockSpec((AT_TQ, HEAD_DIM), lambda h, i: (i, h)),
        scratch_shapes=[
            pltpu.VMEM((KA_W, AT_TQ), BF16),
            pltpu.VMEM((SEQ, AT_TQ), BF16),
            pltpu.VMEM((8, LANES), F32),
        ],
        compiler_params=_params(("arbitrary", "arbitrary")),
        name="attn",
    )(qt, k, vt)


PL_TM = 256
PL_HALO = 16


def _pool_kernel(prev_ref, main_ref, next_ref, wp_ref, scale_ref, o_ref, buf):
    i = pl.program_id(0)
    last = pl.num_programs(0) - 1
    buf[0:PL_HALO, :] = jnp.where(i == 0, 0.0, prev_ref[...].astype(F32))
    buf[PL_HALO:PL_HALO + PL_TM, :] = main_ref[...].astype(F32)
    buf[PL_HALO + PL_TM:, :] = jnp.where(i == last, 0.0, next_ref[...].astype(F32))
    t = i * PL_TM + lax.broadcasted_iota(I32, (PL_TM, 1), 0)
    for gi, w in enumerate(POOL_WINDOWS):
        cols = slice(gi * POOL_GROUP_WIDTH, (gi + 1) * POOL_GROUP_WIDTH)
        win = buf[PL_HALO - w // 2:PL_HALO - w // 2 + PL_TM, cols]
        for d in range(-w // 2 + 1, w // 2):
            win = win + buf[PL_HALO + d:PL_HALO + d + PL_TM, cols]
        lo = jnp.maximum(t - w // 2, 0)
        hi = jnp.minimum(t + w // 2 - 1, SEQ - 1)
        cnt = (hi - lo + 1).astype(F32)
        dlt = win / cnt - buf[PL_HALO:PL_HALO + PL_TM, cols]
        y = jnp.dot(dlt.astype(BF16), wp_ref[gi], preferred_element_type=F32)
        o_ref[:, cols] = (y * scale_ref[:, cols]).astype(BF16)


def _pool(pool_in, w_pool_b, pool_scale):
    nh = PL_TM // PL_HALO
    n_halo_blocks = SEQ // PL_HALO
    return pl.pallas_call(
        _pool_kernel,
        out_shape=jax.ShapeDtypeStruct((SEQ, POOL_WIDTH), BF16),
        grid=(SEQ // PL_TM,),
        in_specs=[
            pl.BlockSpec((PL_HALO, POOL_WIDTH), lambda i: (jnp.maximum(i * nh - 1, 0), 0)),
            pl.BlockSpec((PL_TM, POOL_WIDTH), lambda i: (i, 0)),
            pl.BlockSpec((PL_HALO, POOL_WIDTH), lambda i: (jnp.minimum((i + 1) * nh, n_halo_blocks - 1), 0)),
            pl.BlockSpec((len(POOL_WINDOWS), POOL_GROUP_WIDTH, POOL_GROUP_WIDTH), lambda i: (0, 0, 0)),
            pl.BlockSpec((1, POOL_WIDTH), lambda i: (0, 0)),
        ],
        out_specs=pl.BlockSpec((PL_TM, POOL_WIDTH), lambda i: (i, 0)),
        scratch_shapes=[pltpu.VMEM((PL_TM + 2 * PL_HALO, POOL_WIDTH), F32)],
        compiler_params=_params(("arbitrary",)),
        name="pool",
    )(pool_in, pool_in, pool_in, w_pool_b, pool_scale)


OP_TM = 1024
OP_TN = 512


def _outproj_kernel(a_ref, p_ref, wa_ref, wp_ref, x_ref, g_ref, o_ref):
    acc = jnp.dot(a_ref[...], wa_ref[...], preferred_element_type=F32)
    acc = acc + jnp.dot(p_ref[...], wp_ref[...], preferred_element_type=F32)
    o_ref[...] = x_ref[...] + g_ref[...] * acc


def _outproj(attn, pool, w_out_b, x2, mod):
    return pl.pallas_call(
        _outproj_kernel,
        out_shape=jax.ShapeDtypeStruct((SEQ, D_MODEL), F32),
        grid=(SEQ // OP_TM, D_MODEL // OP_TN),
        in_specs=[
            pl.BlockSpec((OP_TM, ATTN_WIDTH), lambda i, j: (i, 0)),
            pl.BlockSpec((OP_TM, POOL_WIDTH), lambda i, j: (i, 0)),
            pl.BlockSpec((ATTN_WIDTH, OP_TN), lambda i, j: (0, j)),
            pl.BlockSpec((POOL_WIDTH, OP_TN), lambda i, j: (1, j)),
            pl.BlockSpec((OP_TM, OP_TN), lambda i, j: (i, j)),
            pl.BlockSpec((1, OP_TN), lambda i, j: (0, 2 * (D_MODEL // OP_TN) + j)),
        ],
        out_specs=pl.BlockSpec((OP_TM, OP_TN), lambda i, j: (i, j)),
        compiler_params=_params(("arbitrary", "arbitrary")),
        name="outproj",
    )(attn, pool, w_out_b, w_out_b, x2, mod)


N2_TM = 256
RT_PAD = LANES


def _norm2_kernel(x_ref, g_ref, sc_ref, sh_ref, wcat_ref, whi_ref, b_ref, lt_ref, hi_scr, lo_scr,
                  rs_scr, a_scr, s_scr):
    @pl.when(pl.program_id(0) == 0)
    def _():
        _prep_modulation(g_ref, sc_ref, sh_ref, a_scr, s_scr)

    _row_rms(x_ref, rs_scr, N2_TM)

    def body(r, _):
        r0 = pl.multiple_of(r * NORM_ROWS, NORM_ROWS)
        for c in range(D_MODEL // LANES):
            cs = slice(c * LANES, (c + 1) * LANES)
            h = _normed_tile(x_ref, rs_scr, a_scr, s_scr, r0, c)
            hi = h.astype(BF16)
            hi_scr[pl.ds(r0, NORM_ROWS), cs] = hi
            lo_scr[pl.ds(r0, NORM_ROWS), cs] = (h - hi.astype(F32)).astype(BF16)
        return 0
    lax.fori_loop(0, N2_TM // NORM_ROWS, body, 0)
    a = jnp.dot(hi_scr[...], wcat_ref[...], preferred_element_type=F32)
    b = jnp.dot(lo_scr[...], whi_ref[...], preferred_element_type=F32)
    logits = a[:, :RT_PAD] + a[:, RT_PAD:] + b + b_ref[...]
    lt_ref[...] = logits.T[:N_EXPERTS, :]


def _norm2(x1, norm2_g, mod, wcat, whi, b_pad):
    row = lambda n: pl.BlockSpec((1, D_MODEL), lambda i, n=n: (0, n))
    return pl.pallas_call(
        _norm2_kernel,
        out_shape=jax.ShapeDtypeStruct((N_EXPERTS, SEQ), F32),
        grid=(SEQ // N2_TM,),
        in_specs=[
            pl.BlockSpec((N2_TM, D_MODEL), lambda i: (i, 0)),
            pl.BlockSpec((1, D_MODEL), lambda i: (0, 0)),
            row(4), row(3),
            pl.BlockSpec((D_MODEL, 2 * RT_PAD), lambda i: (0, 0)),
            pl.BlockSpec((D_MODEL, RT_PAD), lambda i: (0, 0)),
            pl.BlockSpec((1, RT_PAD), lambda i: (0, 0)),
        ],
        out_specs=pl.BlockSpec((N_EXPERTS, N2_TM), lambda i: (0, i)),
        scratch_shapes=[
            pltpu.VMEM((N2_TM, D_MODEL), BF16),
            pltpu.VMEM((N2_TM, D_MODEL), BF16),
            pltpu.VMEM((N2_TM, LANES), F32),
            pltpu.VMEM((NORM_ROWS, D_MODEL), F32),
            pltpu.VMEM((NORM_ROWS, D_MODEL), F32),
        ],
        compiler_params=_params(("arbitrary",)),
        name="norm2",
    )(x1, norm2_g, mod, mod, wcat, whi, b_pad)


RT_CH = 1024
RT_SB = 256


def _route_kernel(lt_ref, dest_ref, gate_ref, nblk_ref, idx_scr, rank_scr):
    e_col = lax.broadcasted_iota(I32, (N_EXPERTS, RT_CH), 0).astype(F32)
    tri = (lax.broadcasted_iota(I32, (RT_SB, RT_SB), 0) < lax.broadcasted_iota(I32, (RT_SB, RT_SB), 1)).astype(BF16)
    carry = jnp.zeros((N_EXPERTS, 1), F32)
    for c in range(SEQ // RT_CH):
        cs = slice(c * RT_CH, (c + 1) * RT_CH)
        work = lt_ref[:, cs]
        vals = []
        mask = jnp.zeros((N_EXPERTS, RT_CH), F32)
        for k in range(TOP_K):
            m = work.max(axis=0, keepdims=True)
            idx = jnp.where(work == m, e_col, float(N_EXPERTS)).min(axis=0, keepdims=True)
            sel = e_col == idx
            vals.append(m)
            idx_scr[k:k + 1, cs] = idx
            mask = jnp.where(sel, 1.0, mask)
            work = jnp.where(sel, -jnp.inf, work)
        ex = [jnp.exp(v - vals[0]) for v in vals]
        den = ex[0] + ex[1] + ex[2] + ex[3]
        for k in range(TOP_K):
            gate_ref[k:k + 1, cs] = ex[k] / den
        for b in range(RT_CH // RT_SB):
            blk = mask[:, b * RT_SB:(b + 1) * RT_SB]
            pref = jnp.dot(blk.astype(BF16), tri, preferred_element_type=F32)
            rank_scr[:, c * RT_CH + b * RT_SB:c * RT_CH + (b + 1) * RT_SB] = pref + carry
            carry = carry + blk.sum(axis=1, keepdims=True)
    nblk = jnp.floor((carry + (MOE_TM - 1)) * (1.0 / MOE_TM))
    nblk_b = jnp.broadcast_to(nblk, (N_EXPERTS, LANES))
    lower = (lax.broadcasted_iota(I32, (N_EXPERTS, N_EXPERTS), 1) < lax.broadcasted_iota(I32, (N_EXPERTS, N_EXPERTS), 0)).astype(BF16)
    start_blk = jnp.dot(lower, nblk_b.astype(BF16), preferred_element_type=F32)
    nblk_ref[...] = nblk_b.astype(I32)
    start = start_blk[:, 0:1] * float(MOE_TM)
    for c in range(SEQ // RT_CH):
        cs = slice(c * RT_CH, (c + 1) * RT_CH)
        slot = rank_scr[:, cs] + start
        for k in range(TOP_K):
            sel = e_col == idx_scr[k:k + 1, cs]
            dest_ref[k:k + 1, cs] = jnp.where(sel, slot, 0.0).sum(axis=0, keepdims=True).astype(I32)


def _route(logits_t):
    return pl.pallas_call(
        _route_kernel,
        out_shape=(
            jax.ShapeDtypeStruct((TOP_K, SEQ), I32),
            jax.ShapeDtypeStruct((TOP_K, SEQ), F32),
            jax.ShapeDtypeStruct((N_EXPERTS, LANES), I32),
        ),
        scratch_shapes=[pltpu.VMEM((8, SEQ), F32), pltpu.VMEM((N_EXPERTS, SEQ), F32)],
        compiler_params=pltpu.CompilerParams(vmem_limit_bytes=VMEM_LIMIT),
        name="route",
    )(logits_t)


DP_TM = 256
XS_W = D_MODEL // 2
U32 = jnp.uint32


def _dispatch_kernel(dest_ref, lastblk_ref, x_ref, g_ref, sc_ref, sh_ref, xs_hbm, pk, zero_buf, rs_scr, a_scr,
                     s_scr, zsem, sem):
    i = pl.program_id(0)
    par = i % 2

    @pl.when(i == 0)
    def _():
        _prep_modulation(g_ref, sc_ref, sh_ref, a_scr, s_scr)
        zero_buf[...] = jnp.zeros_like(zero_buf)

        def zcopy(e):
            b = jnp.maximum(lastblk_ref[e], 0)
            return pltpu.make_async_copy(zero_buf, xs_hbm.at[pl.ds(pl.multiple_of(b * MOE_TM, MOE_TM), MOE_TM)], zsem)

        def zstart(e, _):
            @pl.when(lastblk_ref[e] >= 0)
            def _():
                zcopy(e).start()
            return 0

        def zwait(e, _):
            @pl.when(lastblk_ref[e] >= 0)
            def _():
                zcopy(e).wait()
            return 0

        lax.fori_loop(0, N_EXPERTS, zstart, 0)
        lax.fori_loop(0, N_EXPERTS, zwait, 0)

    _row_rms(x_ref, rs_scr, DP_TM)

    n_tiles = XS_W // LANES
    assert n_tiles == NORM_ROWS

    def pack_tile(r0, c):
        lo = _normed_tile(x_ref, rs_scr, a_scr, s_scr, r0, c)
        hi = _normed_tile(x_ref, rs_scr, a_scr, s_scr, r0, c + n_tiles)
        lo = lax.bitcast_convert_type(lo.astype(BF16).astype(F32), U32)
        hi = lax.bitcast_convert_type(hi.astype(BF16).astype(F32), U32)
        pk[par, pl.ds(r0, NORM_ROWS), c * LANES:(c + 1) * LANES] = (lo >> 16) | (hi & jnp.uint32(0xFFFF0000))

    def issue_token(u):
        for k in range(TOP_K):
            d = dest_ref[k * SEQ + i * DP_TM + u]
            pltpu.make_async_copy(pk.at[par, pl.ds(u, 1)], xs_hbm.at[pl.ds(d, 1)], sem.at[par]).start()

    for c in range(n_tiles):
        pack_tile(0, c)

    def group(g, _):
        r0 = pl.multiple_of(g * NORM_ROWS, NORM_ROWS)
        for u in range(NORM_ROWS):
            pack_tile(r0, u)
            issue_token(r0 - NORM_ROWS + u)
        return 0

    lax.fori_loop(1, DP_TM // NORM_ROWS, group, 0)

    def tail(u, _):
        issue_token(DP_TM - NORM_ROWS + u)
        return 0

    lax.fori_loop(0, NORM_ROWS, tail, 0)

    def drain(p):
        for _ in range(TOP_K):
            pltpu.make_async_copy(pk.at[p], xs_hbm.at[pl.ds(0, DP_TM)], sem.at[p]).wait()

    @pl.when(i > 0)
    def _():
        drain(1 - par)

    @pl.when(i == pl.num_programs(0) - 1)
    def _():
        drain(par)


def _dispatch(dest_flat, lastblk, x1, norm2_g, mod):
    row = lambda n: pl.BlockSpec((1, D_MODEL), lambda i, d, lb, n=n: (0, n))
    return pl.pallas_call(
        _dispatch_kernel,
        out_shape=jax.ShapeDtypeStruct((MOE_ROWS, XS_W), U32),
        grid_spec=pltpu.PrefetchScalarGridSpec(
            num_scalar_prefetch=2,
            grid=(SEQ // DP_TM,),
            in_specs=[
                pl.BlockSpec((DP_TM, D_MODEL), lambda i, d, lb: (i, 0)),
                pl.BlockSpec((1, D_MODEL), lambda i, d, lb: (0, 0)),
                row(4), row(3),
            ],
            out_specs=pl.BlockSpec(memory_space=pl.ANY),
            scratch_shapes=[
                pltpu.VMEM((2, DP_TM, XS_W), U32),
                pltpu.VMEM((MOE_TM, XS_W), U32),
                pltpu.VMEM((DP_TM, LANES), F32),
                pltpu.VMEM((NORM_ROWS, D_MODEL), F32),
                pltpu.VMEM((NORM_ROWS, D_MODEL), F32),
                pltpu.SemaphoreType.DMA,
                pltpu.SemaphoreType.DMA((2,)),
            ],
        ),
        compiler_params=_params(("arbitrary",)),
        name="dispatch",
    )(dest_flat, lastblk, x1, norm2_g, mod, mod)


F1_TF = 512
F2_TN = 4096


CAST_ROWS = 128
WEIGHT_DMA_PRIORITY = 1


def _expert_schedule(nblk):
    end_blk = jnp.cumsum(nblk)
    n_used = end_blk[-1]
    blocks = jnp.arange(MOE_NB, dtype=I32)
    be = jnp.minimum(jnp.sum(end_blk[None, :] <= blocks[:, None], axis=1), N_EXPERTS - 1).astype(I32)
    first = (blocks < n_used) & (be != jnp.concatenate([jnp.full((1,), -1, I32), be[:-1]]))
    seg = jnp.cumsum(first.astype(I32)) - 1
    seg_end = end_blk[be]
    nxt = jnp.where(seg_end < n_used, be[jnp.minimum(seg_end, MOE_NB - 1)], -1)
    meta = jnp.stack([n_used, jnp.sum(first.astype(I32))]).astype(I32)
    lastblk = jnp.where(nblk > 0, end_blk - 1, -1).astype(I32)
    return meta, be, first.astype(I32), seg.astype(I32), nxt.astype(I32), lastblk


def _stream_expert_weights(j, i, nj, meta_ref, be_ref, first_ref, seg_ref, nxt_ref, tile_copies, stg, wbuf):
    @pl.when(first_ref[i] == 1)
    def _():
        seq = j * meta_ref[1] + seg_ref[i]
        slot = seq % 2

        @pl.when(seq == 0)
        def _():
            for cp in tile_copies(be_ref[i], j, slot):
                cp.start(priority=WEIGHT_DMA_PRIORITY)

        for cp in tile_copies(be_ref[i], j, slot):
            cp.wait()

        nxt = nxt_ref[i]

        @pl.when(nxt >= 0)
        def _():
            for cp in tile_copies(nxt, j, 1 - slot):
                cp.start(priority=WEIGHT_DMA_PRIORITY)

        @pl.when((nxt < 0) & (j + 1 < nj))
        def _():
            for cp in tile_copies(be_ref[0], j + 1, 1 - slot):
                cp.start(priority=WEIGHT_DMA_PRIORITY)

        def cast(r, _):
            r0 = pl.multiple_of(r * CAST_ROWS, CAST_ROWS)
            wbuf[pl.ds(r0, CAST_ROWS), :] = stg[slot, pl.ds(r0, CAST_ROWS), :].astype(BF16)
            return 0

        lax.fori_loop(0, wbuf.shape[0] // CAST_ROWS, cast, 0)


def _ffn1_kernel(meta_ref, be_ref, first_ref, seg_ref, nxt_ref, x_ref, bg_ref, bl_ref, w1_hbm, o_ref,
                 stg, wbuf, sem):
    j = pl.program_id(0)
    i = pl.program_id(1)
    nj = pl.num_programs(0)

    def tile_copies(e, jj, slot):
        cg = pl.multiple_of(jj * F1_TF, F1_TF)
        cl = pl.multiple_of(D_FF + jj * F1_TF, F1_TF)
        return (
            pltpu.make_async_copy(w1_hbm.at[e, :, pl.ds(cg, F1_TF)], stg.at[slot, :, 0:F1_TF], sem.at[slot]),
            pltpu.make_async_copy(w1_hbm.at[e, :, pl.ds(cl, F1_TF)], stg.at[slot, :, F1_TF:2 * F1_TF], sem.at[slot]),
        )

    @pl.when(i < meta_ref[0])
    def _():
        _stream_expert_weights(j, i, nj, meta_ref, be_ref, first_ref, seg_ref, nxt_ref, tile_copies, stg, wbuf)
        xp = x_ref[...]
        x_lo = lax.bitcast_convert_type(xp << 16, F32).astype(BF16)
        x_hi = lax.bitcast_convert_type(xp & jnp.uint32(0xFFFF0000), F32).astype(BF16)
        y = jnp.dot(x_lo, wbuf[:XS_W, :], preferred_element_type=F32)
        y = y + jnp.dot(x_hi, wbuf[XS_W:, :], preferred_element_type=F32)
        glu = jnp.minimum(y[:, :F1_TF] + bg_ref[...], SWIGLU_LIMIT)
        lin = jnp.clip(y[:, F1_TF:] + bl_ref[...], -SWIGLU_LIMIT, SWIGLU_LIMIT)
        o_ref[...] = (glu * jax.nn.sigmoid(SWIGLU_ALPHA * glu) * (lin + 1.0)).astype(BF16)


def _ffn1(sched, xs, w1, b1_3):
    meta, be, first, seg, nxt, _ = sched
    nj = D_FF // F1_TF
    blk = lambda i, meta: jnp.minimum(i, meta[0] - 1)
    return pl.pallas_call(
        _ffn1_kernel,
        out_shape=jax.ShapeDtypeStruct((MOE_ROWS, D_FF), BF16),
        grid_spec=pltpu.PrefetchScalarGridSpec(
            num_scalar_prefetch=5,
            grid=(nj, MOE_NB),
            in_specs=[
                pl.BlockSpec((MOE_TM, XS_W), lambda j, i, meta, be, *_: (blk(i, meta), 0)),
                pl.BlockSpec((None, 1, F1_TF), lambda j, i, meta, be, *_: (be[blk(i, meta)], 0, j)),
                pl.BlockSpec((None, 1, F1_TF), lambda j, i, meta, be, *_: (be[blk(i, meta)], 0, nj + j)),
                pl.BlockSpec(memory_space=pl.ANY),
            ],
            out_specs=pl.BlockSpec((MOE_TM, F1_TF), lambda j, i, meta, be, *_: (blk(i, meta), j)),
            scratch_shapes=[
                pltpu.VMEM((2, D_MODEL, 2 * F1_TF), F32),
                pltpu.VMEM((D_MODEL, 2 * F1_TF), BF16),
                pltpu.SemaphoreType.DMA((2,)),
            ],
        ),
        compiler_params=_params(("arbitrary", "arbitrary")),
        name="ffn1",
    )(meta, be, first, seg, nxt, xs, b1_3, b1_3, w1)


def _ffn2_kernel(meta_ref, be_ref, first_ref, seg_ref, nxt_ref, a_ref, b_ref, w2_hbm, o_ref, stg, wbuf, sem):
    j = pl.program_id(0)
    i = pl.program_id(1)
    nj = pl.num_programs(0)

    def tile_copies(e, jj, slot):
        c0 = pl.multiple_of(jj * F2_TN, F2_TN)
        return (pltpu.make_async_copy(w2_hbm.at[e, :, pl.ds(c0, F2_TN)], stg.at[slot], sem.at[slot]),)

    @pl.when(i < meta_ref[0])
    def _():
        _stream_expert_weights(j, i, nj, meta_ref, be_ref, first_ref, seg_ref, nxt_ref, tile_copies, stg, wbuf)
        o_ref[...] = jnp.dot(a_ref[...], wbuf[...], preferred_element_type=F32) + b_ref[...]


def _ffn2(sched, act, w2, b2_3):
    meta, be, first, seg, nxt, _ = sched
    blk = lambda i, meta: jnp.minimum(i, meta[0] - 1)
    return pl.pallas_call(
        _ffn2_kernel,
        out_shape=jax.ShapeDtypeStruct((MOE_ROWS, D_MODEL), F32),
        grid_spec=pltpu.PrefetchScalarGridSpec(
            num_scalar_prefetch=5,
            grid=(D_MODEL // F2_TN, MOE_NB),
            in_specs=[
                pl.BlockSpec((MOE_TM, D_FF), lambda j, i, meta, be, *_: (blk(i, meta), 0)),
                pl.BlockSpec((None, 1, F2_TN), lambda j, i, meta, be, *_: (be[blk(i, meta)], 0, j)),
                pl.BlockSpec(memory_space=pl.ANY),
            ],
            out_specs=pl.BlockSpec((MOE_TM, F2_TN), lambda j, i, meta, be, *_: (blk(i, meta), j)),
            scratch_shapes=[
                pltpu.VMEM((2, D_FF, F2_TN), F32),
                pltpu.VMEM((D_FF, F2_TN), BF16),
                pltpu.SemaphoreType.DMA((2,)),
            ],
        ),
        compiler_params=_params(("arbitrary", "arbitrary")),
        name="ffn2",
    )(meta, be, first, seg, nxt, act, b2_3, w2)


CB_TM = 128


def _combine_kernel(dest_ref, x_ref, gate_ref, g2_ref, ys_hbm, o_ref, buf, gate_scr, g2_scr, sem):
    i = pl.program_id(0)
    last = pl.num_programs(0) - 1
    par = i % 2
    n_tiles = D_MODEL // LANES
    assert n_tiles == 8 * TOP_K

    def row_start(tile, p, u, k):
        d = dest_ref[k * SEQ + tile * CB_TM + u]
        pltpu.make_async_copy(ys_hbm.at[pl.ds(d, 1)], buf.at[p, k, pl.ds(u, 1)], sem.at[p]).start()

    def drain(p):
        for k in range(TOP_K):
            pltpu.make_async_copy(ys_hbm.at[pl.ds(0, CB_TM)], buf.at[p, k], sem.at[p]).wait()

    @pl.when(i == 0)
    def _():
        def body(u, _):
            for k in range(TOP_K):
                row_start(0, 0, u, k)
            return 0
        lax.fori_loop(0, CB_TM, body, 0)
        g2_scr[...] = jnp.broadcast_to(g2_ref[...], g2_scr.shape)

    drain(par)

    for k in range(TOP_K):
        gate_scr[k] = jnp.broadcast_to(gate_ref[:, k:k + 1], (CB_TM, LANES))

    nxt = jnp.minimum(i + 1, last)

    def body(r, _):
        r0 = pl.multiple_of(r * 8, 8)
        rows = pl.ds(r0, 8)
        gk = [gate_scr[k, rows, :] for k in range(TOP_K)]
        for c in range(n_tiles):
            cs = slice(c * LANES, (c + 1) * LANES)
            y = buf[par, 0, rows, cs] * gk[0]
            for k in range(1, TOP_K):
                y = y + buf[par, k, rows, cs] * gk[k]
            o_ref[rows, cs] = x_ref[rows, cs] + g2_scr[:, cs] * y
            row_start(nxt, 1 - par, r0 + c // TOP_K, c % TOP_K)
        return 0

    lax.fori_loop(0, CB_TM // 8, body, 0)

    @pl.when(i == last)
    def _():
        drain(1 - par)


def _combine(dest_flat, x1, gates_t, mod, ys):
    return pl.pallas_call(
        _combine_kernel,
        out_shape=jax.ShapeDtypeStruct((SEQ, D_MODEL), F32),
        grid_spec=pltpu.PrefetchScalarGridSpec(
            num_scalar_prefetch=1,
            grid=(SEQ // CB_TM,),
            in_specs=[
                pl.BlockSpec((CB_TM, D_MODEL), lambda i, d: (i, 0)),
                pl.BlockSpec((CB_TM, TOP_K), lambda i, d: (i, 0)),
                pl.BlockSpec((1, D_MODEL), lambda i, d: (0, 5)),
                pl.BlockSpec(memory_space=pl.ANY),
            ],
            out_specs=pl.BlockSpec((CB_TM, D_MODEL), lambda i, d: (i, 0)),
            scratch_shapes=[
                pltpu.VMEM((2, TOP_K, CB_TM, D_MODEL), F32),
                pltpu.VMEM((TOP_K, CB_TM, LANES), F32),
                pltpu.VMEM((8, D_MODEL), F32),
                pltpu.SemaphoreType.DMA((2,)),
            ],
        ),
        compiler_params=_params(("arbitrary",)),
        name="combine",
    )(dest_flat, x1, gates_t, mod, ys)


def _rope_tables():
    t = jnp.arange(SEQ, dtype=I32)
    row = (t // GRID_W).astype(F32)
    col = (t % GRID_W).astype(F32)
    inv_freq = ROPE_THETA ** (-jnp.arange(0, ROPE_AXIS_DIM, 2, dtype=F32) / ROPE_AXIS_DIM)
    ang_r = inv_freq[:, None] * row[None, :]
    ang_c = inv_freq[:, None] * col[None, :]
    cos_t = jnp.concatenate([jnp.cos(ang_r), jnp.cos(ang_r), jnp.cos(ang_c), jnp.cos(ang_c)], axis=0)
    sin_t = jnp.concatenate([-jnp.sin(ang_r), jnp.sin(ang_r), -jnp.sin(ang_c), jnp.sin(ang_c)], axis=0)
    return cos_t, sin_t


def kernel(x, c, w_mod, b_mod, norm1_g, w_in, q_norm_g, k_norm_g, w_pool, pool_scale, w_out, norm2_g,
           w_router, b_router, w1, b1, w2, b2):
    assert x.shape == (1, SEQ, D_MODEL) and w_mod.shape[0] == 1
    x2 = x[0]
    cos_t, sin_t = _rope_tables()

    mod = _mod(c.reshape(D_MODEL, 1), w_mod[0], b_mod)

    pool_in, qt, k, vt = _inproj(x2, mod, norm1_g, w_in[0].astype(BF16), cos_t, sin_t,
                                 q_norm_g.reshape(HEAD_DIM, 1), k_norm_g.reshape(HEAD_DIM, 1))
    attn = _attention(qt, k, vt)
    pool = _pool(pool_in, w_pool[0].astype(BF16), pool_scale)
    x1 = _outproj(attn, pool, w_out[0].astype(BF16), x2, mod)

    wr = w_router[0]
    wr_hi = wr.astype(BF16)
    wr_lo = (wr - wr_hi.astype(F32)).astype(BF16)
    pad = lambda a: jnp.pad(a, ((0, 0), (0, RT_PAD - N_EXPERTS)))
    wcat = jnp.concatenate([pad(wr_hi), pad(wr_lo)], axis=1)
    logits_t = _norm2(x1, norm2_g, mod, wcat, pad(wr_hi), pad(b_router))

    dest, gates, nblk = _route(logits_t)
    sched = _expert_schedule(nblk[:, 0])
    dest_flat = dest.reshape(TOP_K * SEQ)

    xs = _dispatch(dest_flat, sched[-1], x1, norm2_g, mod)
    act = _ffn1(sched, xs, w1[0], b1[0].reshape(N_EXPERTS, 1, 2 * D_FF))
    ys = _ffn2(sched, act, w2[0], b2[0].reshape(N_EXPERTS, 1, D_MODEL))
    out = _combine(dest_flat, x1, gates.T, mod, ys)
    return out[None]
```

```python
import functools
import math

import jax
import jax.numpy as jnp
from jax import lax
from jax.experimental import pallas as pl
from jax.experimental.pallas import tpu as pltpu

F32 = jnp.float32
BF16 = jnp.bfloat16
I32 = jnp.int32

D_MODEL = 4096
SEQ = 8192
POOL_WIDTH = 2048
ATTN_WIDTH = 2048
HEAD_DIM = 128
N_HEADS = 16
N_KV_HEADS = 4
GROUP = N_HEADS // N_KV_HEADS
KV_WIDTH = N_KV_HEADS * HEAD_DIM
IN_WIDTH = POOL_WIDTH + ATTN_WIDTH + 2 * KV_WIDTH
POOL_WINDOWS = (2, 4, 8, 16)
POOL_GROUP_WIDTH = POOL_WIDTH // len(POOL_WINDOWS)
GRID_W = 64
ROPE_THETA = 10000.0
ROPE_AXIS_DIM = HEAD_DIM // 2
N_EXPERTS = 32
TOP_K = 4
D_FF = D_MODEL // 4
SWIGLU_ALPHA = 1.702
SWIGLU_LIMIT = 7.0
N_MOD = 6
EPS = 1e-6

LANES = 128
VMEM_LIMIT = 56 * 1024 * 1024
FFN1_VMEM_LIMIT = 60 * 1024 * 1024

Q_SCALE = (HEAD_DIM ** -0.5) * math.log2(math.e)

MOE_TM = 256
MOE_NB = SEQ * TOP_K // MOE_TM + N_EXPERTS
MOE_ROWS = MOE_NB * MOE_TM


def _params(sem, vmem=VMEM_LIMIT):
    return pltpu.CompilerParams(dimension_semantics=sem, vmem_limit_bytes=vmem)


MOD_TN = 1024
MOD_KC = 256


def _mod_kernel(c_ref, w_ref, b_ref, o_ref):
    def body(k, acc):
        r = pl.multiple_of(k * MOD_KC, MOD_KC)
        ck = c_ref[pl.ds(r, MOD_KC), :]
        ck = ck * jax.nn.sigmoid(ck)
        p = w_ref[pl.ds(r, MOD_KC), :] * ck
        return acc + p.reshape(MOD_KC // 8, 8, MOD_TN).sum(axis=0)

    acc = lax.fori_loop(0, D_MODEL // MOD_KC, body, jnp.zeros((8, MOD_TN), F32))
    o_ref[...] = acc.sum(axis=0, keepdims=True) + b_ref[...]


def _mod(c_col, w_mod, b_mod):
    n = N_MOD * D_MODEL
    return pl.pallas_call(
        _mod_kernel,
        out_shape=jax.ShapeDtypeStruct((1, n), F32),
        grid=(n // MOD_TN,),
        in_specs=[
            pl.BlockSpec((D_MODEL, 1), lambda j: (0, 0)),
            pl.BlockSpec((D_MODEL, MOD_TN), lambda j: (0, j)),
            pl.BlockSpec((1, MOD_TN), lambda j: (0, j)),
        ],
        out_specs=pl.BlockSpec((1, MOD_TN), lambda j: (0, j)),
        compiler_params=_params(("arbitrary",)),
        name="mod",
    )(c_col, w_mod, b_mod)


IP_TM = 512
IP_TN = 1024
IP_NJ = IN_WIDTH // IP_TN
IP_J_Q = POOL_WIDTH // IP_TN
IP_J_KV = IP_J_Q + ATTN_WIDTH // IP_TN
NORM_ROWS = 16
BF16_SUBLANES = 16
KA_W = 2 * HEAD_DIM
VA_H = HEAD_DIM + BF16_SUBLANES


def _prep_modulation(g_ref, sc_ref, sh_ref, a_scr, s_scr):
    a_scr[...] = jnp.broadcast_to(g_ref[...] * (1.0 + sc_ref[...]), a_scr.shape)
    s_scr[...] = jnp.broadcast_to(sh_ref[...], s_scr.shape)


def _row_rms(x_ref, rs_scr, n_rows):
    def body(r, _):
        r0 = pl.multiple_of(r * NORM_ROWS, NORM_ROWS)
        width = x_ref.shape[1]
        parts = []
        for c in range(width // LANES):
            xc = x_ref[pl.ds(r0, NORM_ROWS), c * LANES:(c + 1) * LANES]
            parts.append(xc * xc)
        while len(parts) > 1:
            parts = [parts[p] + parts[p + 1] for p in range(0, len(parts), 2)]
        rs_scr[pl.ds(r0, NORM_ROWS), :] = parts[0]
        return 0
    lax.fori_loop(0, n_rows // NORM_ROWS, body, 0)
    ms = jnp.sum(rs_scr[...], axis=-1, keepdims=True) * (1.0 / x_ref.shape[1])
    rs_scr[...] = jnp.broadcast_to(lax.rsqrt(ms + EPS), rs_scr.shape)


def _normed_tile(x_ref, rs_scr, a_scr, s_scr, r0, c):
    cs = slice(c * LANES, (c + 1) * LANES)
    return x_ref[pl.ds(r0, NORM_ROWS), cs] * rs_scr[pl.ds(r0, NORM_ROWS), :] * a_scr[:, cs] + s_scr[:, cs]


def _norm_rope_t(xt, g_col, cos_t, sin_t):
    ms = jnp.mean(xt * xt, axis=0, keepdims=True)
    y = xt * lax.rsqrt(ms + EPS) * g_col
    q = ROPE_AXIS_DIM // 2
    partner = jnp.concatenate([y[q:2 * q], y[0:q], y[3 * q:4 * q], y[2 * q:3 * q]], axis=0)
    return y * cos_t + partner * sin_t


def _inproj_kernel(x_ref, g_ref, sc_ref, sh_ref, w_ref, cos_ref, sin_ref, qg_ref, kg_ref,
                   pool_ref, qt_ref, k_ref, vt_ref, h_scr, rs_scr, a_scr, s_scr):
    j = pl.program_id(1)

    @pl.when(j == 0)
    def _():
        _prep_modulation(g_ref, sc_ref, sh_ref, a_scr, s_scr)
        _row_rms(x_ref, rs_scr, IP_TM)

        def body(r, _):
            r0 = pl.multiple_of(r * NORM_ROWS, NORM_ROWS)
            for c in range(D_MODEL // LANES):
                h_scr[pl.ds(r0, NORM_ROWS), c * LANES:(c + 1) * LANES] = _normed_tile(
                    x_ref, rs_scr, a_scr, s_scr, r0, c).astype(BF16)
            return 0
        lax.fori_loop(0, IP_TM // NORM_ROWS, body, 0)

    acc = jnp.dot(h_scr[...], w_ref[...], preferred_element_type=F32)

    @pl.when(j < IP_J_Q)
    def _():
        pool_ref[...] = acc.astype(BF16)

    @pl.when((j >= IP_J_Q) & (j < IP_J_KV))
    def _():
        for hh in range(IP_TN // HEAD_DIM):
            sl = slice(hh * HEAD_DIM, (hh + 1) * HEAD_DIM)
            r = _norm_rope_t(acc[:, sl].T, qg_ref[...], cos_ref[...], sin_ref[...]) * Q_SCALE
            qt_ref[sl, :] = r.astype(BF16)

    @pl.when(j == IP_J_KV)
    def _():
        lane = lax.broadcasted_iota(I32, (IP_TM, HEAD_DIM), 1)
        one_col = jnp.where(lane == 0, 1.0, 0.0).astype(BF16)
        for hh in range(N_KV_HEADS):
            sl = slice(hh * HEAD_DIM, (hh + 1) * HEAD_DIM)
            k_ref[:, hh * KA_W:hh * KA_W + HEAD_DIM] = _norm_rope_t(
                acc[:, sl].T, kg_ref[...], cos_ref[...], sin_ref[...]).T.astype(BF16)
            k_ref[:, hh * KA_W + HEAD_DIM:(hh + 1) * KA_W] = one_col
        for hh in range(N_KV_HEADS):
            sl = slice(KV_WIDTH + hh * HEAD_DIM, KV_WIDTH + (hh + 1) * HEAD_DIM)
            vt_ref[hh * VA_H:hh * VA_H + HEAD_DIM, :] = acc[:, sl].T.astype(BF16)
            vt_ref[hh * VA_H + HEAD_DIM:(hh + 1) * VA_H, :] = jnp.ones((VA_H - HEAD_DIM, IP_TM), BF16)


def _inproj(x2, mod, norm1_g, w_in_b, cos_t, sin_t, qg, kg):
    row = lambda n: pl.BlockSpec((1, D_MODEL), lambda i, j, n=n: (0, n))
    return pl.pallas_call(
        _inproj_kernel,
        out_shape=(
            jax.ShapeDtypeStruct((SEQ, POOL_WIDTH), BF16),
            jax.ShapeDtypeStruct((ATTN_WIDTH, SEQ), BF16),
            jax.ShapeDtypeStruct((SEQ, N_KV_HEADS * KA_W), BF16),
            jax.ShapeDtypeStruct((N_KV_HEADS * VA_H, SEQ), BF16),
        ),
        grid=(SEQ // IP_TM, IP_NJ),
        in_specs=[
            pl.BlockSpec((IP_TM, D_MODEL), lambda i, j: (i, 0)),
            pl.BlockSpec((1, D_MODEL), lambda i, j: (0, 0)),
            row(1), row(0),
            pl.BlockSpec((D_MODEL, IP_TN), lambda i, j: (0, j)),
            pl.BlockSpec((HEAD_DIM, IP_TM), lambda i, j: (0, i)),
            pl.BlockSpec((HEAD_DIM, IP_TM), lambda i, j: (0, i)),
            pl.BlockSpec((HEAD_DIM, 1), lambda i, j: (0, 0)),
            pl.BlockSpec((HEAD_DIM, 1), lambda i, j: (0, 0)),
        ],
        out_specs=(
            pl.BlockSpec((IP_TM, IP_TN), lambda i, j: (i, jnp.minimum(j, IP_J_Q - 1))),
            pl.BlockSpec((IP_TN, IP_TM), lambda i, j: (jnp.clip(j - IP_J_Q, 0, IP_J_KV - IP_J_Q - 1), i)),
            pl.BlockSpec((IP_TM, N_KV_HEADS * KA_W), lambda i, j: (i, 0)),
            pl.BlockSpec((N_KV_HEADS * VA_H, IP_TM), lambda i, j: (0, i)),
        ),
        scratch_shapes=[
            pltpu.VMEM((IP_TM, D_MODEL), BF16),
            pltpu.VMEM((IP_TM, LANES), F32),
            pltpu.VMEM((NORM_ROWS, D_MODEL), F32),
            pltpu.VMEM((NORM_ROWS, D_MODEL), F32),
        ],
        compiler_params=_params(("arbitrary", "arbitrary")),
        name="inproj",
    )(x2, norm1_g, mod, mod, w_in_b, cos_t, sin_t, qg, kg)


AT_TQ = 512
AT_TK = 8192
AT_TK_ONLINE = 512
SHIFT_LIMIT = 60.0


def _attn_kernel(qt_ref, k_ref, vt_ref, o_ref, qa_scr, p_scr, kmax_scr):
    h = pl.program_id(0)
    i = pl.program_id(1)

    @pl.when((i == 0) & (h % GROUP == 0))
    def _():
        def body(c, mx):
            c0 = pl.multiple_of(c * AT_TK, AT_TK)
            kc = k_ref[pl.ds(c0, AT_TK), :HEAD_DIM].astype(F32)
            n2 = (kc * kc).sum(axis=1, keepdims=True)
            return jnp.maximum(mx, n2.max(axis=0, keepdims=True))
        mx = lax.fori_loop(0, SEQ // AT_TK, body, jnp.zeros((1, 1), F32))
        kmax_scr[...] = jnp.broadcast_to(jnp.sqrt(mx), kmax_scr.shape)

    q = qt_ref[...].astype(F32)
    bound = jnp.sqrt((q * q).sum(axis=0, keepdims=True)) * kmax_scr[0:1, 0:1] * 1.01
    fast = jnp.max(bound) <= SHIFT_LIMIT

    @pl.when(fast)
    def _():
        qa_scr[0:HEAD_DIM, :] = qt_ref[...]
        row = lax.broadcasted_iota(I32, (KA_W - HEAD_DIM, AT_TQ), 0)
        qa_scr[HEAD_DIM:, :] = jnp.where(row == 0, -bound, 0.0).astype(BF16)

        def body(c, _):
            c0 = pl.multiple_of(c * AT_TK, AT_TK)
            s = jnp.dot(k_ref[pl.ds(c0, AT_TK), :], qa_scr[...], preferred_element_type=F32)
            p_scr[pl.ds(c0, AT_TK), :] = jnp.exp2(s).astype(BF16)
            return 0

        lax.fori_loop(0, SEQ // AT_TK, body, 0)
        o = jnp.dot(vt_ref[...], p_scr[...], preferred_element_type=F32)
        o_ref[...] = (o[:HEAD_DIM] * (1.0 / o[HEAD_DIM:HEAD_DIM + 1])).T.astype(BF16)

    @pl.when(jnp.logical_not(fast))
    def _():
        qt = qt_ref[...]

        def chunk(c, carry):
            m, l, acc = carry
            c0 = pl.multiple_of(c * AT_TK_ONLINE, AT_TK_ONLINE)
            s = jnp.dot(k_ref[pl.ds(c0, AT_TK_ONLINE), :HEAD_DIM], qt, preferred_element_type=F32)
            m_new = jnp.maximum(m, s.max(axis=0, keepdims=True))
            alpha = jnp.exp2(m - m_new)
            p = jnp.exp2(s - m_new)
            l = alpha * l + p.sum(axis=0, keepdims=True)
            pv = jnp.dot(vt_ref[:HEAD_DIM, pl.ds(c0, AT_TK_ONLINE)], p.astype(BF16), preferred_element_type=F32)
            return m_new, l, alpha * acc + pv

        init = (jnp.full((1, AT_TQ), -jnp.inf, F32), jnp.zeros((1, AT_TQ), F32),
                jnp.zeros((HEAD_DIM, AT_TQ), F32))
        _, l, acc = lax.fori_loop(0, SEQ // AT_TK_ONLINE, chunk, init)
        o_ref[...] = (acc * (1.0 / l)).T.astype(BF16)


def _attention(qt, k, vt):
    return pl.pallas_call(
        _attn_kernel,
        out_shape=jax.ShapeDtypeStruct((SEQ, ATTN_WIDTH), BF16),
        grid=(N_HEADS, SEQ // AT_TQ),
        in_specs=[
            pl.BlockSpec((HEAD_DIM, AT_TQ), lambda h, i: (h, i)),
            pl.BlockSpec((SEQ, KA_W), lambda h, i: (0, h // GROUP)),
            pl.BlockSpec((VA_H, SEQ), lambda h, i: (h // GROUP, 0)),
        ],
        out_specs=pl.BlockSpec((AT_TQ, HEAD_DIM), lambda h, i: (i, h)),
        scratch_shapes=[
            pltpu.VMEM((KA_W, AT_TQ), BF16),
            pltpu.VMEM((SEQ, AT_TQ), BF16),
            pltpu.VMEM((8, LANES), F32),
        ],
        compiler_params=_params(("arbitrary", "arbitrary")),
        name="attn",
    )(qt, k, vt)


PL_TM = 256
PL_HALO = 16


def _pool_kernel(prev_ref, main_ref, next_ref, wp_ref, scale_ref, o_ref, buf):
    i = pl.program_id(0)
    last = pl.num_programs(0) - 1
    buf[0:PL_HALO, :] = jnp.where(i == 0, 0.0, prev_ref[...].astype(F32))
    buf[PL_HALO:PL_HALO + PL_TM, :] = main_ref[...].astype(F32)
    buf[PL_HALO + PL_TM:, :] = jnp.where(i == last, 0.0, next_ref[...].astype(F32))
    t = i * PL_TM + lax.broadcasted_iota(I32, (PL_TM, 1), 0)
    for gi, w in enumerate(POOL_WINDOWS):
        cols = slice(gi * POOL_GROUP_WIDTH, (gi + 1) * POOL_GROUP_WIDTH)
        win = buf[PL_HALO - w // 2:PL_HALO - w // 2 + PL_TM, cols]
        for d in range(-w // 2 + 1, w // 2):
            win = win + buf[PL_HALO + d:PL_HALO + d + PL_TM, cols]
        lo = jnp.maximum(t - w // 2, 0)
        hi = jnp.minimum(t + w // 2 - 1, SEQ - 1)
        cnt = (hi - lo + 1).astype(F32)
        dlt = win / cnt - buf[PL_HALO:PL_HALO + PL_TM, cols]
        y = jnp.dot(dlt.astype(BF16), wp_ref[gi], preferred_element_type=F32)
        o_ref[:, cols] = (y * scale_ref[:, cols]).astype(BF16)


def _pool(pool_in, w_pool_b, pool_scale):
    nh = PL_TM // PL_HALO
    n_halo_blocks = SEQ // PL_HALO
    return pl.pallas_call(
        _pool_kernel,
        out_shape=jax.ShapeDtypeStruct((SEQ, POOL_WIDTH), BF16),
        grid=(SEQ // PL_TM,),
        in_specs=[
            pl.BlockSpec((PL_HALO, POOL_WIDTH), lambda i: (jnp.maximum(i * nh - 1, 0), 0)),
            pl.BlockSpec((PL_TM, POOL_WIDTH), lambda i: (i, 0)),
            pl.BlockSpec((PL_HALO, POOL_WIDTH), lambda i: (jnp.minimum((i + 1) * nh, n_halo_blocks - 1), 0)),
            pl.BlockSpec((len(POOL_WINDOWS), POOL_GROUP_WIDTH, POOL_GROUP_WIDTH), lambda i: (0, 0, 0)),
            pl.BlockSpec((1, POOL_WIDTH), lambda i: (0, 0)),
        ],
        out_specs=pl.BlockSpec((PL_TM, POOL_WIDTH), lambda i: (i, 0)),
        scratch_shapes=[pltpu.VMEM((PL_TM + 2 * PL_HALO, POOL_WIDTH), F32)],
        compiler_params=_params(("arbitrary",)),
        name="pool",
    )(pool_in, pool_in, pool_in, w_pool_b, pool_scale)


OP_TM = 1024
OP_TN = 512


def _outproj_kernel(a_ref, p_ref, wa_ref, wp_ref, x_ref, g_ref, o_ref):
    acc = jnp.dot(a_ref[...], wa_ref[...], preferred_element_type=F32)
    acc = acc + jnp.dot(p_ref[...], wp_ref[...], preferred_element_type=F32)
    o_ref[...] = x_ref[...] + g_ref[...] * acc


def _outproj(attn, pool, w_out_b, x2, mod):
    return pl.pallas_call(
        _outproj_kernel,
        out_shape=jax.ShapeDtypeStruct((SEQ, D_MODEL), F32),
        grid=(SEQ // OP_TM, D_MODEL // OP_TN),
        in_specs=[
            pl.BlockSpec((OP_TM, ATTN_WIDTH), lambda i, j: (i, 0)),
            pl.BlockSpec((OP_TM, POOL_WIDTH), lambda i, j: (i, 0)),
            pl.BlockSpec((ATTN_WIDTH, OP_TN), lambda i, j: (0, j)),
            pl.BlockSpec((POOL_WIDTH, OP_TN), lambda i, j: (1, j)),
            pl.BlockSpec((OP_TM, OP_TN), lambda i, j: (i, j)),
            pl.BlockSpec((1, OP_TN), lambda i, j: (0, 2 * (D_MODEL // OP_TN) + j)),
        ],
        out_specs=pl.BlockSpec((OP_TM, OP_TN), lambda i, j: (i, j)),
        compiler_params=_params(("arbitrary", "arbitrary")),
        name="outproj",
    )(attn, pool, w_out_b, w_out_b, x2, mod)


N2_TM = 256
RT_PAD = LANES


def _norm2_kernel(x_ref, g_ref, sc_ref, sh_ref, wcat_ref, whi_ref, b_ref, lt_ref, hi_scr, lo_scr,
                  rs_scr, a_scr, s_scr):
    @pl.when(pl.program_id(0) == 0)
    def _():
        _prep_modulation(g_ref, sc_ref, sh_ref, a_scr, s_scr)

    _row_rms(x_ref, rs_scr, N2_TM)

    def body(r, _):
        r0 = pl.multiple_of(r * NORM_ROWS, NORM_ROWS)
        for c in range(D_MODEL // LANES):
            cs = slice(c * LANES, (c + 1) * LANES)
            h = _normed_tile(x_ref, rs_scr, a_scr, s_scr, r0, c)
            hi = h.astype(BF16)
            hi_scr[pl.ds(r0, NORM_ROWS), cs] = hi
            lo_scr[pl.ds(r0, NORM_ROWS), cs] = (h - hi.astype(F32)).astype(BF16)
        return 0
    lax.fori_loop(0, N2_TM // NORM_ROWS, body, 0)
    a = jnp.dot(hi_scr[...], wcat_ref[...], preferred_element_type=F32)
    b = jnp.dot(lo_scr[...], whi_ref[...], preferred_element_type=F32)
    logits = a[:, :RT_PAD] + a[:, RT_PAD:] + b + b_ref[...]
    lt_ref[...] = logits.T[:N_EXPERTS, :]


def _norm2(x1, norm2_g, mod, wcat, whi, b_pad):
    row = lambda n: pl.BlockSpec((1, D_MODEL), lambda i, n=n: (0, n))
    return pl.pallas_call(
        _norm2_kernel,
        out_shape=jax.ShapeDtypeStruct((N_EXPERTS, SEQ), F32),
        grid=(SEQ // N2_TM,),
        in_specs=[
            pl.BlockSpec((N2_TM, D_MODEL), lambda i: (i, 0)),
            pl.BlockSpec((1, D_MODEL), lambda i: (0, 0)),
            row(4), row(3),
            pl.BlockSpec((D_MODEL, 2 * RT_PAD), lambda i: (0, 0)),
            pl.BlockSpec((D_MODEL, RT_PAD), lambda i: (0, 0)),
            pl.BlockSpec((1, RT_PAD), lambda i: (0, 0)),
        ],
        out_specs=pl.BlockSpec((N_EXPERTS, N2_TM), lambda i: (0, i)),
        scratch_shapes=[
            pltpu.VMEM((N2_TM, D_MODEL), BF16),
            pltpu.VMEM((N2_TM, D_MODEL), BF16),
            pltpu.VMEM((N2_TM, LANES), F32),
            pltpu.VMEM((NORM_ROWS, D_MODEL), F32),
            pltpu.VMEM((NORM_ROWS, D_MODEL), F32),
        ],
        compiler_params=_params(("arbitrary",)),
        name="norm2",
    )(x1, norm2_g, mod, mod, wcat, whi, b_pad)


RT_CH = 1024
RT_SB = 256


def _route_kernel(lt_ref, dest_ref, gate_ref, nblk_ref, idx_scr, rank_scr):
    e_col = lax.broadcasted_iota(I32, (N_EXPERTS, RT_CH), 0).astype(F32)
    tri = (lax.broadcasted_iota(I32, (RT_SB, RT_SB), 0) < lax.broadcasted_iota(I32, (RT_SB, RT_SB), 1)).astype(BF16)
    carry = jnp.zeros((N_EXPERTS, 1), F32)
    for c in range(SEQ // RT_CH):
        cs = slice(c * RT_CH, (c + 1) * RT_CH)
        work = lt_ref[:, cs]
        vals = []
        mask = jnp.zeros((N_EXPERTS, RT_CH), F32)
        for k in range(TOP_K):
            m = work.max(axis=0, keepdims=True)
            idx = jnp.where(work == m, e_col, float(N_EXPERTS)).min(axis=0, keepdims=True)
            sel = e_col == idx
            vals.append(m)
            idx_scr[k:k + 1, cs] = idx
            mask = jnp.where(sel, 1.0, mask)
            work = jnp.where(sel, -jnp.inf, work)
        ex = [jnp.exp(v - vals[0]) for v in vals]
        den = ex[0] + ex[1] + ex[2] + ex[3]
        for k in range(TOP_K):
            gate_ref[k:k + 1, cs] = ex[k] / den
        for b in range(RT_CH // RT_SB):
            blk = mask[:, b * RT_SB:(b + 1) * RT_SB]
            pref = jnp.dot(blk.astype(BF16), tri, preferred_element_type=F32)
            rank_scr[:, c * RT_CH + b * RT_SB:c * RT_CH + (b + 1) * RT_SB] = pref + carry
            carry = carry + blk.sum(axis=1, keepdims=True)
    nblk = jnp.floor((carry + (MOE_TM - 1)) * (1.0 / MOE_TM))
    nblk_b = jnp.broadcast_to(nblk, (N_EXPERTS, LANES))
    lower = (lax.broadcasted_iota(I32, (N_EXPERTS, N_EXPERTS), 1) < lax.broadcasted_iota(I32, (N_EXPERTS, N_EXPERTS), 0)).astype(BF16)
    start_blk = jnp.dot(lower, nblk_b.astype(BF16), preferred_element_type=F32)
    nblk_ref[...] = nblk_b.astype(I32)
    start = start_blk[:, 0:1] * float(MOE_TM)
    for c in range(SEQ // RT_CH):
        cs = slice(c * RT_CH, (c + 1) * RT_CH)
        slot = rank_scr[:, cs] + start
        for k in range(TOP_K):
            sel = e_col == idx_scr[k:k + 1, cs]
            dest_ref[k:k + 1, cs] = jnp.where(sel, slot, 0.0).sum(axis=0, keepdims=True).astype(I32)


def _route(logits_t):
    return pl.pallas_call(
        _route_kernel,
        out_shape=(
            jax.ShapeDtypeStruct((TOP_K, SEQ), I32),
            jax.ShapeDtypeStruct((TOP_K, SEQ), F32),
            jax.ShapeDtypeStruct((N_EXPERTS, LANES), I32),
        ),
        scratch_shapes=[pltpu.VMEM((8, SEQ), F32), pltpu.VMEM((N_EXPERTS, SEQ), F32)],
        compiler_params=pltpu.CompilerParams(vmem_limit_bytes=VMEM_LIMIT),
        name="route",
    )(logits_t)


DP_TM = 256
XS_W = D_MODEL // 2
U32 = jnp.uint32


def _dispatch_kernel(dest_ref, lastblk_ref, x_ref, g_ref, sc_ref, sh_ref, xs_hbm, pk, zero_buf, rs_scr, a_scr,
                     s_scr, zsem, sem):
    i = pl.program_id(0)
    par = i % 2

    @pl.when(i == 0)
    def _():
        _prep_modulation(g_ref, sc_ref, sh_ref, a_scr, s_scr)
        zero_buf[...] = jnp.zeros_like(zero_buf)

        def zcopy(e):
            b = jnp.maximum(lastblk_ref[e], 0)
            return pltpu.make_async_copy(zero_buf, xs_hbm.at[pl.ds(pl.multiple_of(b * MOE_TM, MOE_TM), MOE_TM)], zsem)

        def zstart(e, _):
            @pl.when(lastblk_ref[e] >= 0)
            def _():
                zcopy(e).start()
            return 0

        def zwait(e, _):
            @pl.when(lastblk_ref[e] >= 0)
            def _():
                zcopy(e).wait()
            return 0

        lax.fori_loop(0, N_EXPERTS, zstart, 0)
        lax.fori_loop(0, N_EXPERTS, zwait, 0)

    _row_rms(x_ref, rs_scr, DP_TM)

    n_tiles = XS_W // LANES
    assert n_tiles == NORM_ROWS

    def pack_tile(r0, c):
        lo = _normed_tile(x_ref, rs_scr, a_scr, s_scr, r0, c)
        hi = _normed_tile(x_ref, rs_scr, a_scr, s_scr, r0, c + n_tiles)
        lo = lax.bitcast_convert_type(lo.astype(BF16).astype(F32), U32)
        hi = lax.bitcast_convert_type(hi.astype(BF16).astype(F32), U32)
        pk[par, pl.ds(r0, NORM_ROWS), c * LANES:(c + 1) * LANES] = (lo >> 16) | (hi & jnp.uint32(0xFFFF0000))

    def issue_token(u):
        for k in range(TOP_K):
            d = dest_ref[k * SEQ + i * DP_TM + u]
            pltpu.make_async_copy(pk.at[par, pl.ds(u, 1)], xs_hbm.at[pl.ds(d, 1)], sem.at[par]).start()

    for c in range(n_tiles):
        pack_tile(0, c)

    def group(g, _):
        r0 = pl.multiple_of(g * NORM_ROWS, NORM_ROWS)
        for u in range(NORM_ROWS):
            pack_tile(r0, u)
            issue_token(r0 - NORM_ROWS + u)
        return 0

    lax.fori_loop(1, DP_TM // NORM_ROWS, group, 0)

    def tail(u, _):
        issue_token(DP_TM - NORM_ROWS + u)
        return 0

    lax.fori_loop(0, NORM_ROWS, tail, 0)

    def drain(p):
        for _ in range(TOP_K):
            pltpu.make_async_copy(pk.at[p], xs_hbm.at[pl.ds(0, DP_TM)], sem.at[p]).wait()

    @pl.when(i > 0)
    def _():
        drain(1 - par)

    @pl.when(i == pl.num_programs(0) - 1)
    def _():
        drain(par)


def _dispatch(dest_flat, lastblk, x1, norm2_g, mod):
    row = lambda n: pl.BlockSpec((1, D_MODEL), lambda i, d, lb, n=n: (0, n))
    return pl.pallas_call(
        _dispatch_kernel,
        out_shape=jax.ShapeDtypeStruct((MOE_ROWS, XS_W), U32),
        grid_spec=pltpu.PrefetchScalarGridSpec(
            num_scalar_prefetch=2,
            grid=(SEQ // DP_TM,),
            in_specs=[
                pl.BlockSpec((DP_TM, D_MODEL), lambda i, d, lb: (i, 0)),
                pl.BlockSpec((1, D_MODEL), lambda i, d, lb: (0, 0)),
                row(4), row(3),
            ],
            out_specs=pl.BlockSpec(memory_space=pl.ANY),
            scratch_shapes=[
                pltpu.VMEM((2, DP_TM, XS_W), U32),
                pltpu.VMEM((MOE_TM, XS_W), U32),
                pltpu.VMEM((DP_TM, LANES), F32),
                pltpu.VMEM((NORM_ROWS, D_MODEL), F32),
                pltpu.VMEM((NORM_ROWS, D_MODEL), F32),
                pltpu.SemaphoreType.DMA,
                pltpu.SemaphoreType.DMA((2,)),
            ],
        ),
        compiler_params=_params(("arbitrary",)),
        name="dispatch",
    )(dest_flat, lastblk, x1, norm2_g, mod, mod)


F1_TF = 512
F2_TN = 4096


CAST_ROWS = 128
WEIGHT_DMA_PRIORITY = 1


def _expert_schedule(nblk):
    end_blk = jnp.cumsum(nblk)
    n_used = end_blk[-1]
    blocks = jnp.arange(MOE_NB, dtype=I32)
    be = jnp.minimum(jnp.sum(end_blk[None, :] <= blocks[:, None], axis=1), N_EXPERTS - 1).astype(I32)
    first = (blocks < n_used) & (be != jnp.concatenate([jnp.full((1,), -1, I32), be[:-1]]))
    seg = jnp.cumsum(first.astype(I32)) - 1
    seg_end = end_blk[be]
    nxt = jnp.where(seg_end < n_used, be[jnp.minimum(seg_end, MOE_NB - 1)], -1)
    meta = jnp.stack([n_used, jnp.sum(first.astype(I32))]).astype(I32)
    lastblk = jnp.where(nblk > 0, end_blk - 1, -1).astype(I32)
    return meta, be, first.astype(I32), seg.astype(I32), nxt.astype(I32), lastblk


def _stream_expert_weights(j, i, nj, meta_ref, be_ref, first_ref, seg_ref, nxt_ref, tile_copies, stg, wbuf):
    @pl.when(first_ref[i] == 1)
    def _():
        seq = j * meta_ref[1] + seg_ref[i]
        slot = seq % 2

        @pl.when(seq == 0)
        def _():
            for cp in tile_copies(be_ref[i], j, slot):
                cp.start(priority=WEIGHT_DMA_PRIORITY)

        for cp in tile_copies(be_ref[i], j, slot):
            cp.wait()

        nxt = nxt_ref[i]

        @pl.when(nxt >= 0)
        def _():
            for cp in tile_copies(nxt, j, 1 - slot):
                cp.start(priority=WEIGHT_DMA_PRIORITY)

        @pl.when((nxt < 0) & (j + 1 < nj))
        def _():
            for cp in tile_copies(be_ref[0], j + 1, 1 - slot):
                cp.start(priority=WEIGHT_DMA_PRIORITY)

        def cast(r, _):
            r0 = pl.multiple_of(r * CAST_ROWS, CAST_ROWS)
            wbuf[pl.ds(r0, CAST_ROWS), :] = stg[slot, pl.ds(r0, CAST_ROWS), :].astype(BF16)
            return 0

        lax.fori_loop(0, wbuf.shape[0] // CAST_ROWS, cast, 0)


F1_NH = D_FF // F1_TF


def _ffn1_kernel(meta_ref, be_ref, first_ref, nxt_ref, x_ref, bg_ref, bl_ref, w1_hbm, o_ref, stg, wbuf, sem):
    i = pl.program_id(0)

    def slab_copies(e, h):
        return (
            pltpu.make_async_copy(w1_hbm.at[e, :, h * F1_TF:(h + 1) * F1_TF], stg.at[h, :, 0:F1_TF], sem.at[h]),
            pltpu.make_async_copy(w1_hbm.at[e, :, D_FF + h * F1_TF:D_FF + (h + 1) * F1_TF],
                                  stg.at[h, :, F1_TF:2 * F1_TF], sem.at[h]),
        )

    @pl.when(i < meta_ref[0])
    def _():
        @pl.when(first_ref[i] == 1)
        def _():
            @pl.when(i == 0)
            def _():
                for h in range(F1_NH):
                    for cp in slab_copies(be_ref[0], h):
                        cp.start(priority=WEIGHT_DMA_PRIORITY)

            nxt = nxt_ref[i]
            for h in range(F1_NH):
                for cp in slab_copies(be_ref[i], h):
                    cp.wait()

                def cast(r, _):
                    r0 = pl.multiple_of(r * CAST_ROWS, CAST_ROWS)
                    wbuf[h, pl.ds(r0, CAST_ROWS), :] = stg[h, pl.ds(r0, CAST_ROWS), :].astype(BF16)
                    return 0

                lax.fori_loop(0, D_MODEL // CAST_ROWS, cast, 0)

                @pl.when(nxt >= 0)
                def _():
                    for cp in slab_copies(nxt, h):
                        cp.start(priority=WEIGHT_DMA_PRIORITY)

        xp = x_ref[...]
        x_lo = lax.bitcast_convert_type(xp << 16, F32).astype(BF16)
        x_hi = lax.bitcast_convert_type(xp & jnp.uint32(0xFFFF0000), F32).astype(BF16)
        for h in range(F1_NH):
            cols = slice(h * F1_TF, (h + 1) * F1_TF)
            y = jnp.dot(x_lo, wbuf[h, :XS_W, :], preferred_element_type=F32)
            y = y + jnp.dot(x_hi, wbuf[h, XS_W:, :], preferred_element_type=F32)
            glu = jnp.minimum(y[:, :F1_TF] + bg_ref[:, cols], SWIGLU_LIMIT)
            lin = jnp.clip(y[:, F1_TF:] + bl_ref[:, cols], -SWIGLU_LIMIT, SWIGLU_LIMIT)
            o_ref[:, cols] = (glu * jax.nn.sigmoid(SWIGLU_ALPHA * glu) * (lin + 1.0)).astype(BF16)


def _ffn1(sched, xs, w1, b1_3):
    meta, be, first, _, nxt, _ = sched
    blk = lambda i, meta: jnp.minimum(i, meta[0] - 1)
    return pl.pallas_call(
        _ffn1_kernel,
        out_shape=jax.ShapeDtypeStruct((MOE_ROWS, D_FF), BF16),
        grid_spec=pltpu.PrefetchScalarGridSpec(
            num_scalar_prefetch=4,
            grid=(MOE_NB,),
            in_specs=[
                pl.BlockSpec((MOE_TM, XS_W), lambda i, meta, be, *_: (blk(i, meta), 0)),
                pl.BlockSpec((None, 1, D_FF), lambda i, meta, be, *_: (be[blk(i, meta)], 0, 0)),
                pl.BlockSpec((None, 1, D_FF), lambda i, meta, be, *_: (be[blk(i, meta)], 0, 1)),
                pl.BlockSpec(memory_space=pl.ANY),
            ],
            out_specs=pl.BlockSpec((MOE_TM, D_FF), lambda i, meta, be, *_: (blk(i, meta), 0)),
            scratch_shapes=[
                pltpu.VMEM((F1_NH, D_MODEL, 2 * F1_TF), F32),
                pltpu.VMEM((F1_NH, D_MODEL, 2 * F1_TF), BF16),
                pltpu.SemaphoreType.DMA((F1_NH,)),
            ],
        ),
        compiler_params=_params(("arbitrary",), vmem=FFN1_VMEM_LIMIT),
        name="ffn1",
    )(meta, be, first, nxt, xs, b1_3, b1_3, w1)


def _ffn2_kernel(meta_ref, be_ref, first_ref, seg_ref, nxt_ref, a_ref, b_ref, w2_hbm, o_ref, stg, wbuf, sem):
    j = pl.program_id(0)
    i = pl.program_id(1)
    nj = pl.num_programs(0)

    def tile_copies(e, jj, slot):
        c0 = pl.multiple_of(jj * F2_TN, F2_TN)
        return (pltpu.make_async_copy(w2_hbm.at[e, :, pl.ds(c0, F2_TN)], stg.at[slot], sem.at[slot]),)

    @pl.when(i < meta_ref[0])
    def _():
        _stream_expert_weights(j, i, nj, meta_ref, be_ref, first_ref, seg_ref, nxt_ref, tile_copies, stg, wbuf)
        o_ref[...] = jnp.dot(a_ref[...], wbuf[...], preferred_element_type=F32) + b_ref[...]


def _ffn2(sched, act, w2, b2_3):
    meta, be, first, seg, nxt, _ = sched
    blk = lambda i, meta: jnp.minimum(i, meta[0] - 1)
    return pl.pallas_call(
        _ffn2_kernel,
        out_shape=jax.ShapeDtypeStruct((MOE_ROWS, D_MODEL), F32),
        grid_spec=pltpu.PrefetchScalarGridSpec(
            num_scalar_prefetch=5,
            grid=(D_MODEL // F2_TN, MOE_NB),
            in_specs=[
                pl.BlockSpec((MOE_TM, D_FF), lambda j, i, meta, be, *_: (blk(i, meta), 0)),
                pl.BlockSpec((None, 1, F2_TN), lambda j, i, meta, be, *_: (be[blk(i, meta)], 0, j)),
                pl.BlockSpec(memory_space=pl.ANY),
            ],
            out_specs=pl.BlockSpec((MOE_TM, F2_TN), lambda j, i, meta, be, *_: (blk(i, meta), j)),
            scratch_shapes=[
                pltpu.VMEM((2, D_FF, F2_TN), F32),
                pltpu.VMEM((D_FF, F2_TN), BF16),
                pltpu.SemaphoreType.DMA((2,)),
            ],
        ),
        compiler_params=_params(("arbitrary", "arbitrary")),
        name="ffn2",
    )(meta, be, first, seg, nxt, act, b2_3, w2)


CB_TM = 128


def _combine_kernel(dest_ref, x_ref, gate_ref, g2_ref, ys_hbm, o_ref, buf, gate_scr, g2_scr, sem):
    i = pl.program_id(0)
    last = pl.num_programs(0) - 1
    par = i % 2
    n_tiles = D_MODEL // LANES
    assert n_tiles == 8 * TOP_K

    def row_start(tile, p, u, k):
        d = dest_ref[k * SEQ + tile * CB_TM + u]
        pltpu.make_async_copy(ys_hbm.at[pl.ds(d, 1)], buf.at[p, k, pl.ds(u, 1)], sem.at[p]).start()

    def drain(p):
        for k in range(TOP_K):
            pltpu.make_async_copy(ys_hbm.at[pl.ds(0, CB_TM)], buf.at[p, k], sem.at[p]).wait()

    @pl.when(i == 0)
    def _():
        def body(u, _):
            for k in range(TOP_K):
                row_start(0, 0, u, k)
            return 0
        lax.fori_loop(0, CB_TM, body, 0)
        g2_scr[...] = jnp.broadcast_to(g2_ref[...], g2_scr.shape)

    drain(par)

    for k in range(TOP_K):
        gate_scr[k] = jnp.broadcast_to(gate_ref[:, k:k + 1], (CB_TM, LANES))

    nxt = jnp.minimum(i + 1, last)

    def body(r, _):
        r0 = pl.multiple_of(r * 8, 8)
        rows = pl.ds(r0, 8)
        gk = [gate_scr[k, rows, :] for k in range(TOP_K)]
        for c in range(n_tiles):
            cs = slice(c * LANES, (c + 1) * LANES)
            y = buf[par, 0, rows, cs] * gk[0]
            for k in range(1, TOP_K):
                y = y + buf[par, k, rows, cs] * gk[k]
            o_ref[rows, cs] = x_ref[rows, cs] + g2_scr[:, cs] * y
            row_start(nxt, 1 - par, r0 + c // TOP_K, c % TOP_K)
        return 0

    lax.fori_loop(0, CB_TM // 8, body, 0)

    @pl.when(i == last)
    def _():
        drain(1 - par)


def _combine(dest_flat, x1, gates_t, mod, ys):
    return pl.pallas_call(
        _combine_kernel,
        out_shape=jax.ShapeDtypeStruct((SEQ, D_MODEL), F32),
        grid_spec=pltpu.PrefetchScalarGridSpec(
            num_scalar_prefetch=1,
            grid=(SEQ // CB_TM,),
            in_specs=[
                pl.BlockSpec((CB_TM, D_MODEL), lambda i, d: (i, 0)),
                pl.BlockSpec((CB_TM, TOP_K), lambda i, d: (i, 0)),
                pl.BlockSpec((1, D_MODEL), lambda i, d: (0, 5)),
                pl.BlockSpec(memory_space=pl.ANY),
            ],
            out_specs=pl.BlockSpec((CB_TM, D_MODEL), lambda i, d: (i, 0)),
            scratch_shapes=[
                pltpu.VMEM((2, TOP_K, CB_TM, D_MODEL), F32),
                pltpu.VMEM((TOP_K, CB_TM, LANES), F32),
                pltpu.VMEM((8, D_MODEL), F32),
                pltpu.SemaphoreType.DMA((2,)),
            ],
        ),
        compiler_params=_params(("arbitrary",)),
        name="combine",
    )(dest_flat, x1, gates_t, mod, ys)


def _rope_tables():
    t = jnp.arange(SEQ, dtype=I32)
    row = (t // GRID_W).astype(F32)
    col = (t % GRID_W).astype(F32)
    inv_freq = ROPE_THETA ** (-jnp.arange(0, ROPE_AXIS_DIM, 2, dtype=F32) / ROPE_AXIS_DIM)
    ang_r = inv_freq[:, None] * row[None, :]
    ang_c = inv_freq[:, None] * col[None, :]
    cos_t = jnp.concatenate([jnp.cos(ang_r), jnp.cos(ang_r), jnp.cos(ang_c), jnp.cos(ang_c)], axis=0)
    sin_t = jnp.concatenate([-jnp.sin(ang_r), jnp.sin(ang_r), -jnp.sin(ang_c), jnp.sin(ang_c)], axis=0)
    return cos_t, sin_t


def kernel(x, c, w_mod, b_mod, norm1_g, w_in, q_norm_g, k_norm_g, w_pool, pool_scale, w_out, norm2_g,
           w_router, b_router, w1, b1, w2, b2):
    assert x.shape == (1, SEQ, D_MODEL) and w_mod.shape[0] == 1
    x2 = x[0]
    cos_t, sin_t = _rope_tables()

    mod = _mod(c.reshape(D_MODEL, 1), w_mod[0], b_mod)

    pool_in, qt, k, vt = _inproj(x2, mod, norm1_g, w_in[0].astype(BF16), cos_t, sin_t,
                                 q_norm_g.reshape(HEAD_DIM, 1), k_norm_g.reshape(HEAD_DIM, 1))
    attn = _attention(qt, k, vt)
    pool = _pool(pool_in, w_pool[0].astype(BF16), pool_scale)
    x1 = _outproj(attn, pool, w_out[0].astype(BF16), x2, mod)

    wr = w_router[0]
    wr_hi = wr.astype(BF16)
    wr_lo = (wr - wr_hi.astype(F32)).astype(BF16)
    pad = lambda a: jnp.pad(a, ((0, 0), (0, RT_PAD - N_EXPERTS)))
    wcat = jnp.concatenate([pad(wr_hi), pad(wr_lo)], axis=1)
    logits_t = _norm2(x1, norm2_g, mod, wcat, pad(wr_hi), pad(b_router))

    dest, gates, nblk = _route(logits_t)
    sched = _expert_schedule(nblk[:, 0])
    dest_flat = dest.reshape(TOP_K * SEQ)

    xs = _dispatch(dest_flat, sched[-1], x1, norm2_g, mod)
    act = _ffn1(sched, xs, w1[0], b1[0].reshape(N_EXPERTS, 1, 2 * D_FF))
    ys = _ffn2(sched, act, w2[0], b2[0].reshape(N_EXPERTS, 1, D_MODEL))
    out = _combine(dest_flat, x1, gates.T, mod, ys)
    return out[None]
```

```python
import functools
import math

import jax
import jax.numpy as jnp
from jax import lax
from jax.experimental import pallas as pl
from jax.experimental.pallas import tpu as pltpu

F32 = jnp.float32
BF16 = jnp.bfloat16
I32 = jnp.int32

D_MODEL = 4096
SEQ = 8192
POOL_WIDTH = 2048
ATTN_WIDTH = 2048
HEAD_DIM = 128
N_HEADS = 16
N_KV_HEADS = 4
GROUP = N_HEADS // N_KV_HEADS
KV_WIDTH = N_KV_HEADS * HEAD_DIM
IN_WIDTH = POOL_WIDTH + ATTN_WIDTH + 2 * KV_WIDTH
POOL_WINDOWS = (2, 4, 8, 16)
POOL_GROUP_WIDTH = POOL_WIDTH // len(POOL_WINDOWS)
GRID_W = 64
ROPE_THETA = 10000.0
ROPE_AXIS_DIM = HEAD_DIM // 2
N_EXPERTS = 32
TOP_K = 4
D_FF = D_MODEL // 4
SWIGLU_ALPHA = 1.702
SWIGLU_LIMIT = 7.0
N_MOD = 6
EPS = 1e-6

LANES = 128
VMEM_LIMIT = 56 * 1024 * 1024
BIG_VMEM_LIMIT = 62 * 1024 * 1024

Q_SCALE = (HEAD_DIM ** -0.5) * math.log2(math.e)

MOE_TM = 256
MOE_NB = SEQ * TOP_K // MOE_TM + N_EXPERTS
MOE_ROWS = MOE_NB * MOE_TM


def _params(sem, vmem=VMEM_LIMIT):
    return pltpu.CompilerParams(dimension_semantics=sem, vmem_limit_bytes=vmem)


MOD_TN = 1024
MOD_KC = 256


def _mod_kernel(c_ref, w_ref, b_ref, o_ref):
    def body(k, acc):
        r = pl.multiple_of(k * MOD_KC, MOD_KC)
        ck = c_ref[pl.ds(r, MOD_KC), :]
        ck = ck * jax.nn.sigmoid(ck)
        p = w_ref[pl.ds(r, MOD_KC), :] * ck
        return acc + p.reshape(MOD_KC // 8, 8, MOD_TN).sum(axis=0)

    acc = lax.fori_loop(0, D_MODEL // MOD_KC, body, jnp.zeros((8, MOD_TN), F32))
    o_ref[...] = acc.sum(axis=0, keepdims=True) + b_ref[...]


MOD_A = 2
MOD_B = N_MOD - MOD_A
MOD_B_G1, MOD_B_SH2, MOD_B_SC2, MOD_B_G2 = range(MOD_B)


def _mod(c_col, w_mod, b_mod):
    n = MOD_A * D_MODEL
    return pl.pallas_call(
        _mod_kernel,
        out_shape=jax.ShapeDtypeStruct((1, n), F32),
        grid=(n // MOD_TN,),
        in_specs=[
            pl.BlockSpec((D_MODEL, 1), lambda j: (0, 0)),
            pl.BlockSpec((D_MODEL, MOD_TN), lambda j: (0, j)),
            pl.BlockSpec((1, MOD_TN), lambda j: (0, j)),
        ],
        out_specs=pl.BlockSpec((1, MOD_TN), lambda j: (0, j)),
        compiler_params=_params(("arbitrary",)),
        name="mod",
    )(c_col, w_mod, b_mod)


IP_TM = 512
IP_TN = 1024
IP_NJ = IN_WIDTH // IP_TN
IP_J_Q = POOL_WIDTH // IP_TN
IP_J_KV = IP_J_Q + ATTN_WIDTH // IP_TN
NORM_ROWS = 16
BF16_SUBLANES = 16
KA_W = 2 * HEAD_DIM
VA_H = HEAD_DIM + BF16_SUBLANES


def _prep_modulation(g_ref, sc_ref, sh_ref, a_scr, s_scr):
    a_scr[...] = jnp.broadcast_to(g_ref[...] * (1.0 + sc_ref[...]), a_scr.shape)
    s_scr[...] = jnp.broadcast_to(sh_ref[...], s_scr.shape)


def _row_rms(x_ref, rs_scr, n_rows):
    def body(r, _):
        r0 = pl.multiple_of(r * NORM_ROWS, NORM_ROWS)
        width = x_ref.shape[1]
        parts = []
        for c in range(width // LANES):
            xc = x_ref[pl.ds(r0, NORM_ROWS), c * LANES:(c + 1) * LANES]
            parts.append(xc * xc)
        while len(parts) > 1:
            parts = [parts[p] + parts[p + 1] for p in range(0, len(parts), 2)]
        rs_scr[pl.ds(r0, NORM_ROWS), :] = parts[0]
        return 0
    lax.fori_loop(0, n_rows // NORM_ROWS, body, 0)
    ms = jnp.sum(rs_scr[...], axis=-1, keepdims=True) * (1.0 / x_ref.shape[1])
    rs_scr[...] = jnp.broadcast_to(lax.rsqrt(ms + EPS), rs_scr.shape)


def _normed_tile(x_ref, rs_scr, a_scr, s_scr, r0, c):
    cs = slice(c * LANES, (c + 1) * LANES)
    return x_ref[pl.ds(r0, NORM_ROWS), cs] * rs_scr[pl.ds(r0, NORM_ROWS), :] * a_scr[:, cs] + s_scr[:, cs]


def _norm_rope_t(xt, g_col, cos_t, sin_t):
    ms = jnp.mean(xt * xt, axis=0, keepdims=True)
    y = xt * lax.rsqrt(ms + EPS) * g_col
    q = ROPE_AXIS_DIM // 2
    partner = jnp.concatenate([y[q:2 * q], y[0:q], y[3 * q:4 * q], y[2 * q:3 * q]], axis=0)
    return y * cos_t + partner * sin_t


MB_TN = 256
MB_STEPS = MOD_B * D_MODEL // MB_TN
MB_KC = 256


def _tree_sum(parts):
    while len(parts) > 1:
        parts = [parts[p] + parts[p + 1] for p in range(0, len(parts), 2)]
    return parts[0]


def _later_modulation(t, c_ref, wm_ref, bm_ref, mb_ref, cb_scr):
    @pl.when(t == 0)
    def _():
        for r in range(D_MODEL // LANES):
            cv = c_ref[r:r + 1, :]
            cb_scr[r * LANES:(r + 1) * LANES, :] = jnp.broadcast_to(cv * jax.nn.sigmoid(cv), (LANES, LANES)).T

    @pl.when(t < MB_STEPS)
    def _():
        def body(k, accs):
            r0 = pl.multiple_of(k * MB_KC, MB_KC)
            cb = cb_scr[pl.ds(r0, MB_KC), :]
            out = []
            for n in range(MB_TN // LANES):
                p = wm_ref[pl.ds(r0, MB_KC), n * LANES:(n + 1) * LANES] * cb
                out.append(accs[n] + _tree_sum([p[g * 8:(g + 1) * 8] for g in range(MB_KC // 8)]))
            return tuple(out)

        zero = jnp.zeros((8, LANES), F32)
        accs = lax.fori_loop(0, D_MODEL // MB_KC, body, (zero,) * (MB_TN // LANES))
        for n in range(MB_TN // LANES):
            cs = slice(n * LANES, (n + 1) * LANES)
            mb_ref[:, cs] = accs[n].sum(axis=0, keepdims=True) + bm_ref[:, cs]


def _inproj_kernel(x_ref, g_ref, sc_ref, sh_ref, w_ref, cos_ref, sin_ref, qg_ref, kg_ref, c_ref, wm_ref, bm_ref,
                   pool_ref, qt_ref, k_ref, vt_ref, mb_ref, h_scr, rs_scr, a_scr, s_scr, cb_scr):
    j = pl.program_id(1)
    _later_modulation(pl.program_id(0) * IP_NJ + j, c_ref, wm_ref, bm_ref, mb_ref, cb_scr)

    @pl.when(j == 0)
    def _():
        _prep_modulation(g_ref, sc_ref, sh_ref, a_scr, s_scr)
        _row_rms(x_ref, rs_scr, IP_TM)

        def body(r, _):
            r0 = pl.multiple_of(r * NORM_ROWS, NORM_ROWS)
            for c in range(D_MODEL // LANES):
                h_scr[pl.ds(r0, NORM_ROWS), c * LANES:(c + 1) * LANES] = _normed_tile(
                    x_ref, rs_scr, a_scr, s_scr, r0, c).astype(BF16)
            return 0
        lax.fori_loop(0, IP_TM // NORM_ROWS, body, 0)

    acc = jnp.dot(h_scr[...], w_ref[...], preferred_element_type=F32)

    @pl.when(j < IP_J_Q)
    def _():
        pool_ref[...] = acc.astype(BF16)

    @pl.when((j >= IP_J_Q) & (j < IP_J_KV))
    def _():
        for hh in range(IP_TN // HEAD_DIM):
            sl = slice(hh * HEAD_DIM, (hh + 1) * HEAD_DIM)
            r = _norm_rope_t(acc[:, sl].T, qg_ref[...], cos_ref[...], sin_ref[...]) * Q_SCALE
            qt_ref[sl, :] = r.astype(BF16)

    @pl.when(j == IP_J_KV)
    def _():
        lane = lax.broadcasted_iota(I32, (IP_TM, HEAD_DIM), 1)
        one_col = jnp.where(lane == 0, 1.0, 0.0).astype(BF16)
        for hh in range(N_KV_HEADS):
            sl = slice(hh * HEAD_DIM, (hh + 1) * HEAD_DIM)
            k_ref[:, hh * KA_W:hh * KA_W + HEAD_DIM] = _norm_rope_t(
                acc[:, sl].T, kg_ref[...], cos_ref[...], sin_ref[...]).T.astype(BF16)
            k_ref[:, hh * KA_W + HEAD_DIM:(hh + 1) * KA_W] = one_col
        for hh in range(N_KV_HEADS):
            sl = slice(KV_WIDTH + hh * HEAD_DIM, KV_WIDTH + (hh + 1) * HEAD_DIM)
            vt_ref[hh * VA_H:hh * VA_H + HEAD_DIM, :] = acc[:, sl].T.astype(BF16)
            vt_ref[hh * VA_H + HEAD_DIM:(hh + 1) * VA_H, :] = jnp.ones((VA_H - HEAD_DIM, IP_TM), BF16)


def _inproj(x2, mod, norm1_g, w_in_b, cos_t, sin_t, qg, kg, c_col, w_mod, b_mod):
    row = lambda n: pl.BlockSpec((1, D_MODEL), lambda i, j, n=n: (0, n))
    mb_tile = lambda i, j: jnp.minimum(i * IP_NJ + j, MB_STEPS - 1)
    mb_first = MOD_A * D_MODEL // MB_TN
    return pl.pallas_call(
        _inproj_kernel,
        out_shape=(
            jax.ShapeDtypeStruct((SEQ, POOL_WIDTH), BF16),
            jax.ShapeDtypeStruct((ATTN_WIDTH, SEQ), BF16),
            jax.ShapeDtypeStruct((SEQ, N_KV_HEADS * KA_W), BF16),
            jax.ShapeDtypeStruct((N_KV_HEADS * VA_H, SEQ), BF16),
            jax.ShapeDtypeStruct((1, MOD_B * D_MODEL), F32),
        ),
        grid=(SEQ // IP_TM, IP_NJ),
        in_specs=[
            pl.BlockSpec((IP_TM, D_MODEL), lambda i, j: (i, 0)),
            pl.BlockSpec((1, D_MODEL), lambda i, j: (0, 0)),
            row(1), row(0),
            pl.BlockSpec((D_MODEL, IP_TN), lambda i, j: (0, j)),
            pl.BlockSpec((HEAD_DIM, IP_TM), lambda i, j: (0, i)),
            pl.BlockSpec((HEAD_DIM, IP_TM), lambda i, j: (0, i)),
            pl.BlockSpec((HEAD_DIM, 1), lambda i, j: (0, 0)),
            pl.BlockSpec((HEAD_DIM, 1), lambda i, j: (0, 0)),
            pl.BlockSpec((D_MODEL // LANES, LANES), lambda i, j: (0, 0)),
            pl.BlockSpec((D_MODEL, MB_TN), lambda i, j: (0, mb_first + mb_tile(i, j))),
            pl.BlockSpec((1, MB_TN), lambda i, j: (0, mb_first + mb_tile(i, j))),
        ],
        out_specs=(
            pl.BlockSpec((IP_TM, IP_TN), lambda i, j: (i, jnp.minimum(j, IP_J_Q - 1))),
            pl.BlockSpec((IP_TN, IP_TM), lambda i, j: (jnp.clip(j - IP_J_Q, 0, IP_J_KV - IP_J_Q - 1), i)),
            pl.BlockSpec((IP_TM, N_KV_HEADS * KA_W), lambda i, j: (i, 0)),
            pl.BlockSpec((N_KV_HEADS * VA_H, IP_TM), lambda i, j: (0, i)),
            pl.BlockSpec((1, MB_TN), lambda i, j: (0, mb_tile(i, j))),
        ),
        scratch_shapes=[
            pltpu.VMEM((IP_TM, D_MODEL), BF16),
            pltpu.VMEM((IP_TM, LANES), F32),
            pltpu.VMEM((NORM_ROWS, D_MODEL), F32),
            pltpu.VMEM((NORM_ROWS, D_MODEL), F32),
            pltpu.VMEM((D_MODEL, LANES), F32),
        ],
        compiler_params=_params(("arbitrary", "arbitrary"), vmem=BIG_VMEM_LIMIT),
        name="inproj",
    )(x2, norm1_g, mod, mod, w_in_b, cos_t, sin_t, qg, kg, c_col, w_mod, b_mod)


AT_TQ = 512
AT_TK = 8192
AT_TK_ONLINE = 512
SHIFT_LIMIT = 60.0


def _attn_kernel(qt_ref, k_ref, vt_ref, o_ref, qa_scr, p_scr, kmax_scr):
    h = pl.program_id(0)
    i = pl.program_id(1)

    @pl.when((i == 0) & (h % GROUP == 0))
    def _():
        def body(c, mx):
            c0 = pl.multiple_of(c * AT_TK, AT_TK)
            kc = k_ref[pl.ds(c0, AT_TK), :HEAD_DIM].astype(F32)
            n2 = (kc * kc).sum(axis=1, keepdims=True)
            return jnp.maximum(mx, n2.max(axis=0, keepdims=True))
        mx = lax.fori_loop(0, SEQ // AT_TK, body, jnp.zeros((1, 1), F32))
        kmax_scr[...] = jnp.broadcast_to(jnp.sqrt(mx), kmax_scr.shape)

    q = qt_ref[...].astype(F32)
    bound = jnp.sqrt((q * q).sum(axis=0, keepdims=True)) * kmax_scr[0:1, 0:1] * 1.01
    fast = jnp.max(bound) <= SHIFT_LIMIT

    @pl.when(fast)
    def _():
        qa_scr[0:HEAD_DIM, :] = qt_ref[...]
        row = lax.broadcasted_iota(I32, (KA_W - HEAD_DIM, AT_TQ), 0)
        qa_scr[HEAD_DIM:, :] = jnp.where(row == 0, -bound, 0.0).astype(BF16)

        def body(c, _):
            c0 = pl.multiple_of(c * AT_TK, AT_TK)
            s = jnp.dot(k_ref[pl.ds(c0, AT_TK), :], qa_scr[...], preferred_element_type=F32)
            p_scr[pl.ds(c0, AT_TK), :] = jnp.exp2(s).astype(BF16)
            return 0

        lax.fori_loop(0, SEQ // AT_TK, body, 0)
        o = jnp.dot(vt_ref[...], p_scr[...], preferred_element_type=F32)
        o_ref[...] = (o[:HEAD_DIM] * (1.0 / o[HEAD_DIM:HEAD_DIM + 1])).T.astype(BF16)

    @pl.when(jnp.logical_not(fast))
    def _():
        qt = qt_ref[...]

        def chunk(c, carry):
            m, l, acc = carry
            c0 = pl.multiple_of(c * AT_TK_ONLINE, AT_TK_ONLINE)
            s = jnp.dot(k_ref[pl.ds(c0, AT_TK_ONLINE), :HEAD_DIM], qt, preferred_element_type=F32)
            m_new = jnp.maximum(m, s.max(axis=0, keepdims=True))
            alpha = jnp.exp2(m - m_new)
            p = jnp.exp2(s - m_new)
            l = alpha * l + p.sum(axis=0, keepdims=True)
            pv = jnp.dot(vt_ref[:HEAD_DIM, pl.ds(c0, AT_TK_ONLINE)], p.astype(BF16), preferred_element_type=F32)
            return m_new, l, alpha * acc + pv

        init = (jnp.full((1, AT_TQ), -jnp.inf, F32), jnp.zeros((1, AT_TQ), F32),
                jnp.zeros((HEAD_DIM, AT_TQ), F32))
        _, l, acc = lax.fori_loop(0, SEQ // AT_TK_ONLINE, chunk, init)
        o_ref[...] = (acc * (1.0 / l)).T.astype(BF16)


def _attention(qt, k, vt):
    return pl.pallas_call(
        _attn_kernel,
        out_shape=jax.ShapeDtypeStruct((SEQ, ATTN_WIDTH), BF16),
        grid=(N_HEADS, SEQ // AT_TQ),
        in_specs=[
            pl.BlockSpec((HEAD_DIM, AT_TQ), lambda h, i: (h, i)),
            pl.BlockSpec((SEQ, KA_W), lambda h, i: (0, h // GROUP)),
            pl.BlockSpec((VA_H, SEQ), lambda h, i: (h // GROUP, 0)),
        ],
        out_specs=pl.BlockSpec((AT_TQ, HEAD_DIM), lambda h, i: (i, h)),
        scratch_shapes=[
            pltpu.VMEM((KA_W, AT_TQ), BF16),
            pltpu.VMEM((SEQ, AT_TQ), BF16),
            pltpu.VMEM((8, LANES), F32),
        ],
        compiler_params=_params(("arbitrary", "arbitrary")),
        name="attn",
    )(qt, k, vt)


PL_TM = 256
PL_HALO = 16


def _pool_kernel(prev_ref, main_ref, next_ref, wp_ref, scale_ref, o_ref, buf):
    i = pl.program_id(0)
    last = pl.num_programs(0) - 1
    buf[0:PL_HALO, :] = jnp.where(i == 0, 0.0, prev_ref[...].astype(F32))
    buf[PL_HALO:PL_HALO + PL_TM, :] = main_ref[...].astype(F32)
    buf[PL_HALO + PL_TM:, :] = jnp.where(i == last, 0.0, next_ref[...].astype(F32))
    t = i * PL_TM + lax.broadcasted_iota(I32, (PL_TM, 1), 0)
    for gi, w in enumerate(POOL_WINDOWS):
        cols = slice(gi * POOL_GROUP_WIDTH, (gi + 1) * POOL_GROUP_WIDTH)
        win = buf[PL_HALO - w // 2:PL_HALO - w // 2 + PL_TM, cols]
        for d in range(-w // 2 + 1, w // 2):
            win = win + buf[PL_HALO + d:PL_HALO + d + PL_TM, cols]
        lo = jnp.maximum(t - w // 2, 0)
        hi = jnp.minimum(t + w // 2 - 1, SEQ - 1)
        cnt = (hi - lo + 1).astype(F32)
        dlt = win / cnt - buf[PL_HALO:PL_HALO + PL_TM, cols]
        y = jnp.dot(dlt.astype(BF16), wp_ref[gi], preferred_element_type=F32)
        o_ref[:, cols] = (y * scale_ref[:, cols]).astype(BF16)


def _pool(pool_in, w_pool_b, pool_scale):
    nh = PL_TM // PL_HALO
    n_halo_blocks = SEQ // PL_HALO
    return pl.pallas_call(
        _pool_kernel,
        out_shape=jax.ShapeDtypeStruct((SEQ, POOL_WIDTH), BF16),
        grid=(SEQ // PL_TM,),
        in_specs=[
            pl.BlockSpec((PL_HALO, POOL_WIDTH), lambda i: (jnp.maximum(i * nh - 1, 0), 0)),
            pl.BlockSpec((PL_TM, POOL_WIDTH), lambda i: (i, 0)),
            pl.BlockSpec((PL_HALO, POOL_WIDTH), lambda i: (jnp.minimum((i + 1) * nh, n_halo_blocks - 1), 0)),
            pl.BlockSpec((len(POOL_WINDOWS), POOL_GROUP_WIDTH, POOL_GROUP_WIDTH), lambda i: (0, 0, 0)),
            pl.BlockSpec((1, POOL_WIDTH), lambda i: (0, 0)),
        ],
        out_specs=pl.BlockSpec((PL_TM, POOL_WIDTH), lambda i: (i, 0)),
        scratch_shapes=[pltpu.VMEM((PL_TM + 2 * PL_HALO, POOL_WIDTH), F32)],
        compiler_params=_params(("arbitrary",)),
        name="pool",
    )(pool_in, pool_in, pool_in, w_pool_b, pool_scale)


OP_TM = 1024
OP_TN = 512


def _outproj_kernel(a_ref, p_ref, wa_ref, wp_ref, x_ref, g_ref, o_ref):
    acc = jnp.dot(a_ref[...], wa_ref[...], preferred_element_type=F32)
    acc = acc + jnp.dot(p_ref[...], wp_ref[...], preferred_element_type=F32)
    o_ref[...] = x_ref[...] + g_ref[...] * acc


def _outproj(attn, pool, w_out_b, x2, mod):
    return pl.pallas_call(
        _outproj_kernel,
        out_shape=jax.ShapeDtypeStruct((SEQ, D_MODEL), F32),
        grid=(SEQ // OP_TM, D_MODEL // OP_TN),
        in_specs=[
            pl.BlockSpec((OP_TM, ATTN_WIDTH), lambda i, j: (i, 0)),
            pl.BlockSpec((OP_TM, POOL_WIDTH), lambda i, j: (i, 0)),
            pl.BlockSpec((ATTN_WIDTH, OP_TN), lambda i, j: (0, j)),
            pl.BlockSpec((POOL_WIDTH, OP_TN), lambda i, j: (1, j)),
            pl.BlockSpec((OP_TM, OP_TN), lambda i, j: (i, j)),
            pl.BlockSpec((1, OP_TN), lambda i, j: (0, MOD_B_G1 * (D_MODEL // OP_TN) + j)),
        ],
        out_specs=pl.BlockSpec((OP_TM, OP_TN), lambda i, j: (i, j)),
        compiler_params=_params(("arbitrary", "arbitrary")),
        name="outproj",
    )(attn, pool, w_out_b, w_out_b, x2, mod)


N2_TM = 256
RT_PAD = LANES


def _norm2_kernel(x_ref, g_ref, sc_ref, sh_ref, wcat_ref, whi_ref, b_ref, lt_ref, hi_scr, lo_scr,
                  rs_scr, a_scr, s_scr):
    @pl.when(pl.program_id(0) == 0)
    def _():
        _prep_modulation(g_ref, sc_ref, sh_ref, a_scr, s_scr)

    _row_rms(x_ref, rs_scr, N2_TM)

    def body(r, _):
        r0 = pl.multiple_of(r * NORM_ROWS, NORM_ROWS)
        for c in range(D_MODEL // LANES):
            cs = slice(c * LANES, (c + 1) * LANES)
            h = _normed_tile(x_ref, rs_scr, a_scr, s_scr, r0, c)
            hi = h.astype(BF16)
            hi_scr[pl.ds(r0, NORM_ROWS), cs] = hi
            lo_scr[pl.ds(r0, NORM_ROWS), cs] = (h - hi.astype(F32)).astype(BF16)
        return 0
    lax.fori_loop(0, N2_TM // NORM_ROWS, body, 0)
    a = jnp.dot(hi_scr[...], wcat_ref[...], preferred_element_type=F32)
    b = jnp.dot(lo_scr[...], whi_ref[...], preferred_element_type=F32)
    logits = a[:, :RT_PAD] + a[:, RT_PAD:] + b + b_ref[...]
    lt_ref[...] = logits.T[:N_EXPERTS, :]


def _norm2(x1, norm2_g, mod, wcat, whi, b_pad):
    row = lambda n: pl.BlockSpec((1, D_MODEL), lambda i, n=n: (0, n))
    return pl.pallas_call(
        _norm2_kernel,
        out_shape=jax.ShapeDtypeStruct((N_EXPERTS, SEQ), F32),
        grid=(SEQ // N2_TM,),
        in_specs=[
            pl.BlockSpec((N2_TM, D_MODEL), lambda i: (i, 0)),
            pl.BlockSpec((1, D_MODEL), lambda i: (0, 0)),
            row(MOD_B_SC2), row(MOD_B_SH2),
            pl.BlockSpec((D_MODEL, 2 * RT_PAD), lambda i: (0, 0)),
            pl.BlockSpec((D_MODEL, RT_PAD), lambda i: (0, 0)),
            pl.BlockSpec((1, RT_PAD), lambda i: (0, 0)),
        ],
        out_specs=pl.BlockSpec((N_EXPERTS, N2_TM), lambda i: (0, i)),
        scratch_shapes=[
            pltpu.VMEM((N2_TM, D_MODEL), BF16),
            pltpu.VMEM((N2_TM, D_MODEL), BF16),
            pltpu.VMEM((N2_TM, LANES), F32),
            pltpu.VMEM((NORM_ROWS, D_MODEL), F32),
            pltpu.VMEM((NORM_ROWS, D_MODEL), F32),
        ],
        compiler_params=_params(("arbitrary",)),
        name="norm2",
    )(x1, norm2_g, mod, mod, wcat, whi, b_pad)


RT_CH = 1024
RT_SB = 256


SCHED_W = RT_SB
SCHED_BE, SCHED_FIRST, SCHED_SEG, SCHED_NXT, SCHED_META = range(5)


def _route_kernel(lt_ref, dest_ref, gate_ref, sched_ref, lastblk_ref, idx_scr, rank_scr):
    e_col = lax.broadcasted_iota(I32, (N_EXPERTS, RT_CH), 0).astype(F32)
    tri = (lax.broadcasted_iota(I32, (RT_SB, RT_SB), 0) < lax.broadcasted_iota(I32, (RT_SB, RT_SB), 1)).astype(BF16)
    carry = jnp.zeros((N_EXPERTS, 1), F32)
    for c in range(SEQ // RT_CH):
        cs = slice(c * RT_CH, (c + 1) * RT_CH)
        work = lt_ref[:, cs]
        vals = []
        mask = jnp.zeros((N_EXPERTS, RT_CH), F32)
        for k in range(TOP_K):
            m = work.max(axis=0, keepdims=True)
            idx = jnp.where(work == m, e_col, float(N_EXPERTS)).min(axis=0, keepdims=True)
            sel = e_col == idx
            vals.append(m)
            idx_scr[k:k + 1, cs] = idx
            mask = jnp.where(sel, 1.0, mask)
            work = jnp.where(sel, -jnp.inf, work)
        ex = [jnp.exp(v - vals[0]) for v in vals]
        den = ex[0] + ex[1] + ex[2] + ex[3]
        for k in range(TOP_K):
            gate_ref[k:k + 1, cs] = ex[k] / den
        for b in range(RT_CH // RT_SB):
            blk = mask[:, b * RT_SB:(b + 1) * RT_SB]
            pref = jnp.dot(blk.astype(BF16), tri, preferred_element_type=F32)
            rank_scr[:, c * RT_CH + b * RT_SB:c * RT_CH + (b + 1) * RT_SB] = pref + carry
            carry = carry + blk.sum(axis=1, keepdims=True)
    nblk = jnp.floor((carry + (MOE_TM - 1)) * (1.0 / MOE_TM))
    nblk_b = jnp.broadcast_to(nblk, (N_EXPERTS, LANES))
    lower = (lax.broadcasted_iota(I32, (N_EXPERTS, N_EXPERTS), 1) < lax.broadcasted_iota(I32, (N_EXPERTS, N_EXPERTS), 0)).astype(BF16)
    start_blk = jnp.dot(lower, nblk_b.astype(BF16), preferred_element_type=F32)
    start = start_blk[:, 0:1] * float(MOE_TM)

    end_blk = start_blk[:, 0:1] + nblk
    lastblk_ref[...] = jnp.broadcast_to(jnp.where(nblk > 0, end_blk - 1.0, -1.0), (N_EXPERTS, LANES)).astype(I32)
    e_blk = lax.broadcasted_iota(I32, (N_EXPERTS, SCHED_W), 0).astype(F32)
    b_blk = lax.broadcasted_iota(I32, (N_EXPERTS, SCHED_W), 1).astype(F32)
    lane = b_blk[0:1]
    n_used = jnp.sum(nblk, axis=0, keepdims=True)
    be = jnp.minimum(jnp.sum(jnp.where(end_blk <= b_blk, 1.0, 0.0), axis=0, keepdims=True), N_EXPERTS - 1.0)
    prev = jnp.where(lane == 0, -1.0, pltpu.roll(be, 1, 1))
    first = jnp.where((lane < n_used) & (be != prev), 1.0, 0.0)
    excl = jnp.dot(jnp.broadcast_to(first, (8, SCHED_W)).astype(BF16), tri, preferred_element_type=F32)[0:1]
    seg = excl + first - 1.0
    nxt = jnp.min(jnp.where((e_blk > be) & (nblk > 0), e_blk, float(N_EXPERTS)), axis=0, keepdims=True)
    nxt = jnp.where(nxt >= N_EXPERTS, -1.0, nxt)
    n_seg = jnp.sum(first, axis=1, keepdims=True)
    meta = jnp.where(lane == 0, n_used, jnp.where(lane == 1, n_seg, 0.0))
    for r, v in enumerate((be, first, seg, nxt, meta)):
        sched_ref[r:r + 1, :] = v.astype(I32)
    sched_ref[5:8, :] = jnp.zeros((3, SCHED_W), I32)
    for c in range(SEQ // RT_CH):
        cs = slice(c * RT_CH, (c + 1) * RT_CH)
        slot = rank_scr[:, cs] + start
        for k in range(TOP_K):
            sel = e_col == idx_scr[k:k + 1, cs]
            dest_ref[k:k + 1, cs] = jnp.where(sel, slot, 0.0).sum(axis=0, keepdims=True).astype(I32)


def _route(logits_t):
    return pl.pallas_call(
        _route_kernel,
        out_shape=(
            jax.ShapeDtypeStruct((TOP_K, SEQ), I32),
            jax.ShapeDtypeStruct((TOP_K, SEQ), F32),
            jax.ShapeDtypeStruct((8, SCHED_W), I32),
            jax.ShapeDtypeStruct((N_EXPERTS, LANES), I32),
        ),
        scratch_shapes=[pltpu.VMEM((8, SEQ), F32), pltpu.VMEM((N_EXPERTS, SEQ), F32)],
        compiler_params=pltpu.CompilerParams(vmem_limit_bytes=VMEM_LIMIT),
        name="route",
    )(logits_t)


DP_TM = 256
XS_W = D_MODEL // 2
U32 = jnp.uint32


def _dispatch_kernel(dest_ref, lastblk_ref, x_ref, g_ref, sc_ref, sh_ref, xs_hbm, pk, zero_buf, rs_scr, a_scr,
                     s_scr, zsem, sem):
    i = pl.program_id(0)
    par = i % 2

    @pl.when(i == 0)
    def _():
        _prep_modulation(g_ref, sc_ref, sh_ref, a_scr, s_scr)
        zero_buf[...] = jnp.zeros_like(zero_buf)

        def zcopy(e):
            b = jnp.maximum(lastblk_ref[e, 0], 0)
            return pltpu.make_async_copy(zero_buf, xs_hbm.at[pl.ds(pl.multiple_of(b * MOE_TM, MOE_TM), MOE_TM)], zsem)

        def zstart(e, _):
            @pl.when(lastblk_ref[e, 0] >= 0)
            def _():
                zcopy(e).start()
            return 0

        def zwait(e, _):
            @pl.when(lastblk_ref[e, 0] >= 0)
            def _():
                zcopy(e).wait()
            return 0

        lax.fori_loop(0, N_EXPERTS, zstart, 0)
        lax.fori_loop(0, N_EXPERTS, zwait, 0)

    _row_rms(x_ref, rs_scr, DP_TM)

    n_tiles = XS_W // LANES
    assert n_tiles == NORM_ROWS

    def pack_tile(r0, c):
        lo = _normed_tile(x_ref, rs_scr, a_scr, s_scr, r0, c)
        hi = _normed_tile(x_ref, rs_scr, a_scr, s_scr, r0, c + n_tiles)
        lo = lax.bitcast_convert_type(lo.astype(BF16).astype(F32), U32)
        hi = lax.bitcast_convert_type(hi.astype(BF16).astype(F32), U32)
        pk[par, pl.ds(r0, NORM_ROWS), c * LANES:(c + 1) * LANES] = (lo >> 16) | (hi & jnp.uint32(0xFFFF0000))

    def issue_token(u):
        for k in range(TOP_K):
            d = dest_ref[k, i * DP_TM + u]
            pltpu.make_async_copy(pk.at[par, pl.ds(u, 1)], xs_hbm.at[pl.ds(d, 1)], sem.at[par]).start()

    for c in range(n_tiles):
        pack_tile(0, c)

    def group(g, _):
        r0 = pl.multiple_of(g * NORM_ROWS, NORM_ROWS)
        for u in range(NORM_ROWS):
            pack_tile(r0, u)
            issue_token(r0 - NORM_ROWS + u)
        return 0

    lax.fori_loop(1, DP_TM // NORM_ROWS, group, 0)

    def tail(u, _):
        issue_token(DP_TM - NORM_ROWS + u)
        return 0

    lax.fori_loop(0, NORM_ROWS, tail, 0)

    def drain(p):
        for _ in range(TOP_K):
            pltpu.make_async_copy(pk.at[p], xs_hbm.at[pl.ds(0, DP_TM)], sem.at[p]).wait()

    @pl.when(i > 0)
    def _():
        drain(1 - par)

    @pl.when(i == pl.num_programs(0) - 1)
    def _():
        drain(par)


def _dispatch(dest, lastblk, x1, norm2_g, mod):
    row = lambda n: pl.BlockSpec((1, D_MODEL), lambda i, d, lb, n=n: (0, n))
    return pl.pallas_call(
        _dispatch_kernel,
        out_shape=jax.ShapeDtypeStruct((MOE_ROWS, XS_W), U32),
        grid_spec=pltpu.PrefetchScalarGridSpec(
            num_scalar_prefetch=2,
            grid=(SEQ // DP_TM,),
            in_specs=[
                pl.BlockSpec((DP_TM, D_MODEL), lambda i, d, lb: (i, 0)),
                pl.BlockSpec((1, D_MODEL), lambda i, d, lb: (0, 0)),
                row(MOD_B_SC2), row(MOD_B_SH2),
            ],
            out_specs=pl.BlockSpec(memory_space=pl.ANY),
            scratch_shapes=[
                pltpu.VMEM((2, DP_TM, XS_W), U32),
                pltpu.VMEM((MOE_TM, XS_W), U32),
                pltpu.VMEM((DP_TM, LANES), F32),
                pltpu.VMEM((NORM_ROWS, D_MODEL), F32),
                pltpu.VMEM((NORM_ROWS, D_MODEL), F32),
                pltpu.SemaphoreType.DMA,
                pltpu.SemaphoreType.DMA((2,)),
            ],
        ),
        compiler_params=_params(("arbitrary",)),
        name="dispatch",
    )(dest, lastblk, x1, norm2_g, mod, mod)


F1_TF = 512
F2_TN = 4096


CAST_ROWS = 128
WEIGHT_DMA_PRIORITY = 1


def _blocks_used(sched_ref):
    return sched_ref[SCHED_META, 0]


def _used_block(i, sched_ref):
    return jnp.minimum(i, _blocks_used(sched_ref) - 1)


def _stream_expert_weights(j, i, nj, sched_ref, tile_copies, stg, wbuf):
    @pl.when(sched_ref[SCHED_FIRST, i] == 1)
    def _():
        seq = j * sched_ref[SCHED_META, 1] + sched_ref[SCHED_SEG, i]
        slot = seq % 2

        @pl.when(seq == 0)
        def _():
            for cp in tile_copies(sched_ref[SCHED_BE, i], j, slot):
                cp.start(priority=WEIGHT_DMA_PRIORITY)

        for cp in tile_copies(sched_ref[SCHED_BE, i], j, slot):
            cp.wait()

        nxt = sched_ref[SCHED_NXT, i]

        @pl.when(nxt >= 0)
        def _():
            for cp in tile_copies(nxt, j, 1 - slot):
                cp.start(priority=WEIGHT_DMA_PRIORITY)

        @pl.when((nxt < 0) & (j + 1 < nj))
        def _():
            for cp in tile_copies(sched_ref[SCHED_BE, 0], j + 1, 1 - slot):
                cp.start(priority=WEIGHT_DMA_PRIORITY)

        def cast(r, _):
            r0 = pl.multiple_of(r * CAST_ROWS, CAST_ROWS)
            wbuf[pl.ds(r0, CAST_ROWS), :] = stg[slot, pl.ds(r0, CAST_ROWS), :].astype(BF16)
            return 0

        lax.fori_loop(0, wbuf.shape[0] // CAST_ROWS, cast, 0)


F1_NH = D_FF // F1_TF


def _ffn1_kernel(sched_ref, x_ref, bg_ref, bl_ref, w1_hbm, o_ref, stg, wbuf, sem):
    i = pl.program_id(0)

    def slab_copies(e, h):
        return (
            pltpu.make_async_copy(w1_hbm.at[e, :, h * F1_TF:(h + 1) * F1_TF], stg.at[h, :, 0:F1_TF], sem.at[h]),
            pltpu.make_async_copy(w1_hbm.at[e, :, D_FF + h * F1_TF:D_FF + (h + 1) * F1_TF],
                                  stg.at[h, :, F1_TF:2 * F1_TF], sem.at[h]),
        )

    @pl.when(i < _blocks_used(sched_ref))
    def _():
        @pl.when(sched_ref[SCHED_FIRST, i] == 1)
        def _():
            @pl.when(i == 0)
            def _():
                for h in range(F1_NH):
                    for cp in slab_copies(sched_ref[SCHED_BE, 0], h):
                        cp.start(priority=WEIGHT_DMA_PRIORITY)

            nxt = sched_ref[SCHED_NXT, i]
            for h in range(F1_NH):
                for cp in slab_copies(sched_ref[SCHED_BE, i], h):
                    cp.wait()

                def cast(r, _):
                    r0 = pl.multiple_of(r * CAST_ROWS, CAST_ROWS)
                    wbuf[h, pl.ds(r0, CAST_ROWS), :] = stg[h, pl.ds(r0, CAST_ROWS), :].astype(BF16)
                    return 0

                lax.fori_loop(0, D_MODEL // CAST_ROWS, cast, 0)

                @pl.when(nxt >= 0)
                def _():
                    for cp in slab_copies(nxt, h):
                        cp.start(priority=WEIGHT_DMA_PRIORITY)

        xp = x_ref[...]
        x_lo = lax.bitcast_convert_type(xp << 16, F32).astype(BF16)
        x_hi = lax.bitcast_convert_type(xp & jnp.uint32(0xFFFF0000), F32).astype(BF16)
        for h in range(F1_NH):
            cols = slice(h * F1_TF, (h + 1) * F1_TF)
            y = jnp.dot(x_lo, wbuf[h, :XS_W, :], preferred_element_type=F32)
            y = y + jnp.dot(x_hi, wbuf[h, XS_W:, :], preferred_element_type=F32)
            glu = jnp.minimum(y[:, :F1_TF] + bg_ref[:, cols], SWIGLU_LIMIT)
            lin = jnp.clip(y[:, F1_TF:] + bl_ref[:, cols], -SWIGLU_LIMIT, SWIGLU_LIMIT)
            o_ref[:, cols] = (glu * jax.nn.sigmoid(SWIGLU_ALPHA * glu) * (lin + 1.0)).astype(BF16)


def _ffn1(sched, xs, w1, b1_3):
    expert = lambda i, s: s[SCHED_BE, _used_block(i, s)]
    return pl.pallas_call(
        _ffn1_kernel,
        out_shape=jax.ShapeDtypeStruct((MOE_ROWS, D_FF), BF16),
        grid_spec=pltpu.PrefetchScalarGridSpec(
            num_scalar_prefetch=1,
            grid=(MOE_NB,),
            in_specs=[
                pl.BlockSpec((MOE_TM, XS_W), lambda i, s: (_used_block(i, s), 0)),
                pl.BlockSpec((None, 1, D_FF), lambda i, s: (expert(i, s), 0, 0)),
                pl.BlockSpec((None, 1, D_FF), lambda i, s: (expert(i, s), 0, 1)),
                pl.BlockSpec(memory_space=pl.ANY),
            ],
            out_specs=pl.BlockSpec((MOE_TM, D_FF), lambda i, s: (_used_block(i, s), 0)),
            scratch_shapes=[
                pltpu.VMEM((F1_NH, D_MODEL, 2 * F1_TF), F32),
                pltpu.VMEM((F1_NH, D_MODEL, 2 * F1_TF), BF16),
                pltpu.SemaphoreType.DMA((F1_NH,)),
            ],
        ),
        compiler_params=_params(("arbitrary",), vmem=BIG_VMEM_LIMIT),
        name="ffn1",
    )(sched, xs, b1_3, b1_3, w1)


def _ffn2_kernel(sched_ref, a_ref, b_ref, w2_hbm, o_ref, stg, wbuf, sem):
    j = pl.program_id(0)
    i = pl.program_id(1)
    nj = pl.num_programs(0)

    def tile_copies(e, jj, slot):
        c0 = pl.multiple_of(jj * F2_TN, F2_TN)
        return (pltpu.make_async_copy(w2_hbm.at[e, :, pl.ds(c0, F2_TN)], stg.at[slot], sem.at[slot]),)

    @pl.when(i < _blocks_used(sched_ref))
    def _():
        _stream_expert_weights(j, i, nj, sched_ref, tile_copies, stg, wbuf)
        o_ref[...] = jnp.dot(a_ref[...], wbuf[...], preferred_element_type=F32) + b_ref[...]


def _ffn2(sched, act, w2, b2_3):
    return pl.pallas_call(
        _ffn2_kernel,
        out_shape=jax.ShapeDtypeStruct((MOE_ROWS, D_MODEL), F32),
        grid_spec=pltpu.PrefetchScalarGridSpec(
            num_scalar_prefetch=1,
            grid=(D_MODEL // F2_TN, MOE_NB),
            in_specs=[
                pl.BlockSpec((MOE_TM, D_FF), lambda j, i, s: (_used_block(i, s), 0)),
                pl.BlockSpec((None, 1, F2_TN), lambda j, i, s: (s[SCHED_BE, _used_block(i, s)], 0, j)),
                pl.BlockSpec(memory_space=pl.ANY),
            ],
            out_specs=pl.BlockSpec((MOE_TM, F2_TN), lambda j, i, s: (_used_block(i, s), j)),
            scratch_shapes=[
                pltpu.VMEM((2, D_FF, F2_TN), F32),
                pltpu.VMEM((D_FF, F2_TN), BF16),
                pltpu.SemaphoreType.DMA((2,)),
            ],
        ),
        compiler_params=_params(("arbitrary", "arbitrary")),
        name="ffn2",
    )(sched, act, b2_3, w2)


CB_TM = 128


def _combine_kernel(dest_ref, x_ref, gate_ref, g2_ref, ys_hbm, o_ref, buf, gate_scr, g2_scr, sem):
    i = pl.program_id(0)
    last = pl.num_programs(0) - 1
    par = i % 2
    n_tiles = D_MODEL // LANES
    assert n_tiles == 8 * TOP_K

    def row_start(tile, p, u, k):
        d = dest_ref[k, tile * CB_TM + u]
        pltpu.make_async_copy(ys_hbm.at[pl.ds(d, 1)], buf.at[p, k, pl.ds(u, 1)], sem.at[p]).start()

    def drain(p):
        for k in range(TOP_K):
            pltpu.make_async_copy(ys_hbm.at[pl.ds(0, CB_TM)], buf.at[p, k], sem.at[p]).wait()

    @pl.when(i == 0)
    def _():
        def body(u, _):
            for k in range(TOP_K):
                row_start(0, 0, u, k)
            return 0
        lax.fori_loop(0, CB_TM, body, 0)
        g2_scr[...] = jnp.broadcast_to(g2_ref[...], g2_scr.shape)

    drain(par)

    for k in range(TOP_K):
        gate_scr[k] = jnp.broadcast_to(gate_ref[:, k:k + 1], (CB_TM, LANES))

    nxt = jnp.minimum(i + 1, last)

    def body(r, _):
        r0 = pl.multiple_of(r * 8, 8)
        rows = pl.ds(r0, 8)
        gk = [gate_scr[k, rows, :] for k in range(TOP_K)]
        for c in range(n_tiles):
            cs = slice(c * LANES, (c + 1) * LANES)
            y = buf[par, 0, rows, cs] * gk[0]
            for k in range(1, TOP_K):
                y = y + buf[par, k, rows, cs] * gk[k]
            o_ref[rows, cs] = x_ref[rows, cs] + g2_scr[:, cs] * y
            row_start(nxt, 1 - par, r0 + c // TOP_K, c % TOP_K)
        return 0

    lax.fori_loop(0, CB_TM // 8, body, 0)

    @pl.when(i == last)
    def _():
        drain(1 - par)


def _combine(dest, x1, gates_t, mod, ys):
    return pl.pallas_call(
        _combine_kernel,
        out_shape=jax.ShapeDtypeStruct((SEQ, D_MODEL), F32),
        grid_spec=pltpu.PrefetchScalarGridSpec(
            num_scalar_prefetch=1,
            grid=(SEQ // CB_TM,),
            in_specs=[
                pl.BlockSpec((CB_TM, D_MODEL), lambda i, d: (i, 0)),
                pl.BlockSpec((CB_TM, TOP_K), lambda i, d: (i, 0)),
                pl.BlockSpec((1, D_MODEL), lambda i, d: (0, MOD_B_G2)),
                pl.BlockSpec(memory_space=pl.ANY),
            ],
            out_specs=pl.BlockSpec((CB_TM, D_MODEL), lambda i, d: (i, 0)),
            scratch_shapes=[
                pltpu.VMEM((2, TOP_K, CB_TM, D_MODEL), F32),
                pltpu.VMEM((TOP_K, CB_TM, LANES), F32),
                pltpu.VMEM((8, D_MODEL), F32),
                pltpu.SemaphoreType.DMA((2,)),
            ],
        ),
        compiler_params=_params(("arbitrary",)),
        name="combine",
    )(dest, x1, gates_t, mod, ys)


def _rope_tables():
    t = jnp.arange(SEQ, dtype=I32)
    row = (t // GRID_W).astype(F32)
    col = (t % GRID_W).astype(F32)
    inv_freq = ROPE_THETA ** (-jnp.arange(0, ROPE_AXIS_DIM, 2, dtype=F32) / ROPE_AXIS_DIM)
    ang_r = inv_freq[:, None] * row[None, :]
    ang_c = inv_freq[:, None] * col[None, :]
    cos_t = jnp.concatenate([jnp.cos(ang_r), jnp.cos(ang_r), jnp.cos(ang_c), jnp.cos(ang_c)], axis=0)
    sin_t = jnp.concatenate([-jnp.sin(ang_r), jnp.sin(ang_r), -jnp.sin(ang_c), jnp.sin(ang_c)], axis=0)
    return cos_t, sin_t


def kernel(x, c, w_mod, b_mod, norm1_g, w_in, q_norm_g, k_norm_g, w_pool, pool_scale, w_out, norm2_g,
           w_router, b_router, w1, b1, w2, b2):
    assert x.shape == (1, SEQ, D_MODEL) and w_mod.shape[0] == 1
    x2 = x[0]
    cos_t, sin_t = _rope_tables()

    c_col = c.reshape(D_MODEL, 1)
    mod_a = _mod(c_col, w_mod[0], b_mod)

    pool_in, qt, k, vt, mod = _inproj(x2, mod_a, norm1_g, w_in[0].astype(BF16), cos_t, sin_t,
                                      q_norm_g.reshape(HEAD_DIM, 1), k_norm_g.reshape(HEAD_DIM, 1),
                                      c.reshape(D_MODEL // LANES, LANES), w_mod[0], b_mod)
    attn = _attention(qt, k, vt)
    pool = _pool(pool_in, w_pool[0].astype(BF16), pool_scale)
    x1 = _outproj(attn, pool, w_out[0].astype(BF16), x2, mod)

    wr = w_router[0]
    wr_hi = wr.astype(BF16)
    wr_lo = (wr - wr_hi.astype(F32)).astype(BF16)
    pad = lambda a: jnp.pad(a, ((0, 0), (0, RT_PAD - N_EXPERTS)))
    wcat = jnp.concatenate([pad(wr_hi), pad(wr_lo)], axis=1)
    logits_t = _norm2(x1, norm2_g, mod, wcat, pad(wr_hi), pad(b_router))

    dest, gates, sched, lastblk = _route(logits_t)

    xs = _dispatch(dest, lastblk, x1, norm2_g, mod)
    act = _ffn1(sched, xs, w1[0], b1[0].reshape(N_EXPERTS, 1, 2 * D_FF))
    ys = _ffn2(sched, act, w2[0], b2[0].reshape(N_EXPERTS, 1, D_MODEL))
    out = _combine(dest, x1, gates.T, mod, ys)
    return out[None]
```

```python
import functools
import math

import jax
import jax.numpy as jnp
from jax import lax
from jax.experimental import pallas as pl
from jax.experimental.pallas import tpu as pltpu

F32 = jnp.float32
BF16 = jnp.bfloat16
I32 = jnp.int32

D_MODEL = 4096
SEQ = 8192
POOL_WIDTH = 2048
ATTN_WIDTH = 2048
HEAD_DIM = 128
N_HEADS = 16
N_KV_HEADS = 4
GROUP = N_HEADS // N_KV_HEADS
KV_WIDTH = N_KV_HEADS * HEAD_DIM
IN_WIDTH = POOL_WIDTH + ATTN_WIDTH + 2 * KV_WIDTH
POOL_WINDOWS = (2, 4, 8, 16)
POOL_GROUP_WIDTH = POOL_WIDTH // len(POOL_WINDOWS)
GRID_W = 64
ROPE_THETA = 10000.0
ROPE_AXIS_DIM = HEAD_DIM // 2
N_EXPERTS = 32
TOP_K = 4
D_FF = D_MODEL // 4
SWIGLU_ALPHA = 1.702
SWIGLU_LIMIT = 7.0
N_MOD = 6
EPS = 1e-6

LANES = 128
VMEM_LIMIT = 56 * 1024 * 1024
BIG_VMEM_LIMIT = 62 * 1024 * 1024

Q_SCALE = (HEAD_DIM ** -0.5) * math.log2(math.e)

MOE_TM = 256
MOE_NB = SEQ * TOP_K // MOE_TM + N_EXPERTS
MOE_ROWS = MOE_NB * MOE_TM


def _params(sem, vmem=VMEM_LIMIT):
    return pltpu.CompilerParams(dimension_semantics=sem, vmem_limit_bytes=vmem)


MOD_TN = 1024
MOD_KC = 256


def _mod_kernel(c_ref, w_ref, b_ref, o_ref):
    def body(k, acc):
        r = pl.multiple_of(k * MOD_KC, MOD_KC)
        ck = c_ref[pl.ds(r, MOD_KC), :]
        ck = ck * jax.nn.sigmoid(ck)
        p = w_ref[pl.ds(r, MOD_KC), :] * ck
        return acc + p.reshape(MOD_KC // 8, 8, MOD_TN).sum(axis=0)

    acc = lax.fori_loop(0, D_MODEL // MOD_KC, body, jnp.zeros((8, MOD_TN), F32))
    o_ref[...] = acc.sum(axis=0, keepdims=True) + b_ref[...]


MOD_A = 2
MOD_B = N_MOD - MOD_A
MOD_B_G1, MOD_B_SH2, MOD_B_SC2, MOD_B_G2 = range(MOD_B)


def _mod(c_col, w_mod, b_mod):
    n = MOD_A * D_MODEL
    return pl.pallas_call(
        _mod_kernel,
        out_shape=jax.ShapeDtypeStruct((1, n), F32),
        grid=(n // MOD_TN,),
        in_specs=[
            pl.BlockSpec((D_MODEL, 1), lambda j: (0, 0)),
            pl.BlockSpec((D_MODEL, MOD_TN), lambda j: (0, j)),
            pl.BlockSpec((1, MOD_TN), lambda j: (0, j)),
        ],
        out_specs=pl.BlockSpec((1, MOD_TN), lambda j: (0, j)),
        compiler_params=_params(("arbitrary",)),
        name="mod",
    )(c_col, w_mod, b_mod)


IP_TM = 512
IP_TN = 1024
IP_NJ = IN_WIDTH // IP_TN
IP_J_Q = POOL_WIDTH // IP_TN
IP_J_KV = IP_J_Q + ATTN_WIDTH // IP_TN
NORM_ROWS = 16
BF16_SUBLANES = 16
KA_W = 2 * HEAD_DIM
VA_H = HEAD_DIM + BF16_SUBLANES


def _prep_modulation(g_ref, sc_ref, sh_ref, a_scr, s_scr):
    a_scr[...] = jnp.broadcast_to(g_ref[...] * (1.0 + sc_ref[...]), a_scr.shape)
    s_scr[...] = jnp.broadcast_to(sh_ref[...], s_scr.shape)


def _row_rms(x_ref, rs_scr, n_rows):
    def body(r, _):
        r0 = pl.multiple_of(r * NORM_ROWS, NORM_ROWS)
        width = x_ref.shape[1]
        parts = []
        for c in range(width // LANES):
            xc = x_ref[pl.ds(r0, NORM_ROWS), c * LANES:(c + 1) * LANES]
            parts.append(xc * xc)
        while len(parts) > 1:
            parts = [parts[p] + parts[p + 1] for p in range(0, len(parts), 2)]
        rs_scr[pl.ds(r0, NORM_ROWS), :] = parts[0]
        return 0
    lax.fori_loop(0, n_rows // NORM_ROWS, body, 0)
    ms = jnp.sum(rs_scr[...], axis=-1, keepdims=True) * (1.0 / x_ref.shape[1])
    rs_scr[...] = jnp.broadcast_to(lax.rsqrt(ms + EPS), rs_scr.shape)


def _normed_tile(x_ref, rs_scr, a_scr, s_scr, r0, c):
    cs = slice(c * LANES, (c + 1) * LANES)
    return x_ref[pl.ds(r0, NORM_ROWS), cs] * rs_scr[pl.ds(r0, NORM_ROWS), :] * a_scr[:, cs] + s_scr[:, cs]


def _norm_rope_t(xt, g_col, cos_t, sin_t):
    ms = jnp.mean(xt * xt, axis=0, keepdims=True)
    y = xt * lax.rsqrt(ms + EPS) * g_col
    q = ROPE_AXIS_DIM // 2
    partner = jnp.concatenate([y[q:2 * q], y[0:q], y[3 * q:4 * q], y[2 * q:3 * q]], axis=0)
    return y * cos_t + partner * sin_t


MB_TN = 256
MB_STEPS = MOD_B * D_MODEL // MB_TN
MB_KC = 256


def _tree_sum(parts):
    while len(parts) > 1:
        parts = [parts[p] + parts[p + 1] for p in range(0, len(parts), 2)]
    return parts[0]


def _later_modulation(t, c_ref, wm_ref, bm_ref, mb_ref, cb_scr):
    @pl.when(t == 0)
    def _():
        for r in range(D_MODEL // LANES):
            cv = c_ref[r:r + 1, :]
            cb_scr[r * LANES:(r + 1) * LANES, :] = jnp.broadcast_to(cv * jax.nn.sigmoid(cv), (LANES, LANES)).T

    @pl.when(t < MB_STEPS)
    def _():
        def body(k, accs):
            r0 = pl.multiple_of(k * MB_KC, MB_KC)
            cb = cb_scr[pl.ds(r0, MB_KC), :]
            out = []
            for n in range(MB_TN // LANES):
                p = wm_ref[pl.ds(r0, MB_KC), n * LANES:(n + 1) * LANES] * cb
                out.append(accs[n] + _tree_sum([p[g * 8:(g + 1) * 8] for g in range(MB_KC // 8)]))
            return tuple(out)

        zero = jnp.zeros((8, LANES), F32)
        accs = lax.fori_loop(0, D_MODEL // MB_KC, body, (zero,) * (MB_TN // LANES))
        for n in range(MB_TN // LANES):
            cs = slice(n * LANES, (n + 1) * LANES)
            mb_ref[:, cs] = accs[n].sum(axis=0, keepdims=True) + bm_ref[:, cs]


def _inproj_kernel(x_ref, g_ref, sc_ref, sh_ref, w_ref, cos_ref, sin_ref, qg_ref, kg_ref, c_ref, wm_ref, bm_ref,
                   pool_ref, qt_ref, k_ref, vt_ref, mb_ref, h_scr, rs_scr, a_scr, s_scr, cb_scr):
    j = pl.program_id(1)
    _later_modulation(pl.program_id(0) * IP_NJ + j, c_ref, wm_ref, bm_ref, mb_ref, cb_scr)

    @pl.when(j == 0)
    def _():
        _prep_modulation(g_ref, sc_ref, sh_ref, a_scr, s_scr)
        _row_rms(x_ref, rs_scr, IP_TM)

        def body(r, _):
            r0 = pl.multiple_of(r * NORM_ROWS, NORM_ROWS)
            for c in range(D_MODEL // LANES):
                h_scr[pl.ds(r0, NORM_ROWS), c * LANES:(c + 1) * LANES] = _normed_tile(
                    x_ref, rs_scr, a_scr, s_scr, r0, c).astype(BF16)
            return 0
        lax.fori_loop(0, IP_TM // NORM_ROWS, body, 0)

    acc = jnp.dot(h_scr[...], w_ref[...], preferred_element_type=F32)

    @pl.when(j < IP_J_Q)
    def _():
        pool_ref[...] = acc.astype(BF16)

    @pl.when((j >= IP_J_Q) & (j < IP_J_KV))
    def _():
        for hh in range(IP_TN // HEAD_DIM):
            sl = slice(hh * HEAD_DIM, (hh + 1) * HEAD_DIM)
            r = _norm_rope_t(acc[:, sl].T, qg_ref[...], cos_ref[...], sin_ref[...]) * Q_SCALE
            qt_ref[sl, :] = r.astype(BF16)

    @pl.when(j == IP_J_KV)
    def _():
        lane = lax.broadcasted_iota(I32, (IP_TM, HEAD_DIM), 1)
        one_col = jnp.where(lane == 0, 1.0, 0.0).astype(BF16)
        for hh in range(N_KV_HEADS):
            sl = slice(hh * HEAD_DIM, (hh + 1) * HEAD_DIM)
            k_ref[:, hh * KA_W:hh * KA_W + HEAD_DIM] = _norm_rope_t(
                acc[:, sl].T, kg_ref[...], cos_ref[...], sin_ref[...]).T.astype(BF16)
            k_ref[:, hh * KA_W + HEAD_DIM:(hh + 1) * KA_W] = one_col
        for hh in range(N_KV_HEADS):
            sl = slice(KV_WIDTH + hh * HEAD_DIM, KV_WIDTH + (hh + 1) * HEAD_DIM)
            vt_ref[hh * VA_H:hh * VA_H + HEAD_DIM, :] = acc[:, sl].T.astype(BF16)
            vt_ref[hh * VA_H + HEAD_DIM:(hh + 1) * VA_H, :] = jnp.ones((VA_H - HEAD_DIM, IP_TM), BF16)


def _inproj(x2, mod, norm1_g, w_in_b, cos_t, sin_t, qg, kg, c_col, w_mod, b_mod):
    row = lambda n: pl.BlockSpec((1, D_MODEL), lambda i, j, n=n: (0, n))
    mb_tile = lambda i, j: jnp.minimum(i * IP_NJ + j, MB_STEPS - 1)
    mb_first = MOD_A * D_MODEL // MB_TN
    return pl.pallas_call(
        _inproj_kernel,
        out_shape=(
            jax.ShapeDtypeStruct((SEQ, POOL_WIDTH), BF16),
            jax.ShapeDtypeStruct((ATTN_WIDTH, SEQ), BF16),
            jax.ShapeDtypeStruct((SEQ, N_KV_HEADS * KA_W), BF16),
            jax.ShapeDtypeStruct((N_KV_HEADS * VA_H, SEQ), BF16),
            jax.ShapeDtypeStruct((1, MOD_B * D_MODEL), F32),
        ),
        grid=(SEQ // IP_TM, IP_NJ),
        in_specs=[
            pl.BlockSpec((IP_TM, D_MODEL), lambda i, j: (i, 0)),
            pl.BlockSpec((1, D_MODEL), lambda i, j: (0, 0)),
            row(1), row(0),
            pl.BlockSpec((D_MODEL, IP_TN), lambda i, j: (0, j)),
            pl.BlockSpec((HEAD_DIM, IP_TM), lambda i, j: (0, i)),
            pl.BlockSpec((HEAD_DIM, IP_TM), lambda i, j: (0, i)),
            pl.BlockSpec((HEAD_DIM, 1), lambda i, j: (0, 0)),
            pl.BlockSpec((HEAD_DIM, 1), lambda i, j: (0, 0)),
            pl.BlockSpec((D_MODEL // LANES, LANES), lambda i, j: (0, 0)),
            pl.BlockSpec((D_MODEL, MB_TN), lambda i, j: (0, mb_first + mb_tile(i, j))),
            pl.BlockSpec((1, MB_TN), lambda i, j: (0, mb_first + mb_tile(i, j))),
        ],
        out_specs=(
            pl.BlockSpec((IP_TM, IP_TN), lambda i, j: (i, jnp.minimum(j, IP_J_Q - 1))),
            pl.BlockSpec((IP_TN, IP_TM), lambda i, j: (jnp.clip(j - IP_J_Q, 0, IP_J_KV - IP_J_Q - 1), i)),
            pl.BlockSpec((IP_TM, N_KV_HEADS * KA_W), lambda i, j: (i, 0)),
            pl.BlockSpec((N_KV_HEADS * VA_H, IP_TM), lambda i, j: (0, i)),
            pl.BlockSpec((1, MB_TN), lambda i, j: (0, mb_tile(i, j))),
        ),
        scratch_shapes=[
            pltpu.VMEM((IP_TM, D_MODEL), BF16),
            pltpu.VMEM((IP_TM, LANES), F32),
            pltpu.VMEM((NORM_ROWS, D_MODEL), F32),
            pltpu.VMEM((NORM_ROWS, D_MODEL), F32),
            pltpu.VMEM((D_MODEL, LANES), F32),
        ],
        compiler_params=_params(("arbitrary", "arbitrary"), vmem=BIG_VMEM_LIMIT),
        name="inproj",
    )(x2, norm1_g, mod, mod, w_in_b, cos_t, sin_t, qg, kg, c_col, w_mod, b_mod)


AT_TQ = 512
AT_TK = 8192
AT_TK_ONLINE = 512
SHIFT_LIMIT = 60.0


def _attn_kernel(qt_ref, k_ref, vt_ref, o_ref, qa_scr, p_scr, kmax_scr):
    h = pl.program_id(0)
    i = pl.program_id(1)

    @pl.when((i == 0) & (h % GROUP == 0))
    def _():
        def body(c, mx):
            c0 = pl.multiple_of(c * AT_TK, AT_TK)
            kc = k_ref[pl.ds(c0, AT_TK), :HEAD_DIM].astype(F32)
            n2 = (kc * kc).sum(axis=1, keepdims=True)
            return jnp.maximum(mx, n2.max(axis=0, keepdims=True))
        mx = lax.fori_loop(0, SEQ // AT_TK, body, jnp.zeros((1, 1), F32))
        kmax_scr[...] = jnp.broadcast_to(jnp.sqrt(mx), kmax_scr.shape)

    q = qt_ref[...].astype(F32)
    bound = jnp.sqrt((q * q).sum(axis=0, keepdims=True)) * kmax_scr[0:1, 0:1] * 1.01
    fast = jnp.max(bound) <= SHIFT_LIMIT

    @pl.when(fast)
    def _():
        qa_scr[0:HEAD_DIM, :] = qt_ref[...]
        row = lax.broadcasted_iota(I32, (KA_W - HEAD_DIM, AT_TQ), 0)
        qa_scr[HEAD_DIM:, :] = jnp.where(row == 0, -bound, 0.0).astype(BF16)

        def body(c, _):
            c0 = pl.multiple_of(c * AT_TK, AT_TK)
            s = jnp.dot(k_ref[pl.ds(c0, AT_TK), :], qa_scr[...], preferred_element_type=F32)
            p_scr[pl.ds(c0, AT_TK), :] = jnp.exp2(s).astype(BF16)
            return 0

        lax.fori_loop(0, SEQ // AT_TK, body, 0)
        o = jnp.dot(vt_ref[...], p_scr[...], preferred_element_type=F32)
        o_ref[...] = (o[:HEAD_DIM] * (1.0 / o[HEAD_DIM:HEAD_DIM + 1])).T.astype(BF16)

    @pl.when(jnp.logical_not(fast))
    def _():
        qt = qt_ref[...]

        def chunk(c, carry):
            m, l, acc = carry
            c0 = pl.multiple_of(c * AT_TK_ONLINE, AT_TK_ONLINE)
            s = jnp.dot(k_ref[pl.ds(c0, AT_TK_ONLINE), :HEAD_DIM], qt, preferred_element_type=F32)
            m_new = jnp.maximum(m, s.max(axis=0, keepdims=True))
            alpha = jnp.exp2(m - m_new)
            p = jnp.exp2(s - m_new)
            l = alpha * l + p.sum(axis=0, keepdims=True)
            pv = jnp.dot(vt_ref[:HEAD_DIM, pl.ds(c0, AT_TK_ONLINE)], p.astype(BF16), preferred_element_type=F32)
            return m_new, l, alpha * acc + pv

        init = (jnp.full((1, AT_TQ), -jnp.inf, F32), jnp.zeros((1, AT_TQ), F32),
                jnp.zeros((HEAD_DIM, AT_TQ), F32))
        _, l, acc = lax.fori_loop(0, SEQ // AT_TK_ONLINE, chunk, init)
        o_ref[...] = (acc * (1.0 / l)).T.astype(BF16)


def _attention(qt, k, vt):
    return pl.pallas_call(
        _attn_kernel,
        out_shape=jax.ShapeDtypeStruct((SEQ, ATTN_WIDTH), BF16),
        grid=(N_HEADS, SEQ // AT_TQ),
        in_specs=[
            pl.BlockSpec((HEAD_DIM, AT_TQ), lambda h, i: (h, i)),
            pl.BlockSpec((SEQ, KA_W), lambda h, i: (0, h // GROUP)),
            pl.BlockSpec((VA_H, SEQ), lambda h, i: (h // GROUP, 0)),
        ],
        out_specs=pl.BlockSpec((AT_TQ, HEAD_DIM), lambda h, i: (i, h)),
        scratch_shapes=[
            pltpu.VMEM((KA_W, AT_TQ), BF16),
            pltpu.VMEM((SEQ, AT_TQ), BF16),
            pltpu.VMEM((8, LANES), F32),
        ],
        compiler_params=_params(("arbitrary", "arbitrary")),
        name="attn",
    )(qt, k, vt)


PL_TM = 256
PL_HALO = 16


def _pool_kernel(prev_ref, main_ref, next_ref, wp_ref, scale_ref, o_ref, buf):
    i = pl.program_id(0)
    last = pl.num_programs(0) - 1
    buf[0:PL_HALO, :] = jnp.where(i == 0, 0.0, prev_ref[...].astype(F32))
    buf[PL_HALO:PL_HALO + PL_TM, :] = main_ref[...].astype(F32)
    buf[PL_HALO + PL_TM:, :] = jnp.where(i == last, 0.0, next_ref[...].astype(F32))
    t = i * PL_TM + lax.broadcasted_iota(I32, (PL_TM, 1), 0)
    for gi, w in enumerate(POOL_WINDOWS):
        cols = slice(gi * POOL_GROUP_WIDTH, (gi + 1) * POOL_GROUP_WIDTH)
        win = buf[PL_HALO - w // 2:PL_HALO - w // 2 + PL_TM, cols]
        for d in range(-w // 2 + 1, w // 2):
            win = win + buf[PL_HALO + d:PL_HALO + d + PL_TM, cols]
        lo = jnp.maximum(t - w // 2, 0)
        hi = jnp.minimum(t + w // 2 - 1, SEQ - 1)
        cnt = (hi - lo + 1).astype(F32)
        dlt = win / cnt - buf[PL_HALO:PL_HALO + PL_TM, cols]
        y = jnp.dot(dlt.astype(BF16), wp_ref[gi], preferred_element_type=F32)
        o_ref[:, cols] = (y * scale_ref[:, cols]).astype(BF16)


def _pool(pool_in, w_pool_b, pool_scale):
    nh = PL_TM // PL_HALO
    n_halo_blocks = SEQ // PL_HALO
    return pl.pallas_call(
        _pool_kernel,
        out_shape=jax.ShapeDtypeStruct((SEQ, POOL_WIDTH), BF16),
        grid=(SEQ // PL_TM,),
        in_specs=[
            pl.BlockSpec((PL_HALO, POOL_WIDTH), lambda i: (jnp.maximum(i * nh - 1, 0), 0)),
            pl.BlockSpec((PL_TM, POOL_WIDTH), lambda i: (i, 0)),
            pl.BlockSpec((PL_HALO, POOL_WIDTH), lambda i: (jnp.minimum((i + 1) * nh, n_halo_blocks - 1), 0)),
            pl.BlockSpec((len(POOL_WINDOWS), POOL_GROUP_WIDTH, POOL_GROUP_WIDTH), lambda i: (0, 0, 0)),
            pl.BlockSpec((1, POOL_WIDTH), lambda i: (0, 0)),
        ],
        out_specs=pl.BlockSpec((PL_TM, POOL_WIDTH), lambda i: (i, 0)),
        scratch_shapes=[pltpu.VMEM((PL_TM + 2 * PL_HALO, POOL_WIDTH), F32)],
        compiler_params=_params(("arbitrary",)),
        name="pool",
    )(pool_in, pool_in, pool_in, w_pool_b, pool_scale)


OP_TM = 1024
OP_TN = 512


def _outproj_kernel(a_ref, p_ref, wa_ref, wp_ref, x_ref, g_ref, o_ref):
    acc = jnp.dot(a_ref[...], wa_ref[...], preferred_element_type=F32)
    acc = acc + jnp.dot(p_ref[...], wp_ref[...], preferred_element_type=F32)
    o_ref[...] = x_ref[...] + g_ref[...] * acc


def _outproj(attn, pool, w_out_b, x2, mod):
    return pl.pallas_call(
        _outproj_kernel,
        out_shape=jax.ShapeDtypeStruct((SEQ, D_MODEL), F32),
        grid=(SEQ // OP_TM, D_MODEL // OP_TN),
        in_specs=[
            pl.BlockSpec((OP_TM, ATTN_WIDTH), lambda i, j: (i, 0)),
            pl.BlockSpec((OP_TM, POOL_WIDTH), lambda i, j: (i, 0)),
            pl.BlockSpec((ATTN_WIDTH, OP_TN), lambda i, j: (0, j)),
            pl.BlockSpec((POOL_WIDTH, OP_TN), lambda i, j: (1, j)),
            pl.BlockSpec((OP_TM, OP_TN), lambda i, j: (i, j)),
            pl.BlockSpec((1, OP_TN), lambda i, j: (0, MOD_B_G1 * (D_MODEL // OP_TN) + j)),
        ],
        out_specs=pl.BlockSpec((OP_TM, OP_TN), lambda i, j: (i, j)),
        compiler_params=_params(("arbitrary", "arbitrary")),
        name="outproj",
    )(attn, pool, w_out_b, w_out_b, x2, mod)


N2_TM = 256
RT_PAD = LANES


def _norm2_kernel(x_ref, g_ref, sc_ref, sh_ref, wcat_ref, whi_ref, b_ref, lt_ref, hi_scr, lo_scr,
                  rs_scr, a_scr, s_scr):
    @pl.when(pl.program_id(0) == 0)
    def _():
        _prep_modulation(g_ref, sc_ref, sh_ref, a_scr, s_scr)

    _row_rms(x_ref, rs_scr, N2_TM)

    def body(r, _):
        r0 = pl.multiple_of(r * NORM_ROWS, NORM_ROWS)
        for c in range(D_MODEL // LANES):
            cs = slice(c * LANES, (c + 1) * LANES)
            h = _normed_tile(x_ref, rs_scr, a_scr, s_scr, r0, c)
            hi = h.astype(BF16)
            hi_scr[pl.ds(r0, NORM_ROWS), cs] = hi
            lo_scr[pl.ds(r0, NORM_ROWS), cs] = (h - hi.astype(F32)).astype(BF16)
        return 0
    lax.fori_loop(0, N2_TM // NORM_ROWS, body, 0)
    a = jnp.dot(hi_scr[...], wcat_ref[...], preferred_element_type=F32)
    b = jnp.dot(lo_scr[...], whi_ref[...], preferred_element_type=F32)
    logits = a[:, :RT_PAD] + a[:, RT_PAD:] + b + b_ref[...]
    lt_ref[...] = logits.T[:N_EXPERTS, :]


def _norm2(x1, norm2_g, mod, wcat, whi, b_pad):
    row = lambda n: pl.BlockSpec((1, D_MODEL), lambda i, n=n: (0, n))
    return pl.pallas_call(
        _norm2_kernel,
        out_shape=jax.ShapeDtypeStruct((N_EXPERTS, SEQ), F32),
        grid=(SEQ // N2_TM,),
        in_specs=[
            pl.BlockSpec((N2_TM, D_MODEL), lambda i: (i, 0)),
            pl.BlockSpec((1, D_MODEL), lambda i: (0, 0)),
            row(MOD_B_SC2), row(MOD_B_SH2),
            pl.BlockSpec((D_MODEL, 2 * RT_PAD), lambda i: (0, 0)),
            pl.BlockSpec((D_MODEL, RT_PAD), lambda i: (0, 0)),
            pl.BlockSpec((1, RT_PAD), lambda i: (0, 0)),
        ],
        out_specs=pl.BlockSpec((N_EXPERTS, N2_TM), lambda i: (0, i)),
        scratch_shapes=[
            pltpu.VMEM((N2_TM, D_MODEL), BF16),
            pltpu.VMEM((N2_TM, D_MODEL), BF16),
            pltpu.VMEM((N2_TM, LANES), F32),
            pltpu.VMEM((NORM_ROWS, D_MODEL), F32),
            pltpu.VMEM((NORM_ROWS, D_MODEL), F32),
        ],
        compiler_params=_params(("arbitrary",)),
        name="norm2",
    )(x1, norm2_g, mod, mod, wcat, whi, b_pad)


RT_CH = 1024
RT_SB = 256


SCHED_W = RT_SB
SCHED_BE, SCHED_FIRST, SCHED_SEG, SCHED_NXT, SCHED_META = range(5)


def _route_kernel(lt_ref, dest_ref, gate_ref, sched_ref, lastblk_ref, idx_scr, rank_scr):
    e_col = lax.broadcasted_iota(I32, (N_EXPERTS, RT_CH), 0).astype(F32)
    tri = (lax.broadcasted_iota(I32, (RT_SB, RT_SB), 0) < lax.broadcasted_iota(I32, (RT_SB, RT_SB), 1)).astype(BF16)
    carry = jnp.zeros((N_EXPERTS, 1), F32)
    for c in range(SEQ // RT_CH):
        cs = slice(c * RT_CH, (c + 1) * RT_CH)
        work = lt_ref[:, cs]
        vals = []
        mask = jnp.zeros((N_EXPERTS, RT_CH), F32)
        for k in range(TOP_K):
            m = work.max(axis=0, keepdims=True)
            idx = jnp.where(work == m, e_col, float(N_EXPERTS)).min(axis=0, keepdims=True)
            sel = e_col == idx
            vals.append(m)
            idx_scr[k:k + 1, cs] = idx
            mask = jnp.where(sel, 1.0, mask)
            work = jnp.where(sel, -jnp.inf, work)
        ex = [jnp.exp(v - vals[0]) for v in vals]
        den = ex[0] + ex[1] + ex[2] + ex[3]
        for k in range(TOP_K):
            gate_ref[k:k + 1, cs] = ex[k] / den
        for b in range(RT_CH // RT_SB):
            blk = mask[:, b * RT_SB:(b + 1) * RT_SB]
            pref = jnp.dot(blk.astype(BF16), tri, preferred_element_type=F32)
            rank_scr[:, c * RT_CH + b * RT_SB:c * RT_CH + (b + 1) * RT_SB] = pref + carry
            carry = carry + blk.sum(axis=1, keepdims=True)
    nblk = jnp.floor((carry + (MOE_TM - 1)) * (1.0 / MOE_TM))
    nblk_b = jnp.broadcast_to(nblk, (N_EXPERTS, LANES))
    lower = (lax.broadcasted_iota(I32, (N_EXPERTS, N_EXPERTS), 1) < lax.broadcasted_iota(I32, (N_EXPERTS, N_EXPERTS), 0)).astype(BF16)
    start_blk = jnp.dot(lower, nblk_b.astype(BF16), preferred_element_type=F32)
    start = start_blk[:, 0:1] * float(MOE_TM)

    end_blk = start_blk[:, 0:1] + nblk
    lastblk_ref[...] = jnp.broadcast_to(jnp.where(nblk > 0, end_blk - 1.0, -1.0), (N_EXPERTS, LANES)).astype(I32)
    e_blk = lax.broadcasted_iota(I32, (N_EXPERTS, SCHED_W), 0).astype(F32)
    b_blk = lax.broadcasted_iota(I32, (N_EXPERTS, SCHED_W), 1).astype(F32)
    lane = b_blk[0:1]
    n_used = jnp.sum(nblk, axis=0, keepdims=True)
    be = jnp.minimum(jnp.sum(jnp.where(end_blk <= b_blk, 1.0, 0.0), axis=0, keepdims=True), N_EXPERTS - 1.0)
    prev = jnp.where(lane == 0, -1.0, pltpu.roll(be, 1, 1))
    first = jnp.where((lane < n_used) & (be != prev), 1.0, 0.0)
    excl = jnp.dot(jnp.broadcast_to(first, (8, SCHED_W)).astype(BF16), tri, preferred_element_type=F32)[0:1]
    seg = excl + first - 1.0
    nxt = jnp.min(jnp.where((e_blk > be) & (nblk > 0), e_blk, float(N_EXPERTS)), axis=0, keepdims=True)
    nxt = jnp.where(nxt >= N_EXPERTS, -1.0, nxt)
    n_seg = jnp.sum(first, axis=1, keepdims=True)
    meta = jnp.where(lane == 0, n_used, jnp.where(lane == 1, n_seg, 0.0))
    for r, v in enumerate((be, first, seg, nxt, meta)):
        sched_ref[r:r + 1, :] = v.astype(I32)
    sched_ref[5:8, :] = jnp.zeros((3, SCHED_W), I32)
    for c in range(SEQ // RT_CH):
        cs = slice(c * RT_CH, (c + 1) * RT_CH)
        slot = rank_scr[:, cs] + start
        for k in range(TOP_K):
            sel = e_col == idx_scr[k:k + 1, cs]
            dest_ref[k:k + 1, cs] = jnp.where(sel, slot, 0.0).sum(axis=0, keepdims=True).astype(I32)


def _route(logits_t):
    return pl.pallas_call(
        _route_kernel,
        out_shape=(
            jax.ShapeDtypeStruct((TOP_K, SEQ), I32),
            jax.ShapeDtypeStruct((TOP_K, SEQ), F32),
            jax.ShapeDtypeStruct((8, SCHED_W), I32),
            jax.ShapeDtypeStruct((N_EXPERTS, LANES), I32),
        ),
        scratch_shapes=[pltpu.VMEM((8, SEQ), F32), pltpu.VMEM((N_EXPERTS, SEQ), F32)],
        compiler_params=pltpu.CompilerParams(vmem_limit_bytes=VMEM_LIMIT),
        name="route",
    )(logits_t)


DP_TM = 256
XS_W = D_MODEL // 2
U32 = jnp.uint32


def _dispatch_kernel(dest_ref, lastblk_ref, x_ref, g_ref, sc_ref, sh_ref, xs_hbm, pk, zero_buf, rs_scr, a_scr,
                     s_scr, zsem, sem):
    i = pl.program_id(0)
    par = i % 2

    @pl.when(i == 0)
    def _():
        _prep_modulation(g_ref, sc_ref, sh_ref, a_scr, s_scr)
        zero_buf[...] = jnp.zeros_like(zero_buf)

        def zcopy(e):
            b = jnp.maximum(lastblk_ref[e, 0], 0)
            return pltpu.make_async_copy(zero_buf, xs_hbm.at[pl.ds(pl.multiple_of(b * MOE_TM, MOE_TM), MOE_TM)], zsem)

        def zstart(e, _):
            @pl.when(lastblk_ref[e, 0] >= 0)
            def _():
                zcopy(e).start()
            return 0

        def zwait(e, _):
            @pl.when(lastblk_ref[e, 0] >= 0)
            def _():
                zcopy(e).wait()
            return 0

        lax.fori_loop(0, N_EXPERTS, zstart, 0)
        lax.fori_loop(0, N_EXPERTS, zwait, 0)

    _row_rms(x_ref, rs_scr, DP_TM)

    n_tiles = XS_W // LANES
    assert n_tiles == NORM_ROWS

    def pack_tile(r0, c):
        lo = _normed_tile(x_ref, rs_scr, a_scr, s_scr, r0, c)
        hi = _normed_tile(x_ref, rs_scr, a_scr, s_scr, r0, c + n_tiles)
        lo = lax.bitcast_convert_type(lo.astype(BF16).astype(F32), U32)
        hi = lax.bitcast_convert_type(hi.astype(BF16).astype(F32), U32)
        pk[par, pl.ds(r0, NORM_ROWS), c * LANES:(c + 1) * LANES] = (lo >> 16) | (hi & jnp.uint32(0xFFFF0000))

    def issue_token(u):
        for k in range(TOP_K):
            d = dest_ref[k * SEQ + i * DP_TM + u]
            pltpu.make_async_copy(pk.at[par, pl.ds(u, 1)], xs_hbm.at[pl.ds(d, 1)], sem.at[par]).start()

    for c in range(n_tiles):
        pack_tile(0, c)

    def group(g, _):
        r0 = pl.multiple_of(g * NORM_ROWS, NORM_ROWS)
        for u in range(NORM_ROWS):
            pack_tile(r0, u)
            issue_token(r0 - NORM_ROWS + u)
        return 0

    lax.fori_loop(1, DP_TM // NORM_ROWS, group, 0)

    def tail(u, _):
        issue_token(DP_TM - NORM_ROWS + u)
        return 0

    lax.fori_loop(0, NORM_ROWS, tail, 0)

    def drain(p):
        for _ in range(TOP_K):
            pltpu.make_async_copy(pk.at[p], xs_hbm.at[pl.ds(0, DP_TM)], sem.at[p]).wait()

    @pl.when(i > 0)
    def _():
        drain(1 - par)

    @pl.when(i == pl.num_programs(0) - 1)
    def _():
        drain(par)


def _dispatch(dest, lastblk, x1, norm2_g, mod):
    row = lambda n: pl.BlockSpec((1, D_MODEL), lambda i, d, lb, n=n: (0, n))
    return pl.pallas_call(
        _dispatch_kernel,
        out_shape=jax.ShapeDtypeStruct((MOE_ROWS, XS_W), U32),
        grid_spec=pltpu.PrefetchScalarGridSpec(
            num_scalar_prefetch=2,
            grid=(SEQ // DP_TM,),
            in_specs=[
                pl.BlockSpec((DP_TM, D_MODEL), lambda i, d, lb: (i, 0)),
                pl.BlockSpec((1, D_MODEL), lambda i, d, lb: (0, 0)),
                row(MOD_B_SC2), row(MOD_B_SH2),
            ],
            out_specs=pl.BlockSpec(memory_space=pl.ANY),
            scratch_shapes=[
                pltpu.VMEM((2, DP_TM, XS_W), U32),
                pltpu.VMEM((MOE_TM, XS_W), U32),
                pltpu.VMEM((DP_TM, LANES), F32),
                pltpu.VMEM((NORM_ROWS, D_MODEL), F32),
                pltpu.VMEM((NORM_ROWS, D_MODEL), F32),
                pltpu.SemaphoreType.DMA,
                pltpu.SemaphoreType.DMA((2,)),
            ],
        ),
        compiler_params=_params(("arbitrary",)),
        name="dispatch",
    )(dest, lastblk, x1, norm2_g, mod, mod)


F1_TF = 512
F2_TN = 4096


CAST_ROWS = 128
WEIGHT_DMA_PRIORITY = 1


def _blocks_used(sched_ref):
    return sched_ref[SCHED_META, 0]


def _used_block(i, sched_ref):
    return jnp.minimum(i, _blocks_used(sched_ref) - 1)


def _stream_expert_weights(j, i, nj, sched_ref, tile_copies, stg, wbuf):
    @pl.when(sched_ref[SCHED_FIRST, i] == 1)
    def _():
        seq = j * sched_ref[SCHED_META, 1] + sched_ref[SCHED_SEG, i]
        slot = seq % 2

        @pl.when(seq == 0)
        def _():
            for cp in tile_copies(sched_ref[SCHED_BE, i], j, slot):
                cp.start(priority=WEIGHT_DMA_PRIORITY)

        for cp in tile_copies(sched_ref[SCHED_BE, i], j, slot):
            cp.wait()

        nxt = sched_ref[SCHED_NXT, i]

        @pl.when(nxt >= 0)
        def _():
            for cp in tile_copies(nxt, j, 1 - slot):
                cp.start(priority=WEIGHT_DMA_PRIORITY)

        @pl.when((nxt < 0) & (j + 1 < nj))
        def _():
            for cp in tile_copies(sched_ref[SCHED_BE, 0], j + 1, 1 - slot):
                cp.start(priority=WEIGHT_DMA_PRIORITY)

        if wbuf is not None:
            def cast(r, _):
                r0 = pl.multiple_of(r * CAST_ROWS, CAST_ROWS)
                wbuf[pl.ds(r0, CAST_ROWS), :] = stg[slot, pl.ds(r0, CAST_ROWS), :].astype(BF16)
                return 0

            lax.fori_loop(0, wbuf.shape[0] // CAST_ROWS, cast, 0)


F1_NH = D_FF // F1_TF


def _ffn1_kernel(sched_ref, x_ref, bg_ref, bl_ref, w1_hbm, o_ref, stg, wbuf, sem):
    i = pl.program_id(0)

    def slab_copies(e, h):
        return (
            pltpu.make_async_copy(w1_hbm.at[e, :, h * F1_TF:(h + 1) * F1_TF], stg.at[h, :, 0:F1_TF], sem.at[h]),
            pltpu.make_async_copy(w1_hbm.at[e, :, D_FF + h * F1_TF:D_FF + (h + 1) * F1_TF],
                                  stg.at[h, :, F1_TF:2 * F1_TF], sem.at[h]),
        )

    def compute(weights):
        xp = x_ref[...]
        x_lo = lax.bitcast_convert_type(xp << 16, F32).astype(BF16)
        x_hi = lax.bitcast_convert_type(xp & jnp.uint32(0xFFFF0000), F32).astype(BF16)
        for h in range(F1_NH):
            cols = slice(h * F1_TF, (h + 1) * F1_TF)
            y = jnp.dot(x_lo, weights(h, slice(0, XS_W)), preferred_element_type=F32)
            y = y + jnp.dot(x_hi, weights(h, slice(XS_W, D_MODEL)), preferred_element_type=F32)
            glu = jnp.minimum(y[:, :F1_TF] + bg_ref[:, cols], SWIGLU_LIMIT)
            lin = jnp.clip(y[:, F1_TF:] + bl_ref[:, cols], -SWIGLU_LIMIT, SWIGLU_LIMIT)
            o_ref[:, cols] = (glu * jax.nn.sigmoid(SWIGLU_ALPHA * glu) * (lin + 1.0)).astype(BF16)

    used = i < _blocks_used(sched_ref)
    first = sched_ref[SCHED_FIRST, i] == 1

    @pl.when(used & first)
    def _():
        @pl.when(i == 0)
        def _():
            for h in range(F1_NH):
                for cp in slab_copies(sched_ref[SCHED_BE, 0], h):
                    cp.start(priority=WEIGHT_DMA_PRIORITY)

        for h in range(F1_NH):
            for cp in slab_copies(sched_ref[SCHED_BE, i], h):
                cp.wait()

        def convert(h, rows):
            w = stg[h, rows, :].astype(BF16)
            wbuf[h, rows, :] = w
            return w

        compute(convert)

        nxt = sched_ref[SCHED_NXT, i]

        @pl.when(nxt >= 0)
        def _():
            for h in range(F1_NH):
                for cp in slab_copies(nxt, h):
                    cp.start(priority=WEIGHT_DMA_PRIORITY)

    @pl.when(used & jnp.logical_not(first))
    def _():
        compute(lambda h, rows: wbuf[h, rows, :])


def _ffn1(sched, xs, w1, b1_3):
    expert = lambda i, s: s[SCHED_BE, _used_block(i, s)]
    return pl.pallas_call(
        _ffn1_kernel,
        out_shape=jax.ShapeDtypeStruct((MOE_ROWS, D_FF), BF16),
        grid_spec=pltpu.PrefetchScalarGridSpec(
            num_scalar_prefetch=1,
            grid=(MOE_NB,),
            in_specs=[
                pl.BlockSpec((MOE_TM, XS_W), lambda i, s: (_used_block(i, s), 0)),
                pl.BlockSpec((None, 1, D_FF), lambda i, s: (expert(i, s), 0, 0)),
                pl.BlockSpec((None, 1, D_FF), lambda i, s: (expert(i, s), 0, 1)),
                pl.BlockSpec(memory_space=pl.ANY),
            ],
            out_specs=pl.BlockSpec((MOE_TM, D_FF), lambda i, s: (_used_block(i, s), 0)),
            scratch_shapes=[
                pltpu.VMEM((F1_NH, D_MODEL, 2 * F1_TF), F32),
                pltpu.VMEM((F1_NH, D_MODEL, 2 * F1_TF), BF16),
                pltpu.SemaphoreType.DMA((F1_NH,)),
            ],
        ),
        compiler_params=_params(("arbitrary",), vmem=BIG_VMEM_LIMIT),
        name="ffn1",
    )(sched, xs, b1_3, b1_3, w1)


def _ffn2_kernel(sched_ref, a_ref, b_ref, w2_hbm, o_ref, stg, sem):
    j = pl.program_id(0)
    i = pl.program_id(1)
    nj = pl.num_programs(0)

    def tile_copies(e, jj, slot):
        c0 = pl.multiple_of(jj * F2_TN, F2_TN)
        return (pltpu.make_async_copy(w2_hbm.at[e, :, pl.ds(c0, F2_TN)], stg.at[slot], sem.at[slot]),)

    @pl.when(i < _blocks_used(sched_ref))
    def _():
        _stream_expert_weights(j, i, nj, sched_ref, tile_copies, stg, None)
        slot = (j * sched_ref[SCHED_META, 1] + sched_ref[SCHED_SEG, i]) % 2
        o_ref[...] = jnp.dot(a_ref[...], stg[slot].astype(BF16), preferred_element_type=F32) + b_ref[...]


def _ffn2(sched, act, w2, b2_3):
    return pl.pallas_call(
        _ffn2_kernel,
        out_shape=jax.ShapeDtypeStruct((MOE_ROWS, D_MODEL), F32),
        grid_spec=pltpu.PrefetchScalarGridSpec(
            num_scalar_prefetch=1,
            grid=(D_MODEL // F2_TN, MOE_NB),
            in_specs=[
                pl.BlockSpec((MOE_TM, D_FF), lambda j, i, s: (_used_block(i, s), 0)),
                pl.BlockSpec((None, 1, F2_TN), lambda j, i, s: (s[SCHED_BE, _used_block(i, s)], 0, j)),
                pl.BlockSpec(memory_space=pl.ANY),
            ],
            out_specs=pl.BlockSpec((MOE_TM, F2_TN), lambda j, i, s: (_used_block(i, s), j)),
            scratch_shapes=[
                pltpu.VMEM((2, D_FF, F2_TN), F32),
                pltpu.SemaphoreType.DMA((2,)),
            ],
        ),
        compiler_params=_params(("arbitrary", "arbitrary")),
        name="ffn2",
    )(sched, act, b2_3, w2)


CB_TM = 128


def _combine_kernel(dest_ref, x_ref, gate_ref, g2_ref, ys_hbm, o_ref, buf, gate_scr, g2_scr, sem):
    i = pl.program_id(0)
    last = pl.num_programs(0) - 1
    par = i % 2
    n_tiles = D_MODEL // LANES
    assert n_tiles == 8 * TOP_K

    def row_start(tile, p, u, k):
        d = dest_ref[k * SEQ + tile * CB_TM + u]
        pltpu.make_async_copy(ys_hbm.at[pl.ds(d, 1)], buf.at[p, k, pl.ds(u, 1)], sem.at[p]).start()

    def drain(p):
        for k in range(TOP_K):
            pltpu.make_async_copy(ys_hbm.at[pl.ds(0, CB_TM)], buf.at[p, k], sem.at[p]).wait()

    @pl.when(i == 0)
    def _():
        def body(u, _):
            for k in range(TOP_K):
                row_start(0, 0, u, k)
            return 0
        lax.fori_loop(0, CB_TM, body, 0)
        g2_scr[...] = jnp.broadcast_to(g2_ref[...], g2_scr.shape)

    drain(par)

    for k in range(TOP_K):
        gate_scr[k] = jnp.broadcast_to(gate_ref[:, k:k + 1], (CB_TM, LANES))

    nxt = jnp.minimum(i + 1, last)

    def body(r, _):
        r0 = pl.multiple_of(r * 8, 8)
        rows = pl.ds(r0, 8)
        gk = [gate_scr[k, rows, :] for k in range(TOP_K)]
        for c in range(n_tiles):
            cs = slice(c * LANES, (c + 1) * LANES)
            y = buf[par, 0, rows, cs] * gk[0]
            for k in range(1, TOP_K):
                y = y + buf[par, k, rows, cs] * gk[k]
            o_ref[rows, cs] = x_ref[rows, cs] + g2_scr[:, cs] * y
            row_start(nxt, 1 - par, r0 + c // TOP_K, c % TOP_K)
        return 0

    lax.fori_loop(0, CB_TM // 8, body, 0)

    @pl.when(i == last)
    def _():
        drain(1 - par)


def _combine(dest, x1, gates_t, mod, ys):
    return pl.pallas_call(
        _combine_kernel,
        out_shape=jax.ShapeDtypeStruct((SEQ, D_MODEL), F32),
        grid_spec=pltpu.PrefetchScalarGridSpec(
            num_scalar_prefetch=1,
            grid=(SEQ // CB_TM,),
            in_specs=[
                pl.BlockSpec((CB_TM, D_MODEL), lambda i, d: (i, 0)),
                pl.BlockSpec((CB_TM, TOP_K), lambda i, d: (i, 0)),
                pl.BlockSpec((1, D_MODEL), lambda i, d: (0, MOD_B_G2)),
                pl.BlockSpec(memory_space=pl.ANY),
            ],
            out_specs=pl.BlockSpec((CB_TM, D_MODEL), lambda i, d: (i, 0)),
            scratch_shapes=[
                pltpu.VMEM((2, TOP_K, CB_TM, D_MODEL), F32),
                pltpu.VMEM((TOP_K, CB_TM, LANES), F32),
                pltpu.VMEM((8, D_MODEL), F32),
                pltpu.SemaphoreType.DMA((2,)),
            ],
        ),
        compiler_params=_params(("arbitrary",)),
        name="combine",
    )(dest, x1, gates_t, mod, ys)


def _rope_tables():
    t = jnp.arange(SEQ, dtype=I32)
    row = (t // GRID_W).astype(F32)
    col = (t % GRID_W).astype(F32)
    inv_freq = ROPE_THETA ** (-jnp.arange(0, ROPE_AXIS_DIM, 2, dtype=F32) / ROPE_AXIS_DIM)
    ang_r = inv_freq[:, None] * row[None, :]
    ang_c = inv_freq[:, None] * col[None, :]
    cos_t = jnp.concatenate([jnp.cos(ang_r), jnp.cos(ang_r), jnp.cos(ang_c), jnp.cos(ang_c)], axis=0)
    sin_t = jnp.concatenate([-jnp.sin(ang_r), jnp.sin(ang_r), -jnp.sin(ang_c), jnp.sin(ang_c)], axis=0)
    return cos_t, sin_t


def kernel(x, c, w_mod, b_mod, norm1_g, w_in, q_norm_g, k_norm_g, w_pool, pool_scale, w_out, norm2_g,
           w_router, b_router, w1, b1, w2, b2):
    assert x.shape == (1, SEQ, D_MODEL) and w_mod.shape[0] == 1
    x2 = x[0]
    cos_t, sin_t = _rope_tables()

    c_col = c.reshape(D_MODEL, 1)
    mod_a = _mod(c_col, w_mod[0], b_mod)

    pool_in, qt, k, vt, mod = _inproj(x2, mod_a, norm1_g, w_in[0].astype(BF16), cos_t, sin_t,
                                      q_norm_g.reshape(HEAD_DIM, 1), k_norm_g.reshape(HEAD_DIM, 1),
                                      c.reshape(D_MODEL // LANES, LANES), w_mod[0], b_mod)
    attn = _attention(qt, k, vt)
    pool = _pool(pool_in, w_pool[0].astype(BF16), pool_scale)
    x1 = _outproj(attn, pool, w_out[0].astype(BF16), x2, mod)

    wr = w_router[0]
    wr_hi = wr.astype(BF16)
    wr_lo = (wr - wr_hi.astype(F32)).astype(BF16)
    pad = lambda a: jnp.pad(a, ((0, 0), (0, RT_PAD - N_EXPERTS)))
    wcat = jnp.concatenate([pad(wr_hi), pad(wr_lo)], axis=1)
    logits_t = _norm2(x1, norm2_g, mod, wcat, pad(wr_hi), pad(b_router))

    dest, gates, sched, lastblk = _route(logits_t)
    dest = dest.reshape(TOP_K * SEQ)

    xs = _dispatch(dest, lastblk, x1, norm2_g, mod)
    act = _ffn1(sched, xs, w1[0], b1[0].reshape(N_EXPERTS, 1, 2 * D_FF))
    ys = _ffn2(sched, act, w2[0], b2[0].reshape(N_EXPERTS, 1, D_MODEL))
    out = _combine(dest, x1, gates.T, mod, ys)
    return out[None]
```

```python
import functools
import math

import jax
import jax.numpy as jnp
from jax import lax
from jax.experimental import pallas as pl
from jax.experimental.pallas import tpu as pltpu

F32 = jnp.float32
BF16 = jnp.bfloat16
I32 = jnp.int32

D_MODEL = 4096
SEQ = 8192
POOL_WIDTH = 2048
ATTN_WIDTH = 2048
HEAD_DIM = 128
N_HEADS = 16
N_KV_HEADS = 4
GROUP = N_HEADS // N_KV_HEADS
KV_WIDTH = N_KV_HEADS * HEAD_DIM
IN_WIDTH = POOL_WIDTH + ATTN_WIDTH + 2 * KV_WIDTH
POOL_WINDOWS = (2, 4, 8, 16)
POOL_GROUP_WIDTH = POOL_WIDTH // len(POOL_WINDOWS)
GRID_W = 64
ROPE_THETA = 10000.0
ROPE_AXIS_DIM = HEAD_DIM // 2
N_EXPERTS = 32
TOP_K = 4
D_FF = D_MODEL // 4
SWIGLU_ALPHA = 1.702
SWIGLU_LIMIT = 7.0
N_MOD = 6
EPS = 1e-6

LANES = 128
VMEM_LIMIT = 56 * 1024 * 1024
BIG_VMEM_LIMIT = 62 * 1024 * 1024

Q_SCALE = (HEAD_DIM ** -0.5) * math.log2(math.e)

MOE_TM = 256
MOE_NB = SEQ * TOP_K // MOE_TM + N_EXPERTS
MOE_ROWS = MOE_NB * MOE_TM


def _params(sem, vmem=VMEM_LIMIT):
    return pltpu.CompilerParams(dimension_semantics=sem, vmem_limit_bytes=vmem)


MOD_TN = 1024
MOD_KC = 256


def _mod_kernel(c_ref, w_ref, b_ref, o_ref):
    def body(k, acc):
        r = pl.multiple_of(k * MOD_KC, MOD_KC)
        ck = c_ref[pl.ds(r, MOD_KC), :]
        ck = ck * jax.nn.sigmoid(ck)
        p = w_ref[pl.ds(r, MOD_KC), :] * ck
        return acc + p.reshape(MOD_KC // 8, 8, MOD_TN).sum(axis=0)

    acc = lax.fori_loop(0, D_MODEL // MOD_KC, body, jnp.zeros((8, MOD_TN), F32))
    o_ref[...] = acc.sum(axis=0, keepdims=True) + b_ref[...]


MOD_A = 2
MOD_B = N_MOD - MOD_A
MOD_B_G1, MOD_B_SH2, MOD_B_SC2, MOD_B_G2 = range(MOD_B)


def _mod(c_col, w_mod, b_mod):
    n = MOD_A * D_MODEL
    return pl.pallas_call(
        _mod_kernel,
        out_shape=jax.ShapeDtypeStruct((1, n), F32),
        grid=(n // MOD_TN,),
        in_specs=[
            pl.BlockSpec((D_MODEL, 1), lambda j: (0, 0)),
            pl.BlockSpec((D_MODEL, MOD_TN), lambda j: (0, j)),
            pl.BlockSpec((1, MOD_TN), lambda j: (0, j)),
        ],
        out_specs=pl.BlockSpec((1, MOD_TN), lambda j: (0, j)),
        compiler_params=_params(("arbitrary",)),
        name="mod",
    )(c_col, w_mod, b_mod)


IP_TM = 512
IP_TN = 1024
IP_NJ = IN_WIDTH // IP_TN
IP_J_Q = POOL_WIDTH // IP_TN
IP_J_KV = IP_J_Q + ATTN_WIDTH // IP_TN
NORM_ROWS = 16
BF16_SUBLANES = 16
KA_W = 2 * HEAD_DIM
VA_H = HEAD_DIM + BF16_SUBLANES


def _prep_modulation(g_ref, sc_ref, sh_ref, a_scr, s_scr):
    a_scr[...] = jnp.broadcast_to(g_ref[...] * (1.0 + sc_ref[...]), a_scr.shape)
    s_scr[...] = jnp.broadcast_to(sh_ref[...], s_scr.shape)


def _row_rms(x_ref, rs_scr, n_rows):
    def body(r, _):
        r0 = pl.multiple_of(r * NORM_ROWS, NORM_ROWS)
        width = x_ref.shape[1]
        parts = []
        for c in range(width // LANES):
            xc = x_ref[pl.ds(r0, NORM_ROWS), c * LANES:(c + 1) * LANES]
            parts.append(xc * xc)
        while len(parts) > 1:
            parts = [parts[p] + parts[p + 1] for p in range(0, len(parts), 2)]
        rs_scr[pl.ds(r0, NORM_ROWS), :] = parts[0]
        return 0
    lax.fori_loop(0, n_rows // NORM_ROWS, body, 0)
    ms = jnp.sum(rs_scr[...], axis=-1, keepdims=True) * (1.0 / x_ref.shape[1])
    rs_scr[...] = jnp.broadcast_to(lax.rsqrt(ms + EPS), rs_scr.shape)


def _normed_tile(x_ref, rs_scr, a_scr, s_scr, r0, c):
    cs = slice(c * LANES, (c + 1) * LANES)
    return x_ref[pl.ds(r0, NORM_ROWS), cs] * rs_scr[pl.ds(r0, NORM_ROWS), :] * a_scr[:, cs] + s_scr[:, cs]


def _norm_rope_t(xt, g_col, cos_t, sin_t):
    ms = jnp.mean(xt * xt, axis=0, keepdims=True)
    y = xt * lax.rsqrt(ms + EPS) * g_col
    q = ROPE_AXIS_DIM // 2
    partner = jnp.concatenate([y[q:2 * q], y[0:q], y[3 * q:4 * q], y[2 * q:3 * q]], axis=0)
    return y * cos_t + partner * sin_t


MB_TN = 256
MB_STEPS = MOD_B * D_MODEL // MB_TN
MB_KC = 256


def _tree_sum(parts):
    while len(parts) > 1:
        parts = [parts[p] + parts[p + 1] for p in range(0, len(parts), 2)]
    return parts[0]


def _later_modulation(t, c_ref, wm_ref, bm_ref, mb_ref, cb_scr):
    @pl.when(t == 0)
    def _():
        for r in range(D_MODEL // LANES):
            cv = c_ref[r:r + 1, :]
            cb_scr[r * LANES:(r + 1) * LANES, :] = jnp.broadcast_to(cv * jax.nn.sigmoid(cv), (LANES, LANES)).T

    @pl.when(t < MB_STEPS)
    def _():
        def body(k, accs):
            r0 = pl.multiple_of(k * MB_KC, MB_KC)
            cb = cb_scr[pl.ds(r0, MB_KC), :]
            out = []
            for n in range(MB_TN // LANES):
                p = wm_ref[pl.ds(r0, MB_KC), n * LANES:(n + 1) * LANES] * cb
                out.append(accs[n] + _tree_sum([p[g * 8:(g + 1) * 8] for g in range(MB_KC // 8)]))
            return tuple(out)

        zero = jnp.zeros((8, LANES), F32)
        accs = lax.fori_loop(0, D_MODEL // MB_KC, body, (zero,) * (MB_TN // LANES))
        for n in range(MB_TN // LANES):
            cs = slice(n * LANES, (n + 1) * LANES)
            mb_ref[:, cs] = accs[n].sum(axis=0, keepdims=True) + bm_ref[:, cs]


def _inproj_kernel(x_ref, g_ref, sc_ref, sh_ref, w_ref, cos_ref, sin_ref, qg_ref, kg_ref, c_ref, wm_ref, bm_ref,
                   pool_ref, qt_ref, k_ref, vt_ref, mb_ref, h_scr, rs_scr, a_scr, s_scr, cb_scr):
    j = pl.program_id(1)
    _later_modulation(pl.program_id(0) * IP_NJ + j, c_ref, wm_ref, bm_ref, mb_ref, cb_scr)

    @pl.when(j == 0)
    def _():
        _prep_modulation(g_ref, sc_ref, sh_ref, a_scr, s_scr)
        _row_rms(x_ref, rs_scr, IP_TM)

        def body(r, _):
            r0 = pl.multiple_of(r * NORM_ROWS, NORM_ROWS)
            for c in range(D_MODEL // LANES):
                h_scr[pl.ds(r0, NORM_ROWS), c * LANES:(c + 1) * LANES] = _normed_tile(
                    x_ref, rs_scr, a_scr, s_scr, r0, c).astype(BF16)
            return 0
        lax.fori_loop(0, IP_TM // NORM_ROWS, body, 0)

    acc = jnp.dot(h_scr[...], w_ref[...], preferred_element_type=F32)

    @pl.when(j < IP_J_Q)
    def _():
        pool_ref[...] = acc.astype(BF16)

    @pl.when((j >= IP_J_Q) & (j < IP_J_KV))
    def _():
        for hh in range(IP_TN // HEAD_DIM):
            sl = slice(hh * HEAD_DIM, (hh + 1) * HEAD_DIM)
            r = _norm_rope_t(acc[:, sl].T, qg_ref[...], cos_ref[...], sin_ref[...]) * Q_SCALE
            qt_ref[sl, :] = r.astype(BF16)

    @pl.when(j == IP_J_KV)
    def _():
        lane = lax.broadcasted_iota(I32, (IP_TM, HEAD_DIM), 1)
        one_col = jnp.where(lane == 0, 1.0, 0.0).astype(BF16)
        for hh in range(N_KV_HEADS):
            sl = slice(hh * HEAD_DIM, (hh + 1) * HEAD_DIM)
            k_ref[:, hh * KA_W:hh * KA_W + HEAD_DIM] = _norm_rope_t(
                acc[:, sl].T, kg_ref[...], cos_ref[...], sin_ref[...]).T.astype(BF16)
            k_ref[:, hh * KA_W + HEAD_DIM:(hh + 1) * KA_W] = one_col
        for hh in range(N_KV_HEADS):
            sl = slice(KV_WIDTH + hh * HEAD_DIM, KV_WIDTH + (hh + 1) * HEAD_DIM)
            vt_ref[hh * VA_H:hh * VA_H + HEAD_DIM, :] = acc[:, sl].T.astype(BF16)
            vt_ref[hh * VA_H + HEAD_DIM:(hh + 1) * VA_H, :] = jnp.ones((VA_H - HEAD_DIM, IP_TM), BF16)


def _inproj(x2, mod, norm1_g, w_in_b, cos_t, sin_t, qg, kg, c_col, w_mod, b_mod):
    row = lambda n: pl.BlockSpec((1, D_MODEL), lambda i, j, n=n: (0, n))
    mb_tile = lambda i, j: jnp.minimum(i * IP_NJ + j, MB_STEPS - 1)
    mb_first = MOD_A * D_MODEL // MB_TN
    return pl.pallas_call(
        _inproj_kernel,
        out_shape=(
            jax.ShapeDtypeStruct((SEQ, POOL_WIDTH), BF16),
            jax.ShapeDtypeStruct((ATTN_WIDTH, SEQ), BF16),
            jax.ShapeDtypeStruct((SEQ, N_KV_HEADS * KA_W), BF16),
            jax.ShapeDtypeStruct((N_KV_HEADS * VA_H, SEQ), BF16),
            jax.ShapeDtypeStruct((1, MOD_B * D_MODEL), F32),
        ),
        grid=(SEQ // IP_TM, IP_NJ),
        in_specs=[
            pl.BlockSpec((IP_TM, D_MODEL), lambda i, j: (i, 0)),
            pl.BlockSpec((1, D_MODEL), lambda i, j: (0, 0)),
            row(1), row(0),
            pl.BlockSpec((D_MODEL, IP_TN), lambda i, j: (0, j)),
            pl.BlockSpec((HEAD_DIM, IP_TM), lambda i, j: (0, i)),
            pl.BlockSpec((HEAD_DIM, IP_TM), lambda i, j: (0, i)),
            pl.BlockSpec((HEAD_DIM, 1), lambda i, j: (0, 0)),
            pl.BlockSpec((HEAD_DIM, 1), lambda i, j: (0, 0)),
            pl.BlockSpec((D_MODEL // LANES, LANES), lambda i, j: (0, 0)),
            pl.BlockSpec((D_MODEL, MB_TN), lambda i, j: (0, mb_first + mb_tile(i, j))),
            pl.BlockSpec((1, MB_TN), lambda i, j: (0, mb_first + mb_tile(i, j))),
        ],
        out_specs=(
            pl.BlockSpec((IP_TM, IP_TN), lambda i, j: (i, jnp.minimum(j, IP_J_Q - 1))),
            pl.BlockSpec((IP_TN, IP_TM), lambda i, j: (jnp.clip(j - IP_J_Q, 0, IP_J_KV - IP_J_Q - 1), i)),
            pl.BlockSpec((IP_TM, N_KV_HEADS * KA_W), lambda i, j: (i, 0)),
            pl.BlockSpec((N_KV_HEADS * VA_H, IP_TM), lambda i, j: (0, i)),
            pl.BlockSpec((1, MB_TN), lambda i, j: (0, mb_tile(i, j))),
        ),
        scratch_shapes=[
            pltpu.VMEM((IP_TM, D_MODEL), BF16),
            pltpu.VMEM((IP_TM, LANES), F32),
            pltpu.VMEM((NORM_ROWS, D_MODEL), F32),
            pltpu.VMEM((NORM_ROWS, D_MODEL), F32),
            pltpu.VMEM((D_MODEL, LANES), F32),
        ],
        compiler_params=_params(("arbitrary", "arbitrary"), vmem=BIG_VMEM_LIMIT),
        name="inproj",
    )(x2, norm1_g, mod, mod, w_in_b, cos_t, sin_t, qg, kg, c_col, w_mod, b_mod)


AT_TQ = 512
AT_TK = 8192
AT_TK_ONLINE = 512
SHIFT_LIMIT = 60.0


def _attn_kernel(qt_ref, k_ref, vt_ref, o_ref, qa_scr, p_scr, kmax_scr):
    h = pl.program_id(0)
    i = pl.program_id(1)

    @pl.when((i == 0) & (h % GROUP == 0))
    def _():
        def body(c, mx):
            c0 = pl.multiple_of(c * AT_TK, AT_TK)
            kc = k_ref[pl.ds(c0, AT_TK), :HEAD_DIM].astype(F32)
            n2 = (kc * kc).sum(axis=1, keepdims=True)
            return jnp.maximum(mx, n2.max(axis=0, keepdims=True))
        mx = lax.fori_loop(0, SEQ // AT_TK, body, jnp.zeros((1, 1), F32))
        kmax_scr[...] = jnp.broadcast_to(jnp.sqrt(mx), kmax_scr.shape)

    q = qt_ref[...].astype(F32)
    bound = jnp.sqrt((q * q).sum(axis=0, keepdims=True)) * kmax_scr[0:1, 0:1] * 1.01
    fast = jnp.max(bound) <= SHIFT_LIMIT

    @pl.when(fast)
    def _():
        qa_scr[0:HEAD_DIM, :] = qt_ref[...]
        row = lax.broadcasted_iota(I32, (KA_W - HEAD_DIM, AT_TQ), 0)
        qa_scr[HEAD_DIM:, :] = jnp.where(row == 0, -bound, 0.0).astype(BF16)

        def body(c, _):
            c0 = pl.multiple_of(c * AT_TK, AT_TK)
            s = jnp.dot(k_ref[pl.ds(c0, AT_TK), :], qa_scr[...], preferred_element_type=F32)
            p_scr[pl.ds(c0, AT_TK), :] = jnp.exp2(s).astype(BF16)
            return 0

        lax.fori_loop(0, SEQ // AT_TK, body, 0)
        o = jnp.dot(vt_ref[...], p_scr[...], preferred_element_type=F32)
        o_ref[...] = (o[:HEAD_DIM] * (1.0 / o[HEAD_DIM:HEAD_DIM + 1])).T.astype(BF16)

    @pl.when(jnp.logical_not(fast))
    def _():
        qt = qt_ref[...]

        def chunk(c, carry):
            m, l, acc = carry
            c0 = pl.multiple_of(c * AT_TK_ONLINE, AT_TK_ONLINE)
            s = jnp.dot(k_ref[pl.ds(c0, AT_TK_ONLINE), :HEAD_DIM], qt, preferred_element_type=F32)
            m_new = jnp.maximum(m, s.max(axis=0, keepdims=True))
            alpha = jnp.exp2(m - m_new)
            p = jnp.exp2(s - m_new)
            l = alpha * l + p.sum(axis=0, keepdims=True)
            pv = jnp.dot(vt_ref[:HEAD_DIM, pl.ds(c0, AT_TK_ONLINE)], p.astype(BF16), preferred_element_type=F32)
            return m_new, l, alpha * acc + pv

        init = (jnp.full((1, AT_TQ), -jnp.inf, F32), jnp.zeros((1, AT_TQ), F32),
                jnp.zeros((HEAD_DIM, AT_TQ), F32))
        _, l, acc = lax.fori_loop(0, SEQ // AT_TK_ONLINE, chunk, init)
        o_ref[...] = (acc * (1.0 / l)).T.astype(BF16)


def _attention(qt, k, vt):
    return pl.pallas_call(
        _attn_kernel,
        out_shape=jax.ShapeDtypeStruct((SEQ, ATTN_WIDTH), BF16),
        grid=(N_HEADS, SEQ // AT_TQ),
        in_specs=[
            pl.BlockSpec((HEAD_DIM, AT_TQ), lambda h, i: (h, i)),
            pl.BlockSpec((SEQ, KA_W), lambda h, i: (0, h // GROUP)),
            pl.BlockSpec((VA_H, SEQ), lambda h, i: (h // GROUP, 0)),
        ],
        out_specs=pl.BlockSpec((AT_TQ, HEAD_DIM), lambda h, i: (i, h)),
        scratch_shapes=[
            pltpu.VMEM((KA_W, AT_TQ), BF16),
            pltpu.VMEM((SEQ, AT_TQ), BF16),
            pltpu.VMEM((8, LANES), F32),
        ],
        compiler_params=_params(("arbitrary", "arbitrary")),
        name="attn",
    )(qt, k, vt)


PL_TM = 256
PL_HALO = 16


def _pool_kernel(prev_ref, main_ref, next_ref, wp_ref, scale_ref, o_ref, buf):
    i = pl.program_id(0)
    last = pl.num_programs(0) - 1
    buf[0:PL_HALO, :] = jnp.where(i == 0, 0.0, prev_ref[...].astype(F32))
    buf[PL_HALO:PL_HALO + PL_TM, :] = main_ref[...].astype(F32)
    buf[PL_HALO + PL_TM:, :] = jnp.where(i == last, 0.0, next_ref[...].astype(F32))
    t = i * PL_TM + lax.broadcasted_iota(I32, (PL_TM, 1), 0)
    for gi, w in enumerate(POOL_WINDOWS):
        cols = slice(gi * POOL_GROUP_WIDTH, (gi + 1) * POOL_GROUP_WIDTH)
        win = buf[PL_HALO - w // 2:PL_HALO - w // 2 + PL_TM, cols]
        for d in range(-w // 2 + 1, w // 2):
            win = win + buf[PL_HALO + d:PL_HALO + d + PL_TM, cols]
        lo = jnp.maximum(t - w // 2, 0)
        hi = jnp.minimum(t + w // 2 - 1, SEQ - 1)
        cnt = (hi - lo + 1).astype(F32)
        dlt = win / cnt - buf[PL_HALO:PL_HALO + PL_TM, cols]
        y = jnp.dot(dlt.astype(BF16), wp_ref[gi], preferred_element_type=F32)
        o_ref[:, cols] = (y * scale_ref[:, cols]).astype(BF16)


def _pool(pool_in, w_pool_b, pool_scale):
    nh = PL_TM // PL_HALO
    n_halo_blocks = SEQ // PL_HALO
    return pl.pallas_call(
        _pool_kernel,
        out_shape=jax.ShapeDtypeStruct((SEQ, POOL_WIDTH), BF16),
        grid=(SEQ // PL_TM,),
        in_specs=[
            pl.BlockSpec((PL_HALO, POOL_WIDTH), lambda i: (jnp.maximum(i * nh - 1, 0), 0)),
            pl.BlockSpec((PL_TM, POOL_WIDTH), lambda i: (i, 0)),
            pl.BlockSpec((PL_HALO, POOL_WIDTH), lambda i: (jnp.minimum((i + 1) * nh, n_halo_blocks - 1), 0)),
            pl.BlockSpec((len(POOL_WINDOWS), POOL_GROUP_WIDTH, POOL_GROUP_WIDTH), lambda i: (0, 0, 0)),
            pl.BlockSpec((1, POOL_WIDTH), lambda i: (0, 0)),
        ],
        out_specs=pl.BlockSpec((PL_TM, POOL_WIDTH), lambda i: (i, 0)),
        scratch_shapes=[pltpu.VMEM((PL_TM + 2 * PL_HALO, POOL_WIDTH), F32)],
        compiler_params=_params(("arbitrary",)),
        name="pool",
    )(pool_in, pool_in, pool_in, w_pool_b, pool_scale)


OP_TM = 1024
OP_TN = 512


def _outproj_kernel(a_ref, p_ref, wa_ref, wp_ref, x_ref, g_ref, o_ref):
    acc = jnp.dot(a_ref[...], wa_ref[...], preferred_element_type=F32)
    acc = acc + jnp.dot(p_ref[...], wp_ref[...], preferred_element_type=F32)
    o_ref[...] = x_ref[...] + g_ref[...] * acc


def _outproj(attn, pool, w_out_b, x2, mod):
    return pl.pallas_call(
        _outproj_kernel,
        out_shape=jax.ShapeDtypeStruct((SEQ, D_MODEL), F32),
        grid=(SEQ // OP_TM, D_MODEL // OP_TN),
        in_specs=[
            pl.BlockSpec((OP_TM, ATTN_WIDTH), lambda i, j: (i, 0)),
            pl.BlockSpec((OP_TM, POOL_WIDTH), lambda i, j: (i, 0)),
            pl.BlockSpec((ATTN_WIDTH, OP_TN), lambda i, j: (0, j)),
            pl.BlockSpec((POOL_WIDTH, OP_TN), lambda i, j: (1, j)),
            pl.BlockSpec((OP_TM, OP_TN), lambda i, j: (i, j)),
            pl.BlockSpec((1, OP_TN), lambda i, j: (0, MOD_B_G1 * (D_MODEL // OP_TN) + j)),
        ],
        out_specs=pl.BlockSpec((OP_TM, OP_TN), lambda i, j: (i, j)),
        compiler_params=_params(("arbitrary", "arbitrary")),
        name="outproj",
    )(attn, pool, w_out_b, w_out_b, x2, mod)


N2_TM = 256
RT_PAD = LANES


def _norm2_kernel(x_ref, g_ref, sc_ref, sh_ref, wcat_ref, whi_ref, b_ref, lt_ref, hi_scr, lo_scr,
                  rs_scr, a_scr, s_scr):
    @pl.when(pl.program_id(0) == 0)
    def _():
        _prep_modulation(g_ref, sc_ref, sh_ref, a_scr, s_scr)

    _row_rms(x_ref, rs_scr, N2_TM)

    def body(r, _):
        r0 = pl.multiple_of(r * NORM_ROWS, NORM_ROWS)
        for c in range(D_MODEL // LANES):
            cs = slice(c * LANES, (c + 1) * LANES)
            h = _normed_tile(x_ref, rs_scr, a_scr, s_scr, r0, c)
            hi = h.astype(BF16)
            hi_scr[pl.ds(r0, NORM_ROWS), cs] = hi
            lo_scr[pl.ds(r0, NORM_ROWS), cs] = (h - hi.astype(F32)).astype(BF16)
        return 0
    lax.fori_loop(0, N2_TM // NORM_ROWS, body, 0)
    a = jnp.dot(hi_scr[...], wcat_ref[...], preferred_element_type=F32)
    b = jnp.dot(lo_scr[...], whi_ref[...], preferred_element_type=F32)
    logits = a[:, :RT_PAD] + a[:, RT_PAD:] + b + b_ref[...]
    lt_ref[...] = logits.T[:N_EXPERTS, :]


def _norm2(x1, norm2_g, mod, wcat, whi, b_pad):
    row = lambda n: pl.BlockSpec((1, D_MODEL), lambda i, n=n: (0, n))
    return pl.pallas_call(
        _norm2_kernel,
        out_shape=jax.ShapeDtypeStruct((N_EXPERTS, SEQ), F32),
        grid=(SEQ // N2_TM,),
        in_specs=[
            pl.BlockSpec((N2_TM, D_MODEL), lambda i: (i, 0)),
            pl.BlockSpec((1, D_MODEL), lambda i: (0, 0)),
            row(MOD_B_SC2), row(MOD_B_SH2),
            pl.BlockSpec((D_MODEL, 2 * RT_PAD), lambda i: (0, 0)),
            pl.BlockSpec((D_MODEL, RT_PAD), lambda i: (0, 0)),
            pl.BlockSpec((1, RT_PAD), lambda i: (0, 0)),
        ],
        out_specs=pl.BlockSpec((N_EXPERTS, N2_TM), lambda i: (0, i)),
        scratch_shapes=[
            pltpu.VMEM((N2_TM, D_MODEL), BF16),
            pltpu.VMEM((N2_TM, D_MODEL), BF16),
            pltpu.VMEM((N2_TM, LANES), F32),
            pltpu.VMEM((NORM_ROWS, D_MODEL), F32),
            pltpu.VMEM((NORM_ROWS, D_MODEL), F32),
        ],
        compiler_params=_params(("arbitrary",)),
        name="norm2",
    )(x1, norm2_g, mod, mod, wcat, whi, b_pad)


RT_CH = 1024
RT_SB = 256


SCHED_W = RT_SB
SCHED_BE, SCHED_FIRST, SCHED_SEG, SCHED_NXT, SCHED_META = range(5)


def _route_kernel(lt_ref, dest_ref, gate_ref, sched_ref, lastblk_ref, idx_scr, rank_scr):
    e_col = lax.broadcasted_iota(I32, (N_EXPERTS, RT_CH), 0).astype(F32)
    tri = (lax.broadcasted_iota(I32, (RT_SB, RT_SB), 0) < lax.broadcasted_iota(I32, (RT_SB, RT_SB), 1)).astype(BF16)
    carry = jnp.zeros((N_EXPERTS, 1), F32)
    for c in range(SEQ // RT_CH):
        cs = slice(c * RT_CH, (c + 1) * RT_CH)
        work = lt_ref[:, cs]
        vals = []
        mask = jnp.zeros((N_EXPERTS, RT_CH), F32)
        for k in range(TOP_K):
            m = work.max(axis=0, keepdims=True)
            idx = jnp.where(work == m, e_col, float(N_EXPERTS)).min(axis=0, keepdims=True)
            sel = e_col == idx
            vals.append(m)
            idx_scr[k:k + 1, cs] = idx
            mask = jnp.where(sel, 1.0, mask)
            work = jnp.where(sel, -jnp.inf, work)
        ex = [jnp.exp(v - vals[0]) for v in vals]
        den = ex[0] + ex[1] + ex[2] + ex[3]
        for k in range(TOP_K):
            gate_ref[k:k + 1, cs] = ex[k] / den
        for b in range(RT_CH // RT_SB):
            blk = mask[:, b * RT_SB:(b + 1) * RT_SB]
            pref = jnp.dot(blk.astype(BF16), tri, preferred_element_type=F32)
            rank_scr[:, c * RT_CH + b * RT_SB:c * RT_CH + (b + 1) * RT_SB] = pref + carry
            carry = carry + blk.sum(axis=1, keepdims=True)
    nblk = jnp.floor((carry + (MOE_TM - 1)) * (1.0 / MOE_TM))
    nblk_b = jnp.broadcast_to(nblk, (N_EXPERTS, LANES))
    lower = (lax.broadcasted_iota(I32, (N_EXPERTS, N_EXPERTS), 1) < lax.broadcasted_iota(I32, (N_EXPERTS, N_EXPERTS), 0)).astype(BF16)
    start_blk = jnp.dot(lower, nblk_b.astype(BF16), preferred_element_type=F32)
    start = start_blk[:, 0:1] * float(MOE_TM)

    end_blk = start_blk[:, 0:1] + nblk
    lastblk_ref[...] = jnp.broadcast_to(jnp.where(nblk > 0, end_blk - 1.0, -1.0), (N_EXPERTS, LANES)).astype(I32)
    e_blk = lax.broadcasted_iota(I32, (N_EXPERTS, SCHED_W), 0).astype(F32)
    b_blk = lax.broadcasted_iota(I32, (N_EXPERTS, SCHED_W), 1).astype(F32)
    lane = b_blk[0:1]
    n_used = jnp.sum(nblk, axis=0, keepdims=True)
    be = jnp.minimum(jnp.sum(jnp.where(end_blk <= b_blk, 1.0, 0.0), axis=0, keepdims=True), N_EXPERTS - 1.0)
    prev = jnp.where(lane == 0, -1.0, pltpu.roll(be, 1, 1))
    first = jnp.where((lane < n_used) & (be != prev), 1.0, 0.0)
    excl = jnp.dot(jnp.broadcast_to(first, (8, SCHED_W)).astype(BF16), tri, preferred_element_type=F32)[0:1]
    seg = excl + first - 1.0
    nxt = jnp.min(jnp.where((e_blk > be) & (nblk > 0), e_blk, float(N_EXPERTS)), axis=0, keepdims=True)
    nxt = jnp.where(nxt >= N_EXPERTS, -1.0, nxt)
    n_seg = jnp.sum(first, axis=1, keepdims=True)
    meta = jnp.where(lane == 0, n_used, jnp.where(lane == 1, n_seg, 0.0))
    for r, v in enumerate((be, first, seg, nxt, meta)):
        sched_ref[r:r + 1, :] = v.astype(I32)
    sched_ref[5:8, :] = jnp.zeros((3, SCHED_W), I32)
    for c in range(SEQ // RT_CH):
        cs = slice(c * RT_CH, (c + 1) * RT_CH)
        slot = rank_scr[:, cs] + start
        for k in range(TOP_K):
            sel = e_col == idx_scr[k:k + 1, cs]
            dest_ref[k:k + 1, cs] = jnp.where(sel, slot, 0.0).sum(axis=0, keepdims=True).astype(I32)


def _route(logits_t):
    return pl.pallas_call(
        _route_kernel,
        out_shape=(
            jax.ShapeDtypeStruct((TOP_K, SEQ), I32),
            jax.ShapeDtypeStruct((TOP_K, SEQ), F32),
            jax.ShapeDtypeStruct((8, SCHED_W), I32),
            jax.ShapeDtypeStruct((N_EXPERTS, LANES), I32),
        ),
        scratch_shapes=[pltpu.VMEM((8, SEQ), F32), pltpu.VMEM((N_EXPERTS, SEQ), F32)],
        compiler_params=pltpu.CompilerParams(vmem_limit_bytes=VMEM_LIMIT),
        name="route",
    )(logits_t)


DP_TM = 256
XS_W = D_MODEL // 2
U32 = jnp.uint32


def _dispatch_kernel(dest_ref, lastblk_ref, x_ref, g_ref, sc_ref, sh_ref, xs_hbm, pk, zero_buf, rs_scr, a_scr,
                     s_scr, zsem, sem):
    i = pl.program_id(0)
    par = i % 2

    @pl.when(i == 0)
    def _():
        _prep_modulation(g_ref, sc_ref, sh_ref, a_scr, s_scr)
        zero_buf[...] = jnp.zeros_like(zero_buf)

        def zcopy(e):
            b = jnp.maximum(lastblk_ref[e, 0], 0)
            return pltpu.make_async_copy(zero_buf, xs_hbm.at[pl.ds(pl.multiple_of(b * MOE_TM, MOE_TM), MOE_TM)], zsem)

        def zstart(e, _):
            @pl.when(lastblk_ref[e, 0] >= 0)
            def _():
                zcopy(e).start()
            return 0

        def zwait(e, _):
            @pl.when(lastblk_ref[e, 0] >= 0)
            def _():
                zcopy(e).wait()
            return 0

        lax.fori_loop(0, N_EXPERTS, zstart, 0)
        lax.fori_loop(0, N_EXPERTS, zwait, 0)

    _row_rms(x_ref, rs_scr, DP_TM)

    n_tiles = XS_W // LANES
    assert n_tiles == NORM_ROWS

    def pack_tile(r0, c):
        lo = _normed_tile(x_ref, rs_scr, a_scr, s_scr, r0, c)
        hi = _normed_tile(x_ref, rs_scr, a_scr, s_scr, r0, c + n_tiles)
        lo = lax.bitcast_convert_type(lo.astype(BF16).astype(F32), U32)
        hi = lax.bitcast_convert_type(hi.astype(BF16).astype(F32), U32)
        pk[par, pl.ds(r0, NORM_ROWS), c * LANES:(c + 1) * LANES] = (lo >> 16) | (hi & jnp.uint32(0xFFFF0000))

    def issue_token(u):
        for k in range(TOP_K):
            d = dest_ref[k * SEQ + i * DP_TM + u]
            pltpu.make_async_copy(pk.at[par, pl.ds(u, 1)], xs_hbm.at[pl.ds(d, 1)], sem.at[par]).start()

    for c in range(n_tiles):
        pack_tile(0, c)

    def group(g, _):
        r0 = pl.multiple_of(g * NORM_ROWS, NORM_ROWS)
        for u in range(NORM_ROWS):
            pack_tile(r0, u)
            issue_token(r0 - NORM_ROWS + u)
        return 0

    lax.fori_loop(1, DP_TM // NORM_ROWS, group, 0)

    def tail(u, _):
        issue_token(DP_TM - NORM_ROWS + u)
        return 0

    lax.fori_loop(0, NORM_ROWS, tail, 0)

    def drain(p):
        for _ in range(TOP_K):
            pltpu.make_async_copy(pk.at[p], xs_hbm.at[pl.ds(0, DP_TM)], sem.at[p]).wait()

    @pl.when(i > 0)
    def _():
        drain(1 - par)

    @pl.when(i == pl.num_programs(0) - 1)
    def _():
        drain(par)


def _dispatch(dest, lastblk, x1, norm2_g, mod):
    row = lambda n: pl.BlockSpec((1, D_MODEL), lambda i, d, lb, n=n: (0, n))
    return pl.pallas_call(
        _dispatch_kernel,
        out_shape=jax.ShapeDtypeStruct((MOE_ROWS, XS_W), U32),
        grid_spec=pltpu.PrefetchScalarGridSpec(
            num_scalar_prefetch=2,
            grid=(SEQ // DP_TM,),
            in_specs=[
                pl.BlockSpec((DP_TM, D_MODEL), lambda i, d, lb: (i, 0)),
                pl.BlockSpec((1, D_MODEL), lambda i, d, lb: (0, 0)),
                row(MOD_B_SC2), row(MOD_B_SH2),
            ],
            out_specs=pl.BlockSpec(memory_space=pl.ANY),
            scratch_shapes=[
                pltpu.VMEM((2, DP_TM, XS_W), U32),
                pltpu.VMEM((MOE_TM, XS_W), U32),
                pltpu.VMEM((DP_TM, LANES), F32),
                pltpu.VMEM((NORM_ROWS, D_MODEL), F32),
                pltpu.VMEM((NORM_ROWS, D_MODEL), F32),
                pltpu.SemaphoreType.DMA,
                pltpu.SemaphoreType.DMA((2,)),
            ],
        ),
        compiler_params=_params(("arbitrary",)),
        name="dispatch",
    )(dest, lastblk, x1, norm2_g, mod, mod)


F1_TF = 512
F2_TN = 4096
assert F2_TN == D_MODEL


CAST_ROWS = 128
WEIGHT_DMA_PRIORITY = 1


def _blocks_used(sched_ref):
    return sched_ref[SCHED_META, 0]


def _used_block(i, sched_ref):
    return jnp.minimum(i, _blocks_used(sched_ref) - 1)


def _stream_expert_weights(j, i, nj, sched_ref, tile_copies, stg, wbuf):
    @pl.when(sched_ref[SCHED_FIRST, i] == 1)
    def _():
        seq = j * sched_ref[SCHED_META, 1] + sched_ref[SCHED_SEG, i]
        slot = seq % 2

        @pl.when(seq == 0)
        def _():
            for cp in tile_copies(sched_ref[SCHED_BE, i], j, slot):
                cp.start(priority=WEIGHT_DMA_PRIORITY)

        for cp in tile_copies(sched_ref[SCHED_BE, i], j, slot):
            cp.wait()

        nxt = sched_ref[SCHED_NXT, i]

        @pl.when(nxt >= 0)
        def _():
            for cp in tile_copies(nxt, j, 1 - slot):
                cp.start(priority=WEIGHT_DMA_PRIORITY)

        @pl.when((nxt < 0) & (j + 1 < nj))
        def _():
            for cp in tile_copies(sched_ref[SCHED_BE, 0], j + 1, 1 - slot):
                cp.start(priority=WEIGHT_DMA_PRIORITY)

        if wbuf is not None:
            def cast(r, _):
                r0 = pl.multiple_of(r * CAST_ROWS, CAST_ROWS)
                wbuf[pl.ds(r0, CAST_ROWS), :] = stg[slot, pl.ds(r0, CAST_ROWS), :].astype(BF16)
                return 0

            lax.fori_loop(0, wbuf.shape[0] // CAST_ROWS, cast, 0)


F1_NH = D_FF // F1_TF


def _ffn1_kernel(sched_ref, x_ref, bg_ref, bl_ref, w1_hbm, o_ref, stg, wbuf, sem):
    i = pl.program_id(0)

    def slab_copies(e, h):
        return (
            pltpu.make_async_copy(w1_hbm.at[e, :, h * F1_TF:(h + 1) * F1_TF], stg.at[h, :, 0:F1_TF], sem.at[h]),
            pltpu.make_async_copy(w1_hbm.at[e, :, D_FF + h * F1_TF:D_FF + (h + 1) * F1_TF],
                                  stg.at[h, :, F1_TF:2 * F1_TF], sem.at[h]),
        )

    def compute(weights):
        xp = x_ref[...]
        x_lo = lax.bitcast_convert_type(xp << 16, F32).astype(BF16)
        x_hi = lax.bitcast_convert_type(xp & jnp.uint32(0xFFFF0000), F32).astype(BF16)
        for h in range(F1_NH):
            cols = slice(h * F1_TF, (h + 1) * F1_TF)
            y = jnp.dot(x_lo, weights(h, slice(0, XS_W)), preferred_element_type=F32)
            y = y + jnp.dot(x_hi, weights(h, slice(XS_W, D_MODEL)), preferred_element_type=F32)
            glu = jnp.minimum(y[:, :F1_TF] + bg_ref[:, cols], SWIGLU_LIMIT)
            lin = jnp.clip(y[:, F1_TF:] + bl_ref[:, cols], -SWIGLU_LIMIT, SWIGLU_LIMIT)
            o_ref[:, cols] = (glu * jax.nn.sigmoid(SWIGLU_ALPHA * glu) * (lin + 1.0)).astype(BF16)

    used = i < _blocks_used(sched_ref)
    first = sched_ref[SCHED_FIRST, i] == 1

    @pl.when(used & first)
    def _():
        @pl.when(i == 0)
        def _():
            for h in range(F1_NH):
                for cp in slab_copies(sched_ref[SCHED_BE, 0], h):
                    cp.start(priority=WEIGHT_DMA_PRIORITY)

        for h in range(F1_NH):
            for cp in slab_copies(sched_ref[SCHED_BE, i], h):
                cp.wait()

        def convert(h, rows):
            w = stg[h, rows, :].astype(BF16)
            wbuf[h, rows, :] = w
            return w

        compute(convert)

        nxt = sched_ref[SCHED_NXT, i]

        @pl.when(nxt >= 0)
        def _():
            for h in range(F1_NH):
                for cp in slab_copies(nxt, h):
                    cp.start(priority=WEIGHT_DMA_PRIORITY)

    @pl.when(used & jnp.logical_not(first))
    def _():
        compute(lambda h, rows: wbuf[h, rows, :])


def _ffn1(sched, xs, w1, b1_3):
    expert = lambda i, s: s[SCHED_BE, _used_block(i, s)]
    return pl.pallas_call(
        _ffn1_kernel,
        out_shape=jax.ShapeDtypeStruct((MOE_ROWS, D_FF), BF16),
        grid_spec=pltpu.PrefetchScalarGridSpec(
            num_scalar_prefetch=1,
            grid=(MOE_NB,),
            in_specs=[
                pl.BlockSpec((MOE_TM, XS_W), lambda i, s: (_used_block(i, s), 0)),
                pl.BlockSpec((None, 1, D_FF), lambda i, s: (expert(i, s), 0, 0)),
                pl.BlockSpec((None, 1, D_FF), lambda i, s: (expert(i, s), 0, 1)),
                pl.BlockSpec(memory_space=pl.ANY),
            ],
            out_specs=pl.BlockSpec((MOE_TM, D_FF), lambda i, s: (_used_block(i, s), 0)),
            scratch_shapes=[
                pltpu.VMEM((F1_NH, D_MODEL, 2 * F1_TF), F32),
                pltpu.VMEM((F1_NH, D_MODEL, 2 * F1_TF), BF16),
                pltpu.SemaphoreType.DMA((F1_NH,)),
            ],
        ),
        compiler_params=_params(("arbitrary",), vmem=BIG_VMEM_LIMIT),
        name="ffn1",
    )(sched, xs, b1_3, b1_3, w1)


def _ffn2_kernel(sched_ref, a_ref, b_ref, w2_hbm, o_ref, stg, sem):
    j = pl.program_id(0)
    i = pl.program_id(1)
    nj = pl.num_programs(0)

    def tile_copies(e, jj, slot):
        c0 = pl.multiple_of(jj * F2_TN, F2_TN)
        return (pltpu.make_async_copy(w2_hbm.at[e, :, pl.ds(c0, F2_TN)], stg.at[slot], sem.at[slot]),)

    @pl.when(i < _blocks_used(sched_ref))
    def _():
        _stream_expert_weights(j, i, nj, sched_ref, tile_copies, stg, None)
        slot = (j * sched_ref[SCHED_META, 1] + sched_ref[SCHED_SEG, i]) % 2
        y = jnp.dot(a_ref[...], stg[slot].astype(BF16), preferred_element_type=F32) + b_ref[...]
        lo = lax.bitcast_convert_type(y[:, :XS_W].astype(BF16).astype(F32), U32)
        hi = lax.bitcast_convert_type(y[:, XS_W:].astype(BF16).astype(F32), U32)
        o_ref[...] = (lo >> 16) | (hi & jnp.uint32(0xFFFF0000))


def _ffn2(sched, act, w2, b2_3):
    return pl.pallas_call(
        _ffn2_kernel,
        out_shape=jax.ShapeDtypeStruct((MOE_ROWS, XS_W), U32),
        grid_spec=pltpu.PrefetchScalarGridSpec(
            num_scalar_prefetch=1,
            grid=(D_MODEL // F2_TN, MOE_NB),
            in_specs=[
                pl.BlockSpec((MOE_TM, D_FF), lambda j, i, s: (_used_block(i, s), 0)),
                pl.BlockSpec((None, 1, F2_TN), lambda j, i, s: (s[SCHED_BE, _used_block(i, s)], 0, j)),
                pl.BlockSpec(memory_space=pl.ANY),
            ],
            out_specs=pl.BlockSpec((MOE_TM, XS_W), lambda j, i, s: (_used_block(i, s), j)),
            scratch_shapes=[
                pltpu.VMEM((2, D_FF, F2_TN), F32),
                pltpu.SemaphoreType.DMA((2,)),
            ],
        ),
        compiler_params=_params(("arbitrary", "arbitrary")),
        name="ffn2",
    )(sched, act, b2_3, w2)


CB_TM = 128


def _combine_kernel(dest_ref, x_ref, gate_ref, g2_ref, ys_hbm, o_ref, buf, gate_scr, g2_scr, sem):
    i = pl.program_id(0)
    last = pl.num_programs(0) - 1
    par = i % 2
    n_tiles = XS_W // LANES
    copies_per_tile = 8 * TOP_K // n_tiles
    assert copies_per_tile * n_tiles == 8 * TOP_K

    def row_start(tile, p, u, k):
        d = dest_ref[k * SEQ + tile * CB_TM + u]
        pltpu.make_async_copy(ys_hbm.at[pl.ds(d, 1)], buf.at[p, k, pl.ds(u, 1)], sem.at[p]).start()

    def drain(p):
        for k in range(TOP_K):
            pltpu.make_async_copy(ys_hbm.at[pl.ds(0, CB_TM)], buf.at[p, k], sem.at[p]).wait()

    @pl.when(i == 0)
    def _():
        def body(u, _):
            for k in range(TOP_K):
                row_start(0, 0, u, k)
            return 0
        lax.fori_loop(0, CB_TM, body, 0)
        g2_scr[...] = jnp.broadcast_to(g2_ref[...], g2_scr.shape)

    drain(par)

    for k in range(TOP_K):
        gate_scr[k] = jnp.broadcast_to(gate_ref[:, k:k + 1], (CB_TM, LANES))

    nxt = jnp.minimum(i + 1, last)

    def body(r, _):
        r0 = pl.multiple_of(r * 8, 8)
        rows = pl.ds(r0, 8)
        gk = [gate_scr[k, rows, :] for k in range(TOP_K)]
        for c in range(n_tiles):
            cs_lo = slice(c * LANES, (c + 1) * LANES)
            cs_hi = slice(XS_W + c * LANES, XS_W + (c + 1) * LANES)
            y_lo = y_hi = None
            for k in range(TOP_K):
                w = buf[par, k, rows, cs_lo]
                lo = lax.bitcast_convert_type(w << 16, F32) * gk[k]
                hi = lax.bitcast_convert_type(w & jnp.uint32(0xFFFF0000), F32) * gk[k]
                y_lo = lo if y_lo is None else y_lo + lo
                y_hi = hi if y_hi is None else y_hi + hi
            o_ref[rows, cs_lo] = x_ref[rows, cs_lo] + g2_scr[:, cs_lo] * y_lo
            o_ref[rows, cs_hi] = x_ref[rows, cs_hi] + g2_scr[:, cs_hi] * y_hi
            for q in range(copies_per_tile):
                n = c * copies_per_tile + q
                row_start(nxt, 1 - par, r0 + n // TOP_K, n % TOP_K)
        return 0

    lax.fori_loop(0, CB_TM // 8, body, 0)

    @pl.when(i == last)
    def _():
        drain(1 - par)


def _combine(dest, x1, gates_t, mod, ys):
    return pl.pallas_call(
        _combine_kernel,
        out_shape=jax.ShapeDtypeStruct((SEQ, D_MODEL), F32),
        grid_spec=pltpu.PrefetchScalarGridSpec(
            num_scalar_prefetch=1,
            grid=(SEQ // CB_TM,),
            in_specs=[
                pl.BlockSpec((CB_TM, D_MODEL), lambda i, d: (i, 0)),
                pl.BlockSpec((CB_TM, TOP_K), lambda i, d: (i, 0)),
                pl.BlockSpec((1, D_MODEL), lambda i, d: (0, MOD_B_G2)),
                pl.BlockSpec(memory_space=pl.ANY),
            ],
            out_specs=pl.BlockSpec((CB_TM, D_MODEL), lambda i, d: (i, 0)),
            scratch_shapes=[
                pltpu.VMEM((2, TOP_K, CB_TM, XS_W), U32),
                pltpu.VMEM((TOP_K, CB_TM, LANES), F32),
                pltpu.VMEM((8, D_MODEL), F32),
                pltpu.SemaphoreType.DMA((2,)),
            ],
        ),
        compiler_params=_params(("arbitrary",)),
        name="combine",
    )(dest, x1, gates_t, mod, ys)


def _rope_tables():
    t = jnp.arange(SEQ, dtype=I32)
    row = (t // GRID_W).astype(F32)
    col = (t % GRID_W).astype(F32)
    inv_freq = ROPE_THETA ** (-jnp.arange(0, ROPE_AXIS_DIM, 2, dtype=F32) / ROPE_AXIS_DIM)
    ang_r = inv_freq[:, None] * row[None, :]
    ang_c = inv_freq[:, None] * col[None, :]
    cos_t = jnp.concatenate([jnp.cos(ang_r), jnp.cos(ang_r), jnp.cos(ang_c), jnp.cos(ang_c)], axis=0)
    sin_t = jnp.concatenate([-jnp.sin(ang_r), jnp.sin(ang_r), -jnp.sin(ang_c), jnp.sin(ang_c)], axis=0)
    return cos_t, sin_t


def kernel(x, c, w_mod, b_mod, norm1_g, w_in, q_norm_g, k_norm_g, w_pool, pool_scale, w_out, norm2_g,
           w_router, b_router, w1, b1, w2, b2):
    assert x.shape == (1, SEQ, D_MODEL) and w_mod.shape[0] == 1
    x2 = x[0]
    cos_t, sin_t = _rope_tables()

    c_col = c.reshape(D_MODEL, 1)
    mod_a = _mod(c_col, w_mod[0], b_mod)

    pool_in, qt, k, vt, mod = _inproj(x2, mod_a, norm1_g, w_in[0].astype(BF16), cos_t, sin_t,
                                      q_norm_g.reshape(HEAD_DIM, 1), k_norm_g.reshape(HEAD_DIM, 1),
                                      c.reshape(D_MODEL // LANES, LANES), w_mod[0], b_mod)
    attn = _attention(qt, k, vt)
    pool = _pool(pool_in, w_pool[0].astype(BF16), pool_scale)
    x1 = _outproj(attn, pool, w_out[0].astype(BF16), x2, mod)

    wr = w_router[0]
    wr_hi = wr.astype(BF16)
    wr_lo = (wr - wr_hi.astype(F32)).astype(BF16)
    pad = lambda a: jnp.pad(a, ((0, 0), (0, RT_PAD - N_EXPERTS)))
    wcat = jnp.concatenate([pad(wr_hi), pad(wr_lo)], axis=1)
    logits_t = _norm2(x1, norm2_g, mod, wcat, pad(wr_hi), pad(b_router))

    dest, gates, sched, lastblk = _route(logits_t)
    dest = dest.reshape(TOP_K * SEQ)

    xs = _dispatch(dest, lastblk, x1, norm2_g, mod)
    act = _ffn1(sched, xs, w1[0], b1[0].reshape(N_EXPERTS, 1, 2 * D_FF))
    ys = _ffn2(sched, act, w2[0], b2[0].reshape(N_EXPERTS, 1, D_MODEL))
    out = _combine(dest, x1, gates.T, mod, ys)
    return out[None]
```

```python
import functools
import math

import jax
import jax.numpy as jnp
from jax import lax
from jax.experimental import pallas as pl
from jax.experimental.pallas import tpu as pltpu

F32 = jnp.float32
BF16 = jnp.bfloat16
I32 = jnp.int32

D_MODEL = 4096
SEQ = 8192
POOL_WIDTH = 2048
ATTN_WIDTH = 2048
HEAD_DIM = 128
N_HEADS = 16
N_KV_HEADS = 4
GROUP = N_HEADS // N_KV_HEADS
KV_WIDTH = N_KV_HEADS * HEAD_DIM
IN_WIDTH = POOL_WIDTH + ATTN_WIDTH + 2 * KV_WIDTH
POOL_WINDOWS = (2, 4, 8, 16)
POOL_GROUP_WIDTH = POOL_WIDTH // len(POOL_WINDOWS)
GRID_W = 64
ROPE_THETA = 10000.0
ROPE_AXIS_DIM = HEAD_DIM // 2
N_EXPERTS = 32
TOP_K = 4
D_FF = D_MODEL // 4
SWIGLU_ALPHA = 1.702
SWIGLU_LIMIT = 7.0
N_MOD = 6
EPS = 1e-6

LANES = 128
VMEM_LIMIT = 56 * 1024 * 1024
BIG_VMEM_LIMIT = 62 * 1024 * 1024

Q_SCALE = (HEAD_DIM ** -0.5) * math.log2(math.e)

MOE_TM = 256
MOE_NB = SEQ * TOP_K // MOE_TM + N_EXPERTS
MOE_ROWS = MOE_NB * MOE_TM


def _params(sem, vmem=VMEM_LIMIT):
    return pltpu.CompilerParams(dimension_semantics=sem, vmem_limit_bytes=vmem)


MOD_TN = 1024
MOD_KC = 256


def _mod_kernel(c_ref, w_ref, b_ref, o_ref):
    def body(k, acc):
        r = pl.multiple_of(k * MOD_KC, MOD_KC)
        ck = c_ref[pl.ds(r, MOD_KC), :]
        ck = ck * jax.nn.sigmoid(ck)
        p = w_ref[pl.ds(r, MOD_KC), :] * ck
        return acc + p.reshape(MOD_KC // 8, 8, MOD_TN).sum(axis=0)

    acc = lax.fori_loop(0, D_MODEL // MOD_KC, body, jnp.zeros((8, MOD_TN), F32))
    o_ref[...] = acc.sum(axis=0, keepdims=True) + b_ref[...]


MOD_A = 2
MOD_B = N_MOD - MOD_A
MOD_B_G1, MOD_B_SH2, MOD_B_SC2, MOD_B_G2 = range(MOD_B)


def _mod(c_col, w_mod, b_mod):
    n = MOD_A * D_MODEL
    return pl.pallas_call(
        _mod_kernel,
        out_shape=jax.ShapeDtypeStruct((1, n), F32),
        grid=(n // MOD_TN,),
        in_specs=[
            pl.BlockSpec((D_MODEL, 1), lambda j: (0, 0)),
            pl.BlockSpec((D_MODEL, MOD_TN), lambda j: (0, j)),
            pl.BlockSpec((1, MOD_TN), lambda j: (0, j)),
        ],
        out_specs=pl.BlockSpec((1, MOD_TN), lambda j: (0, j)),
        compiler_params=_params(("arbitrary",)),
        name="mod",
    )(c_col, w_mod, b_mod)


IP_TM = 512
IP_TN = 1024
IP_NJ = IN_WIDTH // IP_TN
IP_J_Q = POOL_WIDTH // IP_TN
IP_J_KV = IP_J_Q + ATTN_WIDTH // IP_TN
NORM_ROWS = 16
BF16_SUBLANES = 16
KA_W = 2 * HEAD_DIM
VA_H = HEAD_DIM + BF16_SUBLANES


def _prep_modulation(g_ref, sc_ref, sh_ref, a_scr, s_scr):
    a_scr[...] = jnp.broadcast_to(g_ref[...] * (1.0 + sc_ref[...]), a_scr.shape)
    s_scr[...] = jnp.broadcast_to(sh_ref[...], s_scr.shape)


def _row_rms(x_ref, rs_scr, n_rows):
    def body(r, _):
        r0 = pl.multiple_of(r * NORM_ROWS, NORM_ROWS)
        width = x_ref.shape[1]
        parts = []
        for c in range(width // LANES):
            xc = x_ref[pl.ds(r0, NORM_ROWS), c * LANES:(c + 1) * LANES]
            parts.append(xc * xc)
        while len(parts) > 1:
            parts = [parts[p] + parts[p + 1] for p in range(0, len(parts), 2)]
        rs_scr[pl.ds(r0, NORM_ROWS), :] = parts[0]
        return 0
    lax.fori_loop(0, n_rows // NORM_ROWS, body, 0)
    ms = jnp.sum(rs_scr[...], axis=-1, keepdims=True) * (1.0 / x_ref.shape[1])
    rs_scr[...] = jnp.broadcast_to(lax.rsqrt(ms + EPS), rs_scr.shape)


def _normed_tile(x_ref, rs_scr, a_scr, s_scr, r0, c):
    cs = slice(c * LANES, (c + 1) * LANES)
    return x_ref[pl.ds(r0, NORM_ROWS), cs] * rs_scr[pl.ds(r0, NORM_ROWS), :] * a_scr[:, cs] + s_scr[:, cs]


def _norm_rope_t(xt, g_col, cos_t, sin_t):
    ms = jnp.mean(xt * xt, axis=0, keepdims=True)
    y = xt * lax.rsqrt(ms + EPS) * g_col
    q = ROPE_AXIS_DIM // 2
    partner = jnp.concatenate([y[q:2 * q], y[0:q], y[3 * q:4 * q], y[2 * q:3 * q]], axis=0)
    return y * cos_t + partner * sin_t


MB_TN = 256
MB_STEPS = MOD_B * D_MODEL // MB_TN
MB_KC = 256


def _tree_sum(parts):
    while len(parts) > 1:
        parts = [parts[p] + parts[p + 1] for p in range(0, len(parts), 2)]
    return parts[0]


def _later_modulation(t, c_ref, wm_ref, bm_ref, mb_ref, cb_scr):
    @pl.when(t == 0)
    def _():
        for r in range(D_MODEL // LANES):
            cv = c_ref[r:r + 1, :]
            cb_scr[r * LANES:(r + 1) * LANES, :] = jnp.broadcast_to(cv * jax.nn.sigmoid(cv), (LANES, LANES)).T

    @pl.when(t < MB_STEPS)
    def _():
        def body(k, accs):
            r0 = pl.multiple_of(k * MB_KC, MB_KC)
            cb = cb_scr[pl.ds(r0, MB_KC), :]
            out = []
            for n in range(MB_TN // LANES):
                p = wm_ref[pl.ds(r0, MB_KC), n * LANES:(n + 1) * LANES] * cb
                out.append(accs[n] + _tree_sum([p[g * 8:(g + 1) * 8] for g in range(MB_KC // 8)]))
            return tuple(out)

        zero = jnp.zeros((8, LANES), F32)
        accs = lax.fori_loop(0, D_MODEL // MB_KC, body, (zero,) * (MB_TN // LANES))
        for n in range(MB_TN // LANES):
            cs = slice(n * LANES, (n + 1) * LANES)
            mb_ref[:, cs] = accs[n].sum(axis=0, keepdims=True) + bm_ref[:, cs]


def _inproj_kernel(x_ref, g_ref, sc_ref, sh_ref, w_ref, cos_ref, sin_ref, qg_ref, kg_ref, c_ref, wm_ref, bm_ref,
                   pool_ref, qt_ref, k_ref, vt_ref, mb_ref, h_scr, rs_scr, a_scr, s_scr, cb_scr):
    j = pl.program_id(1)
    _later_modulation(pl.program_id(0) * IP_NJ + j, c_ref, wm_ref, bm_ref, mb_ref, cb_scr)

    @pl.when(j == 0)
    def _():
        _prep_modulation(g_ref, sc_ref, sh_ref, a_scr, s_scr)
        _row_rms(x_ref, rs_scr, IP_TM)

        def body(r, _):
            r0 = pl.multiple_of(r * NORM_ROWS, NORM_ROWS)
            for c in range(D_MODEL // LANES):
                h_scr[pl.ds(r0, NORM_ROWS), c * LANES:(c + 1) * LANES] = _normed_tile(
                    x_ref, rs_scr, a_scr, s_scr, r0, c).astype(BF16)
            return 0
        lax.fori_loop(0, IP_TM // NORM_ROWS, body, 0)

    acc = jnp.dot(h_scr[...], w_ref[...], preferred_element_type=F32)

    @pl.when(j < IP_J_Q)
    def _():
        pool_ref[...] = acc.astype(BF16)

    @pl.when((j >= IP_J_Q) & (j < IP_J_KV))
    def _():
        for hh in range(IP_TN // HEAD_DIM):
            sl = slice(hh * HEAD_DIM, (hh + 1) * HEAD_DIM)
            r = _norm_rope_t(acc[:, sl].T, qg_ref[...], cos_ref[...], sin_ref[...]) * Q_SCALE
            qt_ref[sl, :] = r.astype(BF16)

    @pl.when(j == IP_J_KV)
    def _():
        lane = lax.broadcasted_iota(I32, (IP_TM, HEAD_DIM), 1)
        one_col = jnp.where(lane == 0, 1.0, 0.0).astype(BF16)
        for hh in range(N_KV_HEADS):
            sl = slice(hh * HEAD_DIM, (hh + 1) * HEAD_DIM)
            k_ref[:, hh * KA_W:hh * KA_W + HEAD_DIM] = _norm_rope_t(
                acc[:, sl].T, kg_ref[...], cos_ref[...], sin_ref[...]).T.astype(BF16)
            k_ref[:, hh * KA_W + HEAD_DIM:(hh + 1) * KA_W] = one_col
        for hh in range(N_KV_HEADS):
            sl = slice(KV_WIDTH + hh * HEAD_DIM, KV_WIDTH + (hh + 1) * HEAD_DIM)
            vt_ref[hh * VA_H:hh * VA_H + HEAD_DIM, :] = acc[:, sl].T.astype(BF16)
            vt_ref[hh * VA_H + HEAD_DIM:(hh + 1) * VA_H, :] = jnp.ones((VA_H - HEAD_DIM, IP_TM), BF16)


def _inproj(x2, mod, norm1_g, w_in_b, cos_t, sin_t, qg, kg, c_col, w_mod, b_mod):
    row = lambda n: pl.BlockSpec((1, D_MODEL), lambda i, j, n=n: (0, n))
    mb_tile = lambda i, j: jnp.minimum(i * IP_NJ + j, MB_STEPS - 1)
    mb_first = MOD_A * D_MODEL // MB_TN
    return pl.pallas_call(
        _inproj_kernel,
        out_shape=(
            jax.ShapeDtypeStruct((SEQ, POOL_WIDTH), BF16),
            jax.ShapeDtypeStruct((ATTN_WIDTH, SEQ), BF16),
            jax.ShapeDtypeStruct((SEQ, N_KV_HEADS * KA_W), BF16),
            jax.ShapeDtypeStruct((N_KV_HEADS * VA_H, SEQ), BF16),
            jax.ShapeDtypeStruct((1, MOD_B * D_MODEL), F32),
        ),
        grid=(SEQ // IP_TM, IP_NJ),
        in_specs=[
            pl.BlockSpec((IP_TM, D_MODEL), lambda i, j: (i, 0)),
            pl.BlockSpec((1, D_MODEL), lambda i, j: (0, 0)),
            row(1), row(0),
            pl.BlockSpec((D_MODEL, IP_TN), lambda i, j: (0, j)),
            pl.BlockSpec((HEAD_DIM, IP_TM), lambda i, j: (0, i)),
            pl.BlockSpec((HEAD_DIM, IP_TM), lambda i, j: (0, i)),
            pl.BlockSpec((HEAD_DIM, 1), lambda i, j: (0, 0)),
            pl.BlockSpec((HEAD_DIM, 1), lambda i, j: (0, 0)),
            pl.BlockSpec((D_MODEL // LANES, LANES), lambda i, j: (0, 0)),
            pl.BlockSpec((D_MODEL, MB_TN), lambda i, j: (0, mb_first + mb_tile(i, j))),
            pl.BlockSpec((1, MB_TN), lambda i, j: (0, mb_first + mb_tile(i, j))),
        ],
        out_specs=(
            pl.BlockSpec((IP_TM, IP_TN), lambda i, j: (i, jnp.minimum(j, IP_J_Q - 1))),
            pl.BlockSpec((IP_TN, IP_TM), lambda i, j: (jnp.clip(j - IP_J_Q, 0, IP_J_KV - IP_J_Q - 1), i)),
            pl.BlockSpec((IP_TM, N_KV_HEADS * KA_W), lambda i, j: (i, 0)),
            pl.BlockSpec((N_KV_HEADS * VA_H, IP_TM), lambda i, j: (0, i)),
            pl.BlockSpec((1, MB_TN), lambda i, j: (0, mb_tile(i, j))),
        ),
        scratch_shapes=[
            pltpu.VMEM((IP_TM, D_MODEL), BF16),
            pltpu.VMEM((IP_TM, LANES), F32),
            pltpu.VMEM((NORM_ROWS, D_MODEL), F32),
            pltpu.VMEM((NORM_ROWS, D_MODEL), F32),
            pltpu.VMEM((D_MODEL, LANES), F32),
        ],
        compiler_params=_params(("arbitrary", "arbitrary"), vmem=BIG_VMEM_LIMIT),
        name="inproj",
    )(x2, norm1_g, mod, mod, w_in_b, cos_t, sin_t, qg, kg, c_col, w_mod, b_mod)


AT_TQ = 512
AT_TK = 8192
AT_TK_ONLINE = 512
SHIFT_LIMIT = 60.0


def _attn_kernel(qt_ref, k_ref, vt_ref, o_ref, qa_scr, p_scr, kmax_scr):
    h = pl.program_id(0)
    i = pl.program_id(1)

    @pl.when((i == 0) & (h % GROUP == 0))
    def _():
        def body(c, mx):
            c0 = pl.multiple_of(c * AT_TK, AT_TK)
            kc = k_ref[pl.ds(c0, AT_TK), :HEAD_DIM].astype(F32)
            n2 = (kc * kc).sum(axis=1, keepdims=True)
            return jnp.maximum(mx, n2.max(axis=0, keepdims=True))
        mx = lax.fori_loop(0, SEQ // AT_TK, body, jnp.zeros((1, 1), F32))
        kmax_scr[...] = jnp.broadcast_to(jnp.sqrt(mx), kmax_scr.shape)

    q = qt_ref[...].astype(F32)
    bound = jnp.sqrt((q * q).sum(axis=0, keepdims=True)) * kmax_scr[0:1, 0:1] * 1.01
    fast = jnp.max(bound) <= SHIFT_LIMIT

    @pl.when(fast)
    def _():
        qa_scr[0:HEAD_DIM, :] = qt_ref[...]
        row = lax.broadcasted_iota(I32, (KA_W - HEAD_DIM, AT_TQ), 0)
        qa_scr[HEAD_DIM:, :] = jnp.where(row == 0, -bound, 0.0).astype(BF16)

        def body(c, _):
            c0 = pl.multiple_of(c * AT_TK, AT_TK)
            s = jnp.dot(k_ref[pl.ds(c0, AT_TK), :], qa_scr[...], preferred_element_type=F32)
            p_scr[pl.ds(c0, AT_TK), :] = jnp.exp2(s).astype(BF16)
            return 0

        lax.fori_loop(0, SEQ // AT_TK, body, 0)
        o = jnp.dot(vt_ref[...], p_scr[...], preferred_element_type=F32)
        o_ref[...] = (o[:HEAD_DIM] * (1.0 / o[HEAD_DIM:HEAD_DIM + 1])).T.astype(BF16)

    @pl.when(jnp.logical_not(fast))
    def _():
        qt = qt_ref[...]

        def chunk(c, carry):
            m, l, acc = carry
            c0 = pl.multiple_of(c * AT_TK_ONLINE, AT_TK_ONLINE)
            s = jnp.dot(k_ref[pl.ds(c0, AT_TK_ONLINE), :HEAD_DIM], qt, preferred_element_type=F32)
            m_new = jnp.maximum(m, s.max(axis=0, keepdims=True))
            alpha = jnp.exp2(m - m_new)
            p = jnp.exp2(s - m_new)
            l = alpha * l + p.sum(axis=0, keepdims=True)
            pv = jnp.dot(vt_ref[:HEAD_DIM, pl.ds(c0, AT_TK_ONLINE)], p.astype(BF16), preferred_element_type=F32)
            return m_new, l, alpha * acc + pv

        init = (jnp.full((1, AT_TQ), -jnp.inf, F32), jnp.zeros((1, AT_TQ), F32),
                jnp.zeros((HEAD_DIM, AT_TQ), F32))
        _, l, acc = lax.fori_loop(0, SEQ // AT_TK_ONLINE, chunk, init)
        o_ref[...] = (acc * (1.0 / l)).T.astype(BF16)


def _attention(qt, k, vt):
    return pl.pallas_call(
        _attn_kernel,
        out_shape=jax.ShapeDtypeStruct((SEQ, ATTN_WIDTH), BF16),
        grid=(N_HEADS, SEQ // AT_TQ),
        in_specs=[
            pl.BlockSpec((HEAD_DIM, AT_TQ), lambda h, i: (h, i)),
            pl.BlockSpec((SEQ, KA_W), lambda h, i: (0, h // GROUP)),
            pl.BlockSpec((VA_H, SEQ), lambda h, i: (h // GROUP, 0)),
        ],
        out_specs=pl.BlockSpec((AT_TQ, HEAD_DIM), lambda h, i: (i, h)),
        scratch_shapes=[
            pltpu.VMEM((KA_W, AT_TQ), BF16),
            pltpu.VMEM((SEQ, AT_TQ), BF16),
            pltpu.VMEM((8, LANES), F32),
        ],
        compiler_params=_params(("arbitrary", "arbitrary")),
        name="attn",
    )(qt, k, vt)


PL_TM = 256
PL_HALO = 16


def _pool_kernel(prev_ref, main_ref, next_ref, wp_ref, scale_ref, o_ref, buf):
    i = pl.program_id(0)
    last = pl.num_programs(0) - 1
    buf[0:PL_HALO, :] = jnp.where(i == 0, 0.0, prev_ref[...].astype(F32))
    buf[PL_HALO:PL_HALO + PL_TM, :] = main_ref[...].astype(F32)
    buf[PL_HALO + PL_TM:, :] = jnp.where(i == last, 0.0, next_ref[...].astype(F32))
    t = i * PL_TM + lax.broadcasted_iota(I32, (PL_TM, 1), 0)
    for gi, w in enumerate(POOL_WINDOWS):
        cols = slice(gi * POOL_GROUP_WIDTH, (gi + 1) * POOL_GROUP_WIDTH)
        win = buf[PL_HALO - w // 2:PL_HALO - w // 2 + PL_TM, cols]
        for d in range(-w // 2 + 1, w // 2):
            win = win + buf[PL_HALO + d:PL_HALO + d + PL_TM, cols]
        lo = jnp.maximum(t - w // 2, 0)
        hi = jnp.minimum(t + w // 2 - 1, SEQ - 1)
        cnt = (hi - lo + 1).astype(F32)
        dlt = win / cnt - buf[PL_HALO:PL_HALO + PL_TM, cols]
        y = jnp.dot(dlt.astype(BF16), wp_ref[gi], preferred_element_type=F32)
        o_ref[:, cols] = (y * scale_ref[:, cols]).astype(BF16)


def _pool(pool_in, w_pool_b, pool_scale):
    nh = PL_TM // PL_HALO
    n_halo_blocks = SEQ // PL_HALO
    return pl.pallas_call(
        _pool_kernel,
        out_shape=jax.ShapeDtypeStruct((SEQ, POOL_WIDTH), BF16),
        grid=(SEQ // PL_TM,),
        in_specs=[
            pl.BlockSpec((PL_HALO, POOL_WIDTH), lambda i: (jnp.maximum(i * nh - 1, 0), 0)),
            pl.BlockSpec((PL_TM, POOL_WIDTH), lambda i: (i, 0)),
            pl.BlockSpec((PL_HALO, POOL_WIDTH), lambda i: (jnp.minimum((i + 1) * nh, n_halo_blocks - 1), 0)),
            pl.BlockSpec((len(POOL_WINDOWS), POOL_GROUP_WIDTH, POOL_GROUP_WIDTH), lambda i: (0, 0, 0)),
            pl.BlockSpec((1, POOL_WIDTH), lambda i: (0, 0)),
        ],
        out_specs=pl.BlockSpec((PL_TM, POOL_WIDTH), lambda i: (i, 0)),
        scratch_shapes=[pltpu.VMEM((PL_TM + 2 * PL_HALO, POOL_WIDTH), F32)],
        compiler_params=_params(("arbitrary",)),
        name="pool",
    )(pool_in, pool_in, pool_in, w_pool_b, pool_scale)


OP_TM = 1024
OP_TN = 512


def _outproj_kernel(a_ref, p_ref, wa_ref, wp_ref, x_ref, g_ref, o_ref):
    acc = jnp.dot(a_ref[...], wa_ref[...], preferred_element_type=F32)
    acc = acc + jnp.dot(p_ref[...], wp_ref[...], preferred_element_type=F32)
    o_ref[...] = x_ref[...] + g_ref[...] * acc


def _outproj(attn, pool, w_out_b, x2, mod):
    return pl.pallas_call(
        _outproj_kernel,
        out_shape=jax.ShapeDtypeStruct((SEQ, D_MODEL), F32),
        grid=(SEQ // OP_TM, D_MODEL // OP_TN),
        in_specs=[
            pl.BlockSpec((OP_TM, ATTN_WIDTH), lambda i, j: (i, 0)),
            pl.BlockSpec((OP_TM, POOL_WIDTH), lambda i, j: (i, 0)),
            pl.BlockSpec((ATTN_WIDTH, OP_TN), lambda i, j: (0, j)),
            pl.BlockSpec((POOL_WIDTH, OP_TN), lambda i, j: (1, j)),
            pl.BlockSpec((OP_TM, OP_TN), lambda i, j: (i, j)),
            pl.BlockSpec((1, OP_TN), lambda i, j: (0, MOD_B_G1 * (D_MODEL // OP_TN) + j)),
        ],
        out_specs=pl.BlockSpec((OP_TM, OP_TN), lambda i, j: (i, j)),
        compiler_params=_params(("arbitrary", "arbitrary")),
        name="outproj",
    )(attn, pool, w_out_b, w_out_b, x2, mod)


N2_TM = 256
RT_PAD = LANES


def _norm2_kernel(x_ref, g_ref, sc_ref, sh_ref, wcat_ref, whi_ref, b_ref, lt_ref, hi_scr, lo_scr,
                  rs_scr, a_scr, s_scr):
    @pl.when(pl.program_id(0) == 0)
    def _():
        _prep_modulation(g_ref, sc_ref, sh_ref, a_scr, s_scr)

    _row_rms(x_ref, rs_scr, N2_TM)

    def body(r, _):
        r0 = pl.multiple_of(r * NORM_ROWS, NORM_ROWS)
        for c in range(D_MODEL // LANES):
            cs = slice(c * LANES, (c + 1) * LANES)
            h = _normed_tile(x_ref, rs_scr, a_scr, s_scr, r0, c)
            hi = h.astype(BF16)
            hi_scr[pl.ds(r0, NORM_ROWS), cs] = hi
            lo_scr[pl.ds(r0, NORM_ROWS), cs] = (h - hi.astype(F32)).astype(BF16)
        return 0
    lax.fori_loop(0, N2_TM // NORM_ROWS, body, 0)
    a = jnp.dot(hi_scr[...], wcat_ref[...], preferred_element_type=F32)
    b = jnp.dot(lo_scr[...], whi_ref[...], preferred_element_type=F32)
    logits = a[:, :RT_PAD] + a[:, RT_PAD:] + b + b_ref[...]
    lt_ref[...] = logits.T[:N_EXPERTS, :]


def _norm2(x1, norm2_g, mod, wcat, whi, b_pad):
    row = lambda n: pl.BlockSpec((1, D_MODEL), lambda i, n=n: (0, n))
    return pl.pallas_call(
        _norm2_kernel,
        out_shape=jax.ShapeDtypeStruct((N_EXPERTS, SEQ), F32),
        grid=(SEQ // N2_TM,),
        in_specs=[
            pl.BlockSpec((N2_TM, D_MODEL), lambda i: (i, 0)),
            pl.BlockSpec((1, D_MODEL), lambda i: (0, 0)),
            row(MOD_B_SC2), row(MOD_B_SH2),
            pl.BlockSpec((D_MODEL, 2 * RT_PAD), lambda i: (0, 0)),
            pl.BlockSpec((D_MODEL, RT_PAD), lambda i: (0, 0)),
            pl.BlockSpec((1, RT_PAD), lambda i: (0, 0)),
        ],
        out_specs=pl.BlockSpec((N_EXPERTS, N2_TM), lambda i: (0, i)),
        scratch_shapes=[
            pltpu.VMEM((N2_TM, D_MODEL), BF16),
            pltpu.VMEM((N2_TM, D_MODEL), BF16),
            pltpu.VMEM((N2_TM, LANES), F32),
            pltpu.VMEM((NORM_ROWS, D_MODEL), F32),
            pltpu.VMEM((NORM_ROWS, D_MODEL), F32),
        ],
        compiler_params=_params(("arbitrary",)),
        name="norm2",
    )(x1, norm2_g, mod, mod, wcat, whi, b_pad)


RT_CH = 1024
RT_SB = 256


SCHED_W = RT_SB
SCHED_BE, SCHED_FIRST, SCHED_SEG, SCHED_NXT, SCHED_META, SCHED_HALF = range(6)


def _route_kernel(lt_ref, dest_ref, gate_ref, sched_ref, lastblk_ref, idx_scr, rank_scr):
    e_col = lax.broadcasted_iota(I32, (N_EXPERTS, RT_CH), 0).astype(F32)
    tri = (lax.broadcasted_iota(I32, (RT_SB, RT_SB), 0) < lax.broadcasted_iota(I32, (RT_SB, RT_SB), 1)).astype(BF16)
    carry = jnp.zeros((N_EXPERTS, 1), F32)
    for c in range(SEQ // RT_CH):
        cs = slice(c * RT_CH, (c + 1) * RT_CH)
        work = lt_ref[:, cs]
        vals = []
        mask = jnp.zeros((N_EXPERTS, RT_CH), F32)
        for k in range(TOP_K):
            m = work.max(axis=0, keepdims=True)
            idx = jnp.where(work == m, e_col, float(N_EXPERTS)).min(axis=0, keepdims=True)
            sel = e_col == idx
            vals.append(m)
            idx_scr[k:k + 1, cs] = idx
            mask = jnp.where(sel, 1.0, mask)
            work = jnp.where(sel, -jnp.inf, work)
        ex = [jnp.exp(v - vals[0]) for v in vals]
        den = ex[0] + ex[1] + ex[2] + ex[3]
        for k in range(TOP_K):
            gate_ref[k:k + 1, cs] = ex[k] / den
        for b in range(RT_CH // RT_SB):
            blk = mask[:, b * RT_SB:(b + 1) * RT_SB]
            pref = jnp.dot(blk.astype(BF16), tri, preferred_element_type=F32)
            rank_scr[:, c * RT_CH + b * RT_SB:c * RT_CH + (b + 1) * RT_SB] = pref + carry
            carry = carry + blk.sum(axis=1, keepdims=True)
    nblk = jnp.floor((carry + (MOE_TM - 1)) * (1.0 / MOE_TM))
    nblk_b = jnp.broadcast_to(nblk, (N_EXPERTS, LANES))
    lower = (lax.broadcasted_iota(I32, (N_EXPERTS, N_EXPERTS), 1) < lax.broadcasted_iota(I32, (N_EXPERTS, N_EXPERTS), 0)).astype(BF16)
    start_blk = jnp.dot(lower, nblk_b.astype(BF16), preferred_element_type=F32)
    start = start_blk[:, 0:1] * float(MOE_TM)

    end_blk = start_blk[:, 0:1] + nblk
    lastblk_ref[...] = jnp.broadcast_to(jnp.where(nblk > 0, end_blk - 1.0, -1.0), (N_EXPERTS, LANES)).astype(I32)
    e_blk = lax.broadcasted_iota(I32, (N_EXPERTS, SCHED_W), 0).astype(F32)
    b_blk = lax.broadcasted_iota(I32, (N_EXPERTS, SCHED_W), 1).astype(F32)
    lane = b_blk[0:1]
    n_used = jnp.sum(nblk, axis=0, keepdims=True)
    be = jnp.minimum(jnp.sum(jnp.where(end_blk <= b_blk, 1.0, 0.0), axis=0, keepdims=True), N_EXPERTS - 1.0)
    prev = jnp.where(lane == 0, -1.0, pltpu.roll(be, 1, 1))
    first = jnp.where((lane < n_used) & (be != prev), 1.0, 0.0)
    excl = jnp.dot(jnp.broadcast_to(first, (8, SCHED_W)).astype(BF16), tri, preferred_element_type=F32)[0:1]
    seg = excl + first - 1.0
    nxt = jnp.min(jnp.where((e_blk > be) & (nblk > 0), e_blk, float(N_EXPERTS)), axis=0, keepdims=True)
    nxt = jnp.where(nxt >= N_EXPERTS, -1.0, nxt)
    n_seg = jnp.sum(first, axis=1, keepdims=True)
    meta = jnp.where(lane == 0, n_used, jnp.where(lane == 1, n_seg, 0.0))
    rem = carry - (nblk - 1.0) * float(MOE_TM)
    half_e = (nblk > 0) & (rem <= float(MOE_TM // 2))
    half = jnp.sum(jnp.where((b_blk == end_blk - 1.0) & half_e, 1.0, 0.0), axis=0, keepdims=True)
    for r, v in enumerate((be, first, seg, nxt, meta, half)):
        sched_ref[r:r + 1, :] = v.astype(I32)
    sched_ref[6:8, :] = jnp.zeros((2, SCHED_W), I32)
    for c in range(SEQ // RT_CH):
        cs = slice(c * RT_CH, (c + 1) * RT_CH)
        slot = rank_scr[:, cs] + start
        for k in range(TOP_K):
            sel = e_col == idx_scr[k:k + 1, cs]
            dest_ref[k:k + 1, cs] = jnp.where(sel, slot, 0.0).sum(axis=0, keepdims=True).astype(I32)


def _route(logits_t):
    return pl.pallas_call(
        _route_kernel,
        out_shape=(
            jax.ShapeDtypeStruct((TOP_K, SEQ), I32),
            jax.ShapeDtypeStruct((TOP_K, SEQ), F32),
            jax.ShapeDtypeStruct((8, SCHED_W), I32),
            jax.ShapeDtypeStruct((N_EXPERTS, LANES), I32),
        ),
        scratch_shapes=[pltpu.VMEM((8, SEQ), F32), pltpu.VMEM((N_EXPERTS, SEQ), F32)],
        compiler_params=pltpu.CompilerParams(vmem_limit_bytes=VMEM_LIMIT),
        name="route",
    )(logits_t)


DP_TM = 256
XS_W = D_MODEL // 2
U32 = jnp.uint32


def _dispatch_kernel(dest_ref, lastblk_ref, x_ref, g_ref, sc_ref, sh_ref, xs_hbm, pk, zero_buf, rs_scr, a_scr,
                     s_scr, zsem, sem):
    i = pl.program_id(0)
    par = i % 2

    @pl.when(i == 0)
    def _():
        _prep_modulation(g_ref, sc_ref, sh_ref, a_scr, s_scr)
        zero_buf[...] = jnp.zeros_like(zero_buf)

        def zcopy(e):
            b = jnp.maximum(lastblk_ref[e, 0], 0)
            return pltpu.make_async_copy(zero_buf, xs_hbm.at[pl.ds(pl.multiple_of(b * MOE_TM, MOE_TM), MOE_TM)], zsem)

        def zstart(e, _):
            @pl.when(lastblk_ref[e, 0] >= 0)
            def _():
                zcopy(e).start()
            return 0

        def zwait(e, _):
            @pl.when(lastblk_ref[e, 0] >= 0)
            def _():
                zcopy(e).wait()
            return 0

        lax.fori_loop(0, N_EXPERTS, zstart, 0)
        lax.fori_loop(0, N_EXPERTS, zwait, 0)

    _row_rms(x_ref, rs_scr, DP_TM)

    n_tiles = XS_W // LANES
    assert n_tiles == NORM_ROWS

    def pack_tile(r0, c):
        lo = _normed_tile(x_ref, rs_scr, a_scr, s_scr, r0, c)
        hi = _normed_tile(x_ref, rs_scr, a_scr, s_scr, r0, c + n_tiles)
        lo = lax.bitcast_convert_type(lo.astype(BF16).astype(F32), U32)
        hi = lax.bitcast_convert_type(hi.astype(BF16).astype(F32), U32)
        pk[par, pl.ds(r0, NORM_ROWS), c * LANES:(c + 1) * LANES] = (lo >> 16) | (hi & jnp.uint32(0xFFFF0000))

    def issue_token(u):
        for k in range(TOP_K):
            d = dest_ref[k * SEQ + i * DP_TM + u]
            pltpu.make_async_copy(pk.at[par, pl.ds(u, 1)], xs_hbm.at[pl.ds(d, 1)], sem.at[par]).start()

    for c in range(n_tiles):
        pack_tile(0, c)

    def group(g, _):
        r0 = pl.multiple_of(g * NORM_ROWS, NORM_ROWS)
        for u in range(NORM_ROWS):
            pack_tile(r0, u)
            issue_token(r0 - NORM_ROWS + u)
        return 0

    lax.fori_loop(1, DP_TM // NORM_ROWS, group, 0)

    def tail(u, _):
        issue_token(DP_TM - NORM_ROWS + u)
        return 0

    lax.fori_loop(0, NORM_ROWS, tail, 0)

    def drain(p):
        for _ in range(TOP_K):
            pltpu.make_async_copy(pk.at[p], xs_hbm.at[pl.ds(0, DP_TM)], sem.at[p]).wait()

    @pl.when(i > 0)
    def _():
        drain(1 - par)

    @pl.when(i == pl.num_programs(0) - 1)
    def _():
        drain(par)


def _dispatch(dest, lastblk, x1, norm2_g, mod):
    row = lambda n: pl.BlockSpec((1, D_MODEL), lambda i, d, lb, n=n: (0, n))
    return pl.pallas_call(
        _dispatch_kernel,
        out_shape=jax.ShapeDtypeStruct((MOE_ROWS, XS_W), U32),
        grid_spec=pltpu.PrefetchScalarGridSpec(
            num_scalar_prefetch=2,
            grid=(SEQ // DP_TM,),
            in_specs=[
                pl.BlockSpec((DP_TM, D_MODEL), lambda i, d, lb: (i, 0)),
                pl.BlockSpec((1, D_MODEL), lambda i, d, lb: (0, 0)),
                row(MOD_B_SC2), row(MOD_B_SH2),
            ],
            out_specs=pl.BlockSpec(memory_space=pl.ANY),
            scratch_shapes=[
                pltpu.VMEM((2, DP_TM, XS_W), U32),
                pltpu.VMEM((MOE_TM, XS_W), U32),
                pltpu.VMEM((DP_TM, LANES), F32),
                pltpu.VMEM((NORM_ROWS, D_MODEL), F32),
                pltpu.VMEM((NORM_ROWS, D_MODEL), F32),
                pltpu.SemaphoreType.DMA,
                pltpu.SemaphoreType.DMA((2,)),
            ],
        ),
        compiler_params=_params(("arbitrary",)),
        name="dispatch",
    )(dest, lastblk, x1, norm2_g, mod, mod)


F2_TN = 4096
assert F2_TN == D_MODEL


CAST_ROWS = 128
WEIGHT_DMA_PRIORITY = 1


def _blocks_used(sched_ref):
    return sched_ref[SCHED_META, 0]


def _used_block(i, sched_ref):
    return jnp.minimum(i, _blocks_used(sched_ref) - 1)


def _stream_expert_weights(j, i, nj, sched_ref, tile_copies, stg, wbuf):
    @pl.when(sched_ref[SCHED_FIRST, i] == 1)
    def _():
        seq = j * sched_ref[SCHED_META, 1] + sched_ref[SCHED_SEG, i]
        slot = seq % 2

        @pl.when(seq == 0)
        def _():
            for cp in tile_copies(sched_ref[SCHED_BE, i], j, slot):
                cp.start(priority=WEIGHT_DMA_PRIORITY)

        for cp in tile_copies(sched_ref[SCHED_BE, i], j, slot):
            cp.wait()

        nxt = sched_ref[SCHED_NXT, i]

        @pl.when(nxt >= 0)
        def _():
            for cp in tile_copies(nxt, j, 1 - slot):
                cp.start(priority=WEIGHT_DMA_PRIORITY)

        @pl.when((nxt < 0) & (j + 1 < nj))
        def _():
            for cp in tile_copies(sched_ref[SCHED_BE, 0], j + 1, 1 - slot):
                cp.start(priority=WEIGHT_DMA_PRIORITY)

        if wbuf is not None:
            def cast(r, _):
                r0 = pl.multiple_of(r * CAST_ROWS, CAST_ROWS)
                wbuf[pl.ds(r0, CAST_ROWS), :] = stg[slot, pl.ds(r0, CAST_ROWS), :].astype(BF16)
                return 0

            lax.fori_loop(0, wbuf.shape[0] // CAST_ROWS, cast, 0)


F1_NH = D_MODEL // XS_W


def _ffn1_kernel(sched_ref, x_ref, bg_ref, bl_ref, w1_hbm, o_ref, stg, wbuf, sem):
    i = pl.program_id(0)

    def slab_copies(e, h):
        return (pltpu.make_async_copy(w1_hbm.at[e, h * XS_W:(h + 1) * XS_W, :], stg.at[h], sem.at[h]),)

    def compute(weights, n_rows=MOE_TM):
        xp = x_ref[0:n_rows, :]
        x_lo = lax.bitcast_convert_type(xp << 16, F32).astype(BF16)
        x_hi = lax.bitcast_convert_type(xp & jnp.uint32(0xFFFF0000), F32).astype(BF16)
        y = jnp.dot(x_lo, weights(0), preferred_element_type=F32)
        y = y + jnp.dot(x_hi, weights(1), preferred_element_type=F32)
        glu = jnp.minimum(y[:, :D_FF] + bg_ref[...], SWIGLU_LIMIT)
        lin = jnp.clip(y[:, D_FF:] + bl_ref[...], -SWIGLU_LIMIT, SWIGLU_LIMIT)
        o_ref[0:n_rows, :] = (glu * jax.nn.sigmoid(SWIGLU_ALPHA * glu) * (lin + 1.0)).astype(BF16)

    used = i < _blocks_used(sched_ref)
    first = sched_ref[SCHED_FIRST, i] == 1

    @pl.when(used & first)
    def _():
        @pl.when(i == 0)
        def _():
            for h in range(F1_NH):
                for cp in slab_copies(sched_ref[SCHED_BE, 0], h):
                    cp.start(priority=WEIGHT_DMA_PRIORITY)

        for h in range(F1_NH):
            for cp in slab_copies(sched_ref[SCHED_BE, i], h):
                cp.wait()

        def convert(h):
            w = stg[h].astype(BF16)
            wbuf[h] = w
            return w

        compute(convert)

        nxt = sched_ref[SCHED_NXT, i]

        @pl.when(nxt >= 0)
        def _():
            for h in range(F1_NH):
                for cp in slab_copies(nxt, h):
                    cp.start(priority=WEIGHT_DMA_PRIORITY)

    half = sched_ref[SCHED_HALF, i] == 1

    @pl.when(used & jnp.logical_not(first) & jnp.logical_not(half))
    def _():
        compute(lambda h: wbuf[h])

    @pl.when(used & jnp.logical_not(first) & half)
    def _():
        compute(lambda h: wbuf[h], MOE_TM // 2)


def _ffn1(sched, xs, w1, b1_3):
    expert = lambda i, s: s[SCHED_BE, _used_block(i, s)]
    return pl.pallas_call(
        _ffn1_kernel,
        out_shape=jax.ShapeDtypeStruct((MOE_ROWS, D_FF), BF16),
        grid_spec=pltpu.PrefetchScalarGridSpec(
            num_scalar_prefetch=1,
            grid=(MOE_NB,),
            in_specs=[
                pl.BlockSpec((MOE_TM, XS_W), lambda i, s: (_used_block(i, s), 0)),
                pl.BlockSpec((None, 1, D_FF), lambda i, s: (expert(i, s), 0, 0)),
                pl.BlockSpec((None, 1, D_FF), lambda i, s: (expert(i, s), 0, 1)),
                pl.BlockSpec(memory_space=pl.ANY),
            ],
            out_specs=pl.BlockSpec((MOE_TM, D_FF), lambda i, s: (_used_block(i, s), 0)),
            scratch_shapes=[
                pltpu.VMEM((F1_NH, XS_W, 2 * D_FF), F32),
                pltpu.VMEM((F1_NH, XS_W, 2 * D_FF), BF16),
                pltpu.SemaphoreType.DMA((F1_NH,)),
            ],
        ),
        compiler_params=_params(("arbitrary",), vmem=BIG_VMEM_LIMIT),
        name="ffn1",
    )(sched, xs, b1_3, b1_3, w1)


def _ffn2_kernel(sched_ref, a_ref, b_ref, w2_hbm, o_ref, stg, sem):
    j = pl.program_id(0)
    i = pl.program_id(1)
    nj = pl.num_programs(0)

    def tile_copies(e, jj, slot):
        c0 = pl.multiple_of(jj * F2_TN, F2_TN)
        return (pltpu.make_async_copy(w2_hbm.at[e, :, pl.ds(c0, F2_TN)], stg.at[slot], sem.at[slot]),)

    @pl.when(i < _blocks_used(sched_ref))
    def _():
        _stream_expert_weights(j, i, nj, sched_ref, tile_copies, stg, None)
        slot = (j * sched_ref[SCHED_META, 1] + sched_ref[SCHED_SEG, i]) % 2

        def compute(n_rows):
            y = jnp.dot(a_ref[0:n_rows, :], stg[slot].astype(BF16), preferred_element_type=F32) + b_ref[...]
            lo = lax.bitcast_convert_type(y[:, :XS_W].astype(BF16).astype(F32), U32)
            hi = lax.bitcast_convert_type(y[:, XS_W:].astype(BF16).astype(F32), U32)
            o_ref[0:n_rows, :] = (lo >> 16) | (hi & jnp.uint32(0xFFFF0000))

        half = sched_ref[SCHED_HALF, i] == 1

        @pl.when(jnp.logical_not(half))
        def _():
            compute(MOE_TM)

        @pl.when(half)
        def _():
            compute(MOE_TM // 2)


def _ffn2(sched, act, w2, b2_3):
    return pl.pallas_call(
        _ffn2_kernel,
        out_shape=jax.ShapeDtypeStruct((MOE_ROWS, XS_W), U32),
        grid_spec=pltpu.PrefetchScalarGridSpec(
            num_scalar_prefetch=1,
            grid=(D_MODEL // F2_TN, MOE_NB),
            in_specs=[
                pl.BlockSpec((MOE_TM, D_FF), lambda j, i, s: (_used_block(i, s), 0)),
                pl.BlockSpec((None, 1, F2_TN), lambda j, i, s: (s[SCHED_BE, _used_block(i, s)], 0, j)),
                pl.BlockSpec(memory_space=pl.ANY),
            ],
            out_specs=pl.BlockSpec((MOE_TM, XS_W), lambda j, i, s: (_used_block(i, s), j)),
            scratch_shapes=[
                pltpu.VMEM((2, D_FF, F2_TN), F32),
                pltpu.SemaphoreType.DMA((2,)),
            ],
        ),
        compiler_params=_params(("arbitrary", "arbitrary")),
        name="ffn2",
    )(sched, act, b2_3, w2)


CB_TM = 128


def _combine_kernel(dest_ref, x_ref, gate_ref, g2_ref, ys_hbm, o_ref, buf, gate_scr, g2_scr, sem):
    i = pl.program_id(0)
    last = pl.num_programs(0) - 1
    par = i % 2
    n_tiles = XS_W // LANES
    copies_per_tile = 8 * TOP_K // n_tiles
    assert copies_per_tile * n_tiles == 8 * TOP_K

    def row_start(tile, p, u, k):
        d = dest_ref[k * SEQ + tile * CB_TM + u]
        pltpu.make_async_copy(ys_hbm.at[pl.ds(d, 1)], buf.at[p, k, pl.ds(u, 1)], sem.at[p]).start()

    def drain(p):
        for k in range(TOP_K):
            pltpu.make_async_copy(ys_hbm.at[pl.ds(0, CB_TM)], buf.at[p, k], sem.at[p]).wait()

    @pl.when(i == 0)
    def _():
        def body(u, _):
            for k in range(TOP_K):
                row_start(0, 0, u, k)
            return 0
        lax.fori_loop(0, CB_TM, body, 0)
        g2_scr[...] = jnp.broadcast_to(g2_ref[...], g2_scr.shape)

    drain(par)

    for k in range(TOP_K):
        gate_scr[k] = jnp.broadcast_to(gate_ref[:, k:k + 1], (CB_TM, LANES))

    nxt = jnp.minimum(i + 1, last)

    def body(r, _):
        r0 = pl.multiple_of(r * 8, 8)
        rows = pl.ds(r0, 8)
        gk = [gate_scr[k, rows, :] for k in range(TOP_K)]
        for c in range(n_tiles):
            cs_lo = slice(c * LANES, (c + 1) * LANES)
            cs_hi = slice(XS_W + c * LANES, XS_W + (c + 1) * LANES)
            y_lo = y_hi = None
            for k in range(TOP_K):
                w = buf[par, k, rows, cs_lo]
                lo = lax.bitcast_convert_type(w << 16, F32) * gk[k]
                hi = lax.bitcast_convert_type(w & jnp.uint32(0xFFFF0000), F32) * gk[k]
                y_lo = lo if y_lo is None else y_lo + lo
                y_hi = hi if y_hi is None else y_hi + hi
            o_ref[rows, cs_lo] = x_ref[rows, cs_lo] + g2_scr[:, cs_lo] * y_lo
            o_ref[rows, cs_hi] = x_ref[rows, cs_hi] + g2_scr[:, cs_hi] * y_hi
            for q in range(copies_per_tile):
                n = c * copies_per_tile + q
                row_start(nxt, 1 - par, r0 + n // TOP_K, n % TOP_K)
        return 0

    lax.fori_loop(0, CB_TM // 8, body, 0)

    @pl.when(i == last)
    def _():
        drain(1 - par)


def _combine(dest, x1, gates_t, mod, ys):
    return pl.pallas_call(
        _combine_kernel,
        out_shape=jax.ShapeDtypeStruct((SEQ, D_MODEL), F32),
        grid_spec=pltpu.PrefetchScalarGridSpec(
            num_scalar_prefetch=1,
            grid=(SEQ // CB_TM,),
            in_specs=[
                pl.BlockSpec((CB_TM, D_MODEL), lambda i, d: (i, 0)),
                pl.BlockSpec((CB_TM, TOP_K), lambda i, d: (i, 0)),
                pl.BlockSpec((1, D_MODEL), lambda i, d: (0, MOD_B_G2)),
                pl.BlockSpec(memory_space=pl.ANY),
            ],
            out_specs=pl.BlockSpec((CB_TM, D_MODEL), lambda i, d: (i, 0)),
            scratch_shapes=[
                pltpu.VMEM((2, TOP_K, CB_TM, XS_W), U32),
                pltpu.VMEM((TOP_K, CB_TM, LANES), F32),
                pltpu.VMEM((8, D_MODEL), F32),
                pltpu.SemaphoreType.DMA((2,)),
            ],
        ),
        compiler_params=_params(("arbitrary",)),
        name="combine",
    )(dest, x1, gates_t, mod, ys)


def _rope_tables():
    t = jnp.arange(SEQ, dtype=I32)
    row = (t // GRID_W).astype(F32)
    col = (t % GRID_W).astype(F32)
    inv_freq = ROPE_THETA ** (-jnp.arange(0, ROPE_AXIS_DIM, 2, dtype=F32) / ROPE_AXIS_DIM)
    ang_r = inv_freq[:, None] * row[None, :]
    ang_c = inv_freq[:, None] * col[None, :]
    cos_t = jnp.concatenate([jnp.cos(ang_r), jnp.cos(ang_r), jnp.cos(ang_c), jnp.cos(ang_c)], axis=0)
    sin_t = jnp.concatenate([-jnp.sin(ang_r), jnp.sin(ang_r), -jnp.sin(ang_c), jnp.sin(ang_c)], axis=0)
    return cos_t, sin_t


def kernel(x, c, w_mod, b_mod, norm1_g, w_in, q_norm_g, k_norm_g, w_pool, pool_scale, w_out, norm2_g,
           w_router, b_router, w1, b1, w2, b2):
    assert x.shape == (1, SEQ, D_MODEL) and w_mod.shape[0] == 1
    x2 = x[0]
    cos_t, sin_t = _rope_tables()

    c_col = c.reshape(D_MODEL, 1)
    mod_a = _mod(c_col, w_mod[0], b_mod)

    pool_in, qt, k, vt, mod = _inproj(x2, mod_a, norm1_g, w_in[0].astype(BF16), cos_t, sin_t,
                                      q_norm_g.reshape(HEAD_DIM, 1), k_norm_g.reshape(HEAD_DIM, 1),
                                      c.reshape(D_MODEL // LANES, LANES), w_mod[0], b_mod)
    attn = _attention(qt, k, vt)
    pool = _pool(pool_in, w_pool[0].astype(BF16), pool_scale)
    x1 = _outproj(attn, pool, w_out[0].astype(BF16), x2, mod)

    wr = w_router[0]
    wr_hi = wr.astype(BF16)
    wr_lo = (wr - wr_hi.astype(F32)).astype(BF16)
    pad = lambda a: jnp.pad(a, ((0, 0), (0, RT_PAD - N_EXPERTS)))
    wcat = jnp.concatenate([pad(wr_hi), pad(wr_lo)], axis=1)
    logits_t = _norm2(x1, norm2_g, mod, wcat, pad(wr_hi), pad(b_router))

    dest, gates, sched, lastblk = _route(logits_t)
    dest = dest.reshape(TOP_K * SEQ)

    xs = _dispatch(dest, lastblk, x1, norm2_g, mod)
    act = _ffn1(sched, xs, w1[0], b1[0].reshape(N_EXPERTS, 1, 2 * D_FF))
    ys = _ffn2(sched, act, w2[0], b2[0].reshape(N_EXPERTS, 1, D_MODEL))
    out = _combine(dest, x1, gates.T, mod, ys)
    return out[None]
```

```python
import functools
import math

import jax
import jax.numpy as jnp
from jax import lax
from jax.experimental import pallas as pl
from jax.experimental.pallas import tpu as pltpu

F32 = jnp.float32
BF16 = jnp.bfloat16
I32 = jnp.int32

D_MODEL = 4096
SEQ = 8192
POOL_WIDTH = 2048
ATTN_WIDTH = 2048
HEAD_DIM = 128
N_HEADS = 16
N_KV_HEADS = 4
GROUP = N_HEADS // N_KV_HEADS
KV_WIDTH = N_KV_HEADS * HEAD_DIM
IN_WIDTH = POOL_WIDTH + ATTN_WIDTH + 2 * KV_WIDTH
POOL_WINDOWS = (2, 4, 8, 16)
POOL_GROUP_WIDTH = POOL_WIDTH // len(POOL_WINDOWS)
GRID_W = 64
ROPE_THETA = 10000.0
ROPE_AXIS_DIM = HEAD_DIM // 2
N_EXPERTS = 32
TOP_K = 4
D_FF = D_MODEL // 4
SWIGLU_ALPHA = 1.702
SWIGLU_LIMIT = 7.0
N_MOD = 6
EPS = 1e-6

LANES = 128
VMEM_LIMIT = 56 * 1024 * 1024
BIG_VMEM_LIMIT = 62 * 1024 * 1024

Q_SCALE = (HEAD_DIM ** -0.5) * math.log2(math.e)

MOE_TM = 256
MOE_NB = SEQ * TOP_K // MOE_TM + N_EXPERTS
MOE_ROWS = MOE_NB * MOE_TM


def _params(sem, vmem=VMEM_LIMIT):
    return pltpu.CompilerParams(dimension_semantics=sem, vmem_limit_bytes=vmem)


MOD_TN = 1024
MOD_KC = 256


def _mod_kernel(c_ref, w_ref, b_ref, o_ref):
    def body(k, acc):
        r = pl.multiple_of(k * MOD_KC, MOD_KC)
        ck = c_ref[pl.ds(r, MOD_KC), :]
        ck = ck * jax.nn.sigmoid(ck)
        p = w_ref[pl.ds(r, MOD_KC), :] * ck
        return acc + p.reshape(MOD_KC // 8, 8, MOD_TN).sum(axis=0)

    acc = lax.fori_loop(0, D_MODEL // MOD_KC, body, jnp.zeros((8, MOD_TN), F32))
    o_ref[...] = acc.sum(axis=0, keepdims=True) + b_ref[...]


MOD_A = 2
MOD_B = N_MOD - MOD_A
MOD_B_G1, MOD_B_SH2, MOD_B_SC2, MOD_B_G2 = range(MOD_B)


def _mod(c_col, w_mod, b_mod):
    n = MOD_A * D_MODEL
    return pl.pallas_call(
        _mod_kernel,
        out_shape=jax.ShapeDtypeStruct((1, n), F32),
        grid=(n // MOD_TN,),
        in_specs=[
            pl.BlockSpec((D_MODEL, 1), lambda j: (0, 0)),
            pl.BlockSpec((D_MODEL, MOD_TN), lambda j: (0, j)),
            pl.BlockSpec((1, MOD_TN), lambda j: (0, j)),
        ],
        out_specs=pl.BlockSpec((1, MOD_TN), lambda j: (0, j)),
        compiler_params=_params(("arbitrary",)),
        name="mod",
    )(c_col, w_mod, b_mod)


IP_TM = 512
IP_TN = 1024
IP_NJ = IN_WIDTH // IP_TN
IP_J_Q = POOL_WIDTH // IP_TN
IP_J_KV = IP_J_Q + ATTN_WIDTH // IP_TN
NORM_ROWS = 16
BF16_SUBLANES = 16
KA_W = 2 * HEAD_DIM
VA_H = HEAD_DIM + BF16_SUBLANES


def _prep_modulation(g_ref, sc_ref, sh_ref, a_scr, s_scr):
    a_scr[...] = jnp.broadcast_to(g_ref[...] * (1.0 + sc_ref[...]), a_scr.shape)
    s_scr[...] = jnp.broadcast_to(sh_ref[...], s_scr.shape)


def _row_rms(x_ref, rs_scr, n_rows):
    def body(r, _):
        r0 = pl.multiple_of(r * NORM_ROWS, NORM_ROWS)
        width = x_ref.shape[1]
        parts = []
        for c in range(width // LANES):
            xc = x_ref[pl.ds(r0, NORM_ROWS), c * LANES:(c + 1) * LANES]
            parts.append(xc * xc)
        while len(parts) > 1:
            parts = [parts[p] + parts[p + 1] for p in range(0, len(parts), 2)]
        rs_scr[pl.ds(r0, NORM_ROWS), :] = parts[0]
        return 0
    lax.fori_loop(0, n_rows // NORM_ROWS, body, 0)
    ms = jnp.sum(rs_scr[...], axis=-1, keepdims=True) * (1.0 / x_ref.shape[1])
    rs_scr[...] = jnp.broadcast_to(lax.rsqrt(ms + EPS), rs_scr.shape)


def _normed_tile(x_ref, rs_scr, a_scr, s_scr, r0, c):
    cs = slice(c * LANES, (c + 1) * LANES)
    return x_ref[pl.ds(r0, NORM_ROWS), cs] * rs_scr[pl.ds(r0, NORM_ROWS), :] * a_scr[:, cs] + s_scr[:, cs]


def _norm_rope_t(xt, g_col, cos_t, sin_t):
    ms = jnp.mean(xt * xt, axis=0, keepdims=True)
    y = xt * lax.rsqrt(ms + EPS) * g_col
    q = ROPE_AXIS_DIM // 2
    partner = jnp.concatenate([y[q:2 * q], y[0:q], y[3 * q:4 * q], y[2 * q:3 * q]], axis=0)
    return y * cos_t + partner * sin_t


MB_TN = 256
MB_STEPS = MOD_B * D_MODEL // MB_TN
MB_KC = 256


def _tree_sum(parts):
    while len(parts) > 1:
        parts = [parts[p] + parts[p + 1] for p in range(0, len(parts), 2)]
    return parts[0]


def _later_modulation(t, c_ref, wm_ref, bm_ref, mb_ref, cb_scr):
    @pl.when(t == 0)
    def _():
        for r in range(D_MODEL // LANES):
            cv = c_ref[r:r + 1, :]
            cb_scr[r * LANES:(r + 1) * LANES, :] = jnp.broadcast_to(cv * jax.nn.sigmoid(cv), (LANES, LANES)).T

    @pl.when(t < MB_STEPS)
    def _():
        def body(k, accs):
            r0 = pl.multiple_of(k * MB_KC, MB_KC)
            cb = cb_scr[pl.ds(r0, MB_KC), :]
            out = []
            for n in range(MB_TN // LANES):
                p = wm_ref[pl.ds(r0, MB_KC), n * LANES:(n + 1) * LANES] * cb
                out.append(accs[n] + _tree_sum([p[g * 8:(g + 1) * 8] for g in range(MB_KC // 8)]))
            return tuple(out)

        zero = jnp.zeros((8, LANES), F32)
        accs = lax.fori_loop(0, D_MODEL // MB_KC, body, (zero,) * (MB_TN // LANES))
        for n in range(MB_TN // LANES):
            cs = slice(n * LANES, (n + 1) * LANES)
            mb_ref[:, cs] = accs[n].sum(axis=0, keepdims=True) + bm_ref[:, cs]


def _inproj_kernel(x_ref, g_ref, sc_ref, sh_ref, w_ref, cos_ref, sin_ref, qg_ref, kg_ref, c_ref, wm_ref, bm_ref,
                   pool_ref, qt_ref, k_ref, vt_ref, mb_ref, h_scr, rs_scr, a_scr, s_scr, cb_scr):
    j = pl.program_id(1)
    _later_modulation(pl.program_id(0) * IP_NJ + j, c_ref, wm_ref, bm_ref, mb_ref, cb_scr)

    @pl.when(j == 0)
    def _():
        _prep_modulation(g_ref, sc_ref, sh_ref, a_scr, s_scr)
        _row_rms(x_ref, rs_scr, IP_TM)

        def body(r, _):
            r0 = pl.multiple_of(r * NORM_ROWS, NORM_ROWS)
            for c in range(D_MODEL // LANES):
                h_scr[pl.ds(r0, NORM_ROWS), c * LANES:(c + 1) * LANES] = _normed_tile(
                    x_ref, rs_scr, a_scr, s_scr, r0, c).astype(BF16)
            return 0
        lax.fori_loop(0, IP_TM // NORM_ROWS, body, 0)

    acc = jnp.dot(h_scr[...], w_ref[...], preferred_element_type=F32)

    @pl.when(j < IP_J_Q)
    def _():
        pool_ref[...] = acc.astype(BF16)

    @pl.when((j >= IP_J_Q) & (j < IP_J_KV))
    def _():
        for hh in range(IP_TN // HEAD_DIM):
            sl = slice(hh * HEAD_DIM, (hh + 1) * HEAD_DIM)
            r = _norm_rope_t(acc[:, sl].T, qg_ref[...], cos_ref[...], sin_ref[...]) * Q_SCALE
            qt_ref[sl, :] = r.astype(BF16)

    @pl.when(j == IP_J_KV)
    def _():
        lane = lax.broadcasted_iota(I32, (IP_TM, HEAD_DIM), 1)
        one_col = jnp.where(lane == 0, 1.0, 0.0).astype(BF16)
        for hh in range(N_KV_HEADS):
            sl = slice(hh * HEAD_DIM, (hh + 1) * HEAD_DIM)
            k_ref[:, hh * KA_W:hh * KA_W + HEAD_DIM] = _norm_rope_t(
                acc[:, sl].T, kg_ref[...], cos_ref[...], sin_ref[...]).T.astype(BF16)
            k_ref[:, hh * KA_W + HEAD_DIM:(hh + 1) * KA_W] = one_col
        for hh in range(N_KV_HEADS):
            sl = slice(KV_WIDTH + hh * HEAD_DIM, KV_WIDTH + (hh + 1) * HEAD_DIM)
            vt_ref[hh * VA_H:hh * VA_H + HEAD_DIM, :] = acc[:, sl].T.astype(BF16)
            vt_ref[hh * VA_H + HEAD_DIM:(hh + 1) * VA_H, :] = jnp.ones((VA_H - HEAD_DIM, IP_TM), BF16)


def _inproj(x2, mod, norm1_g, w_in_b, cos_t, sin_t, qg, kg, c_col, w_mod, b_mod):
    row = lambda n: pl.BlockSpec((1, D_MODEL), lambda i, j, n=n: (0, n))
    mb_tile = lambda i, j: jnp.minimum(i * IP_NJ + j, MB_STEPS - 1)
    mb_first = MOD_A * D_MODEL // MB_TN
    return pl.pallas_call(
        _inproj_kernel,
        out_shape=(
            jax.ShapeDtypeStruct((SEQ, POOL_WIDTH), BF16),
            jax.ShapeDtypeStruct((ATTN_WIDTH, SEQ), BF16),
            jax.ShapeDtypeStruct((SEQ, N_KV_HEADS * KA_W), BF16),
            jax.ShapeDtypeStruct((N_KV_HEADS * VA_H, SEQ), BF16),
            jax.ShapeDtypeStruct((1, MOD_B * D_MODEL), F32),
        ),
        grid=(SEQ // IP_TM, IP_NJ),
        in_specs=[
            pl.BlockSpec((IP_TM, D_MODEL), lambda i, j: (i, 0)),
            pl.BlockSpec((1, D_MODEL), lambda i, j: (0, 0)),
            row(1), row(0),
            pl.BlockSpec((D_MODEL, IP_TN), lambda i, j: (0, j)),
            pl.BlockSpec((HEAD_DIM, IP_TM), lambda i, j: (0, i)),
            pl.BlockSpec((HEAD_DIM, IP_TM), lambda i, j: (0, i)),
            pl.BlockSpec((HEAD_DIM, 1), lambda i, j: (0, 0)),
            pl.BlockSpec((HEAD_DIM, 1), lambda i, j: (0, 0)),
            pl.BlockSpec((D_MODEL // LANES, LANES), lambda i, j: (0, 0)),
            pl.BlockSpec((D_MODEL, MB_TN), lambda i, j: (0, mb_first + mb_tile(i, j))),
            pl.BlockSpec((1, MB_TN), lambda i, j: (0, mb_first + mb_tile(i, j))),
        ],
        out_specs=(
            pl.BlockSpec((IP_TM, IP_TN), lambda i, j: (i, jnp.minimum(j, IP_J_Q - 1))),
            pl.BlockSpec((IP_TN, IP_TM), lambda i, j: (jnp.clip(j - IP_J_Q, 0, IP_J_KV - IP_J_Q - 1), i)),
            pl.BlockSpec((IP_TM, N_KV_HEADS * KA_W), lambda i, j: (i, 0)),
            pl.BlockSpec((N_KV_HEADS * VA_H, IP_TM), lambda i, j: (0, i)),
            pl.BlockSpec((1, MB_TN), lambda i, j: (0, mb_tile(i, j))),
        ),
        scratch_shapes=[
            pltpu.VMEM((IP_TM, D_MODEL), BF16),
            pltpu.VMEM((IP_TM, LANES), F32),
            pltpu.VMEM((NORM_ROWS, D_MODEL), F32),
            pltpu.VMEM((NORM_ROWS, D_MODEL), F32),
            pltpu.VMEM((D_MODEL, LANES), F32),
        ],
        compiler_params=_params(("arbitrary", "arbitrary"), vmem=BIG_VMEM_LIMIT),
        name="inproj",
    )(x2, norm1_g, mod, mod, w_in_b, cos_t, sin_t, qg, kg, c_col, w_mod, b_mod)


AT_TQ = 1024
AT_TK = 8192
AT_TK_ONLINE = 512
SHIFT_LIMIT = 60.0


def _attn_kernel(qt_ref, k_ref, vt_ref, o_ref, qa_scr, p_scr, kmax_scr):
    h = pl.program_id(0)
    i = pl.program_id(1)

    @pl.when((i == 0) & (h % GROUP == 0))
    def _():
        def body(c, mx):
            c0 = pl.multiple_of(c * AT_TK, AT_TK)
            kc = k_ref[pl.ds(c0, AT_TK), :HEAD_DIM].astype(F32)
            n2 = (kc * kc).sum(axis=1, keepdims=True)
            return jnp.maximum(mx, n2.max(axis=0, keepdims=True))
        mx = lax.fori_loop(0, SEQ // AT_TK, body, jnp.zeros((1, 1), F32))
        kmax_scr[...] = jnp.broadcast_to(jnp.sqrt(mx), kmax_scr.shape)

    q = qt_ref[...].astype(F32)
    bound = jnp.sqrt((q * q).sum(axis=0, keepdims=True)) * kmax_scr[0:1, 0:1] * 1.01
    fast = jnp.max(bound) <= SHIFT_LIMIT

    @pl.when(fast)
    def _():
        qa_scr[0:HEAD_DIM, :] = qt_ref[...]
        row = lax.broadcasted_iota(I32, (KA_W - HEAD_DIM, AT_TQ), 0)
        qa_scr[HEAD_DIM:, :] = jnp.where(row == 0, -bound, 0.0).astype(BF16)

        def body(c, _):
            c0 = pl.multiple_of(c * AT_TK, AT_TK)
            s = jnp.dot(k_ref[pl.ds(c0, AT_TK), :], qa_scr[...], preferred_element_type=F32)
            p_scr[pl.ds(c0, AT_TK), :] = jnp.exp2(s).astype(BF16)
            return 0

        lax.fori_loop(0, SEQ // AT_TK, body, 0)
        o = jnp.dot(vt_ref[...], p_scr[...], preferred_element_type=F32)
        o_ref[...] = (o[:HEAD_DIM] * (1.0 / o[HEAD_DIM:HEAD_DIM + 1])).T.astype(BF16)

    @pl.when(jnp.logical_not(fast))
    def _():
        qt = qt_ref[...]

        def chunk(c, carry):
            m, l, acc = carry
            c0 = pl.multiple_of(c * AT_TK_ONLINE, AT_TK_ONLINE)
            s = jnp.dot(k_ref[pl.ds(c0, AT_TK_ONLINE), :HEAD_DIM], qt, preferred_element_type=F32)
            m_new = jnp.maximum(m, s.max(axis=0, keepdims=True))
            alpha = jnp.exp2(m - m_new)
            p = jnp.exp2(s - m_new)
            l = alpha * l + p.sum(axis=0, keepdims=True)
            pv = jnp.dot(vt_ref[:HEAD_DIM, pl.ds(c0, AT_TK_ONLINE)], p.astype(BF16), preferred_element_type=F32)
            return m_new, l, alpha * acc + pv

        init = (jnp.full((1, AT_TQ), -jnp.inf, F32), jnp.zeros((1, AT_TQ), F32),
                jnp.zeros((HEAD_DIM, AT_TQ), F32))
        _, l, acc = lax.fori_loop(0, SEQ // AT_TK_ONLINE, chunk, init)
        o_ref[...] = (acc * (1.0 / l)).T.astype(BF16)


def _attention(qt, k, vt):
    return pl.pallas_call(
        _attn_kernel,
        out_shape=jax.ShapeDtypeStruct((SEQ, ATTN_WIDTH), BF16),
        grid=(N_HEADS, SEQ // AT_TQ),
        in_specs=[
            pl.BlockSpec((HEAD_DIM, AT_TQ), lambda h, i: (h, i)),
            pl.BlockSpec((SEQ, KA_W), lambda h, i: (0, h // GROUP)),
            pl.BlockSpec((VA_H, SEQ), lambda h, i: (h // GROUP, 0)),
        ],
        out_specs=pl.BlockSpec((AT_TQ, HEAD_DIM), lambda h, i: (i, h)),
        scratch_shapes=[
            pltpu.VMEM((KA_W, AT_TQ), BF16),
            pltpu.VMEM((SEQ, AT_TQ), BF16),
            pltpu.VMEM((8, LANES), F32),
        ],
        compiler_params=_params(("arbitrary", "arbitrary")),
        name="attn",
    )(qt, k, vt)


PL_TM = 256
PL_HALO = 16


def _pool_kernel(prev_ref, main_ref, next_ref, wp_ref, scale_ref, o_ref, buf, band):
    i = pl.program_id(0)
    last = pl.num_programs(0) - 1

    @pl.when(i == 0)
    def _():
        tt = lax.broadcasted_iota(I32, (PL_TM, PL_TM + 2 * PL_HALO), 0)
        ss = lax.broadcasted_iota(I32, (PL_TM, PL_TM + 2 * PL_HALO), 1)
        off = ss - PL_HALO - tt
        for gi, w in enumerate(POOL_WINDOWS):
            band[gi] = jnp.where((off >= -(w // 2)) & (off <= w // 2 - 1), 1.0, 0.0).astype(BF16)

    buf[0:PL_HALO, :] = jnp.where(i == 0, jnp.zeros_like(prev_ref[...]), prev_ref[...])
    buf[PL_HALO:PL_HALO + PL_TM, :] = main_ref[...]
    buf[PL_HALO + PL_TM:, :] = jnp.where(i == last, jnp.zeros_like(next_ref[...]), next_ref[...])
    t = i * PL_TM + lax.broadcasted_iota(I32, (PL_TM, 1), 0)
    for gi, w in enumerate(POOL_WINDOWS):
        cols = slice(gi * POOL_GROUP_WIDTH, (gi + 1) * POOL_GROUP_WIDTH)
        win = jnp.dot(band[gi], buf[:, cols], preferred_element_type=F32)
        lo = jnp.maximum(t - w // 2, 0)
        hi = jnp.minimum(t + w // 2 - 1, SEQ - 1)
        cnt = (hi - lo + 1).astype(F32)
        dlt = win / cnt - main_ref[:, cols].astype(F32)
        y = jnp.dot(dlt.astype(BF16), wp_ref[gi], preferred_element_type=F32)
        o_ref[:, cols] = (y * scale_ref[:, cols]).astype(BF16)


def _pool(pool_in, w_pool_b, pool_scale):
    nh = PL_TM // PL_HALO
    n_halo_blocks = SEQ // PL_HALO
    return pl.pallas_call(
        _pool_kernel,
        out_shape=jax.ShapeDtypeStruct((SEQ, POOL_WIDTH), BF16),
        grid=(SEQ // PL_TM,),
        in_specs=[
            pl.BlockSpec((PL_HALO, POOL_WIDTH), lambda i: (jnp.maximum(i * nh - 1, 0), 0)),
            pl.BlockSpec((PL_TM, POOL_WIDTH), lambda i: (i, 0)),
            pl.BlockSpec((PL_HALO, POOL_WIDTH), lambda i: (jnp.minimum((i + 1) * nh, n_halo_blocks - 1), 0)),
            pl.BlockSpec((len(POOL_WINDOWS), POOL_GROUP_WIDTH, POOL_GROUP_WIDTH), lambda i: (0, 0, 0)),
            pl.BlockSpec((1, POOL_WIDTH), lambda i: (0, 0)),
        ],
        out_specs=pl.BlockSpec((PL_TM, POOL_WIDTH), lambda i: (i, 0)),
        scratch_shapes=[
            pltpu.VMEM((PL_TM + 2 * PL_HALO, POOL_WIDTH), BF16),
            pltpu.VMEM((len(POOL_WINDOWS), PL_TM, PL_TM + 2 * PL_HALO), BF16),
        ],
        compiler_params=_params(("arbitrary",)),
        name="pool",
    )(pool_in, pool_in, pool_in, w_pool_b, pool_scale)


OP_TM = 1024
OP_TN = 512


def _outproj_kernel(a_ref, p_ref, wa_ref, wp_ref, x_ref, g_ref, o_ref):
    acc = jnp.dot(a_ref[...], wa_ref[...], preferred_element_type=F32)
    acc = acc + jnp.dot(p_ref[...], wp_ref[...], preferred_element_type=F32)
    o_ref[...] = x_ref[...] + g_ref[...] * acc


def _outproj(attn, pool, w_out_b, x2, mod):
    return pl.pallas_call(
        _outproj_kernel,
        out_shape=jax.ShapeDtypeStruct((SEQ, D_MODEL), F32),
        grid=(SEQ // OP_TM, D_MODEL // OP_TN),
        in_specs=[
            pl.BlockSpec((OP_TM, ATTN_WIDTH), lambda i, j: (i, 0)),
            pl.BlockSpec((OP_TM, POOL_WIDTH), lambda i, j: (i, 0)),
            pl.BlockSpec((ATTN_WIDTH, OP_TN), lambda i, j: (0, j)),
            pl.BlockSpec((POOL_WIDTH, OP_TN), lambda i, j: (1, j)),
            pl.BlockSpec((OP_TM, OP_TN), lambda i, j: (i, j)),
            pl.BlockSpec((1, OP_TN), lambda i, j: (0, MOD_B_G1 * (D_MODEL // OP_TN) + j)),
        ],
        out_specs=pl.BlockSpec((OP_TM, OP_TN), lambda i, j: (i, j)),
        compiler_params=_params(("arbitrary", "arbitrary")),
        name="outproj",
    )(attn, pool, w_out_b, w_out_b, x2, mod)


N2_TM = 256
RT_PAD = LANES


def _norm2_kernel(x_ref, g_ref, sc_ref, sh_ref, wcat_ref, whi_ref, b_ref, lt_ref, hi_scr, lo_scr,
                  rs_scr, a_scr, s_scr):
    @pl.when(pl.program_id(0) == 0)
    def _():
        _prep_modulation(g_ref, sc_ref, sh_ref, a_scr, s_scr)

    _row_rms(x_ref, rs_scr, N2_TM)

    def body(r, _):
        r0 = pl.multiple_of(r * NORM_ROWS, NORM_ROWS)
        for c in range(D_MODEL // LANES):
            cs = slice(c * LANES, (c + 1) * LANES)
            h = _normed_tile(x_ref, rs_scr, a_scr, s_scr, r0, c)
            hi = h.astype(BF16)
            hi_scr[pl.ds(r0, NORM_ROWS), cs] = hi
            lo_scr[pl.ds(r0, NORM_ROWS), cs] = (h - hi.astype(F32)).astype(BF16)
        return 0
    lax.fori_loop(0, N2_TM // NORM_ROWS, body, 0)
    a = jnp.dot(hi_scr[...], wcat_ref[...], preferred_element_type=F32)
    b = jnp.dot(lo_scr[...], whi_ref[...], preferred_element_type=F32)
    logits = a[:, :RT_PAD] + a[:, RT_PAD:] + b + b_ref[...]
    lt_ref[...] = logits.T[:N_EXPERTS, :]


def _norm2(x1, norm2_g, mod, wcat, whi, b_pad):
    row = lambda n: pl.BlockSpec((1, D_MODEL), lambda i, n=n: (0, n))
    return pl.pallas_call(
        _norm2_kernel,
        out_shape=jax.ShapeDtypeStruct((N_EXPERTS, SEQ), F32),
        grid=(SEQ // N2_TM,),
        in_specs=[
            pl.BlockSpec((N2_TM, D_MODEL), lambda i: (i, 0)),
            pl.BlockSpec((1, D_MODEL), lambda i: (0, 0)),
            row(MOD_B_SC2), row(MOD_B_SH2),
            pl.BlockSpec((D_MODEL, 2 * RT_PAD), lambda i: (0, 0)),
            pl.BlockSpec((D_MODEL, RT_PAD), lambda i: (0, 0)),
            pl.BlockSpec((1, RT_PAD), lambda i: (0, 0)),
        ],
        out_specs=pl.BlockSpec((N_EXPERTS, N2_TM), lambda i: (0, i)),
        scratch_shapes=[
            pltpu.VMEM((N2_TM, D_MODEL), BF16),
            pltpu.VMEM((N2_TM, D_MODEL), BF16),
            pltpu.VMEM((N2_TM, LANES), F32),
            pltpu.VMEM((NORM_ROWS, D_MODEL), F32),
            pltpu.VMEM((NORM_ROWS, D_MODEL), F32),
        ],
        compiler_params=_params(("arbitrary",)),
        name="norm2",
    )(x1, norm2_g, mod, mod, wcat, whi, b_pad)


RT_CH = 1024
RT_SB = 256


SCHED_W = RT_SB
SCHED_BE, SCHED_FIRST, SCHED_SEG, SCHED_NXT, SCHED_META, SCHED_HALF = range(6)


def _route_kernel(lt_ref, dest_ref, gate_ref, sched_ref, lastblk_ref, idx_scr, rank_scr):
    e_col = lax.broadcasted_iota(I32, (N_EXPERTS, RT_CH), 0).astype(F32)
    tri = (lax.broadcasted_iota(I32, (RT_SB, RT_SB), 0) < lax.broadcasted_iota(I32, (RT_SB, RT_SB), 1)).astype(BF16)
    carry = jnp.zeros((N_EXPERTS, 1), F32)
    for c in range(SEQ // RT_CH):
        cs = slice(c * RT_CH, (c + 1) * RT_CH)
        work = lt_ref[:, cs]
        vals = []
        mask = jnp.zeros((N_EXPERTS, RT_CH), F32)
        for k in range(TOP_K):
            m = work.max(axis=0, keepdims=True)
            idx = jnp.where(work == m, e_col, float(N_EXPERTS)).min(axis=0, keepdims=True)
            sel = e_col == idx
            vals.append(m)
            idx_scr[k:k + 1, cs] = idx
            mask = jnp.where(sel, 1.0, mask)
            work = jnp.where(sel, -jnp.inf, work)
        ex = [jnp.exp(v - vals[0]) for v in vals]
        den = ex[0] + ex[1] + ex[2] + ex[3]
        for k in range(TOP_K):
            gate_ref[k:k + 1, cs] = ex[k] / den
        for b in range(RT_CH // RT_SB):
            blk = mask[:, b * RT_SB:(b + 1) * RT_SB]
            pref = jnp.dot(blk.astype(BF16), tri, preferred_element_type=F32)
            rank_scr[:, c * RT_CH + b * RT_SB:c * RT_CH + (b + 1) * RT_SB] = pref + carry
            carry = carry + blk.sum(axis=1, keepdims=True)
    nblk = jnp.floor((carry + (MOE_TM - 1)) * (1.0 / MOE_TM))
    nblk_b = jnp.broadcast_to(nblk, (N_EXPERTS, LANES))
    lower = (lax.broadcasted_iota(I32, (N_EXPERTS, N_EXPERTS), 1) < lax.broadcasted_iota(I32, (N_EXPERTS, N_EXPERTS), 0)).astype(BF16)
    start_blk = jnp.dot(lower, nblk_b.astype(BF16), preferred_element_type=F32)
    start = start_blk[:, 0:1] * float(MOE_TM)

    end_blk = start_blk[:, 0:1] + nblk
    lastblk_ref[...] = jnp.broadcast_to(jnp.where(nblk > 0, end_blk - 1.0, -1.0), (N_EXPERTS, LANES)).astype(I32)
    e_blk = lax.broadcasted_iota(I32, (N_EXPERTS, SCHED_W), 0).astype(F32)
    b_blk = lax.broadcasted_iota(I32, (N_EXPERTS, SCHED_W), 1).astype(F32)
    lane = b_blk[0:1]
    n_used = jnp.sum(nblk, axis=0, keepdims=True)
    be = jnp.minimum(jnp.sum(jnp.where(end_blk <= b_blk, 1.0, 0.0), axis=0, keepdims=True), N_EXPERTS - 1.0)
    prev = jnp.where(lane == 0, -1.0, pltpu.roll(be, 1, 1))
    first = jnp.where((lane < n_used) & (be != prev), 1.0, 0.0)
    excl = jnp.dot(jnp.broadcast_to(first, (8, SCHED_W)).astype(BF16), tri, preferred_element_type=F32)[0:1]
    seg = excl + first - 1.0
    nxt = jnp.min(jnp.where((e_blk > be) & (nblk > 0), e_blk, float(N_EXPERTS)), axis=0, keepdims=True)
    nxt = jnp.where(nxt >= N_EXPERTS, -1.0, nxt)
    n_seg = jnp.sum(first, axis=1, keepdims=True)
    meta = jnp.where(lane == 0, n_used, jnp.where(lane == 1, n_seg, 0.0))
    rem = carry - (nblk - 1.0) * float(MOE_TM)
    half_e = (nblk > 0) & (rem <= float(MOE_TM // 2))
    half = jnp.sum(jnp.where((b_blk == end_blk - 1.0) & half_e, 1.0, 0.0), axis=0, keepdims=True)
    for r, v in enumerate((be, first, seg, nxt, meta, half)):
        sched_ref[r:r + 1, :] = v.astype(I32)
    sched_ref[6:8, :] = jnp.zeros((2, SCHED_W), I32)
    for c in range(SEQ // RT_CH):
        cs = slice(c * RT_CH, (c + 1) * RT_CH)
        slot = rank_scr[:, cs] + start
        for k in range(TOP_K):
            sel = e_col == idx_scr[k:k + 1, cs]
            dest_ref[k:k + 1, cs] = jnp.where(sel, slot, 0.0).sum(axis=0, keepdims=True).astype(I32)


def _route(logits_t):
    return pl.pallas_call(
        _route_kernel,
        out_shape=(
            jax.ShapeDtypeStruct((TOP_K, SEQ), I32),
            jax.ShapeDtypeStruct((TOP_K, SEQ), F32),
            jax.ShapeDtypeStruct((8, SCHED_W), I32),
            jax.ShapeDtypeStruct((N_EXPERTS, LANES), I32),
        ),
        scratch_shapes=[pltpu.VMEM((8, SEQ), F32), pltpu.VMEM((N_EXPERTS, SEQ), F32)],
        compiler_params=pltpu.CompilerParams(vmem_limit_bytes=VMEM_LIMIT),
        name="route",
    )(logits_t)


DP_TM = 256
XS_W = D_MODEL // 2
U32 = jnp.uint32


def _dispatch_kernel(dest_ref, lastblk_ref, x_ref, g_ref, sc_ref, sh_ref, xs_hbm, pk, zero_buf, rs_scr, a_scr,
                     s_scr, zsem, sem):
    i = pl.program_id(0)
    par = i % 2

    @pl.when(i == 0)
    def _():
        _prep_modulation(g_ref, sc_ref, sh_ref, a_scr, s_scr)
        zero_buf[...] = jnp.zeros_like(zero_buf)

        def zcopy(e):
            b = jnp.maximum(lastblk_ref[e, 0], 0)
            return pltpu.make_async_copy(zero_buf, xs_hbm.at[pl.ds(pl.multiple_of(b * MOE_TM, MOE_TM), MOE_TM)], zsem)

        def zstart(e, _):
            @pl.when(lastblk_ref[e, 0] >= 0)
            def _():
                zcopy(e).start()
            return 0

        def zwait(e, _):
            @pl.when(lastblk_ref[e, 0] >= 0)
            def _():
                zcopy(e).wait()
            return 0

        lax.fori_loop(0, N_EXPERTS, zstart, 0)
        lax.fori_loop(0, N_EXPERTS, zwait, 0)

    _row_rms(x_ref, rs_scr, DP_TM)

    n_tiles = XS_W // LANES
    assert n_tiles == NORM_ROWS

    def pack_tile(r0, c):
        lo = _normed_tile(x_ref, rs_scr, a_scr, s_scr, r0, c)
        hi = _normed_tile(x_ref, rs_scr, a_scr, s_scr, r0, c + n_tiles)
        lo = lax.bitcast_convert_type(lo.astype(BF16).astype(F32), U32)
        hi = lax.bitcast_convert_type(hi.astype(BF16).astype(F32), U32)
        pk[par, pl.ds(r0, NORM_ROWS), c * LANES:(c + 1) * LANES] = (lo >> 16) | (hi & jnp.uint32(0xFFFF0000))

    def issue_token(u):
        for k in range(TOP_K):
            d = dest_ref[k * SEQ + i * DP_TM + u]
            pltpu.make_async_copy(pk.at[par, pl.ds(u, 1)], xs_hbm.at[pl.ds(d, 1)], sem.at[par]).start()

    for c in range(n_tiles):
        pack_tile(0, c)

    def group(g, _):
        r0 = pl.multiple_of(g * NORM_ROWS, NORM_ROWS)
        for u in range(NORM_ROWS):
            pack_tile(r0, u)
            issue_token(r0 - NORM_ROWS + u)
        return 0

    lax.fori_loop(1, DP_TM // NORM_ROWS, group, 0)

    def tail(u, _):
        issue_token(DP_TM - NORM_ROWS + u)
        return 0

    lax.fori_loop(0, NORM_ROWS, tail, 0)

    def drain(p):
        for _ in range(TOP_K):
            pltpu.make_async_copy(pk.at[p], xs_hbm.at[pl.ds(0, DP_TM)], sem.at[p]).wait()

    @pl.when(i > 0)
    def _():
        drain(1 - par)

    @pl.when(i == pl.num_programs(0) - 1)
    def _():
        drain(par)


def _dispatch(dest, lastblk, x1, norm2_g, mod):
    row = lambda n: pl.BlockSpec((1, D_MODEL), lambda i, d, lb, n=n: (0, n))
    return pl.pallas_call(
        _dispatch_kernel,
        out_shape=jax.ShapeDtypeStruct((MOE_ROWS, XS_W), U32),
        grid_spec=pltpu.PrefetchScalarGridSpec(
            num_scalar_prefetch=2,
            grid=(SEQ // DP_TM,),
            in_specs=[
                pl.BlockSpec((DP_TM, D_MODEL), lambda i, d, lb: (i, 0)),
                pl.BlockSpec((1, D_MODEL), lambda i, d, lb: (0, 0)),
                row(MOD_B_SC2), row(MOD_B_SH2),
            ],
            out_specs=pl.BlockSpec(memory_space=pl.ANY),
            scratch_shapes=[
                pltpu.VMEM((2, DP_TM, XS_W), U32),
                pltpu.VMEM((MOE_TM, XS_W), U32),
                pltpu.VMEM((DP_TM, LANES), F32),
                pltpu.VMEM((NORM_ROWS, D_MODEL), F32),
                pltpu.VMEM((NORM_ROWS, D_MODEL), F32),
                pltpu.SemaphoreType.DMA,
                pltpu.SemaphoreType.DMA((2,)),
            ],
        ),
        compiler_params=_params(("arbitrary",)),
        name="dispatch",
    )(dest, lastblk, x1, norm2_g, mod, mod)


F2_TN = 4096
assert F2_TN == D_MODEL


CAST_ROWS = 128
WEIGHT_DMA_PRIORITY = 1


def _blocks_used(sched_ref):
    return sched_ref[SCHED_META, 0]


def _used_block(i, sched_ref):
    return jnp.minimum(i, _blocks_used(sched_ref) - 1)


def _stream_expert_weights(j, i, nj, sched_ref, tile_copies, stg, wbuf):
    @pl.when(sched_ref[SCHED_FIRST, i] == 1)
    def _():
        seq = j * sched_ref[SCHED_META, 1] + sched_ref[SCHED_SEG, i]
        slot = seq % 2

        @pl.when(seq == 0)
        def _():
            for cp in tile_copies(sched_ref[SCHED_BE, i], j, slot):
                cp.start(priority=WEIGHT_DMA_PRIORITY)

        for cp in tile_copies(sched_ref[SCHED_BE, i], j, slot):
            cp.wait()

        nxt = sched_ref[SCHED_NXT, i]

        @pl.when(nxt >= 0)
        def _():
            for cp in tile_copies(nxt, j, 1 - slot):
                cp.start(priority=WEIGHT_DMA_PRIORITY)

        @pl.when((nxt < 0) & (j + 1 < nj))
        def _():
            for cp in tile_copies(sched_ref[SCHED_BE, 0], j + 1, 1 - slot):
                cp.start(priority=WEIGHT_DMA_PRIORITY)

        if wbuf is not None:
            def cast(r, _):
                r0 = pl.multiple_of(r * CAST_ROWS, CAST_ROWS)
                wbuf[pl.ds(r0, CAST_ROWS), :] = stg[slot, pl.ds(r0, CAST_ROWS), :].astype(BF16)
                return 0

            lax.fori_loop(0, wbuf.shape[0] // CAST_ROWS, cast, 0)


F1_NH = D_MODEL // XS_W


def _ffn1_kernel(sched_ref, x_ref, bg_ref, bl_ref, w1_hbm, o_ref, stg, wbuf, sem):
    i = pl.program_id(0)

    def slab_copies(e, h):
        return (pltpu.make_async_copy(w1_hbm.at[e, h * XS_W:(h + 1) * XS_W, :], stg.at[h], sem.at[h]),)

    def compute(weights, n_rows=MOE_TM):
        xp = x_ref[0:n_rows, :]
        x_lo = lax.bitcast_convert_type(xp << 16, F32).astype(BF16)
        x_hi = lax.bitcast_convert_type(xp & jnp.uint32(0xFFFF0000), F32).astype(BF16)
        y = jnp.dot(x_lo, weights(0), preferred_element_type=F32)
        y = y + jnp.dot(x_hi, weights(1), preferred_element_type=F32)
        glu = jnp.minimum(y[:, :D_FF] + bg_ref[...], SWIGLU_LIMIT)
        lin = jnp.clip(y[:, D_FF:] + bl_ref[...], -SWIGLU_LIMIT, SWIGLU_LIMIT)
        o_ref[0:n_rows, :] = (glu * jax.nn.sigmoid(SWIGLU_ALPHA * glu) * (lin + 1.0)).astype(BF16)

    used = i < _blocks_used(sched_ref)
    first = sched_ref[SCHED_FIRST, i] == 1

    @pl.when(used & first)
    def _():
        @pl.when(i == 0)
        def _():
            for h in range(F1_NH):
                for cp in slab_copies(sched_ref[SCHED_BE, 0], h):
                    cp.start(priority=WEIGHT_DMA_PRIORITY)

        for h in range(F1_NH):
            for cp in slab_copies(sched_ref[SCHED_BE, i], h):
                cp.wait()

        def convert(h):
            w = stg[h].astype(BF16)
            wbuf[h] = w
            return w

        compute(convert)

        nxt = sched_ref[SCHED_NXT, i]

        @pl.when(nxt >= 0)
        def _():
            for h in range(F1_NH):
                for cp in slab_copies(nxt, h):
                    cp.start(priority=WEIGHT_DMA_PRIORITY)

    half = sched_ref[SCHED_HALF, i] == 1

    @pl.when(used & jnp.logical_not(first) & jnp.logical_not(half))
    def _():
        compute(lambda h: wbuf[h])

    @pl.when(used & jnp.logical_not(first) & half)
    def _():
        compute(lambda h: wbuf[h], MOE_TM // 2)


def _ffn1(sched, xs, w1, b1_3):
    expert = lambda i, s: s[SCHED_BE, _used_block(i, s)]
    return pl.pallas_call(
        _ffn1_kernel,
        out_shape=jax.ShapeDtypeStruct((MOE_ROWS, D_FF), BF16),
        grid_spec=pltpu.PrefetchScalarGridSpec(
            num_scalar_prefetch=1,
            grid=(MOE_NB,),
            in_specs=[
                pl.BlockSpec((MOE_TM, XS_W), lambda i, s: (_used_block(i, s), 0)),
                pl.BlockSpec((None, 1, D_FF), lambda i, s: (expert(i, s), 0, 0)),
                pl.BlockSpec((None, 1, D_FF), lambda i, s: (expert(i, s), 0, 1)),
                pl.BlockSpec(memory_space=pl.ANY),
            ],
            out_specs=pl.BlockSpec((MOE_TM, D_FF), lambda i, s: (_used_block(i, s), 0)),
            scratch_shapes=[
                pltpu.VMEM((F1_NH, XS_W, 2 * D_FF), F32),
                pltpu.VMEM((F1_NH, XS_W, 2 * D_FF), BF16),
                pltpu.SemaphoreType.DMA((F1_NH,)),
            ],
        ),
        compiler_params=_params(("arbitrary",), vmem=BIG_VMEM_LIMIT),
        name="ffn1",
    )(sched, xs, b1_3, b1_3, w1)


def _ffn2_kernel(sched_ref, a_ref, b_ref, w2_hbm, o_ref, stg, sem):
    j = pl.program_id(0)
    i = pl.program_id(1)
    nj = pl.num_programs(0)

    def tile_copies(e, jj, slot):
        c0 = pl.multiple_of(jj * F2_TN, F2_TN)
        return (pltpu.make_async_copy(w2_hbm.at[e, :, pl.ds(c0, F2_TN)], stg.at[slot], sem.at[slot]),)

    @pl.when(i < _blocks_used(sched_ref))
    def _():
        _stream_expert_weights(j, i, nj, sched_ref, tile_copies, stg, None)
        slot = (j * sched_ref[SCHED_META, 1] + sched_ref[SCHED_SEG, i]) % 2

        def compute(n_rows):
            y = jnp.dot(a_ref[0:n_rows, :], stg[slot].astype(BF16), preferred_element_type=F32) + b_ref[...]
            lo = lax.bitcast_convert_type(y[:, :XS_W].astype(BF16).astype(F32), U32)
            hi = lax.bitcast_convert_type(y[:, XS_W:].astype(BF16).astype(F32), U32)
            o_ref[0:n_rows, :] = (lo >> 16) | (hi & jnp.uint32(0xFFFF0000))

        half = sched_ref[SCHED_HALF, i] == 1

        @pl.when(jnp.logical_not(half))
        def _():
            compute(MOE_TM)

        @pl.when(half)
        def _():
            compute(MOE_TM // 2)


def _ffn2(sched, act, w2, b2_3):
    return pl.pallas_call(
        _ffn2_kernel,
        out_shape=jax.ShapeDtypeStruct((MOE_ROWS, XS_W), U32),
        grid_spec=pltpu.PrefetchScalarGridSpec(
            num_scalar_prefetch=1,
            grid=(D_MODEL // F2_TN, MOE_NB),
            in_specs=[
                pl.BlockSpec((MOE_TM, D_FF), lambda j, i, s: (_used_block(i, s), 0)),
                pl.BlockSpec((None, 1, F2_TN), lambda j, i, s: (s[SCHED_BE, _used_block(i, s)], 0, j)),
                pl.BlockSpec(memory_space=pl.ANY),
            ],
            out_specs=pl.BlockSpec((MOE_TM, XS_W), lambda j, i, s: (_used_block(i, s), j)),
            scratch_shapes=[
                pltpu.VMEM((2, D_FF, F2_TN), F32),
                pltpu.SemaphoreType.DMA((2,)),
            ],
        ),
        compiler_params=_params(("arbitrary", "arbitrary")),
        name="ffn2",
    )(sched, act, b2_3, w2)


CB_TM = 128


def _combine_kernel(dest_ref, x_ref, gate_ref, g2_ref, ys_hbm, o_ref, buf, gate_scr, g2_scr, sem):
    i = pl.program_id(0)
    last = pl.num_programs(0) - 1
    par = i % 2
    n_tiles = XS_W // LANES
    copies_per_tile = 8 * TOP_K // n_tiles
    assert copies_per_tile * n_tiles == 8 * TOP_K

    def row_start(tile, p, u, k):
        d = dest_ref[k * SEQ + tile * CB_TM + u]
        pltpu.make_async_copy(ys_hbm.at[pl.ds(d, 1)], buf.at[p, k, pl.ds(u, 1)], sem.at[p]).start()

    def drain(p):
        for k in range(TOP_K):
            pltpu.make_async_copy(ys_hbm.at[pl.ds(0, CB_TM)], buf.at[p, k], sem.at[p]).wait()

    @pl.when(i == 0)
    def _():
        def body(u, _):
            for k in range(TOP_K):
                row_start(0, 0, u, k)
            return 0
        lax.fori_loop(0, CB_TM, body, 0)
        g2_scr[...] = jnp.broadcast_to(g2_ref[...], g2_scr.shape)

    drain(par)

    for k in range(TOP_K):
        gate_scr[k] = jnp.broadcast_to(gate_ref[:, k:k + 1], (CB_TM, LANES))

    nxt = jnp.minimum(i + 1, last)

    def body(r, _):
        r0 = pl.multiple_of(r * 8, 8)
        rows = pl.ds(r0, 8)
        gk = [gate_scr[k, rows, :] for k in range(TOP_K)]
        for c in range(n_tiles):
            cs_lo = slice(c * LANES, (c + 1) * LANES)
            cs_hi = slice(XS_W + c * LANES, XS_W + (c + 1) * LANES)
            y_lo = y_hi = None
            for k in range(TOP_K):
                w = buf[par, k, rows, cs_lo]
                lo = lax.bitcast_convert_type(w << 16, F32) * gk[k]
                hi = lax.bitcast_convert_type(w & jnp.uint32(0xFFFF0000), F32) * gk[k]
                y_lo = lo if y_lo is None else y_lo + lo
                y_hi = hi if y_hi is None else y_hi + hi
            o_ref[rows, cs_lo] = x_ref[rows, cs_lo] + g2_scr[:, cs_lo] * y_lo
            o_ref[rows, cs_hi] = x_ref[rows, cs_hi] + g2_scr[:, cs_hi] * y_hi
            for q in range(copies_per_tile):
                n = c * copies_per_tile + q
                row_start(nxt, 1 - par, r0 + n // TOP_K, n % TOP_K)
        return 0

    lax.fori_loop(0, CB_TM // 8, body, 0)

    @pl.when(i == last)
    def _():
        drain(1 - par)


def _combine(dest, x1, gates_t, mod, ys):
    return pl.pallas_call(
        _combine_kernel,
        out_shape=jax.ShapeDtypeStruct((SEQ, D_MODEL), F32),
        grid_spec=pltpu.PrefetchScalarGridSpec(
            num_scalar_prefetch=1,
            grid=(SEQ // CB_TM,),
            in_specs=[
                pl.BlockSpec((CB_TM, D_MODEL), lambda i, d: (i, 0)),
                pl.BlockSpec((CB_TM, TOP_K), lambda i, d: (i, 0)),
                pl.BlockSpec((1, D_MODEL), lambda i, d: (0, MOD_B_G2)),
                pl.BlockSpec(memory_space=pl.ANY),
            ],
            out_specs=pl.BlockSpec((CB_TM, D_MODEL), lambda i, d: (i, 0)),
            scratch_shapes=[
                pltpu.VMEM((2, TOP_K, CB_TM, XS_W), U32),
                pltpu.VMEM((TOP_K, CB_TM, LANES), F32),
                pltpu.VMEM((8, D_MODEL), F32),
                pltpu.SemaphoreType.DMA((2,)),
            ],
        ),
        compiler_params=_params(("arbitrary",)),
        name="combine",
    )(dest, x1, gates_t, mod, ys)


def _rope_tables():
    t = jnp.arange(SEQ, dtype=I32)
    row = (t // GRID_W).astype(F32)
    col = (t % GRID_W).astype(F32)
    inv_freq = ROPE_THETA ** (-jnp.arange(0, ROPE_AXIS_DIM, 2, dtype=F32) / ROPE_AXIS_DIM)
    ang_r = inv_freq[:, None] * row[None, :]
    ang_c = inv_freq[:, None] * col[None, :]
    cos_t = jnp.concatenate([jnp.cos(ang_r), jnp.cos(ang_r), jnp.cos(ang_c), jnp.cos(ang_c)], axis=0)
    sin_t = jnp.concatenate([-jnp.sin(ang_r), jnp.sin(ang_r), -jnp.sin(ang_c), jnp.sin(ang_c)], axis=0)
    return cos_t, sin_t


def kernel(x, c, w_mod, b_mod, norm1_g, w_in, q_norm_g, k_norm_g, w_pool, pool_scale, w_out, norm2_g,
           w_router, b_router, w1, b1, w2, b2):
    assert x.shape == (1, SEQ, D_MODEL) and w_mod.shape[0] == 1
    x2 = x[0]
    cos_t, sin_t = _rope_tables()

    c_col = c.reshape(D_MODEL, 1)
    mod_a = _mod(c_col, w_mod[0], b_mod)

    pool_in, qt, k, vt, mod = _inproj(x2, mod_a, norm1_g, w_in[0].astype(BF16), cos_t, sin_t,
                                      q_norm_g.reshape(HEAD_DIM, 1), k_norm_g.reshape(HEAD_DIM, 1),
                                      c.reshape(D_MODEL // LANES, LANES), w_mod[0], b_mod)
    attn = _attention(qt, k, vt)
    pool = _pool(pool_in, w_pool[0].astype(BF16), pool_scale)
    x1 = _outproj(attn, pool, w_out[0].astype(BF16), x2, mod)

    wr = w_router[0]
    wr_hi = wr.astype(BF16)
    wr_lo = (wr - wr_hi.astype(F32)).astype(BF16)
    pad = lambda a: jnp.pad(a, ((0, 0), (0, RT_PAD - N_EXPERTS)))
    wcat = jnp.concatenate([pad(wr_hi), pad(wr_lo)], axis=1)
    logits_t = _norm2(x1, norm2_g, mod, wcat, pad(wr_hi), pad(b_router))

    dest, gates, sched, lastblk = _route(logits_t)
    dest = dest.reshape(TOP_K * SEQ)

    xs = _dispatch(dest, lastblk, x1, norm2_g, mod)
    act = _ffn1(sched, xs, w1[0], b1[0].reshape(N_EXPERTS, 1, 2 * D_FF))
    ys = _ffn2(sched, act, w2[0], b2[0].reshape(N_EXPERTS, 1, D_MODEL))
    out = _combine(dest, x1, gates.T, mod, ys)
    return out[None]
```

```python
import functools
import math

import jax
import jax.numpy as jnp
from jax import lax
from jax.experimental import pallas as pl
from jax.experimental.pallas import tpu as pltpu

F32 = jnp.float32
BF16 = jnp.bfloat16
I32 = jnp.int32

D_MODEL = 4096
SEQ = 8192
POOL_WIDTH = 2048
ATTN_WIDTH = 2048
HEAD_DIM = 128
N_HEADS = 16
N_KV_HEADS = 4
GROUP = N_HEADS // N_KV_HEADS
KV_WIDTH = N_KV_HEADS * HEAD_DIM
IN_WIDTH = POOL_WIDTH + ATTN_WIDTH + 2 * KV_WIDTH
POOL_WINDOWS = (2, 4, 8, 16)
POOL_GROUP_WIDTH = POOL_WIDTH // len(POOL_WINDOWS)
GRID_W = 64
ROPE_THETA = 10000.0
ROPE_AXIS_DIM = HEAD_DIM // 2
N_EXPERTS = 32
TOP_K = 4
D_FF = D_MODEL // 4
SWIGLU_ALPHA = 1.702
SWIGLU_LIMIT = 7.0
N_MOD = 6
EPS = 1e-6

LANES = 128
VMEM_LIMIT = 56 * 1024 * 1024
BIG_VMEM_LIMIT = 62 * 1024 * 1024

Q_SCALE = (HEAD_DIM ** -0.5) * math.log2(math.e)

MOE_TM = 256
MOE_NB = SEQ * TOP_K // MOE_TM + N_EXPERTS
MOE_ROWS = MOE_NB * MOE_TM


def _params(sem, vmem=VMEM_LIMIT):
    return pltpu.CompilerParams(dimension_semantics=sem, vmem_limit_bytes=vmem)


MOD_TN = 1024
MOD_KC = 256


def _mod_kernel(c_ref, w_ref, b_ref, o_ref):
    def body(k, acc):
        r = pl.multiple_of(k * MOD_KC, MOD_KC)
        ck = c_ref[pl.ds(r, MOD_KC), :]
        ck = ck * jax.nn.sigmoid(ck)
        p = w_ref[pl.ds(r, MOD_KC), :] * ck
        return acc + p.reshape(MOD_KC // 8, 8, MOD_TN).sum(axis=0)

    acc = lax.fori_loop(0, D_MODEL // MOD_KC, body, jnp.zeros((8, MOD_TN), F32))
    o_ref[...] = acc.sum(axis=0, keepdims=True) + b_ref[...]


MOD_A = 2
MOD_B = N_MOD - MOD_A
MOD_B_G1, MOD_B_SH2, MOD_B_SC2, MOD_B_G2 = range(MOD_B)


def _mod(c_col, w_mod, b_mod):
    n = MOD_A * D_MODEL
    return pl.pallas_call(
        _mod_kernel,
        out_shape=jax.ShapeDtypeStruct((1, n), F32),
        grid=(n // MOD_TN,),
        in_specs=[
            pl.BlockSpec((D_MODEL, 1), lambda j: (0, 0)),
            pl.BlockSpec((D_MODEL, MOD_TN), lambda j: (0, j)),
            pl.BlockSpec((1, MOD_TN), lambda j: (0, j)),
        ],
        out_specs=pl.BlockSpec((1, MOD_TN), lambda j: (0, j)),
        compiler_params=_params(("arbitrary",)),
        name="mod",
    )(c_col, w_mod, b_mod)


IP_TM = 512
IP_TN = 1024
IP_NJ = IN_WIDTH // IP_TN
IP_J_Q = POOL_WIDTH // IP_TN
IP_J_KV = IP_J_Q + ATTN_WIDTH // IP_TN
NORM_ROWS = 16
BF16_SUBLANES = 16
KA_W = 2 * HEAD_DIM
VA_H = HEAD_DIM + BF16_SUBLANES


def _prep_modulation(g_ref, sc_ref, sh_ref, a_scr, s_scr):
    a_scr[...] = jnp.broadcast_to(g_ref[...] * (1.0 + sc_ref[...]), a_scr.shape)
    s_scr[...] = jnp.broadcast_to(sh_ref[...], s_scr.shape)


def _row_rms(x_ref, rs_scr, n_rows):
    def body(r, _):
        r0 = pl.multiple_of(r * NORM_ROWS, NORM_ROWS)
        width = x_ref.shape[1]
        parts = []
        for c in range(width // LANES):
            xc = x_ref[pl.ds(r0, NORM_ROWS), c * LANES:(c + 1) * LANES]
            parts.append(xc * xc)
        while len(parts) > 1:
            parts = [parts[p] + parts[p + 1] for p in range(0, len(parts), 2)]
        rs_scr[pl.ds(r0, NORM_ROWS), :] = parts[0]
        return 0
    lax.fori_loop(0, n_rows // NORM_ROWS, body, 0)
    ms = jnp.sum(rs_scr[...], axis=-1, keepdims=True) * (1.0 / x_ref.shape[1])
    rs_scr[...] = jnp.broadcast_to(lax.rsqrt(ms + EPS), rs_scr.shape)


def _normed_tile(x_ref, rs_scr, a_scr, s_scr, r0, c):
    cs = slice(c * LANES, (c + 1) * LANES)
    return x_ref[pl.ds(r0, NORM_ROWS), cs] * rs_scr[pl.ds(r0, NORM_ROWS), :] * a_scr[:, cs] + s_scr[:, cs]


def _norm_rope_t(xt, g_col, cos_t, sin_t):
    ms = jnp.mean(xt * xt, axis=0, keepdims=True)
    y = xt * lax.rsqrt(ms + EPS) * g_col
    q = ROPE_AXIS_DIM // 2
    partner = jnp.concatenate([y[q:2 * q], y[0:q], y[3 * q:4 * q], y[2 * q:3 * q]], axis=0)
    return y * cos_t + partner * sin_t


MB_TN = 256
MB_STEPS = MOD_B * D_MODEL // MB_TN
MB_KC = 256


def _tree_sum(parts):
    while len(parts) > 1:
        parts = [parts[p] + parts[p + 1] for p in range(0, len(parts), 2)]
    return parts[0]


def _later_modulation(t, c_ref, wm_ref, bm_ref, mb_ref, cb_scr):
    @pl.when(t == 0)
    def _():
        for r in range(D_MODEL // LANES):
            cv = c_ref[r:r + 1, :]
            cb_scr[r * LANES:(r + 1) * LANES, :] = jnp.broadcast_to(cv * jax.nn.sigmoid(cv), (LANES, LANES)).T

    @pl.when(t < MB_STEPS)
    def _():
        def body(k, accs):
            r0 = pl.multiple_of(k * MB_KC, MB_KC)
            cb = cb_scr[pl.ds(r0, MB_KC), :]
            out = []
            for n in range(MB_TN // LANES):
                p = wm_ref[pl.ds(r0, MB_KC), n * LANES:(n + 1) * LANES] * cb
                out.append(accs[n] + _tree_sum([p[g * 8:(g + 1) * 8] for g in range(MB_KC // 8)]))
            return tuple(out)

        zero = jnp.zeros((8, LANES), F32)
        accs = lax.fori_loop(0, D_MODEL // MB_KC, body, (zero,) * (MB_TN // LANES))
        for n in range(MB_TN // LANES):
            cs = slice(n * LANES, (n + 1) * LANES)
            mb_ref[:, cs] = accs[n].sum(axis=0, keepdims=True) + bm_ref[:, cs]


def _inproj_kernel(x_ref, g_ref, sc_ref, sh_ref, w_ref, cos_ref, sin_ref, qg_ref, kg_ref, c_ref, wm_ref, bm_ref,
                   pool_ref, qt_ref, k_ref, vt_ref, mb_ref, h_scr, rs_scr, a_scr, s_scr, cb_scr):
    j = pl.program_id(1)
    _later_modulation(pl.program_id(0) * IP_NJ + j, c_ref, wm_ref, bm_ref, mb_ref, cb_scr)

    @pl.when(j == 0)
    def _():
        _prep_modulation(g_ref, sc_ref, sh_ref, a_scr, s_scr)
        _row_rms(x_ref, rs_scr, IP_TM)

        def body(r, _):
            r0 = pl.multiple_of(r * NORM_ROWS, NORM_ROWS)
            for c in range(D_MODEL // LANES):
                h_scr[pl.ds(r0, NORM_ROWS), c * LANES:(c + 1) * LANES] = _normed_tile(
                    x_ref, rs_scr, a_scr, s_scr, r0, c).astype(BF16)
            return 0
        lax.fori_loop(0, IP_TM // NORM_ROWS, body, 0)

    acc = jnp.dot(h_scr[...], w_ref[...], preferred_element_type=F32)

    @pl.when(j < IP_J_Q)
    def _():
        pool_ref[...] = acc.astype(BF16)

    @pl.when((j >= IP_J_Q) & (j < IP_J_KV))
    def _():
        for hh in range(IP_TN // HEAD_DIM):
            sl = slice(hh * HEAD_DIM, (hh + 1) * HEAD_DIM)
            r = _norm_rope_t(acc[:, sl].T, qg_ref[...], cos_ref[...], sin_ref[...]) * Q_SCALE
            qt_ref[sl, :] = r.astype(BF16)

    @pl.when(j == IP_J_KV)
    def _():
        lane = lax.broadcasted_iota(I32, (IP_TM, HEAD_DIM), 1)
        one_col = jnp.where(lane == 0, 1.0, 0.0).astype(BF16)
        for hh in range(N_KV_HEADS):
            sl = slice(hh * HEAD_DIM, (hh + 1) * HEAD_DIM)
            k_ref[:, hh * KA_W:hh * KA_W + HEAD_DIM] = _norm_rope_t(
                acc[:, sl].T, kg_ref[...], cos_ref[...], sin_ref[...]).T.astype(BF16)
            k_ref[:, hh * KA_W + HEAD_DIM:(hh + 1) * KA_W] = one_col
        for hh in range(N_KV_HEADS):
            sl = slice(KV_WIDTH + hh * HEAD_DIM, KV_WIDTH + (hh + 1) * HEAD_DIM)
            vt_ref[hh * VA_H:hh * VA_H + HEAD_DIM, :] = acc[:, sl].T.astype(BF16)
            vt_ref[hh * VA_H + HEAD_DIM:(hh + 1) * VA_H, :] = jnp.ones((VA_H - HEAD_DIM, IP_TM), BF16)


def _inproj(x2, mod, norm1_g, w_in_b, cos_t, sin_t, qg, kg, c_col, w_mod, b_mod):
    row = lambda n: pl.BlockSpec((1, D_MODEL), lambda i, j, n=n: (0, n))
    mb_tile = lambda i, j: jnp.minimum(i * IP_NJ + j, MB_STEPS - 1)
    mb_first = MOD_A * D_MODEL // MB_TN
    return pl.pallas_call(
        _inproj_kernel,
        out_shape=(
            jax.ShapeDtypeStruct((SEQ, POOL_WIDTH), BF16),
            jax.ShapeDtypeStruct((ATTN_WIDTH, SEQ), BF16),
            jax.ShapeDtypeStruct((SEQ, N_KV_HEADS * KA_W), BF16),
            jax.ShapeDtypeStruct((N_KV_HEADS * VA_H, SEQ), BF16),
            jax.ShapeDtypeStruct((1, MOD_B * D_MODEL), F32),
        ),
        grid=(SEQ // IP_TM, IP_NJ),
        in_specs=[
            pl.BlockSpec((IP_TM, D_MODEL), lambda i, j: (i, 0)),
            pl.BlockSpec((1, D_MODEL), lambda i, j: (0, 0)),
            row(1), row(0),
            pl.BlockSpec((D_MODEL, IP_TN), lambda i, j: (0, j)),
            pl.BlockSpec((HEAD_DIM, IP_TM), lambda i, j: (0, i)),
            pl.BlockSpec((HEAD_DIM, IP_TM), lambda i, j: (0, i)),
            pl.BlockSpec((HEAD_DIM, 1), lambda i, j: (0, 0)),
            pl.BlockSpec((HEAD_DIM, 1), lambda i, j: (0, 0)),
            pl.BlockSpec((D_MODEL // LANES, LANES), lambda i, j: (0, 0)),
            pl.BlockSpec((D_MODEL, MB_TN), lambda i, j: (0, mb_first + mb_tile(i, j))),
            pl.BlockSpec((1, MB_TN), lambda i, j: (0, mb_first + mb_tile(i, j))),
        ],
        out_specs=(
            pl.BlockSpec((IP_TM, IP_TN), lambda i, j: (i, jnp.minimum(j, IP_J_Q - 1))),
            pl.BlockSpec((IP_TN, IP_TM), lambda i, j: (jnp.clip(j - IP_J_Q, 0, IP_J_KV - IP_J_Q - 1), i)),
            pl.BlockSpec((IP_TM, N_KV_HEADS * KA_W), lambda i, j: (i, 0)),
            pl.BlockSpec((N_KV_HEADS * VA_H, IP_TM), lambda i, j: (0, i)),
            pl.BlockSpec((1, MB_TN), lambda i, j: (0, mb_tile(i, j))),
        ),
        scratch_shapes=[
            pltpu.VMEM((IP_TM, D_MODEL), BF16),
            pltpu.VMEM((IP_TM, LANES), F32),
            pltpu.VMEM((NORM_ROWS, D_MODEL), F32),
            pltpu.VMEM((NORM_ROWS, D_MODEL), F32),
            pltpu.VMEM((D_MODEL, LANES), F32),
        ],
        compiler_params=_params(("arbitrary", "arbitrary"), vmem=BIG_VMEM_LIMIT),
        name="inproj",
    )(x2, norm1_g, mod, mod, w_in_b, cos_t, sin_t, qg, kg, c_col, w_mod, b_mod)


AT_TQ = 1024
AT_TK = 8192
AT_TK_ONLINE = 512
SHIFT_LIMIT = 60.0


def _attn_kernel(qt_ref, k_ref, vt_ref, o_ref, qa_scr, p_scr, kmax_scr):
    h = pl.program_id(0)
    i = pl.program_id(1)

    @pl.when((i == 0) & (h % GROUP == 0))
    def _():
        def body(c, mx):
            c0 = pl.multiple_of(c * AT_TK, AT_TK)
            kc = k_ref[pl.ds(c0, AT_TK), :HEAD_DIM].astype(F32)
            n2 = (kc * kc).sum(axis=1, keepdims=True)
            return jnp.maximum(mx, n2.max(axis=0, keepdims=True))
        mx = lax.fori_loop(0, SEQ // AT_TK, body, jnp.zeros((1, 1), F32))
        kmax_scr[...] = jnp.broadcast_to(jnp.sqrt(mx), kmax_scr.shape)

    q = qt_ref[...].astype(F32)
    bound = jnp.sqrt((q * q).sum(axis=0, keepdims=True)) * kmax_scr[0:1, 0:1] * 1.01
    fast = jnp.max(bound) <= SHIFT_LIMIT

    @pl.when(fast)
    def _():
        qa_scr[0:HEAD_DIM, :] = qt_ref[...]
        row = lax.broadcasted_iota(I32, (KA_W - HEAD_DIM, AT_TQ), 0)
        qa_scr[HEAD_DIM:, :] = jnp.where(row == 0, -bound, 0.0).astype(BF16)

        def body(c, _):
            c0 = pl.multiple_of(c * AT_TK, AT_TK)
            s = jnp.dot(k_ref[pl.ds(c0, AT_TK), :], qa_scr[...], preferred_element_type=F32)
            p_scr[pl.ds(c0, AT_TK), :] = jnp.exp2(s).astype(BF16)
            return 0

        lax.fori_loop(0, SEQ // AT_TK, body, 0)
        o = jnp.dot(vt_ref[...], p_scr[...], preferred_element_type=F32)
        o_ref[...] = (o[:HEAD_DIM] * (1.0 / o[HEAD_DIM:HEAD_DIM + 1])).T.astype(BF16)

    @pl.when(jnp.logical_not(fast))
    def _():
        qt = qt_ref[...]

        def chunk(c, carry):
            m, l, acc = carry
            c0 = pl.multiple_of(c * AT_TK_ONLINE, AT_TK_ONLINE)
            s = jnp.dot(k_ref[pl.ds(c0, AT_TK_ONLINE), :HEAD_DIM], qt, preferred_element_type=F32)
            m_new = jnp.maximum(m, s.max(axis=0, keepdims=True))
            alpha = jnp.exp2(m - m_new)
            p = jnp.exp2(s - m_new)
            l = alpha * l + p.sum(axis=0, keepdims=True)
            pv = jnp.dot(vt_ref[:HEAD_DIM, pl.ds(c0, AT_TK_ONLINE)], p.astype(BF16), preferred_element_type=F32)
            return m_new, l, alpha * acc + pv

        init = (jnp.full((1, AT_TQ), -jnp.inf, F32), jnp.zeros((1, AT_TQ), F32),
                jnp.zeros((HEAD_DIM, AT_TQ), F32))
        _, l, acc = lax.fori_loop(0, SEQ // AT_TK_ONLINE, chunk, init)
        o_ref[...] = (acc * (1.0 / l)).T.astype(BF16)


def _attention(qt, k, vt):
    return pl.pallas_call(
        _attn_kernel,
        out_shape=jax.ShapeDtypeStruct((SEQ, ATTN_WIDTH), BF16),
        grid=(N_HEADS, SEQ // AT_TQ),
        in_specs=[
            pl.BlockSpec((HEAD_DIM, AT_TQ), lambda h, i: (h, i)),
            pl.BlockSpec((SEQ, KA_W), lambda h, i: (0, h // GROUP)),
            pl.BlockSpec((VA_H, SEQ), lambda h, i: (h // GROUP, 0)),
        ],
        out_specs=pl.BlockSpec((AT_TQ, HEAD_DIM), lambda h, i: (i, h)),
        scratch_shapes=[
            pltpu.VMEM((KA_W, AT_TQ), BF16),
            pltpu.VMEM((SEQ, AT_TQ), BF16),
            pltpu.VMEM((8, LANES), F32),
        ],
        compiler_params=_params(("arbitrary", "arbitrary")),
        name="attn",
    )(qt, k, vt)


PL_TM = 256
PL_HALO = 16


def _pool_kernel(prev_ref, main_ref, next_ref, wp_ref, scale_ref, o_ref, buf, band):
    i = pl.program_id(0)
    last = pl.num_programs(0) - 1

    @pl.when(i == 0)
    def _():
        tt = lax.broadcasted_iota(I32, (PL_TM, PL_TM + 2 * PL_HALO), 0)
        ss = lax.broadcasted_iota(I32, (PL_TM, PL_TM + 2 * PL_HALO), 1)
        off = ss - PL_HALO - tt
        for gi, w in enumerate(POOL_WINDOWS):
            band[gi] = jnp.where((off >= -(w // 2)) & (off <= w // 2 - 1), 1.0, 0.0).astype(BF16)

    buf[0:PL_HALO, :] = jnp.where(i == 0, jnp.zeros_like(prev_ref[...]), prev_ref[...])
    buf[PL_HALO:PL_HALO + PL_TM, :] = main_ref[...]
    buf[PL_HALO + PL_TM:, :] = jnp.where(i == last, jnp.zeros_like(next_ref[...]), next_ref[...])
    t = i * PL_TM + lax.broadcasted_iota(I32, (PL_TM, 1), 0)
    for gi, w in enumerate(POOL_WINDOWS):
        cols = slice(gi * POOL_GROUP_WIDTH, (gi + 1) * POOL_GROUP_WIDTH)
        win = jnp.dot(band[gi], buf[:, cols], preferred_element_type=F32)
        lo = jnp.maximum(t - w // 2, 0)
        hi = jnp.minimum(t + w // 2 - 1, SEQ - 1)
        cnt = (hi - lo + 1).astype(F32)
        dlt = win / cnt - main_ref[:, cols].astype(F32)
        y = jnp.dot(dlt.astype(BF16), wp_ref[gi], preferred_element_type=F32)
        o_ref[:, cols] = (y * scale_ref[:, cols]).astype(BF16)


def _pool(pool_in, w_pool_b, pool_scale):
    nh = PL_TM // PL_HALO
    n_halo_blocks = SEQ // PL_HALO
    return pl.pallas_call(
        _pool_kernel,
        out_shape=jax.ShapeDtypeStruct((SEQ, POOL_WIDTH), BF16),
        grid=(SEQ // PL_TM,),
        in_specs=[
            pl.BlockSpec((PL_HALO, POOL_WIDTH), lambda i: (jnp.maximum(i * nh - 1, 0), 0)),
            pl.BlockSpec((PL_TM, POOL_WIDTH), lambda i: (i, 0)),
            pl.BlockSpec((PL_HALO, POOL_WIDTH), lambda i: (jnp.minimum((i + 1) * nh, n_halo_blocks - 1), 0)),
            pl.BlockSpec((len(POOL_WINDOWS), POOL_GROUP_WIDTH, POOL_GROUP_WIDTH), lambda i: (0, 0, 0)),
            pl.BlockSpec((1, POOL_WIDTH), lambda i: (0, 0)),
        ],
        out_specs=pl.BlockSpec((PL_TM, POOL_WIDTH), lambda i: (i, 0)),
        scratch_shapes=[
            pltpu.VMEM((PL_TM + 2 * PL_HALO, POOL_WIDTH), BF16),
            pltpu.VMEM((len(POOL_WINDOWS), PL_TM, PL_TM + 2 * PL_HALO), BF16),
        ],
        compiler_params=_params(("arbitrary",)),
        name="pool",
    )(pool_in, pool_in, pool_in, w_pool_b, pool_scale)


OP_TM = 512
OP_TN = 1024


def _outproj_kernel(a_ref, p_ref, wa_ref, wp_ref, x_ref, g_ref, o_ref):
    acc = jnp.dot(a_ref[...], wa_ref[...].astype(BF16), preferred_element_type=F32)
    acc = acc + jnp.dot(p_ref[...], wp_ref[...].astype(BF16), preferred_element_type=F32)
    o_ref[...] = x_ref[...] + g_ref[...] * acc


def _outproj(attn, pool, w_out, x2, mod):
    return pl.pallas_call(
        _outproj_kernel,
        out_shape=jax.ShapeDtypeStruct((SEQ, D_MODEL), F32),
        grid=(D_MODEL // OP_TN, SEQ // OP_TM),
        in_specs=[
            pl.BlockSpec((OP_TM, ATTN_WIDTH), lambda j, i: (i, 0)),
            pl.BlockSpec((OP_TM, POOL_WIDTH), lambda j, i: (i, 0)),
            pl.BlockSpec((ATTN_WIDTH, OP_TN), lambda j, i: (0, j)),
            pl.BlockSpec((POOL_WIDTH, OP_TN), lambda j, i: (1, j)),
            pl.BlockSpec((OP_TM, OP_TN), lambda j, i: (i, j)),
            pl.BlockSpec((1, OP_TN), lambda j, i: (0, MOD_B_G1 * (D_MODEL // OP_TN) + j)),
        ],
        out_specs=pl.BlockSpec((OP_TM, OP_TN), lambda j, i: (i, j)),
        compiler_params=_params(("arbitrary", "arbitrary")),
        name="outproj",
    )(attn, pool, w_out, w_out, x2, mod)


N2_TM = 256
RT_PAD = LANES


def _norm2_kernel(x_ref, g_ref, sc_ref, sh_ref, wcat_ref, whi_ref, b_ref, lt_ref, hi_scr, lo_scr,
                  rs_scr, a_scr, s_scr):
    @pl.when(pl.program_id(0) == 0)
    def _():
        _prep_modulation(g_ref, sc_ref, sh_ref, a_scr, s_scr)

    _row_rms(x_ref, rs_scr, N2_TM)

    def body(r, _):
        r0 = pl.multiple_of(r * NORM_ROWS, NORM_ROWS)
        for c in range(D_MODEL // LANES):
            cs = slice(c * LANES, (c + 1) * LANES)
            h = _normed_tile(x_ref, rs_scr, a_scr, s_scr, r0, c)
            hi = h.astype(BF16)
            hi_scr[pl.ds(r0, NORM_ROWS), cs] = hi
            lo_scr[pl.ds(r0, NORM_ROWS), cs] = (h - hi.astype(F32)).astype(BF16)
        return 0
    lax.fori_loop(0, N2_TM // NORM_ROWS, body, 0)
    a = jnp.dot(hi_scr[...], wcat_ref[...], preferred_element_type=F32)
    b = jnp.dot(lo_scr[...], whi_ref[...], preferred_element_type=F32)
    logits = a[:, :RT_PAD] + a[:, RT_PAD:] + b + b_ref[...]
    lt_ref[...] = logits.T[:N_EXPERTS, :]


def _norm2(x1, norm2_g, mod, wcat, whi, b_pad):
    row = lambda n: pl.BlockSpec((1, D_MODEL), lambda i, n=n: (0, n))
    return pl.pallas_call(
        _norm2_kernel,
        out_shape=jax.ShapeDtypeStruct((N_EXPERTS, SEQ), F32),
        grid=(SEQ // N2_TM,),
        in_specs=[
            pl.BlockSpec((N2_TM, D_MODEL), lambda i: (i, 0)),
            pl.BlockSpec((1, D_MODEL), lambda i: (0, 0)),
            row(MOD_B_SC2), row(MOD_B_SH2),
            pl.BlockSpec((D_MODEL, 2 * RT_PAD), lambda i: (0, 0)),
            pl.BlockSpec((D_MODEL, RT_PAD), lambda i: (0, 0)),
            pl.BlockSpec((1, RT_PAD), lambda i: (0, 0)),
        ],
        out_specs=pl.BlockSpec((N_EXPERTS, N2_TM), lambda i: (0, i)),
        scratch_shapes=[
            pltpu.VMEM((N2_TM, D_MODEL), BF16),
            pltpu.VMEM((N2_TM, D_MODEL), BF16),
            pltpu.VMEM((N2_TM, LANES), F32),
            pltpu.VMEM((NORM_ROWS, D_MODEL), F32),
            pltpu.VMEM((NORM_ROWS, D_MODEL), F32),
        ],
        compiler_params=_params(("arbitrary",)),
        name="norm2",
    )(x1, norm2_g, mod, mod, wcat, whi, b_pad)


RT_CH = 1024
RT_SB = 256


SCHED_W = RT_SB
SCHED_BE, SCHED_FIRST, SCHED_SEG, SCHED_NXT, SCHED_META, SCHED_HALF = range(6)


def _route_kernel(lt_ref, dest_ref, gate_ref, sched_ref, lastblk_ref, idx_scr, rank_scr):
    e_col = lax.broadcasted_iota(I32, (N_EXPERTS, RT_CH), 0).astype(F32)
    tri = (lax.broadcasted_iota(I32, (RT_SB, RT_SB), 0) < lax.broadcasted_iota(I32, (RT_SB, RT_SB), 1)).astype(BF16)
    carry = jnp.zeros((N_EXPERTS, 1), F32)
    for c in range(SEQ // RT_CH):
        cs = slice(c * RT_CH, (c + 1) * RT_CH)
        work = lt_ref[:, cs]
        vals = []
        mask = jnp.zeros((N_EXPERTS, RT_CH), F32)
        for k in range(TOP_K):
            m = work.max(axis=0, keepdims=True)
            idx = jnp.where(work == m, e_col, float(N_EXPERTS)).min(axis=0, keepdims=True)
            sel = e_col == idx
            vals.append(m)
            idx_scr[k:k + 1, cs] = idx
            mask = jnp.where(sel, 1.0, mask)
            work = jnp.where(sel, -jnp.inf, work)
        ex = [jnp.exp(v - vals[0]) for v in vals]
        den = ex[0] + ex[1] + ex[2] + ex[3]
        for k in range(TOP_K):
            gate_ref[k:k + 1, cs] = ex[k] / den
        for b in range(RT_CH // RT_SB):
            blk = mask[:, b * RT_SB:(b + 1) * RT_SB]
            pref = jnp.dot(blk.astype(BF16), tri, preferred_element_type=F32)
            rank_scr[:, c * RT_CH + b * RT_SB:c * RT_CH + (b + 1) * RT_SB] = pref + carry
            carry = carry + blk.sum(axis=1, keepdims=True)
    nblk = jnp.floor((carry + (MOE_TM - 1)) * (1.0 / MOE_TM))
    nblk_b = jnp.broadcast_to(nblk, (N_EXPERTS, LANES))
    lower = (lax.broadcasted_iota(I32, (N_EXPERTS, N_EXPERTS), 1) < lax.broadcasted_iota(I32, (N_EXPERTS, N_EXPERTS), 0)).astype(BF16)
    start_blk = jnp.dot(lower, nblk_b.astype(BF16), preferred_element_type=F32)
    start = start_blk[:, 0:1] * float(MOE_TM)

    end_blk = start_blk[:, 0:1] + nblk
    lastblk_ref[...] = jnp.broadcast_to(jnp.where(nblk > 0, end_blk - 1.0, -1.0), (N_EXPERTS, LANES)).astype(I32)
    e_blk = lax.broadcasted_iota(I32, (N_EXPERTS, SCHED_W), 0).astype(F32)
    b_blk = lax.broadcasted_iota(I32, (N_EXPERTS, SCHED_W), 1).astype(F32)
    lane = b_blk[0:1]
    n_used = jnp.sum(nblk, axis=0, keepdims=True)
    be = jnp.minimum(jnp.sum(jnp.where(end_blk <= b_blk, 1.0, 0.0), axis=0, keepdims=True), N_EXPERTS - 1.0)
    prev = jnp.where(lane == 0, -1.0, pltpu.roll(be, 1, 1))
    first = jnp.where((lane < n_used) & (be != prev), 1.0, 0.0)
    excl = jnp.dot(jnp.broadcast_to(first, (8, SCHED_W)).astype(BF16), tri, preferred_element_type=F32)[0:1]
    seg = excl + first - 1.0
    nxt = jnp.min(jnp.where((e_blk > be) & (nblk > 0), e_blk, float(N_EXPERTS)), axis=0, keepdims=True)
    nxt = jnp.where(nxt >= N_EXPERTS, -1.0, nxt)
    n_seg = jnp.sum(first, axis=1, keepdims=True)
    meta = jnp.where(lane == 0, n_used, jnp.where(lane == 1, n_seg, 0.0))
    rem = carry - (nblk - 1.0) * float(MOE_TM)
    half_e = (nblk > 0) & (rem <= float(MOE_TM // 2))
    half = jnp.sum(jnp.where((b_blk == end_blk - 1.0) & half_e, 1.0, 0.0), axis=0, keepdims=True)
    for r, v in enumerate((be, first, seg, nxt, meta, half)):
        sched_ref[r:r + 1, :] = v.astype(I32)
    sched_ref[6:8, :] = jnp.zeros((2, SCHED_W), I32)
    for c in range(SEQ // RT_CH):
        cs = slice(c * RT_CH, (c + 1) * RT_CH)
        slot = rank_scr[:, cs] + start
        for k in range(TOP_K):
            sel = e_col == idx_scr[k:k + 1, cs]
            dest_ref[k:k + 1, cs] = jnp.where(sel, slot, 0.0).sum(axis=0, keepdims=True).astype(I32)


def _route(logits_t):
    return pl.pallas_call(
        _route_kernel,
        out_shape=(
            jax.ShapeDtypeStruct((TOP_K, SEQ), I32),
            jax.ShapeDtypeStruct((TOP_K, SEQ), F32),
            jax.ShapeDtypeStruct((8, SCHED_W), I32),
            jax.ShapeDtypeStruct((N_EXPERTS, LANES), I32),
        ),
        scratch_shapes=[pltpu.VMEM((8, SEQ), F32), pltpu.VMEM((N_EXPERTS, SEQ), F32)],
        compiler_params=pltpu.CompilerParams(vmem_limit_bytes=VMEM_LIMIT),
        name="route",
    )(logits_t)


DP_TM = 256
XS_W = D_MODEL // 2
U32 = jnp.uint32


def _dispatch_kernel(dest_ref, lastblk_ref, x_ref, g_ref, sc_ref, sh_ref, xs_hbm, pk, zero_buf, rs_scr, a_scr,
                     s_scr, zsem, sem):
    i = pl.program_id(0)
    par = i % 2

    @pl.when(i == 0)
    def _():
        _prep_modulation(g_ref, sc_ref, sh_ref, a_scr, s_scr)
        zero_buf[...] = jnp.zeros_like(zero_buf)

        def zcopy(e):
            b = jnp.maximum(lastblk_ref[e, 0], 0)
            return pltpu.make_async_copy(zero_buf, xs_hbm.at[pl.ds(pl.multiple_of(b * MOE_TM, MOE_TM), MOE_TM)], zsem)

        def zstart(e, _):
            @pl.when(lastblk_ref[e, 0] >= 0)
            def _():
                zcopy(e).start()
            return 0

        def zwait(e, _):
            @pl.when(lastblk_ref[e, 0] >= 0)
            def _():
                zcopy(e).wait()
            return 0

        lax.fori_loop(0, N_EXPERTS, zstart, 0)
        lax.fori_loop(0, N_EXPERTS, zwait, 0)

    _row_rms(x_ref, rs_scr, DP_TM)

    n_tiles = XS_W // LANES
    assert n_tiles == NORM_ROWS

    def pack_tile(r0, c):
        lo = _normed_tile(x_ref, rs_scr, a_scr, s_scr, r0, c)
        hi = _normed_tile(x_ref, rs_scr, a_scr, s_scr, r0, c + n_tiles)
        lo = lax.bitcast_convert_type(lo.astype(BF16).astype(F32), U32)
        hi = lax.bitcast_convert_type(hi.astype(BF16).astype(F32), U32)
        pk[par, pl.ds(r0, NORM_ROWS), c * LANES:(c + 1) * LANES] = (lo >> 16) | (hi & jnp.uint32(0xFFFF0000))

    def issue_token(u):
        for k in range(TOP_K):
            d = dest_ref[k * SEQ + i * DP_TM + u]
            pltpu.make_async_copy(pk.at[par, pl.ds(u, 1)], xs_hbm.at[pl.ds(d, 1)], sem.at[par]).start()

    for c in range(n_tiles):
        pack_tile(0, c)

    def group(g, _):
        r0 = pl.multiple_of(g * NORM_ROWS, NORM_ROWS)
        for u in range(NORM_ROWS):
            pack_tile(r0, u)
            issue_token(r0 - NORM_ROWS + u)
        return 0

    lax.fori_loop(1, DP_TM // NORM_ROWS, group, 0)

    def tail(u, _):
        issue_token(DP_TM - NORM_ROWS + u)
        return 0

    lax.fori_loop(0, NORM_ROWS, tail, 0)

    def drain(p):
        for _ in range(TOP_K):
            pltpu.make_async_copy(pk.at[p], xs_hbm.at[pl.ds(0, DP_TM)], sem.at[p]).wait()

    @pl.when(i > 0)
    def _():
        drain(1 - par)

    @pl.when(i == pl.num_programs(0) - 1)
    def _():
        drain(par)


def _dispatch(dest, lastblk, x1, norm2_g, mod):
    row = lambda n: pl.BlockSpec((1, D_MODEL), lambda i, d, lb, n=n: (0, n))
    return pl.pallas_call(
        _dispatch_kernel,
        out_shape=jax.ShapeDtypeStruct((MOE_ROWS, XS_W), U32),
        grid_spec=pltpu.PrefetchScalarGridSpec(
            num_scalar_prefetch=2,
            grid=(SEQ // DP_TM,),
            in_specs=[
                pl.BlockSpec((DP_TM, D_MODEL), lambda i, d, lb: (i, 0)),
                pl.BlockSpec((1, D_MODEL), lambda i, d, lb: (0, 0)),
                row(MOD_B_SC2), row(MOD_B_SH2),
            ],
            out_specs=pl.BlockSpec(memory_space=pl.ANY),
            scratch_shapes=[
                pltpu.VMEM((2, DP_TM, XS_W), U32),
                pltpu.VMEM((MOE_TM, XS_W), U32),
                pltpu.VMEM((DP_TM, LANES), F32),
                pltpu.VMEM((NORM_ROWS, D_MODEL), F32),
                pltpu.VMEM((NORM_ROWS, D_MODEL), F32),
                pltpu.SemaphoreType.DMA,
                pltpu.SemaphoreType.DMA((2,)),
            ],
        ),
        compiler_params=_params(("arbitrary",)),
        name="dispatch",
    )(dest, lastblk, x1, norm2_g, mod, mod)


F2_TN = 4096
assert F2_TN == D_MODEL


CAST_ROWS = 128
WEIGHT_DMA_PRIORITY = 1


def _blocks_used(sched_ref):
    return sched_ref[SCHED_META, 0]


def _used_block(i, sched_ref):
    return jnp.minimum(i, _blocks_used(sched_ref) - 1)


def _stream_expert_weights(j, i, nj, sched_ref, tile_copies, stg, wbuf):
    @pl.when(sched_ref[SCHED_FIRST, i] == 1)
    def _():
        seq = j * sched_ref[SCHED_META, 1] + sched_ref[SCHED_SEG, i]
        slot = seq % 2

        @pl.when(seq == 0)
        def _():
            for cp in tile_copies(sched_ref[SCHED_BE, i], j, slot):
                cp.start(priority=WEIGHT_DMA_PRIORITY)

        for cp in tile_copies(sched_ref[SCHED_BE, i], j, slot):
            cp.wait()

        nxt = sched_ref[SCHED_NXT, i]

        @pl.when(nxt >= 0)
        def _():
            for cp in tile_copies(nxt, j, 1 - slot):
                cp.start(priority=WEIGHT_DMA_PRIORITY)

        @pl.when((nxt < 0) & (j + 1 < nj))
        def _():
            for cp in tile_copies(sched_ref[SCHED_BE, 0], j + 1, 1 - slot):
                cp.start(priority=WEIGHT_DMA_PRIORITY)

        if wbuf is not None:
            def cast(r, _):
                r0 = pl.multiple_of(r * CAST_ROWS, CAST_ROWS)
                wbuf[pl.ds(r0, CAST_ROWS), :] = stg[slot, pl.ds(r0, CAST_ROWS), :].astype(BF16)
                return 0

            lax.fori_loop(0, wbuf.shape[0] // CAST_ROWS, cast, 0)


F1_NH = D_MODEL // XS_W


def _ffn1_kernel(sched_ref, x_ref, bg_ref, bl_ref, w1_hbm, o_ref, stg, wbuf, sem):
    i = pl.program_id(0)

    def slab_copies(e, h):
        return (pltpu.make_async_copy(w1_hbm.at[e, h * XS_W:(h + 1) * XS_W, :], stg.at[h], sem.at[h]),)

    def compute(weights, n_rows=MOE_TM):
        xp = x_ref[0:n_rows, :]
        x_lo = lax.bitcast_convert_type(xp << 16, F32).astype(BF16)
        x_hi = lax.bitcast_convert_type(xp & jnp.uint32(0xFFFF0000), F32).astype(BF16)
        y = jnp.dot(x_lo, weights(0), preferred_element_type=F32)
        y = y + jnp.dot(x_hi, weights(1), preferred_element_type=F32)
        glu = jnp.minimum(y[:, :D_FF] + bg_ref[...], SWIGLU_LIMIT)
        lin = jnp.clip(y[:, D_FF:] + bl_ref[...], -SWIGLU_LIMIT, SWIGLU_LIMIT)
        o_ref[0:n_rows, :] = (glu * jax.nn.sigmoid(SWIGLU_ALPHA * glu) * (lin + 1.0)).astype(BF16)

    used = i < _blocks_used(sched_ref)
    first = sched_ref[SCHED_FIRST, i] == 1

    @pl.when(used & first)
    def _():
        @pl.when(i == 0)
        def _():
            for h in range(F1_NH):
                for cp in slab_copies(sched_ref[SCHED_BE, 0], h):
                    cp.start(priority=WEIGHT_DMA_PRIORITY)

        for h in range(F1_NH):
            for cp in slab_copies(sched_ref[SCHED_BE, i], h):
                cp.wait()

        def convert(h):
            w = stg[h].astype(BF16)
            wbuf[h] = w
            return w

        compute(convert)

        nxt = sched_ref[SCHED_NXT, i]

        @pl.when(nxt >= 0)
        def _():
            for h in range(F1_NH):
                for cp in slab_copies(nxt, h):
                    cp.start(priority=WEIGHT_DMA_PRIORITY)

    half = sched_ref[SCHED_HALF, i] == 1

    @pl.when(used & jnp.logical_not(first) & jnp.logical_not(half))
    def _():
        compute(lambda h: wbuf[h])

    @pl.when(used & jnp.logical_not(first) & half)
    def _():
        compute(lambda h: wbuf[h], MOE_TM // 2)


def _ffn1(sched, xs, w1, b1_3):
    expert = lambda i, s: s[SCHED_BE, _used_block(i, s)]
    return pl.pallas_call(
        _ffn1_kernel,
        out_shape=jax.ShapeDtypeStruct((MOE_ROWS, D_FF), BF16),
        grid_spec=pltpu.PrefetchScalarGridSpec(
            num_scalar_prefetch=1,
            grid=(MOE_NB,),
            in_specs=[
                pl.BlockSpec((MOE_TM, XS_W), lambda i, s: (_used_block(i, s), 0)),
                pl.BlockSpec((None, 1, D_FF), lambda i, s: (expert(i, s), 0, 0)),
                pl.BlockSpec((None, 1, D_FF), lambda i, s: (expert(i, s), 0, 1)),
                pl.BlockSpec(memory_space=pl.ANY),
            ],
            out_specs=pl.BlockSpec((MOE_TM, D_FF), lambda i, s: (_used_block(i, s), 0)),
            scratch_shapes=[
                pltpu.VMEM((F1_NH, XS_W, 2 * D_FF), F32),
                pltpu.VMEM((F1_NH, XS_W, 2 * D_FF), BF16),
                pltpu.SemaphoreType.DMA((F1_NH,)),
            ],
        ),
        compiler_params=_params(("arbitrary",), vmem=BIG_VMEM_LIMIT),
        name="ffn1",
    )(sched, xs, b1_3, b1_3, w1)


def _ffn2_kernel(sched_ref, a_ref, b_ref, w2_hbm, o_ref, stg, sem):
    j = pl.program_id(0)
    i = pl.program_id(1)
    nj = pl.num_programs(0)

    def tile_copies(e, jj, slot):
        c0 = pl.multiple_of(jj * F2_TN, F2_TN)
        return (pltpu.make_async_copy(w2_hbm.at[e, :, pl.ds(c0, F2_TN)], stg.at[slot], sem.at[slot]),)

    @pl.when(i < _blocks_used(sched_ref))
    def _():
        _stream_expert_weights(j, i, nj, sched_ref, tile_copies, stg, None)
        slot = (j * sched_ref[SCHED_META, 1] + sched_ref[SCHED_SEG, i]) % 2

        def compute(n_rows):
            y = jnp.dot(a_ref[0:n_rows, :], stg[slot].astype(BF16), preferred_element_type=F32) + b_ref[...]
            lo = lax.bitcast_convert_type(y[:, :XS_W].astype(BF16).astype(F32), U32)
            hi = lax.bitcast_convert_type(y[:, XS_W:].astype(BF16).astype(F32), U32)
            o_ref[0:n_rows, :] = (lo >> 16) | (hi & jnp.uint32(0xFFFF0000))

        half = sched_ref[SCHED_HALF, i] == 1

        @pl.when(jnp.logical_not(half))
        def _():
            compute(MOE_TM)

        @pl.when(half)
        def _():
            compute(MOE_TM // 2)


def _ffn2(sched, act, w2, b2_3):
    return pl.pallas_call(
        _ffn2_kernel,
        out_shape=jax.ShapeDtypeStruct((MOE_ROWS, XS_W), U32),
        grid_spec=pltpu.PrefetchScalarGridSpec(
            num_scalar_prefetch=1,
            grid=(D_MODEL // F2_TN, MOE_NB),
            in_specs=[
                pl.BlockSpec((MOE_TM, D_FF), lambda j, i, s: (_used_block(i, s), 0)),
                pl.BlockSpec((None, 1, F2_TN), lambda j, i, s: (s[SCHED_BE, _used_block(i, s)], 0, j)),
                pl.BlockSpec(memory_space=pl.ANY),
            ],
            out_specs=pl.BlockSpec((MOE_TM, XS_W), lambda j, i, s: (_used_block(i, s), j)),
            scratch_shapes=[
                pltpu.VMEM((2, D_FF, F2_TN), F32),
                pltpu.SemaphoreType.DMA((2,)),
            ],
        ),
        compiler_params=_params(("arbitrary", "arbitrary")),
        name="ffn2",
    )(sched, act, b2_3, w2)


CB_TM = 128


def _combine_kernel(dest_ref, x_ref, gate_ref, g2_ref, ys_hbm, o_ref, buf, gate_scr, g2_scr, sem):
    i = pl.program_id(0)
    last = pl.num_programs(0) - 1
    par = i % 2
    n_tiles = XS_W // LANES
    copies_per_tile = 8 * TOP_K // n_tiles
    assert copies_per_tile * n_tiles == 8 * TOP_K

    def row_start(tile, p, u, k):
        d = dest_ref[k * SEQ + tile * CB_TM + u]
        pltpu.make_async_copy(ys_hbm.at[pl.ds(d, 1)], buf.at[p, k, pl.ds(u, 1)], sem.at[p]).start()

    def drain(p):
        for k in range(TOP_K):
            pltpu.make_async_copy(ys_hbm.at[pl.ds(0, CB_TM)], buf.at[p, k], sem.at[p]).wait()

    @pl.when(i == 0)
    def _():
        def body(u, _):
            for k in range(TOP_K):
                row_start(0, 0, u, k)
            return 0
        lax.fori_loop(0, CB_TM, body, 0)
        g2_scr[...] = jnp.broadcast_to(g2_ref[...], g2_scr.shape)

    drain(par)

    for k in range(TOP_K):
        gate_scr[k] = jnp.broadcast_to(gate_ref[:, k:k + 1], (CB_TM, LANES))

    nxt = jnp.minimum(i + 1, last)

    def body(r, _):
        r0 = pl.multiple_of(r * 8, 8)
        rows = pl.ds(r0, 8)
        gk = [gate_scr[k, rows, :] for k in range(TOP_K)]
        for c in range(n_tiles):
            cs_lo = slice(c * LANES, (c + 1) * LANES)
            cs_hi = slice(XS_W + c * LANES, XS_W + (c + 1) * LANES)
            y_lo = y_hi = None
            for k in range(TOP_K):
                w = buf[par, k, rows, cs_lo]
                lo = lax.bitcast_convert_type(w << 16, F32) * gk[k]
                hi = lax.bitcast_convert_type(w & jnp.uint32(0xFFFF0000), F32) * gk[k]
                y_lo = lo if y_lo is None else y_lo + lo
                y_hi = hi if y_hi is None else y_hi + hi
            o_ref[rows, cs_lo] = x_ref[rows, cs_lo] + g2_scr[:, cs_lo] * y_lo
            o_ref[rows, cs_hi] = x_ref[rows, cs_hi] + g2_scr[:, cs_hi] * y_hi
            for q in range(copies_per_tile):
                n = c * copies_per_tile + q
                row_start(nxt, 1 - par, r0 + n // TOP_K, n % TOP_K)
        return 0

    lax.fori_loop(0, CB_TM // 8, body, 0)

    @pl.when(i == last)
    def _():
        drain(1 - par)


def _combine(dest, x1, gates_t, mod, ys):
    return pl.pallas_call(
        _combine_kernel,
        out_shape=jax.ShapeDtypeStruct((SEQ, D_MODEL), F32),
        grid_spec=pltpu.PrefetchScalarGridSpec(
            num_scalar_prefetch=1,
            grid=(SEQ // CB_TM,),
            in_specs=[
                pl.BlockSpec((CB_TM, D_MODEL), lambda i, d: (i, 0)),
                pl.BlockSpec((CB_TM, TOP_K), lambda i, d: (i, 0)),
                pl.BlockSpec((1, D_MODEL), lambda i, d: (0, MOD_B_G2)),
                pl.BlockSpec(memory_space=pl.ANY),
            ],
            out_specs=pl.BlockSpec((CB_TM, D_MODEL), lambda i, d: (i, 0)),
            scratch_shapes=[
                pltpu.VMEM((2, TOP_K, CB_TM, XS_W), U32),
                pltpu.VMEM((TOP_K, CB_TM, LANES), F32),
                pltpu.VMEM((8, D_MODEL), F32),
                pltpu.SemaphoreType.DMA((2,)),
            ],
        ),
        compiler_params=_params(("arbitrary",)),
        name="combine",
    )(dest, x1, gates_t, mod, ys)


def _rope_tables():
    t = jnp.arange(SEQ, dtype=I32)
    row = (t // GRID_W).astype(F32)
    col = (t % GRID_W).astype(F32)
    inv_freq = ROPE_THETA ** (-jnp.arange(0, ROPE_AXIS_DIM, 2, dtype=F32) / ROPE_AXIS_DIM)
    ang_r = inv_freq[:, None] * row[None, :]
    ang_c = inv_freq[:, None] * col[None, :]
    cos_t = jnp.concatenate([jnp.cos(ang_r), jnp.cos(ang_r), jnp.cos(ang_c), jnp.cos(ang_c)], axis=0)
    sin_t = jnp.concatenate([-jnp.sin(ang_r), jnp.sin(ang_r), -jnp.sin(ang_c), jnp.sin(ang_c)], axis=0)
    return cos_t, sin_t


def kernel(x, c, w_mod, b_mod, norm1_g, w_in, q_norm_g, k_norm_g, w_pool, pool_scale, w_out, norm2_g,
           w_router, b_router, w1, b1, w2, b2):
    assert x.shape == (1, SEQ, D_MODEL) and w_mod.shape[0] == 1
    x2 = x[0]
    cos_t, sin_t = _rope_tables()

    c_col = c.reshape(D_MODEL, 1)
    mod_a = _mod(c_col, w_mod[0], b_mod)

    pool_in, qt, k, vt, mod = _inproj(x2, mod_a, norm1_g, w_in[0].astype(BF16), cos_t, sin_t,
                                      q_norm_g.reshape(HEAD_DIM, 1), k_norm_g.reshape(HEAD_DIM, 1),
                                      c.reshape(D_MODEL // LANES, LANES), w_mod[0], b_mod)
    attn = _attention(qt, k, vt)
    pool = _pool(pool_in, w_pool[0].astype(BF16), pool_scale)
    x1 = _outproj(attn, pool, w_out[0], x2, mod)

    wr = w_router[0]
    wr_hi = wr.astype(BF16)
    wr_lo = (wr - wr_hi.astype(F32)).astype(BF16)
    pad = lambda a: jnp.pad(a, ((0, 0), (0, RT_PAD - N_EXPERTS)))
    wcat = jnp.concatenate([pad(wr_hi), pad(wr_lo)], axis=1)
    logits_t = _norm2(x1, norm2_g, mod, wcat, pad(wr_hi), pad(b_router))

    dest, gates, sched, lastblk = _route(logits_t)
    dest = dest.reshape(TOP_K * SEQ)

    xs = _dispatch(dest, lastblk, x1, norm2_g, mod)
    act = _ffn1(sched, xs, w1[0], b1[0].reshape(N_EXPERTS, 1, 2 * D_FF))
    ys = _ffn2(sched, act, w2[0], b2[0].reshape(N_EXPERTS, 1, D_MODEL))
    out = _combine(dest, x1, gates.T, mod, ys)
    return out[None]
```

```python
import functools
import math

import jax
import jax.numpy as jnp
from jax import lax
from jax.experimental import pallas as pl
from jax.experimental.pallas import tpu as pltpu

F32 = jnp.float32
BF16 = jnp.bfloat16
I32 = jnp.int32

D_MODEL = 4096
SEQ = 8192
POOL_WIDTH = 2048
ATTN_WIDTH = 2048
HEAD_DIM = 128
N_HEADS = 16
N_KV_HEADS = 4
GROUP = N_HEADS // N_KV_HEADS
KV_WIDTH = N_KV_HEADS * HEAD_DIM
IN_WIDTH = POOL_WIDTH + ATTN_WIDTH + 2 * KV_WIDTH
POOL_WINDOWS = (2, 4, 8, 16)
POOL_GROUP_WIDTH = POOL_WIDTH // len(POOL_WINDOWS)
GRID_W = 64
ROPE_THETA = 10000.0
ROPE_AXIS_DIM = HEAD_DIM // 2
N_EXPERTS = 32
TOP_K = 4
D_FF = D_MODEL // 4
SWIGLU_ALPHA = 1.702
SWIGLU_LIMIT = 7.0
N_MOD = 6
EPS = 1e-6

LANES = 128
VMEM_LIMIT = 56 * 1024 * 1024
BIG_VMEM_LIMIT = 62 * 1024 * 1024

Q_SCALE = (HEAD_DIM ** -0.5) * math.log2(math.e)

MOE_TM = 256
MOE_NB = SEQ * TOP_K // MOE_TM + N_EXPERTS
MOE_ROWS = MOE_NB * MOE_TM


def _params(sem, vmem=VMEM_LIMIT):
    return pltpu.CompilerParams(dimension_semantics=sem, vmem_limit_bytes=vmem)


MOD_TN = 1024
MOD_KC = 256


def _mod_kernel(c_ref, w_ref, b_ref, o_ref):
    def body(k, acc):
        r = pl.multiple_of(k * MOD_KC, MOD_KC)
        ck = c_ref[pl.ds(r, MOD_KC), :]
        ck = ck * jax.nn.sigmoid(ck)
        p = w_ref[pl.ds(r, MOD_KC), :] * ck
        return acc + p.reshape(MOD_KC // 8, 8, MOD_TN).sum(axis=0)

    acc = lax.fori_loop(0, D_MODEL // MOD_KC, body, jnp.zeros((8, MOD_TN), F32))
    o_ref[...] = acc.sum(axis=0, keepdims=True) + b_ref[...]


MOD_A = 2
MOD_B = N_MOD - MOD_A
MOD_B_G1, MOD_B_SH2, MOD_B_SC2, MOD_B_G2 = range(MOD_B)


def _mod(c_col, w_mod, b_mod):
    n = MOD_A * D_MODEL
    return pl.pallas_call(
        _mod_kernel,
        out_shape=jax.ShapeDtypeStruct((1, n), F32),
        grid=(n // MOD_TN,),
        in_specs=[
            pl.BlockSpec((D_MODEL, 1), lambda j: (0, 0)),
            pl.BlockSpec((D_MODEL, MOD_TN), lambda j: (0, j)),
            pl.BlockSpec((1, MOD_TN), lambda j: (0, j)),
        ],
        out_specs=pl.BlockSpec((1, MOD_TN), lambda j: (0, j)),
        compiler_params=_params(("arbitrary",)),
        name="mod",
    )(c_col, w_mod, b_mod)


IP_TM = 512
IP_TN = 1024
IP_NJ = IN_WIDTH // IP_TN
IP_J_Q = POOL_WIDTH // IP_TN
IP_J_KV = IP_J_Q + ATTN_WIDTH // IP_TN
NORM_ROWS = 16
BF16_SUBLANES = 16
KA_W = 2 * HEAD_DIM
VA_H = HEAD_DIM + BF16_SUBLANES


def _prep_modulation(g_ref, sc_ref, sh_ref, a_scr, s_scr):
    a_scr[...] = jnp.broadcast_to(g_ref[...] * (1.0 + sc_ref[...]), a_scr.shape)
    s_scr[...] = jnp.broadcast_to(sh_ref[...], s_scr.shape)


def _row_rms(x_ref, rs_scr, n_rows):
    def body(r, _):
        r0 = pl.multiple_of(r * NORM_ROWS, NORM_ROWS)
        width = x_ref.shape[1]
        parts = []
        for c in range(width // LANES):
            xc = x_ref[pl.ds(r0, NORM_ROWS), c * LANES:(c + 1) * LANES]
            parts.append(xc * xc)
        while len(parts) > 1:
            parts = [parts[p] + parts[p + 1] for p in range(0, len(parts), 2)]
        rs_scr[pl.ds(r0, NORM_ROWS), :] = parts[0]
        return 0
    lax.fori_loop(0, n_rows // NORM_ROWS, body, 0)
    ms = jnp.sum(rs_scr[...], axis=-1, keepdims=True) * (1.0 / x_ref.shape[1])
    rs_scr[...] = jnp.broadcast_to(lax.rsqrt(ms + EPS), rs_scr.shape)


def _normed_tile(x_ref, rs_scr, a_scr, s_scr, r0, c):
    cs = slice(c * LANES, (c + 1) * LANES)
    return x_ref[pl.ds(r0, NORM_ROWS), cs] * rs_scr[pl.ds(r0, NORM_ROWS), :] * a_scr[:, cs] + s_scr[:, cs]


def _norm_rope_t(xt, g_col, cos_t, sin_t):
    ms = jnp.mean(xt * xt, axis=0, keepdims=True)
    y = xt * lax.rsqrt(ms + EPS) * g_col
    q = ROPE_AXIS_DIM // 2
    partner = jnp.concatenate([y[q:2 * q], y[0:q], y[3 * q:4 * q], y[2 * q:3 * q]], axis=0)
    return y * cos_t + partner * sin_t


MB_TN = 256
MB_STEPS = MOD_B * D_MODEL // MB_TN
MB_KC = 256


def _tree_sum(parts):
    while len(parts) > 1:
        parts = [parts[p] + parts[p + 1] for p in range(0, len(parts), 2)]
    return parts[0]


def _later_modulation(t, c_ref, wm_ref, bm_ref, mb_ref, cb_scr):
    @pl.when(t == 0)
    def _():
        for r in range(D_MODEL // LANES):
            cv = c_ref[r:r + 1, :]
            cb_scr[r * LANES:(r + 1) * LANES, :] = jnp.broadcast_to(cv * jax.nn.sigmoid(cv), (LANES, LANES)).T

    @pl.when(t < MB_STEPS)
    def _():
        def body(k, accs):
            r0 = pl.multiple_of(k * MB_KC, MB_KC)
            cb = cb_scr[pl.ds(r0, MB_KC), :]
            out = []
            for n in range(MB_TN // LANES):
                p = wm_ref[pl.ds(r0, MB_KC), n * LANES:(n + 1) * LANES] * cb
                out.append(accs[n] + _tree_sum([p[g * 8:(g + 1) * 8] for g in range(MB_KC // 8)]))
            return tuple(out)

        zero = jnp.zeros((8, LANES), F32)
        accs = lax.fori_loop(0, D_MODEL // MB_KC, body, (zero,) * (MB_TN // LANES))
        for n in range(MB_TN // LANES):
            cs = slice(n * LANES, (n + 1) * LANES)
            mb_ref[:, cs] = accs[n].sum(axis=0, keepdims=True) + bm_ref[:, cs]


def _inproj_kernel(x_ref, g_ref, sc_ref, sh_ref, w_ref, cos_ref, sin_ref, qg_ref, kg_ref, c_ref, wm_ref, bm_ref,
                   pool_ref, qt_ref, k_ref, vt_ref, mb_ref, h_scr, rs_scr, a_scr, s_scr, cb_scr):
    j = pl.program_id(1)
    _later_modulation(pl.program_id(0) * IP_NJ + j, c_ref, wm_ref, bm_ref, mb_ref, cb_scr)

    @pl.when(j == 0)
    def _():
        _prep_modulation(g_ref, sc_ref, sh_ref, a_scr, s_scr)
        _row_rms(x_ref, rs_scr, IP_TM)

        def body(r, _):
            r0 = pl.multiple_of(r * NORM_ROWS, NORM_ROWS)
            for c in range(D_MODEL // LANES):
                h_scr[pl.ds(r0, NORM_ROWS), c * LANES:(c + 1) * LANES] = _normed_tile(
                    x_ref, rs_scr, a_scr, s_scr, r0, c).astype(BF16)
            return 0
        lax.fori_loop(0, IP_TM // NORM_ROWS, body, 0)

    acc = jnp.dot(h_scr[...], w_ref[...], preferred_element_type=F32)

    @pl.when(j < IP_J_Q)
    def _():
        pool_ref[...] = acc.astype(BF16)

    @pl.when((j >= IP_J_Q) & (j < IP_J_KV))
    def _():
        for hh in range(IP_TN // HEAD_DIM):
            sl = slice(hh * HEAD_DIM, (hh + 1) * HEAD_DIM)
            r = _norm_rope_t(acc[:, sl].T, qg_ref[...], cos_ref[...], sin_ref[...]) * Q_SCALE
            qt_ref[sl, :] = r.astype(BF16)

    @pl.when(j == IP_J_KV)
    def _():
        lane = lax.broadcasted_iota(I32, (IP_TM, HEAD_DIM), 1)
        one_col = jnp.where(lane == 0, 1.0, 0.0).astype(BF16)
        for hh in range(N_KV_HEADS):
            sl = slice(hh * HEAD_DIM, (hh + 1) * HEAD_DIM)
            k_ref[:, hh * KA_W:hh * KA_W + HEAD_DIM] = _norm_rope_t(
                acc[:, sl].T, kg_ref[...], cos_ref[...], sin_ref[...]).T.astype(BF16)
            k_ref[:, hh * KA_W + HEAD_DIM:(hh + 1) * KA_W] = one_col
        for hh in range(N_KV_HEADS):
            sl = slice(KV_WIDTH + hh * HEAD_DIM, KV_WIDTH + (hh + 1) * HEAD_DIM)
            vt_ref[hh * VA_H:hh * VA_H + HEAD_DIM, :] = acc[:, sl].T.astype(BF16)
            vt_ref[hh * VA_H + HEAD_DIM:(hh + 1) * VA_H, :] = jnp.ones((VA_H - HEAD_DIM, IP_TM), BF16)


def _inproj(x2, mod, norm1_g, w_in_b, cos_t, sin_t, qg, kg, c_col, w_mod, b_mod):
    row = lambda n: pl.BlockSpec((1, D_MODEL), lambda i, j, n=n: (0, n))
    mb_tile = lambda i, j: jnp.minimum(i * IP_NJ + j, MB_STEPS - 1)
    mb_first = MOD_A * D_MODEL // MB_TN
    return pl.pallas_call(
        _inproj_kernel,
        out_shape=(
            jax.ShapeDtypeStruct((SEQ, POOL_WIDTH), BF16),
            jax.ShapeDtypeStruct((ATTN_WIDTH, SEQ), BF16),
            jax.ShapeDtypeStruct((SEQ, N_KV_HEADS * KA_W), BF16),
            jax.ShapeDtypeStruct((N_KV_HEADS * VA_H, SEQ), BF16),
            jax.ShapeDtypeStruct((1, MOD_B * D_MODEL), F32),
        ),
        grid=(SEQ // IP_TM, IP_NJ),
        in_specs=[
            pl.BlockSpec((IP_TM, D_MODEL), lambda i, j: (i, 0)),
            pl.BlockSpec((1, D_MODEL), lambda i, j: (0, 0)),
            row(1), row(0),
            pl.BlockSpec((D_MODEL, IP_TN), lambda i, j: (0, j)),
            pl.BlockSpec((HEAD_DIM, IP_TM), lambda i, j: (0, i)),
            pl.BlockSpec((HEAD_DIM, IP_TM), lambda i, j: (0, i)),
            pl.BlockSpec((HEAD_DIM, 1), lambda i, j: (0, 0)),
            pl.BlockSpec((HEAD_DIM, 1), lambda i, j: (0, 0)),
            pl.BlockSpec((D_MODEL // LANES, LANES), lambda i, j: (0, 0)),
            pl.BlockSpec((D_MODEL, MB_TN), lambda i, j: (0, mb_first + mb_tile(i, j))),
            pl.BlockSpec((1, MB_TN), lambda i, j: (0, mb_first + mb_tile(i, j))),
        ],
        out_specs=(
            pl.BlockSpec((IP_TM, IP_TN), lambda i, j: (i, jnp.minimum(j, IP_J_Q - 1))),
            pl.BlockSpec((IP_TN, IP_TM), lambda i, j: (jnp.clip(j - IP_J_Q, 0, IP_J_KV - IP_J_Q - 1), i)),
            pl.BlockSpec((IP_TM, N_KV_HEADS * KA_W), lambda i, j: (i, 0)),
            pl.BlockSpec((N_KV_HEADS * VA_H, IP_TM), lambda i, j: (0, i)),
            pl.BlockSpec((1, MB_TN), lambda i, j: (0, mb_tile(i, j))),
        ),
        scratch_shapes=[
            pltpu.VMEM((IP_TM, D_MODEL), BF16),
            pltpu.VMEM((IP_TM, LANES), F32),
            pltpu.VMEM((NORM_ROWS, D_MODEL), F32),
            pltpu.VMEM((NORM_ROWS, D_MODEL), F32),
            pltpu.VMEM((D_MODEL, LANES), F32),
        ],
        compiler_params=_params(("arbitrary", "arbitrary"), vmem=BIG_VMEM_LIMIT),
        name="inproj",
    )(x2, norm1_g, mod, mod, w_in_b, cos_t, sin_t, qg, kg, c_col, w_mod, b_mod)


AT_TQ = 1024
AT_TK = 8192
AT_TK_ONLINE = 512
SHIFT_LIMIT = 60.0


def _attn_kernel(qt_ref, k_ref, vt_ref, o_ref, qa_scr, p_scr, kmax_scr):
    h = pl.program_id(0)
    i = pl.program_id(1)

    @pl.when((i == 0) & (h % GROUP == 0))
    def _():
        def body(c, mx):
            c0 = pl.multiple_of(c * AT_TK, AT_TK)
            kc = k_ref[pl.ds(c0, AT_TK), :HEAD_DIM].astype(F32)
            n2 = (kc * kc).sum(axis=1, keepdims=True)
            return jnp.maximum(mx, n2.max(axis=0, keepdims=True))
        mx = lax.fori_loop(0, SEQ // AT_TK, body, jnp.zeros((1, 1), F32))
        kmax_scr[...] = jnp.broadcast_to(jnp.sqrt(mx), kmax_scr.shape)

    q = qt_ref[...].astype(F32)
    bound = jnp.sqrt((q * q).sum(axis=0, keepdims=True)) * kmax_scr[0:1, 0:1] * 1.01
    fast = jnp.max(bound) <= SHIFT_LIMIT

    @pl.when(fast)
    def _():
        qa_scr[0:HEAD_DIM, :] = qt_ref[...]
        row = lax.broadcasted_iota(I32, (KA_W - HEAD_DIM, AT_TQ), 0)
        qa_scr[HEAD_DIM:, :] = jnp.where(row == 0, -bound, 0.0).astype(BF16)

        def body(c, _):
            c0 = pl.multiple_of(c * AT_TK, AT_TK)
            s = jnp.dot(k_ref[pl.ds(c0, AT_TK), :], qa_scr[...], preferred_element_type=F32)
            p_scr[pl.ds(c0, AT_TK), :] = jnp.exp2(s).astype(BF16)
            return 0

        lax.fori_loop(0, SEQ // AT_TK, body, 0)
        o = jnp.dot(vt_ref[...], p_scr[...], preferred_element_type=F32)
        o_ref[...] = (o[:HEAD_DIM] * (1.0 / o[HEAD_DIM:HEAD_DIM + 1])).T.astype(BF16)

    @pl.when(jnp.logical_not(fast))
    def _():
        qt = qt_ref[...]

        def chunk(c, carry):
            m, l, acc = carry
            c0 = pl.multiple_of(c * AT_TK_ONLINE, AT_TK_ONLINE)
            s = jnp.dot(k_ref[pl.ds(c0, AT_TK_ONLINE), :HEAD_DIM], qt, preferred_element_type=F32)
            m_new = jnp.maximum(m, s.max(axis=0, keepdims=True))
            alpha = jnp.exp2(m - m_new)
            p = jnp.exp2(s - m_new)
            l = alpha * l + p.sum(axis=0, keepdims=True)
            pv = jnp.dot(vt_ref[:HEAD_DIM, pl.ds(c0, AT_TK_ONLINE)], p.astype(BF16), preferred_element_type=F32)
            return m_new, l, alpha * acc + pv

        init = (jnp.full((1, AT_TQ), -jnp.inf, F32), jnp.zeros((1, AT_TQ), F32),
                jnp.zeros((HEAD_DIM, AT_TQ), F32))
        _, l, acc = lax.fori_loop(0, SEQ // AT_TK_ONLINE, chunk, init)
        o_ref[...] = (acc * (1.0 / l)).T.astype(BF16)


def _attention(qt, k, vt):
    return pl.pallas_call(
        _attn_kernel,
        out_shape=jax.ShapeDtypeStruct((SEQ, ATTN_WIDTH), BF16),
        grid=(N_HEADS, SEQ // AT_TQ),
        in_specs=[
            pl.BlockSpec((HEAD_DIM, AT_TQ), lambda h, i: (h, i)),
            pl.BlockSpec((SEQ, KA_W), lambda h, i: (0, h // GROUP)),
            pl.BlockSpec((VA_H, SEQ), lambda h, i: (h // GROUP, 0)),
        ],
        out_specs=pl.BlockSpec((AT_TQ, HEAD_DIM), lambda h, i: (i, h)),
        scratch_shapes=[
            pltpu.VMEM((KA_W, AT_TQ), BF16),
            pltpu.VMEM((SEQ, AT_TQ), BF16),
            pltpu.VMEM((8, LANES), F32),
        ],
        compiler_params=_params(("arbitrary", "arbitrary")),
        name="attn",
    )(qt, k, vt)


PL_TM = 256
PL_HALO = 16


def _pool_kernel(prev_ref, main_ref, next_ref, wp_ref, scale_ref, o_ref, buf, band):
    i = pl.program_id(0)
    last = pl.num_programs(0) - 1

    @pl.when(i == 0)
    def _():
        tt = lax.broadcasted_iota(I32, (PL_TM, PL_TM + 2 * PL_HALO), 0)
        ss = lax.broadcasted_iota(I32, (PL_TM, PL_TM + 2 * PL_HALO), 1)
        off = ss - PL_HALO - tt
        for gi, w in enumerate(POOL_WINDOWS):
            band[gi] = jnp.where((off >= -(w // 2)) & (off <= w // 2 - 1), 1.0, 0.0).astype(BF16)

    buf[0:PL_HALO, :] = jnp.where(i == 0, jnp.zeros_like(prev_ref[...]), prev_ref[...])
    buf[PL_HALO:PL_HALO + PL_TM, :] = main_ref[...]
    buf[PL_HALO + PL_TM:, :] = jnp.where(i == last, jnp.zeros_like(next_ref[...]), next_ref[...])
    t = i * PL_TM + lax.broadcasted_iota(I32, (PL_TM, 1), 0)
    for gi, w in enumerate(POOL_WINDOWS):
        cols = slice(gi * POOL_GROUP_WIDTH, (gi + 1) * POOL_GROUP_WIDTH)
        win = jnp.dot(band[gi], buf[:, cols], preferred_element_type=F32)
        lo = jnp.maximum(t - w // 2, 0)
        hi = jnp.minimum(t + w // 2 - 1, SEQ - 1)
        cnt = (hi - lo + 1).astype(F32)
        dlt = win / cnt - main_ref[:, cols].astype(F32)
        y = jnp.dot(dlt.astype(BF16), wp_ref[gi], preferred_element_type=F32)
        o_ref[:, cols] = (y * scale_ref[:, cols]).astype(BF16)


def _pool(pool_in, w_pool_b, pool_scale):
    nh = PL_TM // PL_HALO
    n_halo_blocks = SEQ // PL_HALO
    return pl.pallas_call(
        _pool_kernel,
        out_shape=jax.ShapeDtypeStruct((SEQ, POOL_WIDTH), BF16),
        grid=(SEQ // PL_TM,),
        in_specs=[
            pl.BlockSpec((PL_HALO, POOL_WIDTH), lambda i: (jnp.maximum(i * nh - 1, 0), 0)),
            pl.BlockSpec((PL_TM, POOL_WIDTH), lambda i: (i, 0)),
            pl.BlockSpec((PL_HALO, POOL_WIDTH), lambda i: (jnp.minimum((i + 1) * nh, n_halo_blocks - 1), 0)),
            pl.BlockSpec((len(POOL_WINDOWS), POOL_GROUP_WIDTH, POOL_GROUP_WIDTH), lambda i: (0, 0, 0)),
            pl.BlockSpec((1, POOL_WIDTH), lambda i: (0, 0)),
        ],
        out_specs=pl.BlockSpec((PL_TM, POOL_WIDTH), lambda i: (i, 0)),
        scratch_shapes=[
            pltpu.VMEM((PL_TM + 2 * PL_HALO, POOL_WIDTH), BF16),
            pltpu.VMEM((len(POOL_WINDOWS), PL_TM, PL_TM + 2 * PL_HALO), BF16),
        ],
        compiler_params=_params(("arbitrary",)),
        name="pool",
    )(pool_in, pool_in, pool_in, w_pool_b, pool_scale)


OP_TM = 512
OP_TN = 1024


def _outproj_kernel(a_ref, p_ref, wa_ref, wp_ref, x_ref, g_ref, o_ref):
    acc = jnp.dot(a_ref[...], wa_ref[...].astype(BF16), preferred_element_type=F32)
    acc = acc + jnp.dot(p_ref[...], wp_ref[...].astype(BF16), preferred_element_type=F32)
    o_ref[...] = x_ref[...] + g_ref[...] * acc


def _outproj(attn, pool, w_out, x2, mod):
    return pl.pallas_call(
        _outproj_kernel,
        out_shape=jax.ShapeDtypeStruct((SEQ, D_MODEL), F32),
        grid=(D_MODEL // OP_TN, SEQ // OP_TM),
        in_specs=[
            pl.BlockSpec((OP_TM, ATTN_WIDTH), lambda j, i: (i, 0)),
            pl.BlockSpec((OP_TM, POOL_WIDTH), lambda j, i: (i, 0)),
            pl.BlockSpec((ATTN_WIDTH, OP_TN), lambda j, i: (0, j)),
            pl.BlockSpec((POOL_WIDTH, OP_TN), lambda j, i: (1, j)),
            pl.BlockSpec((OP_TM, OP_TN), lambda j, i: (i, j)),
            pl.BlockSpec((1, OP_TN), lambda j, i: (0, MOD_B_G1 * (D_MODEL // OP_TN) + j)),
        ],
        out_specs=pl.BlockSpec((OP_TM, OP_TN), lambda j, i: (i, j)),
        compiler_params=_params(("arbitrary", "arbitrary")),
        name="outproj",
    )(attn, pool, w_out, w_out, x2, mod)


N2_TM = 512
RT_PAD = LANES


def _norm2_kernel(x_ref, g_ref, sc_ref, sh_ref, wcat_ref, whi_ref, b_ref, lt_ref, hi_scr, lo_scr,
                  rs_scr, a_scr, s_scr):
    @pl.when(pl.program_id(0) == 0)
    def _():
        _prep_modulation(g_ref, sc_ref, sh_ref, a_scr, s_scr)

    _row_rms(x_ref, rs_scr, N2_TM)

    def body(r, _):
        r0 = pl.multiple_of(r * NORM_ROWS, NORM_ROWS)
        for c in range(D_MODEL // LANES):
            cs = slice(c * LANES, (c + 1) * LANES)
            h = _normed_tile(x_ref, rs_scr, a_scr, s_scr, r0, c)
            hi = h.astype(BF16)
            hi_scr[pl.ds(r0, NORM_ROWS), cs] = hi
            lo_scr[pl.ds(r0, NORM_ROWS), cs] = (h - hi.astype(F32)).astype(BF16)
        return 0
    lax.fori_loop(0, N2_TM // NORM_ROWS, body, 0)
    a = jnp.dot(hi_scr[...], wcat_ref[...], preferred_element_type=F32)
    b = jnp.dot(lo_scr[...], whi_ref[...], preferred_element_type=F32)
    logits = a[:, :RT_PAD] + a[:, RT_PAD:] + b + b_ref[...]
    lt_ref[...] = logits.T[:N_EXPERTS, :]


def _norm2(x1, norm2_g, mod, wcat, whi, b_pad):
    row = lambda n: pl.BlockSpec((1, D_MODEL), lambda i, n=n: (0, n))
    return pl.pallas_call(
        _norm2_kernel,
        out_shape=jax.ShapeDtypeStruct((N_EXPERTS, SEQ), F32),
        grid=(SEQ // N2_TM,),
        in_specs=[
            pl.BlockSpec((N2_TM, D_MODEL), lambda i: (i, 0)),
            pl.BlockSpec((1, D_MODEL), lambda i: (0, 0)),
            row(MOD_B_SC2), row(MOD_B_SH2),
            pl.BlockSpec((D_MODEL, 2 * RT_PAD), lambda i: (0, 0)),
            pl.BlockSpec((D_MODEL, RT_PAD), lambda i: (0, 0)),
            pl.BlockSpec((1, RT_PAD), lambda i: (0, 0)),
        ],
        out_specs=pl.BlockSpec((N_EXPERTS, N2_TM), lambda i: (0, i)),
        scratch_shapes=[
            pltpu.VMEM((N2_TM, D_MODEL), BF16),
            pltpu.VMEM((N2_TM, D_MODEL), BF16),
            pltpu.VMEM((N2_TM, LANES), F32),
            pltpu.VMEM((NORM_ROWS, D_MODEL), F32),
            pltpu.VMEM((NORM_ROWS, D_MODEL), F32),
        ],
        compiler_params=_params(("arbitrary",)),
        name="norm2",
    )(x1, norm2_g, mod, mod, wcat, whi, b_pad)


RT_CH = 1024
RT_SB = 256


SCHED_W = RT_SB
SCHED_BE, SCHED_FIRST, SCHED_SEG, SCHED_NXT, SCHED_META, SCHED_HALF = range(6)


def _route_kernel(lt_ref, dest_ref, gate_ref, sched_ref, lastblk_ref, idx_scr, rank_scr):
    e_col = lax.broadcasted_iota(I32, (N_EXPERTS, RT_CH), 0).astype(F32)
    tri = (lax.broadcasted_iota(I32, (RT_SB, RT_SB), 0) < lax.broadcasted_iota(I32, (RT_SB, RT_SB), 1)).astype(BF16)
    carry = jnp.zeros((N_EXPERTS, 1), F32)
    for c in range(SEQ // RT_CH):
        cs = slice(c * RT_CH, (c + 1) * RT_CH)
        work = lt_ref[:, cs]
        vals = []
        mask = jnp.zeros((N_EXPERTS, RT_CH), F32)
        for k in range(TOP_K):
            m = work.max(axis=0, keepdims=True)
            idx = jnp.where(work == m, e_col, float(N_EXPERTS)).min(axis=0, keepdims=True)
            sel = e_col == idx
            vals.append(m)
            idx_scr[k:k + 1, cs] = idx
            mask = jnp.where(sel, 1.0, mask)
            work = jnp.where(sel, -jnp.inf, work)
        ex = [jnp.exp(v - vals[0]) for v in vals]
        den = ex[0] + ex[1] + ex[2] + ex[3]
        for k in range(TOP_K):
            gate_ref[k:k + 1, cs] = ex[k] / den
        for b in range(RT_CH // RT_SB):
            blk = mask[:, b * RT_SB:(b + 1) * RT_SB]
            pref = jnp.dot(blk.astype(BF16), tri, preferred_element_type=F32)
            rank_scr[:, c * RT_CH + b * RT_SB:c * RT_CH + (b + 1) * RT_SB] = pref + carry
            carry = carry + blk.sum(axis=1, keepdims=True)
    nblk = jnp.floor((carry + (MOE_TM - 1)) * (1.0 / MOE_TM))
    nblk_b = jnp.broadcast_to(nblk, (N_EXPERTS, LANES))
    lower = (lax.broadcasted_iota(I32, (N_EXPERTS, N_EXPERTS), 1) < lax.broadcasted_iota(I32, (N_EXPERTS, N_EXPERTS), 0)).astype(BF16)
    start_blk = jnp.dot(lower, nblk_b.astype(BF16), preferred_element_type=F32)
    start = start_blk[:, 0:1] * float(MOE_TM)

    end_blk = start_blk[:, 0:1] + nblk
    lastblk_ref[...] = jnp.broadcast_to(jnp.where(nblk > 0, end_blk - 1.0, -1.0), (N_EXPERTS, LANES)).astype(I32)
    e_blk = lax.broadcasted_iota(I32, (N_EXPERTS, SCHED_W), 0).astype(F32)
    b_blk = lax.broadcasted_iota(I32, (N_EXPERTS, SCHED_W), 1).astype(F32)
    lane = b_blk[0:1]
    n_used = jnp.sum(nblk, axis=0, keepdims=True)
    be = jnp.minimum(jnp.sum(jnp.where(end_blk <= b_blk, 1.0, 0.0), axis=0, keepdims=True), N_EXPERTS - 1.0)
    prev = jnp.where(lane == 0, -1.0, pltpu.roll(be, 1, 1))
    first = jnp.where((lane < n_used) & (be != prev), 1.0, 0.0)
    excl = jnp.dot(jnp.broadcast_to(first, (8, SCHED_W)).astype(BF16), tri, preferred_element_type=F32)[0:1]
    seg = excl + first - 1.0
    nxt = jnp.min(jnp.where((e_blk > be) & (nblk > 0), e_blk, float(N_EXPERTS)), axis=0, keepdims=True)
    nxt = jnp.where(nxt >= N_EXPERTS, -1.0, nxt)
    n_seg = jnp.sum(first, axis=1, keepdims=True)
    meta = jnp.where(lane == 0, n_used, jnp.where(lane == 1, n_seg, 0.0))
    rem = carry - (nblk - 1.0) * float(MOE_TM)
    half_e = (nblk > 0) & (rem <= float(MOE_TM // 2))
    half = jnp.sum(jnp.where((b_blk == end_blk - 1.0) & half_e, 1.0, 0.0), axis=0, keepdims=True)
    for r, v in enumerate((be, first, seg, nxt, meta, half)):
        sched_ref[r:r + 1, :] = v.astype(I32)
    sched_ref[6:8, :] = jnp.zeros((2, SCHED_W), I32)
    for c in range(SEQ // RT_CH):
        cs = slice(c * RT_CH, (c + 1) * RT_CH)
        slot = rank_scr[:, cs] + start
        for k in range(TOP_K):
            sel = e_col == idx_scr[k:k + 1, cs]
            dest_ref[k:k + 1, cs] = jnp.where(sel, slot, 0.0).sum(axis=0, keepdims=True).astype(I32)


def _route(logits_t):
    return pl.pallas_call(
        _route_kernel,
        out_shape=(
            jax.ShapeDtypeStruct((TOP_K, SEQ), I32),
            jax.ShapeDtypeStruct((TOP_K, SEQ), F32),
            jax.ShapeDtypeStruct((8, SCHED_W), I32),
            jax.ShapeDtypeStruct((N_EXPERTS, LANES), I32),
        ),
        scratch_shapes=[pltpu.VMEM((8, SEQ), F32), pltpu.VMEM((N_EXPERTS, SEQ), F32)],
        compiler_params=pltpu.CompilerParams(vmem_limit_bytes=VMEM_LIMIT),
        name="route",
    )(logits_t)


DP_TM = 512
XS_W = D_MODEL // 2
U32 = jnp.uint32


def _dispatch_kernel(dest_ref, lastblk_ref, x_ref, g_ref, sc_ref, sh_ref, xs_hbm, pk, zero_buf, rs_scr, a_scr,
                     s_scr, zsem, sem):
    i = pl.program_id(0)
    par = i % 2

    @pl.when(i == 0)
    def _():
        _prep_modulation(g_ref, sc_ref, sh_ref, a_scr, s_scr)
        zero_buf[...] = jnp.zeros_like(zero_buf)

        def zcopy(e):
            b = jnp.maximum(lastblk_ref[e, 0], 0)
            return pltpu.make_async_copy(zero_buf, xs_hbm.at[pl.ds(pl.multiple_of(b * MOE_TM, MOE_TM), MOE_TM)], zsem)

        def zstart(e, _):
            @pl.when(lastblk_ref[e, 0] >= 0)
            def _():
                zcopy(e).start()
            return 0

        def zwait(e, _):
            @pl.when(lastblk_ref[e, 0] >= 0)
            def _():
                zcopy(e).wait()
            return 0

        lax.fori_loop(0, N_EXPERTS, zstart, 0)
        lax.fori_loop(0, N_EXPERTS, zwait, 0)

    _row_rms(x_ref, rs_scr, DP_TM)

    n_tiles = XS_W // LANES
    assert n_tiles == NORM_ROWS

    def pack_tile(r0, c):
        lo = _normed_tile(x_ref, rs_scr, a_scr, s_scr, r0, c)
        hi = _normed_tile(x_ref, rs_scr, a_scr, s_scr, r0, c + n_tiles)
        lo = lax.bitcast_convert_type(lo.astype(BF16).astype(F32), U32)
        hi = lax.bitcast_convert_type(hi.astype(BF16).astype(F32), U32)
        pk[par, pl.ds(r0, NORM_ROWS), c * LANES:(c + 1) * LANES] = (lo >> 16) | (hi & jnp.uint32(0xFFFF0000))

    def issue_token(u):
        for k in range(TOP_K):
            d = dest_ref[k * SEQ + i * DP_TM + u]
            pltpu.make_async_copy(pk.at[par, pl.ds(u, 1)], xs_hbm.at[pl.ds(d, 1)], sem.at[par]).start()

    for c in range(n_tiles):
        pack_tile(0, c)

    def group(g, _):
        r0 = pl.multiple_of(g * NORM_ROWS, NORM_ROWS)
        for u in range(NORM_ROWS):
            pack_tile(r0, u)
            issue_token(r0 - NORM_ROWS + u)
        return 0

    lax.fori_loop(1, DP_TM // NORM_ROWS, group, 0)

    def tail(u, _):
        issue_token(DP_TM - NORM_ROWS + u)
        return 0

    lax.fori_loop(0, NORM_ROWS, tail, 0)

    def drain(p):
        for _ in range(TOP_K):
            pltpu.make_async_copy(pk.at[p], xs_hbm.at[pl.ds(0, DP_TM)], sem.at[p]).wait()

    @pl.when(i > 0)
    def _():
        drain(1 - par)

    @pl.when(i == pl.num_programs(0) - 1)
    def _():
        drain(par)


def _dispatch(dest, lastblk, x1, norm2_g, mod):
    row = lambda n: pl.BlockSpec((1, D_MODEL), lambda i, d, lb, n=n: (0, n))
    return pl.pallas_call(
        _dispatch_kernel,
        out_shape=jax.ShapeDtypeStruct((MOE_ROWS, XS_W), U32),
        grid_spec=pltpu.PrefetchScalarGridSpec(
            num_scalar_prefetch=2,
            grid=(SEQ // DP_TM,),
            in_specs=[
                pl.BlockSpec((DP_TM, D_MODEL), lambda i, d, lb: (i, 0)),
                pl.BlockSpec((1, D_MODEL), lambda i, d, lb: (0, 0)),
                row(MOD_B_SC2), row(MOD_B_SH2),
            ],
            out_specs=pl.BlockSpec(memory_space=pl.ANY),
            scratch_shapes=[
                pltpu.VMEM((2, DP_TM, XS_W), U32),
                pltpu.VMEM((MOE_TM, XS_W), U32),
                pltpu.VMEM((DP_TM, LANES), F32),
                pltpu.VMEM((NORM_ROWS, D_MODEL), F32),
                pltpu.VMEM((NORM_ROWS, D_MODEL), F32),
                pltpu.SemaphoreType.DMA,
                pltpu.SemaphoreType.DMA((2,)),
            ],
        ),
        compiler_params=_params(("arbitrary",)),
        name="dispatch",
    )(dest, lastblk, x1, norm2_g, mod, mod)


F2_TN = 4096
assert F2_TN == D_MODEL


CAST_ROWS = 128
WEIGHT_DMA_PRIORITY = 1


def _blocks_used(sched_ref):
    return sched_ref[SCHED_META, 0]


def _used_block(i, sched_ref):
    return jnp.minimum(i, _blocks_used(sched_ref) - 1)


def _stream_expert_weights(j, i, nj, sched_ref, tile_copies, stg, wbuf):
    @pl.when(sched_ref[SCHED_FIRST, i] == 1)
    def _():
        seq = j * sched_ref[SCHED_META, 1] + sched_ref[SCHED_SEG, i]
        slot = seq % 2

        @pl.when(seq == 0)
        def _():
            for cp in tile_copies(sched_ref[SCHED_BE, i], j, slot):
                cp.start(priority=WEIGHT_DMA_PRIORITY)

        for cp in tile_copies(sched_ref[SCHED_BE, i], j, slot):
            cp.wait()

        nxt = sched_ref[SCHED_NXT, i]

        @pl.when(nxt >= 0)
        def _():
            for cp in tile_copies(nxt, j, 1 - slot):
                cp.start(priority=WEIGHT_DMA_PRIORITY)

        @pl.when((nxt < 0) & (j + 1 < nj))
        def _():
            for cp in tile_copies(sched_ref[SCHED_BE, 0], j + 1, 1 - slot):
                cp.start(priority=WEIGHT_DMA_PRIORITY)

        if wbuf is not None:
            def cast(r, _):
                r0 = pl.multiple_of(r * CAST_ROWS, CAST_ROWS)
                wbuf[pl.ds(r0, CAST_ROWS), :] = stg[slot, pl.ds(r0, CAST_ROWS), :].astype(BF16)
                return 0

            lax.fori_loop(0, wbuf.shape[0] // CAST_ROWS, cast, 0)


F1_NH = D_MODEL // XS_W


def _ffn1_kernel(sched_ref, x_ref, bg_ref, bl_ref, w1_hbm, o_ref, stg, wbuf, sem):
    i = pl.program_id(0)

    def slab_copies(e, h):
        return (pltpu.make_async_copy(w1_hbm.at[e, h * XS_W:(h + 1) * XS_W, :], stg.at[h], sem.at[h]),)

    def compute(weights, n_rows=MOE_TM):
        xp = x_ref[0:n_rows, :]
        x_lo = lax.bitcast_convert_type(xp << 16, F32).astype(BF16)
        x_hi = lax.bitcast_convert_type(xp & jnp.uint32(0xFFFF0000), F32).astype(BF16)
        y = jnp.dot(x_lo, weights(0), preferred_element_type=F32)
        y = y + jnp.dot(x_hi, weights(1), preferred_element_type=F32)
        glu = jnp.minimum(y[:, :D_FF] + bg_ref[...], SWIGLU_LIMIT)
        lin = jnp.clip(y[:, D_FF:] + bl_ref[...], -SWIGLU_LIMIT, SWIGLU_LIMIT)
        o_ref[0:n_rows, :] = (glu * jax.nn.sigmoid(SWIGLU_ALPHA * glu) * (lin + 1.0)).astype(BF16)

    used = i < _blocks_used(sched_ref)
    first = sched_ref[SCHED_FIRST, i] == 1

    @pl.when(used & first)
    def _():
        @pl.when(i == 0)
        def _():
            for h in range(F1_NH):
                for cp in slab_copies(sched_ref[SCHED_BE, 0], h):
                    cp.start(priority=WEIGHT_DMA_PRIORITY)

        for h in range(F1_NH):
            for cp in slab_copies(sched_ref[SCHED_BE, i], h):
                cp.wait()

        def convert(h):
            w = stg[h].astype(BF16)
            wbuf[h] = w
            return w

        compute(convert)

        nxt = sched_ref[SCHED_NXT, i]

        @pl.when(nxt >= 0)
        def _():
            for h in range(F1_NH):
                for cp in slab_copies(nxt, h):
                    cp.start(priority=WEIGHT_DMA_PRIORITY)

    half = sched_ref[SCHED_HALF, i] == 1

    @pl.when(used & jnp.logical_not(first) & jnp.logical_not(half))
    def _():
        compute(lambda h: wbuf[h])

    @pl.when(used & jnp.logical_not(first) & half)
    def _():
        compute(lambda h: wbuf[h], MOE_TM // 2)


def _ffn1(sched, xs, w1, b1_3):
    expert = lambda i, s: s[SCHED_BE, _used_block(i, s)]
    return pl.pallas_call(
        _ffn1_kernel,
        out_shape=jax.ShapeDtypeStruct((MOE_ROWS, D_FF), BF16),
        grid_spec=pltpu.PrefetchScalarGridSpec(
            num_scalar_prefetch=1,
            grid=(MOE_NB,),
            in_specs=[
                pl.BlockSpec((MOE_TM, XS_W), lambda i, s: (_used_block(i, s), 0)),
                pl.BlockSpec((None, 1, D_FF), lambda i, s: (expert(i, s), 0, 0)),
                pl.BlockSpec((None, 1, D_FF), lambda i, s: (expert(i, s), 0, 1)),
                pl.BlockSpec(memory_space=pl.ANY),
            ],
            out_specs=pl.BlockSpec((MOE_TM, D_FF), lambda i, s: (_used_block(i, s), 0)),
            scratch_shapes=[
                pltpu.VMEM((F1_NH, XS_W, 2 * D_FF), F32),
                pltpu.VMEM((F1_NH, XS_W, 2 * D_FF), BF16),
                pltpu.SemaphoreType.DMA((F1_NH,)),
            ],
        ),
        compiler_params=_params(("arbitrary",), vmem=BIG_VMEM_LIMIT),
        name="ffn1",
    )(sched, xs, b1_3, b1_3, w1)


def _ffn2_kernel(sched_ref, a_ref, b_ref, w2_hbm, o_ref, stg, sem):
    j = pl.program_id(0)
    i = pl.program_id(1)
    nj = pl.num_programs(0)

    def tile_copies(e, jj, slot):
        c0 = pl.multiple_of(jj * F2_TN, F2_TN)
        return (pltpu.make_async_copy(w2_hbm.at[e, :, pl.ds(c0, F2_TN)], stg.at[slot], sem.at[slot]),)

    @pl.when(i < _blocks_used(sched_ref))
    def _():
        _stream_expert_weights(j, i, nj, sched_ref, tile_copies, stg, None)
        slot = (j * sched_ref[SCHED_META, 1] + sched_ref[SCHED_SEG, i]) % 2

        def compute(n_rows):
            y = jnp.dot(a_ref[0:n_rows, :], stg[slot].astype(BF16), preferred_element_type=F32) + b_ref[...]
            lo = lax.bitcast_convert_type(y[:, :XS_W].astype(BF16).astype(F32), U32)
            hi = lax.bitcast_convert_type(y[:, XS_W:].astype(BF16).astype(F32), U32)
            o_ref[0:n_rows, :] = (lo >> 16) | (hi & jnp.uint32(0xFFFF0000))

        half = sched_ref[SCHED_HALF, i] == 1

        @pl.when(jnp.logical_not(half))
        def _():
            compute(MOE_TM)

        @pl.when(half)
        def _():
            compute(MOE_TM // 2)


def _ffn2(sched, act, w2, b2_3):
    return pl.pallas_call(
        _ffn2_kernel,
        out_shape=jax.ShapeDtypeStruct((MOE_ROWS, XS_W), U32),
        grid_spec=pltpu.PrefetchScalarGridSpec(
            num_scalar_prefetch=1,
            grid=(D_MODEL // F2_TN, MOE_NB),
            in_specs=[
                pl.BlockSpec((MOE_TM, D_FF), lambda j, i, s: (_used_block(i, s), 0)),
                pl.BlockSpec((None, 1, F2_TN), lambda j, i, s: (s[SCHED_BE, _used_block(i, s)], 0, j)),
                pl.BlockSpec(memory_space=pl.ANY),
            ],
            out_specs=pl.BlockSpec((MOE_TM, XS_W), lambda j, i, s: (_used_block(i, s), j)),
            scratch_shapes=[
                pltpu.VMEM((2, D_FF, F2_TN), F32),
                pltpu.SemaphoreType.DMA((2,)),
            ],
        ),
        compiler_params=_params(("arbitrary", "arbitrary")),
        name="ffn2",
    )(sched, act, b2_3, w2)


CB_TM = 256


def _combine_kernel(dest_ref, x_ref, gate_ref, g2_ref, ys_hbm, o_ref, buf, gate_scr, g2_scr, sem):
    i = pl.program_id(0)
    last = pl.num_programs(0) - 1
    par = i % 2
    n_tiles = XS_W // LANES
    copies_per_tile = 8 * TOP_K // n_tiles
    assert copies_per_tile * n_tiles == 8 * TOP_K

    def row_start(tile, p, u, k):
        d = dest_ref[k * SEQ + tile * CB_TM + u]
        pltpu.make_async_copy(ys_hbm.at[pl.ds(d, 1)], buf.at[p, k, pl.ds(u, 1)], sem.at[p]).start()

    def drain(p):
        for k in range(TOP_K):
            pltpu.make_async_copy(ys_hbm.at[pl.ds(0, CB_TM)], buf.at[p, k], sem.at[p]).wait()

    @pl.when(i == 0)
    def _():
        def body(u, _):
            for k in range(TOP_K):
                row_start(0, 0, u, k)
            return 0
        lax.fori_loop(0, CB_TM, body, 0)
        g2_scr[...] = jnp.broadcast_to(g2_ref[...], g2_scr.shape)

    drain(par)

    for k in range(TOP_K):
        gate_scr[k] = jnp.broadcast_to(gate_ref[:, k:k + 1], (CB_TM, LANES))

    nxt = jnp.minimum(i + 1, last)

    def body(r, _):
        r0 = pl.multiple_of(r * 8, 8)
        rows = pl.ds(r0, 8)
        gk = [gate_scr[k, rows, :] for k in range(TOP_K)]
        for c in range(n_tiles):
            cs_lo = slice(c * LANES, (c + 1) * LANES)
            cs_hi = slice(XS_W + c * LANES, XS_W + (c + 1) * LANES)
            y_lo = y_hi = None
            for k in range(TOP_K):
                w = buf[par, k, rows, cs_lo]
                lo = lax.bitcast_convert_type(w << 16, F32) * gk[k]
                hi = lax.bitcast_convert_type(w & jnp.uint32(0xFFFF0000), F32) * gk[k]
                y_lo = lo if y_lo is None else y_lo + lo
                y_hi = hi if y_hi is None else y_hi + hi
            o_ref[rows, cs_lo] = x_ref[rows, cs_lo] + g2_scr[:, cs_lo] * y_lo
            o_ref[rows, cs_hi] = x_ref[rows, cs_hi] + g2_scr[:, cs_hi] * y_hi
            for q in range(copies_per_tile):
                n = c * copies_per_tile + q
                row_start(nxt, 1 - par, r0 + n // TOP_K, n % TOP_K)
        return 0

    lax.fori_loop(0, CB_TM // 8, body, 0)

    @pl.when(i == last)
    def _():
        drain(1 - par)


def _combine(dest, x1, gates_t, mod, ys):
    return pl.pallas_call(
        _combine_kernel,
        out_shape=jax.ShapeDtypeStruct((SEQ, D_MODEL), F32),
        grid_spec=pltpu.PrefetchScalarGridSpec(
            num_scalar_prefetch=1,
            grid=(SEQ // CB_TM,),
            in_specs=[
                pl.BlockSpec((CB_TM, D_MODEL), lambda i, d: (i, 0)),
                pl.BlockSpec((CB_TM, TOP_K), lambda i, d: (i, 0)),
                pl.BlockSpec((1, D_MODEL), lambda i, d: (0, MOD_B_G2)),
                pl.BlockSpec(memory_space=pl.ANY),
            ],
            out_specs=pl.BlockSpec((CB_TM, D_MODEL), lambda i, d: (i, 0)),
            scratch_shapes=[
                pltpu.VMEM((2, TOP_K, CB_TM, XS_W), U32),
                pltpu.VMEM((TOP_K, CB_TM, LANES), F32),
                pltpu.VMEM((8, D_MODEL), F32),
                pltpu.SemaphoreType.DMA((2,)),
            ],
        ),
        compiler_params=_params(("arbitrary",)),
        name="combine",
    )(dest, x1, gates_t, mod, ys)


def _rope_tables():
    t = jnp.arange(SEQ, dtype=I32)
    row = (t // GRID_W).astype(F32)
    col = (t % GRID_W).astype(F32)
    inv_freq = ROPE_THETA ** (-jnp.arange(0, ROPE_AXIS_DIM, 2, dtype=F32) / ROPE_AXIS_DIM)
    ang_r = inv_freq[:, None] * row[None, :]
    ang_c = inv_freq[:, None] * col[None, :]
    cos_t = jnp.concatenate([jnp.cos(ang_r), jnp.cos(ang_r), jnp.cos(ang_c), jnp.cos(ang_c)], axis=0)
    sin_t = jnp.concatenate([-jnp.sin(ang_r), jnp.sin(ang_r), -jnp.sin(ang_c), jnp.sin(ang_c)], axis=0)
    return cos_t, sin_t


def kernel(x, c, w_mod, b_mod, norm1_g, w_in, q_norm_g, k_norm_g, w_pool, pool_scale, w_out, norm2_g,
           w_router, b_router, w1, b1, w2, b2):
    assert x.shape == (1, SEQ, D_MODEL) and w_mod.shape[0] == 1
    x2 = x[0]
    cos_t, sin_t = _rope_tables()

    c_col = c.reshape(D_MODEL, 1)
    mod_a = _mod(c_col, w_mod[0], b_mod)

    pool_in, qt, k, vt, mod = _inproj(x2, mod_a, norm1_g, w_in[0].astype(BF16), cos_t, sin_t,
                                      q_norm_g.reshape(HEAD_DIM, 1), k_norm_g.reshape(HEAD_DIM, 1),
                                      c.reshape(D_MODEL // LANES, LANES), w_mod[0], b_mod)
    attn = _attention(qt, k, vt)
    pool = _pool(pool_in, w_pool[0].astype(BF16), pool_scale)
    x1 = _outproj(attn, pool, w_out[0], x2, mod)

    wr = w_router[0]
    wr_hi = wr.astype(BF16)
    wr_lo = (wr - wr_hi.astype(F32)).astype(BF16)
    pad = lambda a: jnp.pad(a, ((0, 0), (0, RT_PAD - N_EXPERTS)))
    wcat = jnp.concatenate([pad(wr_hi), pad(wr_lo)], axis=1)
    logits_t = _norm2(x1, norm2_g, mod, wcat, pad(wr_hi), pad(b_router))

    dest, gates, sched, lastblk = _route(logits_t)
    dest = dest.reshape(TOP_K * SEQ)

    xs = _dispatch(dest, lastblk, x1, norm2_g, mod)
    act = _ffn1(sched, xs, w1[0], b1[0].reshape(N_EXPERTS, 1, 2 * D_FF))
    ys = _ffn2(sched, act, w2[0], b2[0].reshape(N_EXPERTS, 1, D_MODEL))
    out = _combine(dest, x1, gates.T, mod, ys)
    return out[None]
```

```python
import functools
import math

import jax
import jax.numpy as jnp
from jax import lax
from jax.experimental import pallas as pl
from jax.experimental.pallas import tpu as pltpu

F32 = jnp.float32
BF16 = jnp.bfloat16
I32 = jnp.int32

D_MODEL = 4096
SEQ = 8192
POOL_WIDTH = 2048
ATTN_WIDTH = 2048
HEAD_DIM = 128
N_HEADS = 16
N_KV_HEADS = 4
GROUP = N_HEADS // N_KV_HEADS
KV_WIDTH = N_KV_HEADS * HEAD_DIM
IN_WIDTH = POOL_WIDTH + ATTN_WIDTH + 2 * KV_WIDTH
POOL_WINDOWS = (2, 4, 8, 16)
POOL_GROUP_WIDTH = POOL_WIDTH // len(POOL_WINDOWS)
GRID_W = 64
ROPE_THETA = 10000.0
ROPE_AXIS_DIM = HEAD_DIM // 2
N_EXPERTS = 32
TOP_K = 4
D_FF = D_MODEL // 4
SWIGLU_ALPHA = 1.702
SWIGLU_LIMIT = 7.0
N_MOD = 6
EPS = 1e-6

LANES = 128
VMEM_LIMIT = 56 * 1024 * 1024
BIG_VMEM_LIMIT = 62 * 1024 * 1024

Q_SCALE = (HEAD_DIM ** -0.5) * math.log2(math.e)

MOE_TM = 256
MOE_NB = SEQ * TOP_K // MOE_TM + N_EXPERTS
MOE_ROWS = MOE_NB * MOE_TM


def _params(sem, vmem=VMEM_LIMIT):
    return pltpu.CompilerParams(dimension_semantics=sem, vmem_limit_bytes=vmem)


MOD_TN = 1024
MOD_KC = 256


def _mod_kernel(c_ref, w_ref, b_ref, o_ref):
    def body(k, acc):
        r = pl.multiple_of(k * MOD_KC, MOD_KC)
        ck = c_ref[pl.ds(r, MOD_KC), :]
        ck = ck * jax.nn.sigmoid(ck)
        p = w_ref[pl.ds(r, MOD_KC), :] * ck
        return acc + p.reshape(MOD_KC // 8, 8, MOD_TN).sum(axis=0)

    acc = lax.fori_loop(0, D_MODEL // MOD_KC, body, jnp.zeros((8, MOD_TN), F32))
    o_ref[...] = acc.sum(axis=0, keepdims=True) + b_ref[...]


MOD_A = 2
MOD_B = N_MOD - MOD_A
MOD_B_G1, MOD_B_SH2, MOD_B_SC2, MOD_B_G2 = range(MOD_B)


def _mod(c_col, w_mod, b_mod):
    n = MOD_A * D_MODEL
    return pl.pallas_call(
        _mod_kernel,
        out_shape=jax.ShapeDtypeStruct((1, n), F32),
        grid=(n // MOD_TN,),
        in_specs=[
            pl.BlockSpec((D_MODEL, 1), lambda j: (0, 0)),
            pl.BlockSpec((D_MODEL, MOD_TN), lambda j: (0, j)),
            pl.BlockSpec((1, MOD_TN), lambda j: (0, j)),
        ],
        out_specs=pl.BlockSpec((1, MOD_TN), lambda j: (0, j)),
        compiler_params=_params(("arbitrary",)),
        name="mod",
    )(c_col, w_mod, b_mod)


IP_TM = 512
IP_TN = 1024
IP_NJ = IN_WIDTH // IP_TN
IP_J_Q = POOL_WIDTH // IP_TN
IP_J_KV = IP_J_Q + ATTN_WIDTH // IP_TN
NORM_ROWS = 16
BF16_SUBLANES = 16
KA_W = 2 * HEAD_DIM
VA_H = HEAD_DIM + BF16_SUBLANES


def _prep_modulation(g_ref, sc_ref, sh_ref, a_scr, s_scr):
    a_scr[...] = jnp.broadcast_to(g_ref[...] * (1.0 + sc_ref[...]), a_scr.shape)
    s_scr[...] = jnp.broadcast_to(sh_ref[...], s_scr.shape)


def _row_rms(x_ref, rs_scr, n_rows):
    def body(r, _):
        r0 = pl.multiple_of(r * NORM_ROWS, NORM_ROWS)
        width = x_ref.shape[1]
        parts = []
        for c in range(width // LANES):
            xc = x_ref[pl.ds(r0, NORM_ROWS), c * LANES:(c + 1) * LANES]
            parts.append(xc * xc)
        while len(parts) > 1:
            parts = [parts[p] + parts[p + 1] for p in range(0, len(parts), 2)]
        rs_scr[pl.ds(r0, NORM_ROWS), :] = parts[0]
        return 0
    lax.fori_loop(0, n_rows // NORM_ROWS, body, 0)
    ms = jnp.sum(rs_scr[...], axis=-1, keepdims=True) * (1.0 / x_ref.shape[1])
    rs_scr[...] = jnp.broadcast_to(lax.rsqrt(ms + EPS), rs_scr.shape)


def _normed_tile(x_ref, rs_scr, a_scr, s_scr, r0, c):
    cs = slice(c * LANES, (c + 1) * LANES)
    return x_ref[pl.ds(r0, NORM_ROWS), cs] * rs_scr[pl.ds(r0, NORM_ROWS), :] * a_scr[:, cs] + s_scr[:, cs]


def _norm_rope_t(xt, g_col, cos_t, sin_t):
    ms = jnp.mean(xt * xt, axis=0, keepdims=True)
    y = xt * lax.rsqrt(ms + EPS) * g_col
    q = ROPE_AXIS_DIM // 2
    partner = jnp.concatenate([y[q:2 * q], y[0:q], y[3 * q:4 * q], y[2 * q:3 * q]], axis=0)
    return y * cos_t + partner * sin_t


MB_TN = 256
MB_STEPS = MOD_B * D_MODEL // MB_TN
MB_KC = 256


def _tree_sum(parts):
    while len(parts) > 1:
        parts = [parts[p] + parts[p + 1] for p in range(0, len(parts), 2)]
    return parts[0]


def _later_modulation(t, c_ref, wm_ref, bm_ref, mb_ref, cb_scr):
    @pl.when(t == 0)
    def _():
        for r in range(D_MODEL // LANES):
            cv = c_ref[r:r + 1, :]
            cb_scr[r * LANES:(r + 1) * LANES, :] = jnp.broadcast_to(cv * jax.nn.sigmoid(cv), (LANES, LANES)).T

    @pl.when(t < MB_STEPS)
    def _():
        def body(k, accs):
            r0 = pl.multiple_of(k * MB_KC, MB_KC)
            cb = cb_scr[pl.ds(r0, MB_KC), :]
            out = []
            for n in range(MB_TN // LANES):
                p = wm_ref[pl.ds(r0, MB_KC), n * LANES:(n + 1) * LANES] * cb
                out.append(accs[n] + _tree_sum([p[g * 8:(g + 1) * 8] for g in range(MB_KC // 8)]))
            return tuple(out)

        zero = jnp.zeros((8, LANES), F32)
        accs = lax.fori_loop(0, D_MODEL // MB_KC, body, (zero,) * (MB_TN // LANES))
        for n in range(MB_TN // LANES):
            cs = slice(n * LANES, (n + 1) * LANES)
            mb_ref[:, cs] = accs[n].sum(axis=0, keepdims=True) + bm_ref[:, cs]


def _inproj_kernel(x_ref, g_ref, sc_ref, sh_ref, w_ref, cos_ref, sin_ref, qg_ref, kg_ref, c_ref, wm_ref, bm_ref,
                   pool_ref, qt_ref, k_ref, vt_ref, mb_ref, h_scr, rs_scr, a_scr, s_scr, cb_scr):
    j = pl.program_id(1)
    _later_modulation(pl.program_id(0) * IP_NJ + j, c_ref, wm_ref, bm_ref, mb_ref, cb_scr)

    @pl.when(j == 0)
    def _():
        _prep_modulation(g_ref, sc_ref, sh_ref, a_scr, s_scr)
        _row_rms(x_ref, rs_scr, IP_TM)

        def body(r, _):
            r0 = pl.multiple_of(r * NORM_ROWS, NORM_ROWS)
            for c in range(D_MODEL // LANES):
                h_scr[pl.ds(r0, NORM_ROWS), c * LANES:(c + 1) * LANES] = _normed_tile(
                    x_ref, rs_scr, a_scr, s_scr, r0, c).astype(BF16)
            return 0
        lax.fori_loop(0, IP_TM // NORM_ROWS, body, 0)

    acc = jnp.dot(h_scr[...], w_ref[...], preferred_element_type=F32)

    @pl.when(j < IP_J_Q)
    def _():
        pool_ref[...] = acc.astype(BF16)

    @pl.when((j >= IP_J_Q) & (j < IP_J_KV))
    def _():
        for hh in range(IP_TN // HEAD_DIM):
            sl = slice(hh * HEAD_DIM, (hh + 1) * HEAD_DIM)
            r = _norm_rope_t(acc[:, sl].T, qg_ref[...], cos_ref[...], sin_ref[...]) * Q_SCALE
            qt_ref[sl, :] = r.astype(BF16)

    @pl.when(j == IP_J_KV)
    def _():
        lane = lax.broadcasted_iota(I32, (IP_TM, HEAD_DIM), 1)
        one_col = jnp.where(lane == 0, 1.0, 0.0).astype(BF16)
        for hh in range(N_KV_HEADS):
            sl = slice(hh * HEAD_DIM, (hh + 1) * HEAD_DIM)
            k_ref[:, hh * KA_W:hh * KA_W + HEAD_DIM] = _norm_rope_t(
                acc[:, sl].T, kg_ref[...], cos_ref[...], sin_ref[...]).T.astype(BF16)
            k_ref[:, hh * KA_W + HEAD_DIM:(hh + 1) * KA_W] = one_col
        for hh in range(N_KV_HEADS):
            sl = slice(KV_WIDTH + hh * HEAD_DIM, KV_WIDTH + (hh + 1) * HEAD_DIM)
            vt_ref[hh * VA_H:hh * VA_H + HEAD_DIM, :] = acc[:, sl].T.astype(BF16)
            vt_ref[hh * VA_H + HEAD_DIM:(hh + 1) * VA_H, :] = jnp.ones((VA_H - HEAD_DIM, IP_TM), BF16)


def _inproj(x2, mod, norm1_g, w_in_b, cos_t, sin_t, qg, kg, c_col, w_mod, b_mod):
    row = lambda n: pl.BlockSpec((1, D_MODEL), lambda i, j, n=n: (0, n))
    mb_tile = lambda i, j: jnp.minimum(i * IP_NJ + j, MB_STEPS - 1)
    mb_first = MOD_A * D_MODEL // MB_TN
    return pl.pallas_call(
        _inproj_kernel,
        out_shape=(
            jax.ShapeDtypeStruct((SEQ, POOL_WIDTH), BF16),
            jax.ShapeDtypeStruct((ATTN_WIDTH, SEQ), BF16),
            jax.ShapeDtypeStruct((SEQ, N_KV_HEADS * KA_W), BF16),
            jax.ShapeDtypeStruct((N_KV_HEADS * VA_H, SEQ), BF16),
            jax.ShapeDtypeStruct((1, MOD_B * D_MODEL), F32),
        ),
        grid=(SEQ // IP_TM, IP_NJ),
        in_specs=[
            pl.BlockSpec((IP_TM, D_MODEL), lambda i, j: (i, 0)),
            pl.BlockSpec((1, D_MODEL), lambda i, j: (0, 0)),
            row(1), row(0),
            pl.BlockSpec((D_MODEL, IP_TN), lambda i, j: (0, j)),
            pl.BlockSpec((HEAD_DIM, IP_TM), lambda i, j: (0, i)),
            pl.BlockSpec((HEAD_DIM, IP_TM), lambda i, j: (0, i)),
            pl.BlockSpec((HEAD_DIM, 1), lambda i, j: (0, 0)),
            pl.BlockSpec((HEAD_DIM, 1), lambda i, j: (0, 0)),
            pl.BlockSpec((D_MODEL // LANES, LANES), lambda i, j: (0, 0)),
            pl.BlockSpec((D_MODEL, MB_TN), lambda i, j: (0, mb_first + mb_tile(i, j))),
            pl.BlockSpec((1, MB_TN), lambda i, j: (0, mb_first + mb_tile(i, j))),
        ],
        out_specs=(
            pl.BlockSpec((IP_TM, IP_TN), lambda i, j: (i, jnp.minimum(j, IP_J_Q - 1))),
            pl.BlockSpec((IP_TN, IP_TM), lambda i, j: (jnp.clip(j - IP_J_Q, 0, IP_J_KV - IP_J_Q - 1), i)),
            pl.BlockSpec((IP_TM, N_KV_HEADS * KA_W), lambda i, j: (i, 0)),
            pl.BlockSpec((N_KV_HEADS * VA_H, IP_TM), lambda i, j: (0, i)),
            pl.BlockSpec((1, MB_TN), lambda i, j: (0, mb_tile(i, j))),
        ),
        scratch_shapes=[
            pltpu.VMEM((IP_TM, D_MODEL), BF16),
            pltpu.VMEM((IP_TM, LANES), F32),
            pltpu.VMEM((NORM_ROWS, D_MODEL), F32),
            pltpu.VMEM((NORM_ROWS, D_MODEL), F32),
            pltpu.VMEM((D_MODEL, LANES), F32),
        ],
        compiler_params=_params(("arbitrary", "arbitrary"), vmem=BIG_VMEM_LIMIT),
        name="inproj",
    )(x2, norm1_g, mod, mod, w_in_b, cos_t, sin_t, qg, kg, c_col, w_mod, b_mod)


AT_TQ = 1024
AT_TK = 8192
AT_TK_ONLINE = 512
SHIFT_LIMIT = 60.0


def _attn_kernel(qt_ref, k_ref, vt_ref, o_ref, qa_scr, p_scr, kmax_scr):
    h = pl.program_id(0)
    i = pl.program_id(1)

    @pl.when((i == 0) & (h % GROUP == 0))
    def _():
        def body(c, mx):
            c0 = pl.multiple_of(c * AT_TK, AT_TK)
            kc = k_ref[pl.ds(c0, AT_TK), :HEAD_DIM].astype(F32)
            n2 = (kc * kc).sum(axis=1, keepdims=True)
            return jnp.maximum(mx, n2.max(axis=0, keepdims=True))
        mx = lax.fori_loop(0, SEQ // AT_TK, body, jnp.zeros((1, 1), F32))
        kmax_scr[...] = jnp.broadcast_to(jnp.sqrt(mx), kmax_scr.shape)

    q = qt_ref[...].astype(F32)
    bound = jnp.sqrt((q * q).sum(axis=0, keepdims=True)) * kmax_scr[0:1, 0:1] * 1.01
    fast = jnp.max(bound) <= SHIFT_LIMIT

    @pl.when(fast)
    def _():
        qa_scr[0:HEAD_DIM, :] = qt_ref[...]
        row = lax.broadcasted_iota(I32, (KA_W - HEAD_DIM, AT_TQ), 0)
        qa_scr[HEAD_DIM:, :] = jnp.where(row == 0, -bound, 0.0).astype(BF16)

        def body(c, _):
            c0 = pl.multiple_of(c * AT_TK, AT_TK)
            s = jnp.dot(k_ref[pl.ds(c0, AT_TK), :], qa_scr[...], preferred_element_type=F32)
            p_scr[pl.ds(c0, AT_TK), :] = jnp.exp2(s).astype(BF16)
            return 0

        lax.fori_loop(0, SEQ // AT_TK, body, 0)
        o = jnp.dot(vt_ref[...], p_scr[...], preferred_element_type=F32)
        o_ref[...] = (o[:HEAD_DIM] * (1.0 / o[HEAD_DIM:HEAD_DIM + 1])).T.astype(BF16)

    @pl.when(jnp.logical_not(fast))
    def _():
        qt = qt_ref[...]

        def chunk(c, carry):
            m, l, acc = carry
            c0 = pl.multiple_of(c * AT_TK_ONLINE, AT_TK_ONLINE)
            s = jnp.dot(k_ref[pl.ds(c0, AT_TK_ONLINE), :HEAD_DIM], qt, preferred_element_type=F32)
            m_new = jnp.maximum(m, s.max(axis=0, keepdims=True))
            alpha = jnp.exp2(m - m_new)
            p = jnp.exp2(s - m_new)
            l = alpha * l + p.sum(axis=0, keepdims=True)
            pv = jnp.dot(vt_ref[:HEAD_DIM, pl.ds(c0, AT_TK_ONLINE)], p.astype(BF16), preferred_element_type=F32)
            return m_new, l, alpha * acc + pv

        init = (jnp.full((1, AT_TQ), -jnp.inf, F32), jnp.zeros((1, AT_TQ), F32),
                jnp.zeros((HEAD_DIM, AT_TQ), F32))
        _, l, acc = lax.fori_loop(0, SEQ // AT_TK_ONLINE, chunk, init)
        o_ref[...] = (acc * (1.0 / l)).T.astype(BF16)


def _attention(qt, k, vt):
    return pl.pallas_call(
        _attn_kernel,
        out_shape=jax.ShapeDtypeStruct((SEQ, ATTN_WIDTH), BF16),
        grid=(N_HEADS, SEQ // AT_TQ),
        in_specs=[
            pl.BlockSpec((HEAD_DIM, AT_TQ), lambda h, i: (h, i)),
            pl.BlockSpec((SEQ, KA_W), lambda h, i: (0, h // GROUP)),
            pl.BlockSpec((VA_H, SEQ), lambda h, i: (h // GROUP, 0)),
        ],
        out_specs=pl.BlockSpec((AT_TQ, HEAD_DIM), lambda h, i: (i, h)),
        scratch_shapes=[
            pltpu.VMEM((KA_W, AT_TQ), BF16),
            pltpu.VMEM((SEQ, AT_TQ), BF16),
            pltpu.VMEM((8, LANES), F32),
        ],
        compiler_params=_params(("arbitrary", "arbitrary")),
        name="attn",
    )(qt, k, vt)


PL_TM = 256
PL_HALO = 16


def _pool_kernel(prev_ref, main_ref, next_ref, wp_ref, scale_ref, o_ref, buf, band):
    i = pl.program_id(0)
    last = pl.num_programs(0) - 1

    @pl.when(i == 0)
    def _():
        tt = lax.broadcasted_iota(I32, (PL_TM, PL_TM + 2 * PL_HALO), 0)
        ss = lax.broadcasted_iota(I32, (PL_TM, PL_TM + 2 * PL_HALO), 1)
        off = ss - PL_HALO - tt
        for gi, w in enumerate(POOL_WINDOWS):
            band[gi] = jnp.where((off >= -(w // 2)) & (off <= w // 2 - 1), 1.0, 0.0).astype(BF16)

    buf[0:PL_HALO, :] = jnp.where(i == 0, jnp.zeros_like(prev_ref[...]), prev_ref[...])
    buf[PL_HALO:PL_HALO + PL_TM, :] = main_ref[...]
    buf[PL_HALO + PL_TM:, :] = jnp.where(i == last, jnp.zeros_like(next_ref[...]), next_ref[...])
    t = i * PL_TM + lax.broadcasted_iota(I32, (PL_TM, 1), 0)
    for gi, w in enumerate(POOL_WINDOWS):
        cols = slice(gi * POOL_GROUP_WIDTH, (gi + 1) * POOL_GROUP_WIDTH)
        win = jnp.dot(band[gi], buf[:, cols], preferred_element_type=F32)
        lo = jnp.maximum(t - w // 2, 0)
        hi = jnp.minimum(t + w // 2 - 1, SEQ - 1)
        cnt = (hi - lo + 1).astype(F32)
        dlt = win / cnt - main_ref[:, cols].astype(F32)
        y = jnp.dot(dlt.astype(BF16), wp_ref[gi], preferred_element_type=F32)
        o_ref[:, cols] = (y * scale_ref[:, cols]).astype(BF16)


def _pool(pool_in, w_pool_b, pool_scale):
    nh = PL_TM // PL_HALO
    n_halo_blocks = SEQ // PL_HALO
    return pl.pallas_call(
        _pool_kernel,
        out_shape=jax.ShapeDtypeStruct((SEQ, POOL_WIDTH), BF16),
        grid=(SEQ // PL_TM,),
        in_specs=[
            pl.BlockSpec((PL_HALO, POOL_WIDTH), lambda i: (jnp.maximum(i * nh - 1, 0), 0)),
            pl.BlockSpec((PL_TM, POOL_WIDTH), lambda i: (i, 0)),
            pl.BlockSpec((PL_HALO, POOL_WIDTH), lambda i: (jnp.minimum((i + 1) * nh, n_halo_blocks - 1), 0)),
            pl.BlockSpec((len(POOL_WINDOWS), POOL_GROUP_WIDTH, POOL_GROUP_WIDTH), lambda i: (0, 0, 0)),
            pl.BlockSpec((1, POOL_WIDTH), lambda i: (0, 0)),
        ],
        out_specs=pl.BlockSpec((PL_TM, POOL_WIDTH), lambda i: (i, 0)),
        scratch_shapes=[
            pltpu.VMEM((PL_TM + 2 * PL_HALO, POOL_WIDTH), BF16),
            pltpu.VMEM((len(POOL_WINDOWS), PL_TM, PL_TM + 2 * PL_HALO), BF16),
        ],
        compiler_params=_params(("arbitrary",)),
        name="pool",
    )(pool_in, pool_in, pool_in, w_pool_b, pool_scale)


OP_TM = 512
OP_TN = 1024


def _outproj_kernel(a_ref, p_ref, wa_ref, wp_ref, x_ref, g_ref, o_ref):
    acc = jnp.dot(a_ref[...], wa_ref[...].astype(BF16), preferred_element_type=F32)
    acc = acc + jnp.dot(p_ref[...], wp_ref[...].astype(BF16), preferred_element_type=F32)
    o_ref[...] = x_ref[...] + g_ref[...] * acc


def _outproj(attn, pool, w_out, x2, mod):
    return pl.pallas_call(
        _outproj_kernel,
        out_shape=jax.ShapeDtypeStruct((SEQ, D_MODEL), F32),
        grid=(D_MODEL // OP_TN, SEQ // OP_TM),
        in_specs=[
            pl.BlockSpec((OP_TM, ATTN_WIDTH), lambda j, i: (i, 0)),
            pl.BlockSpec((OP_TM, POOL_WIDTH), lambda j, i: (i, 0)),
            pl.BlockSpec((ATTN_WIDTH, OP_TN), lambda j, i: (0, j)),
            pl.BlockSpec((POOL_WIDTH, OP_TN), lambda j, i: (1, j)),
            pl.BlockSpec((OP_TM, OP_TN), lambda j, i: (i, j)),
            pl.BlockSpec((1, OP_TN), lambda j, i: (0, MOD_B_G1 * (D_MODEL // OP_TN) + j)),
        ],
        out_specs=pl.BlockSpec((OP_TM, OP_TN), lambda j, i: (i, j)),
        compiler_params=_params(("arbitrary", "arbitrary")),
        name="outproj",
    )(attn, pool, w_out, w_out, x2, mod)


N2_TM = 512
RT_PAD = LANES


def _norm2_kernel(x_ref, g_ref, sc_ref, sh_ref, wcat_ref, whi_ref, b_ref, lt_ref, hi_scr, lo_scr,
                  rs_scr, a_scr, s_scr):
    @pl.when(pl.program_id(0) == 0)
    def _():
        _prep_modulation(g_ref, sc_ref, sh_ref, a_scr, s_scr)

    _row_rms(x_ref, rs_scr, N2_TM)

    def body(r, _):
        r0 = pl.multiple_of(r * NORM_ROWS, NORM_ROWS)
        for c in range(D_MODEL // LANES):
            cs = slice(c * LANES, (c + 1) * LANES)
            h = _normed_tile(x_ref, rs_scr, a_scr, s_scr, r0, c)
            hi = h.astype(BF16)
            hi_scr[pl.ds(r0, NORM_ROWS), cs] = hi
            lo_scr[pl.ds(r0, NORM_ROWS), cs] = (h - hi.astype(F32)).astype(BF16)
        return 0
    lax.fori_loop(0, N2_TM // NORM_ROWS, body, 0)
    a = jnp.dot(hi_scr[...], wcat_ref[...], preferred_element_type=F32)
    b = jnp.dot(lo_scr[...], whi_ref[...], preferred_element_type=F32)
    logits = a[:, :RT_PAD] + a[:, RT_PAD:] + b + b_ref[...]
    lt_ref[...] = logits.T[:N_EXPERTS, :]


def _norm2(x1, norm2_g, mod, wcat, whi, b_pad):
    row = lambda n: pl.BlockSpec((1, D_MODEL), lambda i, n=n: (0, n))
    return pl.pallas_call(
        _norm2_kernel,
        out_shape=jax.ShapeDtypeStruct((N_EXPERTS, SEQ), F32),
        grid=(SEQ // N2_TM,),
        in_specs=[
            pl.BlockSpec((N2_TM, D_MODEL), lambda i: (i, 0)),
            pl.BlockSpec((1, D_MODEL), lambda i: (0, 0)),
            row(MOD_B_SC2), row(MOD_B_SH2),
            pl.BlockSpec((D_MODEL, 2 * RT_PAD), lambda i: (0, 0)),
            pl.BlockSpec((D_MODEL, RT_PAD), lambda i: (0, 0)),
            pl.BlockSpec((1, RT_PAD), lambda i: (0, 0)),
        ],
        out_specs=pl.BlockSpec((N_EXPERTS, N2_TM), lambda i: (0, i)),
        scratch_shapes=[
            pltpu.VMEM((N2_TM, D_MODEL), BF16),
            pltpu.VMEM((N2_TM, D_MODEL), BF16),
            pltpu.VMEM((N2_TM, LANES), F32),
            pltpu.VMEM((NORM_ROWS, D_MODEL), F32),
            pltpu.VMEM((NORM_ROWS, D_MODEL), F32),
        ],
        compiler_params=_params(("arbitrary",)),
        name="norm2",
    )(x1, norm2_g, mod, mod, wcat, whi, b_pad)


RT_CH = 1024
RT_SB = 256


SCHED_W = RT_SB
SCHED_BE, SCHED_FIRST, SCHED_SEG, SCHED_NXT, SCHED_META, SCHED_HALF = range(6)


def _route_kernel(lt_ref, dest_ref, gate_ref, sched_ref, lastblk_ref, idx_scr, rank_scr):
    e_col = lax.broadcasted_iota(I32, (N_EXPERTS, RT_CH), 0).astype(F32)
    tri = (lax.broadcasted_iota(I32, (RT_SB, RT_SB), 0) < lax.broadcasted_iota(I32, (RT_SB, RT_SB), 1)).astype(BF16)
    carry = jnp.zeros((N_EXPERTS, 1), F32)
    for c in range(SEQ // RT_CH):
        cs = slice(c * RT_CH, (c + 1) * RT_CH)
        work = lt_ref[:, cs]
        vals = []
        mask = jnp.zeros((N_EXPERTS, RT_CH), F32)
        for k in range(TOP_K):
            m = work.max(axis=0, keepdims=True)
            idx = jnp.where(work == m, e_col, float(N_EXPERTS)).min(axis=0, keepdims=True)
            sel = e_col == idx
            vals.append(m)
            idx_scr[k:k + 1, cs] = idx
            mask = jnp.where(sel, 1.0, mask)
            work = jnp.where(sel, -jnp.inf, work)
        ex = [jnp.exp(v - vals[0]) for v in vals]
        den = ex[0] + ex[1] + ex[2] + ex[3]
        for k in range(TOP_K):
            gate_ref[k:k + 1, cs] = ex[k] / den
        for b in range(RT_CH // RT_SB):
            blk = mask[:, b * RT_SB:(b + 1) * RT_SB]
            pref = jnp.dot(blk.astype(BF16), tri, preferred_element_type=F32)
            rank_scr[:, c * RT_CH + b * RT_SB:c * RT_CH + (b + 1) * RT_SB] = pref + carry
            carry = carry + blk.sum(axis=1, keepdims=True)
    nblk = jnp.floor((carry + (MOE_TM - 1)) * (1.0 / MOE_TM))
    nblk_b = jnp.broadcast_to(nblk, (N_EXPERTS, LANES))
    lower = (lax.broadcasted_iota(I32, (N_EXPERTS, N_EXPERTS), 1) < lax.broadcasted_iota(I32, (N_EXPERTS, N_EXPERTS), 0)).astype(BF16)
    start_blk = jnp.dot(lower, nblk_b.astype(BF16), preferred_element_type=F32)
    start = start_blk[:, 0:1] * float(MOE_TM)

    end_blk = start_blk[:, 0:1] + nblk
    lastblk_ref[...] = jnp.broadcast_to(jnp.where(nblk > 0, end_blk - 1.0, -1.0), (N_EXPERTS, LANES)).astype(I32)
    e_blk = lax.broadcasted_iota(I32, (N_EXPERTS, SCHED_W), 0).astype(F32)
    b_blk = lax.broadcasted_iota(I32, (N_EXPERTS, SCHED_W), 1).astype(F32)
    lane = b_blk[0:1]
    n_used = jnp.sum(nblk, axis=0, keepdims=True)
    be = jnp.minimum(jnp.sum(jnp.where(end_blk <= b_blk, 1.0, 0.0), axis=0, keepdims=True), N_EXPERTS - 1.0)
    prev = jnp.where(lane == 0, -1.0, pltpu.roll(be, 1, 1))
    first = jnp.where((lane < n_used) & (be != prev), 1.0, 0.0)
    excl = jnp.dot(jnp.broadcast_to(first, (8, SCHED_W)).astype(BF16), tri, preferred_element_type=F32)[0:1]
    seg = excl + first - 1.0
    nxt = jnp.min(jnp.where((e_blk > be) & (nblk > 0), e_blk, float(N_EXPERTS)), axis=0, keepdims=True)
    nxt = jnp.where(nxt >= N_EXPERTS, -1.0, nxt)
    n_seg = jnp.sum(first, axis=1, keepdims=True)
    meta = jnp.where(lane == 0, n_used, jnp.where(lane == 1, n_seg, 0.0))
    rem = carry - (nblk - 1.0) * float(MOE_TM)
    half_e = (nblk > 0) & (rem <= float(MOE_TM // 2))
    half = jnp.sum(jnp.where((b_blk == end_blk - 1.0) & half_e, 1.0, 0.0), axis=0, keepdims=True)
    for r, v in enumerate((be, first, seg, nxt, meta, half)):
        sched_ref[r:r + 1, :] = v.astype(I32)
    sched_ref[6:8, :] = jnp.zeros((2, SCHED_W), I32)
    for c in range(SEQ // RT_CH):
        cs = slice(c * RT_CH, (c + 1) * RT_CH)
        slot = rank_scr[:, cs] + start
        for k in range(TOP_K):
            sel = e_col == idx_scr[k:k + 1, cs]
            dest_ref[k:k + 1, cs] = jnp.where(sel, slot, 0.0).sum(axis=0, keepdims=True).astype(I32)


def _route(logits_t):
    return pl.pallas_call(
        _route_kernel,
        out_shape=(
            jax.ShapeDtypeStruct((TOP_K, SEQ), I32),
            jax.ShapeDtypeStruct((TOP_K, SEQ), F32),
            jax.ShapeDtypeStruct((8, SCHED_W), I32),
            jax.ShapeDtypeStruct((N_EXPERTS, LANES), I32),
        ),
        scratch_shapes=[pltpu.VMEM((8, SEQ), F32), pltpu.VMEM((N_EXPERTS, SEQ), F32)],
        compiler_params=pltpu.CompilerParams(vmem_limit_bytes=VMEM_LIMIT),
        name="route",
    )(logits_t)


DP_TM = 512
XS_W = D_MODEL // 2
U32 = jnp.uint32


def _dispatch_kernel(dest_ref, lastblk_ref, x_ref, g_ref, sc_ref, sh_ref, xs_hbm, pk, zero_buf, rs_scr, a_scr,
                     s_scr, zsem, sem):
    i = pl.program_id(0)
    par = i % 2

    @pl.when(i == 0)
    def _():
        _prep_modulation(g_ref, sc_ref, sh_ref, a_scr, s_scr)
        zero_buf[...] = jnp.zeros_like(zero_buf)

        def zcopy(e):
            b = jnp.maximum(lastblk_ref[e, 0], 0)
            return pltpu.make_async_copy(zero_buf, xs_hbm.at[pl.ds(pl.multiple_of(b * MOE_TM, MOE_TM), MOE_TM)], zsem)

        def zstart(e, _):
            @pl.when(lastblk_ref[e, 0] >= 0)
            def _():
                zcopy(e).start()
            return 0

        def zwait(e, _):
            @pl.when(lastblk_ref[e, 0] >= 0)
            def _():
                zcopy(e).wait()
            return 0

        lax.fori_loop(0, N_EXPERTS, zstart, 0)
        lax.fori_loop(0, N_EXPERTS, zwait, 0)

    _row_rms(x_ref, rs_scr, DP_TM)

    n_tiles = XS_W // LANES
    assert n_tiles == NORM_ROWS

    def pack_tile(r0, c):
        lo = _normed_tile(x_ref, rs_scr, a_scr, s_scr, r0, c)
        hi = _normed_tile(x_ref, rs_scr, a_scr, s_scr, r0, c + n_tiles)
        lo = lax.bitcast_convert_type(lo.astype(BF16).astype(F32), U32)
        hi = lax.bitcast_convert_type(hi.astype(BF16).astype(F32), U32)
        pk[par, pl.ds(r0, NORM_ROWS), c * LANES:(c + 1) * LANES] = (lo >> 16) | (hi & jnp.uint32(0xFFFF0000))

    def issue_token(u):
        for k in range(TOP_K):
            d = dest_ref[k * SEQ + i * DP_TM + u]
            pltpu.make_async_copy(pk.at[par, pl.ds(u, 1)], xs_hbm.at[pl.ds(d, 1)], sem.at[par]).start()

    for c in range(n_tiles):
        pack_tile(0, c)

    def group(g, _):
        r0 = pl.multiple_of(g * NORM_ROWS, NORM_ROWS)
        for u in range(NORM_ROWS):
            pack_tile(r0, u)
            issue_token(r0 - NORM_ROWS + u)
        return 0

    lax.fori_loop(1, DP_TM // NORM_ROWS, group, 0)

    def tail(u, _):
        issue_token(DP_TM - NORM_ROWS + u)
        return 0

    lax.fori_loop(0, NORM_ROWS, tail, 0)

    def drain(p):
        for _ in range(TOP_K):
            pltpu.make_async_copy(pk.at[p], xs_hbm.at[pl.ds(0, DP_TM)], sem.at[p]).wait()

    @pl.when(i > 0)
    def _():
        drain(1 - par)

    @pl.when(i == pl.num_programs(0) - 1)
    def _():
        drain(par)


def _dispatch(dest, lastblk, x1, norm2_g, mod):
    row = lambda n: pl.BlockSpec((1, D_MODEL), lambda i, d, lb, n=n: (0, n))
    return pl.pallas_call(
        _dispatch_kernel,
        out_shape=jax.ShapeDtypeStruct((MOE_ROWS, XS_W), U32),
        grid_spec=pltpu.PrefetchScalarGridSpec(
            num_scalar_prefetch=2,
            grid=(SEQ // DP_TM,),
            in_specs=[
                pl.BlockSpec((DP_TM, D_MODEL), lambda i, d, lb: (i, 0)),
                pl.BlockSpec((1, D_MODEL), lambda i, d, lb: (0, 0)),
                row(MOD_B_SC2), row(MOD_B_SH2),
            ],
            out_specs=pl.BlockSpec(memory_space=pl.ANY),
            scratch_shapes=[
                pltpu.VMEM((2, DP_TM, XS_W), U32),
                pltpu.VMEM((MOE_TM, XS_W), U32),
                pltpu.VMEM((DP_TM, LANES), F32),
                pltpu.VMEM((NORM_ROWS, D_MODEL), F32),
                pltpu.VMEM((NORM_ROWS, D_MODEL), F32),
                pltpu.SemaphoreType.DMA,
                pltpu.SemaphoreType.DMA((2,)),
            ],
        ),
        compiler_params=_params(("arbitrary",)),
        name="dispatch",
    )(dest, lastblk, x1, norm2_g, mod, mod)


F2_TN = 4096
assert F2_TN == D_MODEL


CAST_ROWS = 128
WEIGHT_DMA_PRIORITY = 1


def _blocks_used(sched_ref):
    return sched_ref[SCHED_META, 0]


def _used_block(i, sched_ref):
    return jnp.minimum(i, _blocks_used(sched_ref) - 1)


def _stream_expert_weights(j, i, nj, sched_ref, tile_copies, stg, wbuf):
    @pl.when(sched_ref[SCHED_FIRST, i] == 1)
    def _():
        seq = j * sched_ref[SCHED_META, 1] + sched_ref[SCHED_SEG, i]
        slot = seq % 2

        @pl.when(seq == 0)
        def _():
            for cp in tile_copies(sched_ref[SCHED_BE, i], j, slot):
                cp.start(priority=WEIGHT_DMA_PRIORITY)

        for cp in tile_copies(sched_ref[SCHED_BE, i], j, slot):
            cp.wait()

        nxt = sched_ref[SCHED_NXT, i]

        @pl.when(nxt >= 0)
        def _():
            for cp in tile_copies(nxt, j, 1 - slot):
                cp.start(priority=WEIGHT_DMA_PRIORITY)

        @pl.when((nxt < 0) & (j + 1 < nj))
        def _():
            for cp in tile_copies(sched_ref[SCHED_BE, 0], j + 1, 1 - slot):
                cp.start(priority=WEIGHT_DMA_PRIORITY)

        if wbuf is not None:
            def cast(r, _):
                r0 = pl.multiple_of(r * CAST_ROWS, CAST_ROWS)
                wbuf[pl.ds(r0, CAST_ROWS), :] = stg[slot, pl.ds(r0, CAST_ROWS), :].astype(BF16)
                return 0

            lax.fori_loop(0, wbuf.shape[0] // CAST_ROWS, cast, 0)


F1_NH = D_MODEL // XS_W


def _ffn1_kernel(sched_ref, x_ref, bg_ref, bl_ref, w1_hbm, o_ref, stg, wbuf, sem):
    i = pl.program_id(0)

    def slab_copies(e, h):
        return (pltpu.make_async_copy(w1_hbm.at[e, h * XS_W:(h + 1) * XS_W, :], stg.at[h], sem.at[h]),)

    def compute(weights, n_rows=MOE_TM):
        xp = x_ref[0:n_rows, :]
        x_lo = lax.bitcast_convert_type(xp << 16, F32).astype(BF16)
        x_hi = lax.bitcast_convert_type(xp & jnp.uint32(0xFFFF0000), F32).astype(BF16)
        y = jnp.dot(x_lo, weights(0), preferred_element_type=F32)
        y = y + jnp.dot(x_hi, weights(1), preferred_element_type=F32)
        glu = jnp.minimum(y[:, :D_FF] + bg_ref[...], SWIGLU_LIMIT)
        lin = jnp.clip(y[:, D_FF:] + bl_ref[...], -SWIGLU_LIMIT, SWIGLU_LIMIT)
        o_ref[0:n_rows, :] = (glu * jax.nn.sigmoid(SWIGLU_ALPHA * glu) * (lin + 1.0)).astype(BF16)

    used = i < _blocks_used(sched_ref)
    first = sched_ref[SCHED_FIRST, i] == 1

    @pl.when(used & first)
    def _():
        @pl.when(i == 0)
        def _():
            for h in range(F1_NH):
                for cp in slab_copies(sched_ref[SCHED_BE, 0], h):
                    cp.start(priority=WEIGHT_DMA_PRIORITY)

        for h in range(F1_NH):
            for cp in slab_copies(sched_ref[SCHED_BE, i], h):
                cp.wait()

        def convert(h):
            w = stg[h].astype(BF16)
            wbuf[h] = w
            return w

        compute(convert)

        nxt = sched_ref[SCHED_NXT, i]

        @pl.when(nxt >= 0)
        def _():
            for h in range(F1_NH):
                for cp in slab_copies(nxt, h):
                    cp.start(priority=WEIGHT_DMA_PRIORITY)

    half = sched_ref[SCHED_HALF, i] == 1

    @pl.when(used & jnp.logical_not(first) & jnp.logical_not(half))
    def _():
        compute(lambda h: wbuf[h])

    @pl.when(used & jnp.logical_not(first) & half)
    def _():
        compute(lambda h: wbuf[h], MOE_TM // 2)


def _ffn1(sched, xs, w1, b1_3):
    expert = lambda i, s: s[SCHED_BE, _used_block(i, s)]
    return pl.pallas_call(
        _ffn1_kernel,
        out_shape=jax.ShapeDtypeStruct((MOE_ROWS, D_FF), BF16),
        grid_spec=pltpu.PrefetchScalarGridSpec(
            num_scalar_prefetch=1,
            grid=(MOE_NB,),
            in_specs=[
                pl.BlockSpec((MOE_TM, XS_W), lambda i, s: (_used_block(i, s), 0)),
                pl.BlockSpec((None, 1, D_FF), lambda i, s: (expert(i, s), 0, 0)),
                pl.BlockSpec((None, 1, D_FF), lambda i, s: (expert(i, s), 0, 1)),
                pl.BlockSpec(memory_space=pl.ANY),
            ],
            out_specs=pl.BlockSpec((MOE_TM, D_FF), lambda i, s: (_used_block(i, s), 0)),
            scratch_shapes=[
                pltpu.VMEM((F1_NH, XS_W, 2 * D_FF), F32),
                pltpu.VMEM((F1_NH, XS_W, 2 * D_FF), BF16),
                pltpu.SemaphoreType.DMA((F1_NH,)),
            ],
        ),
        compiler_params=_params(("arbitrary",), vmem=BIG_VMEM_LIMIT),
        name="ffn1",
    )(sched, xs, b1_3, b1_3, w1)


def _ffn2_kernel(sched_ref, a_ref, b_ref, w2_hbm, o_ref, stg, sem):
    j = pl.program_id(0)
    i = pl.program_id(1)
    nj = pl.num_programs(0)

    def tile_copies(e, jj, slot):
        c0 = pl.multiple_of(jj * F2_TN, F2_TN)
        return (pltpu.make_async_copy(w2_hbm.at[e, :, pl.ds(c0, F2_TN)], stg.at[slot], sem.at[slot]),)

    @pl.when(i < _blocks_used(sched_ref))
    def _():
        _stream_expert_weights(j, i, nj, sched_ref, tile_copies, stg, None)
        slot = (j * sched_ref[SCHED_META, 1] + sched_ref[SCHED_SEG, i]) % 2

        def compute(n_rows):
            y = jnp.dot(a_ref[0:n_rows, :], stg[slot].astype(BF16), preferred_element_type=F32) + b_ref[...]
            lo = lax.bitcast_convert_type(y[:, :XS_W].astype(BF16).astype(F32), U32)
            hi = lax.bitcast_convert_type(y[:, XS_W:].astype(BF16).astype(F32), U32)
            o_ref[0:n_rows, :] = (lo >> 16) | (hi & jnp.uint32(0xFFFF0000))

        half = sched_ref[SCHED_HALF, i] == 1

        @pl.when(jnp.logical_not(half))
        def _():
            compute(MOE_TM)

        @pl.when(half)
        def _():
            compute(MOE_TM // 2)


def _ffn2(sched, act, w2, b2_3):
    return pl.pallas_call(
        _ffn2_kernel,
        out_shape=jax.ShapeDtypeStruct((MOE_ROWS, XS_W), U32),
        grid_spec=pltpu.PrefetchScalarGridSpec(
            num_scalar_prefetch=1,
            grid=(D_MODEL // F2_TN, MOE_NB),
            in_specs=[
                pl.BlockSpec((MOE_TM, D_FF), lambda j, i, s: (_used_block(i, s), 0)),
                pl.BlockSpec((None, 1, F2_TN), lambda j, i, s: (s[SCHED_BE, _used_block(i, s)], 0, j)),
                pl.BlockSpec(memory_space=pl.ANY),
            ],
            out_specs=pl.BlockSpec((MOE_TM, XS_W), lambda j, i, s: (_used_block(i, s), j)),
            scratch_shapes=[
                pltpu.VMEM((2, D_FF, F2_TN), F32),
                pltpu.SemaphoreType.DMA((2,)),
            ],
        ),
        compiler_params=_params(("arbitrary", "arbitrary")),
        name="ffn2",
    )(sched, act, b2_3, w2)


CB_TM = 256
CB_RING = 3


def _combine_kernel(dest_ref, x_ref, gate_ref, g2_ref, ys_hbm, o_ref, buf, gate_scr, g2_scr, sem):
    i = pl.program_id(0)
    last = pl.num_programs(0) - 1
    par = i % CB_RING
    ahead = (i + CB_RING - 1) % CB_RING
    n_tiles = XS_W // LANES
    copies_per_tile = 8 * TOP_K // n_tiles
    assert copies_per_tile * n_tiles == 8 * TOP_K

    def row_start(tile, p, u, k):
        d = dest_ref[k * SEQ + tile * CB_TM + u]
        pltpu.make_async_copy(ys_hbm.at[pl.ds(d, 1)], buf.at[p, k, pl.ds(u, 1)], sem.at[p]).start()

    def drain(p):
        for k in range(TOP_K):
            pltpu.make_async_copy(ys_hbm.at[pl.ds(0, CB_TM)], buf.at[p, k], sem.at[p]).wait()

    @pl.when(i == 0)
    def _():
        for t in range(CB_RING - 1):
            def body(u, _, t=t):
                for k in range(TOP_K):
                    row_start(t, t, u, k)
                return 0
            lax.fori_loop(0, CB_TM, body, 0)
        g2_scr[...] = jnp.broadcast_to(g2_ref[...], g2_scr.shape)

    drain(par)

    for k in range(TOP_K):
        gate_scr[k] = jnp.broadcast_to(gate_ref[:, k:k + 1], (CB_TM, LANES))

    nxt = jnp.minimum(i + CB_RING - 1, last)

    def body(r, _):
        r0 = pl.multiple_of(r * 8, 8)
        rows = pl.ds(r0, 8)
        gk = [gate_scr[k, rows, :] for k in range(TOP_K)]
        for c in range(n_tiles):
            cs_lo = slice(c * LANES, (c + 1) * LANES)
            cs_hi = slice(XS_W + c * LANES, XS_W + (c + 1) * LANES)
            y_lo = y_hi = None
            for k in range(TOP_K):
                w = buf[par, k, rows, cs_lo]
                lo = lax.bitcast_convert_type(w << 16, F32) * gk[k]
                hi = lax.bitcast_convert_type(w & jnp.uint32(0xFFFF0000), F32) * gk[k]
                y_lo = lo if y_lo is None else y_lo + lo
                y_hi = hi if y_hi is None else y_hi + hi
            o_ref[rows, cs_lo] = x_ref[rows, cs_lo] + g2_scr[:, cs_lo] * y_lo
            o_ref[rows, cs_hi] = x_ref[rows, cs_hi] + g2_scr[:, cs_hi] * y_hi
            for q in range(copies_per_tile):
                n = c * copies_per_tile + q
                row_start(nxt, ahead, r0 + n // TOP_K, n % TOP_K)
        return 0

    lax.fori_loop(0, CB_TM // 8, body, 0)

    @pl.when(i == last)
    def _():
        drain((i + 1) % CB_RING)
        drain(ahead)


def _combine(dest, x1, gates_t, mod, ys):
    return pl.pallas_call(
        _combine_kernel,
        out_shape=jax.ShapeDtypeStruct((SEQ, D_MODEL), F32),
        grid_spec=pltpu.PrefetchScalarGridSpec(
            num_scalar_prefetch=1,
            grid=(SEQ // CB_TM,),
            in_specs=[
                pl.BlockSpec((CB_TM, D_MODEL), lambda i, d: (i, 0)),
                pl.BlockSpec((CB_TM, TOP_K), lambda i, d: (i, 0)),
                pl.BlockSpec((1, D_MODEL), lambda i, d: (0, MOD_B_G2)),
                pl.BlockSpec(memory_space=pl.ANY),
            ],
            out_specs=pl.BlockSpec((CB_TM, D_MODEL), lambda i, d: (i, 0)),
            scratch_shapes=[
                pltpu.VMEM((CB_RING, TOP_K, CB_TM, XS_W), U32),
                pltpu.VMEM((TOP_K, CB_TM, LANES), F32),
                pltpu.VMEM((8, D_MODEL), F32),
                pltpu.SemaphoreType.DMA((CB_RING,)),
            ],
        ),
        compiler_params=_params(("arbitrary",)),
        name="combine",
    )(dest, x1, gates_t, mod, ys)


def _rope_tables():
    t = jnp.arange(SEQ, dtype=I32)
    row = (t // GRID_W).astype(F32)
    col = (t % GRID_W).astype(F32)
    inv_freq = ROPE_THETA ** (-jnp.arange(0, ROPE_AXIS_DIM, 2, dtype=F32) / ROPE_AXIS_DIM)
    ang_r = inv_freq[:, None] * row[None, :]
    ang_c = inv_freq[:, None] * col[None, :]
    cos_t = jnp.concatenate([jnp.cos(ang_r), jnp.cos(ang_r), jnp.cos(ang_c), jnp.cos(ang_c)], axis=0)
    sin_t = jnp.concatenate([-jnp.sin(ang_r), jnp.sin(ang_r), -jnp.sin(ang_c), jnp.sin(ang_c)], axis=0)
    return cos_t, sin_t


def kernel(x, c, w_mod, b_mod, norm1_g, w_in, q_norm_g, k_norm_g, w_pool, pool_scale, w_out, norm2_g,
           w_router, b_router, w1, b1, w2, b2):
    assert x.shape == (1, SEQ, D_MODEL) and w_mod.shape[0] == 1
    x2 = x[0]
    cos_t, sin_t = _rope_tables()

    c_col = c.reshape(D_MODEL, 1)
    mod_a = _mod(c_col, w_mod[0], b_mod)

    pool_in, qt, k, vt, mod = _inproj(x2, mod_a, norm1_g, w_in[0].astype(BF16), cos_t, sin_t,
                                      q_norm_g.reshape(HEAD_DIM, 1), k_norm_g.reshape(HEAD_DIM, 1),
                                      c.reshape(D_MODEL // LANES, LANES), w_mod[0], b_mod)
    attn = _attention(qt, k, vt)
    pool = _pool(pool_in, w_pool[0].astype(BF16), pool_scale)
    x1 = _outproj(attn, pool, w_out[0], x2, mod)

    wr = w_router[0]
    wr_hi = wr.astype(BF16)
    wr_lo = (wr - wr_hi.astype(F32)).astype(BF16)
    pad = lambda a: jnp.pad(a, ((0, 0), (0, RT_PAD - N_EXPERTS)))
    wcat = jnp.concatenate([pad(wr_hi), pad(wr_lo)], axis=1)
    logits_t = _norm2(x1, norm2_g, mod, wcat, pad(wr_hi), pad(b_router))

    dest, gates, sched, lastblk = _route(logits_t)
    dest = dest.reshape(TOP_K * SEQ)

    xs = _dispatch(dest, lastblk, x1, norm2_g, mod)
    act = _ffn1(sched, xs, w1[0], b1[0].reshape(N_EXPERTS, 1, 2 * D_FF))
    ys = _ffn2(sched, act, w2[0], b2[0].reshape(N_EXPERTS, 1, D_MODEL))
    out = _combine(dest, x1, gates.T, mod, ys)
    return out[None]
```

```python
import functools
import math

import jax
import jax.numpy as jnp
from jax import lax
from jax.experimental import pallas as pl
from jax.experimental.pallas import tpu as pltpu

F32 = jnp.float32
BF16 = jnp.bfloat16
I32 = jnp.int32

D_MODEL = 4096
SEQ = 8192
POOL_WIDTH = 2048
ATTN_WIDTH = 2048
HEAD_DIM = 128
N_HEADS = 16
N_KV_HEADS = 4
GROUP = N_HEADS // N_KV_HEADS
KV_WIDTH = N_KV_HEADS * HEAD_DIM
IN_WIDTH = POOL_WIDTH + ATTN_WIDTH + 2 * KV_WIDTH
POOL_WINDOWS = (2, 4, 8, 16)
POOL_GROUP_WIDTH = POOL_WIDTH // len(POOL_WINDOWS)
GRID_W = 64
ROPE_THETA = 10000.0
ROPE_AXIS_DIM = HEAD_DIM // 2
N_EXPERTS = 32
TOP_K = 4
D_FF = D_MODEL // 4
SWIGLU_ALPHA = 1.702
SWIGLU_LIMIT = 7.0
N_MOD = 6
EPS = 1e-6

LANES = 128
VMEM_LIMIT = 56 * 1024 * 1024
BIG_VMEM_LIMIT = 62 * 1024 * 1024

Q_SCALE = (HEAD_DIM ** -0.5) * math.log2(math.e)

MOE_TM = 256
MOE_NB = SEQ * TOP_K // MOE_TM + N_EXPERTS
MOE_ROWS = MOE_NB * MOE_TM


def _params(sem, vmem=VMEM_LIMIT):
    return pltpu.CompilerParams(dimension_semantics=sem, vmem_limit_bytes=vmem)


MOD_TN = 1024
MOD_KC = 256


def _mod_kernel(c_ref, w_ref, b_ref, o_ref):
    def body(k, acc):
        r = pl.multiple_of(k * MOD_KC, MOD_KC)
        ck = c_ref[pl.ds(r, MOD_KC), :]
        ck = ck * jax.nn.sigmoid(ck)
        p = w_ref[pl.ds(r, MOD_KC), :] * ck
        return acc + p.reshape(MOD_KC // 8, 8, MOD_TN).sum(axis=0)

    acc = lax.fori_loop(0, D_MODEL // MOD_KC, body, jnp.zeros((8, MOD_TN), F32))
    o_ref[...] = acc.sum(axis=0, keepdims=True) + b_ref[...]


MOD_A = 2
MOD_B = N_MOD - MOD_A
MOD_B_G1, MOD_B_SH2, MOD_B_SC2, MOD_B_G2 = range(MOD_B)


def _mod(c_col, w_mod, b_mod):
    n = MOD_A * D_MODEL
    return pl.pallas_call(
        _mod_kernel,
        out_shape=jax.ShapeDtypeStruct((1, n), F32),
        grid=(n // MOD_TN,),
        in_specs=[
            pl.BlockSpec((D_MODEL, 1), lambda j: (0, 0)),
            pl.BlockSpec((D_MODEL, MOD_TN), lambda j: (0, j)),
            pl.BlockSpec((1, MOD_TN), lambda j: (0, j)),
        ],
        out_specs=pl.BlockSpec((1, MOD_TN), lambda j: (0, j)),
        compiler_params=_params(("arbitrary",)),
        name="mod",
    )(c_col, w_mod, b_mod)


IP_TM = 512
IP_TN = 1024
IP_NJ = IN_WIDTH // IP_TN
IP_J_Q = POOL_WIDTH // IP_TN
IP_J_KV = IP_J_Q + ATTN_WIDTH // IP_TN
NORM_ROWS = 16
BF16_SUBLANES = 16
KA_W = 2 * HEAD_DIM
VA_H = HEAD_DIM + BF16_SUBLANES


def _prep_modulation(g_ref, sc_ref, sh_ref, a_scr, s_scr):
    a_scr[...] = jnp.broadcast_to(g_ref[...] * (1.0 + sc_ref[...]), a_scr.shape)
    s_scr[...] = jnp.broadcast_to(sh_ref[...], s_scr.shape)


def _row_rms(x_ref, rs_scr, n_rows):
    def body(r, _):
        r0 = pl.multiple_of(r * NORM_ROWS, NORM_ROWS)
        width = x_ref.shape[1]
        parts = []
        for c in range(width // LANES):
            xc = x_ref[pl.ds(r0, NORM_ROWS), c * LANES:(c + 1) * LANES]
            parts.append(xc * xc)
        while len(parts) > 1:
            parts = [parts[p] + parts[p + 1] for p in range(0, len(parts), 2)]
        rs_scr[pl.ds(r0, NORM_ROWS), :] = parts[0]
        return 0
    lax.fori_loop(0, n_rows // NORM_ROWS, body, 0)
    ms = jnp.sum(rs_scr[...], axis=-1, keepdims=True) * (1.0 / x_ref.shape[1])
    rs_scr[...] = jnp.broadcast_to(lax.rsqrt(ms + EPS), rs_scr.shape)


def _normed_tile(x_ref, rs_scr, a_scr, s_scr, r0, c):
    cs = slice(c * LANES, (c + 1) * LANES)
    return x_ref[pl.ds(r0, NORM_ROWS), cs] * rs_scr[pl.ds(r0, NORM_ROWS), :] * a_scr[:, cs] + s_scr[:, cs]


def _norm_rope_t(xt, g_col, cos_t, sin_t):
    ms = jnp.mean(xt * xt, axis=0, keepdims=True)
    y = xt * lax.rsqrt(ms + EPS) * g_col
    q = ROPE_AXIS_DIM // 2
    partner = jnp.concatenate([y[q:2 * q], y[0:q], y[3 * q:4 * q], y[2 * q:3 * q]], axis=0)
    return y * cos_t + partner * sin_t


MB_TN = 256
MB_STEPS = MOD_B * D_MODEL // MB_TN
MB_KC = 256


def _tree_sum(parts):
    while len(parts) > 1:
        parts = [parts[p] + parts[p + 1] for p in range(0, len(parts), 2)]
    return parts[0]


def _later_modulation(t, c_ref, wm_ref, bm_ref, mb_ref, cb_scr):
    @pl.when(t == 0)
    def _():
        for r in range(D_MODEL // LANES):
            cv = c_ref[r:r + 1, :]
            cb_scr[r * LANES:(r + 1) * LANES, :] = jnp.broadcast_to(cv * jax.nn.sigmoid(cv), (LANES, LANES)).T

    @pl.when(t < MB_STEPS)
    def _():
        def body(k, accs):
            r0 = pl.multiple_of(k * MB_KC, MB_KC)
            cb = cb_scr[pl.ds(r0, MB_KC), :]
            out = []
            for n in range(MB_TN // LANES):
                p = wm_ref[pl.ds(r0, MB_KC), n * LANES:(n + 1) * LANES] * cb
                out.append(accs[n] + _tree_sum([p[g * 8:(g + 1) * 8] for g in range(MB_KC // 8)]))
            return tuple(out)

        zero = jnp.zeros((8, LANES), F32)
        accs = lax.fori_loop(0, D_MODEL // MB_KC, body, (zero,) * (MB_TN // LANES))
        for n in range(MB_TN // LANES):
            cs = slice(n * LANES, (n + 1) * LANES)
            mb_ref[:, cs] = accs[n].sum(axis=0, keepdims=True) + bm_ref[:, cs]


def _inproj_kernel(x_ref, g_ref, sc_ref, sh_ref, w_ref, cos_ref, sin_ref, qg_ref, kg_ref, c_ref, wm_ref, bm_ref,
                   pool_ref, qt_ref, k_ref, vt_ref, mb_ref, h_scr, rs_scr, a_scr, s_scr, cb_scr):
    j = pl.program_id(1)
    _later_modulation(pl.program_id(0) * IP_NJ + j, c_ref, wm_ref, bm_ref, mb_ref, cb_scr)

    @pl.when(j == 0)
    def _():
        _prep_modulation(g_ref, sc_ref, sh_ref, a_scr, s_scr)
        _row_rms(x_ref, rs_scr, IP_TM)

        def body(r, _):
            r0 = pl.multiple_of(r * NORM_ROWS, NORM_ROWS)
            for c in range(D_MODEL // LANES):
                h_scr[pl.ds(r0, NORM_ROWS), c * LANES:(c + 1) * LANES] = _normed_tile(
                    x_ref, rs_scr, a_scr, s_scr, r0, c).astype(BF16)
            return 0
        lax.fori_loop(0, IP_TM // NORM_ROWS, body, 0)

    acc = jnp.dot(h_scr[...], w_ref[...], preferred_element_type=F32)

    @pl.when(j < IP_J_Q)
    def _():
        pool_ref[...] = acc.astype(BF16)

    @pl.when((j >= IP_J_Q) & (j < IP_J_KV))
    def _():
        for hh in range(IP_TN // HEAD_DIM):
            sl = slice(hh * HEAD_DIM, (hh + 1) * HEAD_DIM)
            r = _norm_rope_t(acc[:, sl].T, qg_ref[...], cos_ref[...], sin_ref[...]) * Q_SCALE
            qt_ref[sl, :] = r.astype(BF16)

    @pl.when(j == IP_J_KV)
    def _():
        lane = lax.broadcasted_iota(I32, (IP_TM, HEAD_DIM), 1)
        one_col = jnp.where(lane == 0, 1.0, 0.0).astype(BF16)
        for hh in range(N_KV_HEADS):
            sl = slice(hh * HEAD_DIM, (hh + 1) * HEAD_DIM)
            k_ref[:, hh * KA_W:hh * KA_W + HEAD_DIM] = _norm_rope_t(
                acc[:, sl].T, kg_ref[...], cos_ref[...], sin_ref[...]).T.astype(BF16)
            k_ref[:, hh * KA_W + HEAD_DIM:(hh + 1) * KA_W] = one_col
        for hh in range(N_KV_HEADS):
            sl = slice(KV_WIDTH + hh * HEAD_DIM, KV_WIDTH + (hh + 1) * HEAD_DIM)
            vt_ref[hh * VA_H:hh * VA_H + HEAD_DIM, :] = acc[:, sl].T.astype(BF16)
            vt_ref[hh * VA_H + HEAD_DIM:(hh + 1) * VA_H, :] = jnp.ones((VA_H - HEAD_DIM, IP_TM), BF16)


def _inproj(x2, mod, norm1_g, w_in_b, cos_t, sin_t, qg, kg, c_col, w_mod, b_mod):
    row = lambda n: pl.BlockSpec((1, D_MODEL), lambda i, j, n=n: (0, n))
    mb_tile = lambda i, j: jnp.minimum(i * IP_NJ + j, MB_STEPS - 1)
    mb_first = MOD_A * D_MODEL // MB_TN
    return pl.pallas_call(
        _inproj_kernel,
        out_shape=(
            jax.ShapeDtypeStruct((SEQ, POOL_WIDTH), BF16),
            jax.ShapeDtypeStruct((ATTN_WIDTH, SEQ), BF16),
            jax.ShapeDtypeStruct((SEQ, N_KV_HEADS * KA_W), BF16),
            jax.ShapeDtypeStruct((N_KV_HEADS * VA_H, SEQ), BF16),
            jax.ShapeDtypeStruct((1, MOD_B * D_MODEL), F32),
        ),
        grid=(SEQ // IP_TM, IP_NJ),
        in_specs=[
            pl.BlockSpec((IP_TM, D_MODEL), lambda i, j: (i, 0)),
            pl.BlockSpec((1, D_MODEL), lambda i, j: (0, 0)),
            row(1), row(0),
            pl.BlockSpec((D_MODEL, IP_TN), lambda i, j: (0, j)),
            pl.BlockSpec((HEAD_DIM, IP_TM), lambda i, j: (0, i)),
            pl.BlockSpec((HEAD_DIM, IP_TM), lambda i, j: (0, i)),
            pl.BlockSpec((HEAD_DIM, 1), lambda i, j: (0, 0)),
            pl.BlockSpec((HEAD_DIM, 1), lambda i, j: (0, 0)),
            pl.BlockSpec((D_MODEL // LANES, LANES), lambda i, j: (0, 0)),
            pl.BlockSpec((D_MODEL, MB_TN), lambda i, j: (0, mb_first + mb_tile(i, j))),
            pl.BlockSpec((1, MB_TN), lambda i, j: (0, mb_first + mb_tile(i, j))),
        ],
        out_specs=(
            pl.BlockSpec((IP_TM, IP_TN), lambda i, j: (i, jnp.minimum(j, IP_J_Q - 1))),
            pl.BlockSpec((IP_TN, IP_TM), lambda i, j: (jnp.clip(j - IP_J_Q, 0, IP_J_KV - IP_J_Q - 1), i)),
            pl.BlockSpec((IP_TM, N_KV_HEADS * KA_W), lambda i, j: (i, 0)),
            pl.BlockSpec((N_KV_HEADS * VA_H, IP_TM), lambda i, j: (0, i)),
            pl.BlockSpec((1, MB_TN), lambda i, j: (0, mb_tile(i, j))),
        ),
        scratch_shapes=[
            pltpu.VMEM((IP_TM, D_MODEL), BF16),
            pltpu.VMEM((IP_TM, LANES), F32),
            pltpu.VMEM((NORM_ROWS, D_MODEL), F32),
            pltpu.VMEM((NORM_ROWS, D_MODEL), F32),
            pltpu.VMEM((D_MODEL, LANES), F32),
        ],
        compiler_params=_params(("arbitrary", "arbitrary"), vmem=BIG_VMEM_LIMIT),
        name="inproj",
    )(x2, norm1_g, mod, mod, w_in_b, cos_t, sin_t, qg, kg, c_col, w_mod, b_mod)


AT_TQ = 1024
AT_TK = 8192
AT_TK_ONLINE = 512
SHIFT_LIMIT = 60.0


def _attn_kernel(qt_ref, k_ref, vt_ref, o_ref, qa_scr, p_scr, kmax_scr):
    h = pl.program_id(0)
    i = pl.program_id(1)

    @pl.when((i == 0) & (h % GROUP == 0))
    def _():
        def body(c, mx):
            c0 = pl.multiple_of(c * AT_TK, AT_TK)
            kc = k_ref[pl.ds(c0, AT_TK), :HEAD_DIM].astype(F32)
            n2 = (kc * kc).sum(axis=1, keepdims=True)
            return jnp.maximum(mx, n2.max(axis=0, keepdims=True))
        mx = lax.fori_loop(0, SEQ // AT_TK, body, jnp.zeros((1, 1), F32))
        kmax_scr[...] = jnp.broadcast_to(jnp.sqrt(mx), kmax_scr.shape)

    q = qt_ref[...].astype(F32)
    bound = jnp.sqrt((q * q).sum(axis=0, keepdims=True)) * kmax_scr[0:1, 0:1] * 1.01
    fast = jnp.max(bound) <= SHIFT_LIMIT

    @pl.when(fast)
    def _():
        qa_scr[0:HEAD_DIM, :] = qt_ref[...]
        row = lax.broadcasted_iota(I32, (KA_W - HEAD_DIM, AT_TQ), 0)
        qa_scr[HEAD_DIM:, :] = jnp.where(row == 0, -bound, 0.0).astype(BF16)

        def body(c, _):
            c0 = pl.multiple_of(c * AT_TK, AT_TK)
            s = jnp.dot(k_ref[pl.ds(c0, AT_TK), :], qa_scr[...], preferred_element_type=F32)
            p_scr[pl.ds(c0, AT_TK), :] = jnp.exp2(s).astype(BF16)
            return 0

        lax.fori_loop(0, SEQ // AT_TK, body, 0)
        o = jnp.dot(vt_ref[...], p_scr[...], preferred_element_type=F32)
        o_ref[...] = (o[:HEAD_DIM] * (1.0 / o[HEAD_DIM:HEAD_DIM + 1])).T.astype(BF16)

    @pl.when(jnp.logical_not(fast))
    def _():
        qt = qt_ref[...]

        def chunk(c, carry):
            m, l, acc = carry
            c0 = pl.multiple_of(c * AT_TK_ONLINE, AT_TK_ONLINE)
            s = jnp.dot(k_ref[pl.ds(c0, AT_TK_ONLINE), :HEAD_DIM], qt, preferred_element_type=F32)
            m_new = jnp.maximum(m, s.max(axis=0, keepdims=True))
            alpha = jnp.exp2(m - m_new)
            p = jnp.exp2(s - m_new)
            l = alpha * l + p.sum(axis=0, keepdims=True)
            pv = jnp.dot(vt_ref[:HEAD_DIM, pl.ds(c0, AT_TK_ONLINE)], p.astype(BF16), preferred_element_type=F32)
            return m_new, l, alpha * acc + pv

        init = (jnp.full((1, AT_TQ), -jnp.inf, F32), jnp.zeros((1, AT_TQ), F32),
                jnp.zeros((HEAD_DIM, AT_TQ), F32))
        _, l, acc = lax.fori_loop(0, SEQ // AT_TK_ONLINE, chunk, init)
        o_ref[...] = (acc * (1.0 / l)).T.astype(BF16)


def _attention(qt, k, vt):
    return pl.pallas_call(
        _attn_kernel,
        out_shape=jax.ShapeDtypeStruct((SEQ, ATTN_WIDTH), BF16),
        grid=(N_HEADS, SEQ // AT_TQ),
        in_specs=[
            pl.BlockSpec((HEAD_DIM, AT_TQ), lambda h, i: (h, i)),
            pl.BlockSpec((SEQ, KA_W), lambda h, i: (0, h // GROUP)),
            pl.BlockSpec((VA_H, SEQ), lambda h, i: (h // GROUP, 0)),
        ],
        out_specs=pl.BlockSpec((AT_TQ, HEAD_DIM), lambda h, i: (i, h)),
        scratch_shapes=[
            pltpu.VMEM((KA_W, AT_TQ), BF16),
            pltpu.VMEM((SEQ, AT_TQ), BF16),
            pltpu.VMEM((8, LANES), F32),
        ],
        compiler_params=_params(("arbitrary", "arbitrary")),
        name="attn",
    )(qt, k, vt)


PL_TM = 256
PL_HALO = 16


def _pool_kernel(prev_ref, main_ref, next_ref, wp_ref, scale_ref, o_ref, buf, band):
    i = pl.program_id(0)
    last = pl.num_programs(0) - 1

    @pl.when(i == 0)
    def _():
        tt = lax.broadcasted_iota(I32, (PL_TM, PL_TM + 2 * PL_HALO), 0)
        ss = lax.broadcasted_iota(I32, (PL_TM, PL_TM + 2 * PL_HALO), 1)
        off = ss - PL_HALO - tt
        for gi, w in enumerate(POOL_WINDOWS):
            band[gi] = jnp.where((off >= -(w // 2)) & (off <= w // 2 - 1), 1.0, 0.0).astype(BF16)

    buf[0:PL_HALO, :] = jnp.where(i == 0, jnp.zeros_like(prev_ref[...]), prev_ref[...])
    buf[PL_HALO:PL_HALO + PL_TM, :] = main_ref[...]
    buf[PL_HALO + PL_TM:, :] = jnp.where(i == last, jnp.zeros_like(next_ref[...]), next_ref[...])
    t = i * PL_TM + lax.broadcasted_iota(I32, (PL_TM, 1), 0)
    for gi, w in enumerate(POOL_WINDOWS):
        cols = slice(gi * POOL_GROUP_WIDTH, (gi + 1) * POOL_GROUP_WIDTH)
        win = jnp.dot(band[gi], buf[:, cols], preferred_element_type=F32)
        lo = jnp.maximum(t - w // 2, 0)
        hi = jnp.minimum(t + w // 2 - 1, SEQ - 1)
        cnt = (hi - lo + 1).astype(F32)
        dlt = win / cnt - main_ref[:, cols].astype(F32)
        y = jnp.dot(dlt.astype(BF16), wp_ref[gi], preferred_element_type=F32)
        o_ref[:, cols] = (y * scale_ref[:, cols]).astype(BF16)


def _pool(pool_in, w_pool_b, pool_scale):
    nh = PL_TM // PL_HALO
    n_halo_blocks = SEQ // PL_HALO
    return pl.pallas_call(
        _pool_kernel,
        out_shape=jax.ShapeDtypeStruct((SEQ, POOL_WIDTH), BF16),
        grid=(SEQ // PL_TM,),
        in_specs=[
            pl.BlockSpec((PL_HALO, POOL_WIDTH), lambda i: (jnp.maximum(i * nh - 1, 0), 0)),
            pl.BlockSpec((PL_TM, POOL_WIDTH), lambda i: (i, 0)),
            pl.BlockSpec((PL_HALO, POOL_WIDTH), lambda i: (jnp.minimum((i + 1) * nh, n_halo_blocks - 1), 0)),
            pl.BlockSpec((len(POOL_WINDOWS), POOL_GROUP_WIDTH, POOL_GROUP_WIDTH), lambda i: (0, 0, 0)),
            pl.BlockSpec((1, POOL_WIDTH), lambda i: (0, 0)),
        ],
        out_specs=pl.BlockSpec((PL_TM, POOL_WIDTH), lambda i: (i, 0)),
        scratch_shapes=[
            pltpu.VMEM((PL_TM + 2 * PL_HALO, POOL_WIDTH), BF16),
            pltpu.VMEM((len(POOL_WINDOWS), PL_TM, PL_TM + 2 * PL_HALO), BF16),
        ],
        compiler_params=_params(("arbitrary",)),
        name="pool",
    )(pool_in, pool_in, pool_in, w_pool_b, pool_scale)


OP_TM = 512
OP_TN = 1024


def _outproj_kernel(a_ref, p_ref, wa_ref, wp_ref, x_ref, g_ref, o_ref):
    acc = jnp.dot(a_ref[...], wa_ref[...].astype(BF16), preferred_element_type=F32)
    acc = acc + jnp.dot(p_ref[...], wp_ref[...].astype(BF16), preferred_element_type=F32)
    o_ref[...] = x_ref[...] + g_ref[...] * acc


def _outproj(attn, pool, w_out, x2, mod):
    return pl.pallas_call(
        _outproj_kernel,
        out_shape=jax.ShapeDtypeStruct((SEQ, D_MODEL), F32),
        grid=(D_MODEL // OP_TN, SEQ // OP_TM),
        in_specs=[
            pl.BlockSpec((OP_TM, ATTN_WIDTH), lambda j, i: (i, 0)),
            pl.BlockSpec((OP_TM, POOL_WIDTH), lambda j, i: (i, 0)),
            pl.BlockSpec((ATTN_WIDTH, OP_TN), lambda j, i: (0, j)),
            pl.BlockSpec((POOL_WIDTH, OP_TN), lambda j, i: (1, j)),
            pl.BlockSpec((OP_TM, OP_TN), lambda j, i: (i, j)),
            pl.BlockSpec((1, OP_TN), lambda j, i: (0, MOD_B_G1 * (D_MODEL // OP_TN) + j)),
        ],
        out_specs=pl.BlockSpec((OP_TM, OP_TN), lambda j, i: (i, j)),
        compiler_params=_params(("arbitrary", "arbitrary")),
        name="outproj",
    )(attn, pool, w_out, w_out, x2, mod)


N2_TM = 512
RT_PAD = LANES


def _norm2_kernel(x_ref, g_ref, sc_ref, sh_ref, wcat_ref, whi_ref, b_ref, lt_ref, hi_scr, lo_scr,
                  rs_scr, a_scr, s_scr):
    @pl.when(pl.program_id(0) == 0)
    def _():
        _prep_modulation(g_ref, sc_ref, sh_ref, a_scr, s_scr)

    _row_rms(x_ref, rs_scr, N2_TM)

    def body(r, _):
        r0 = pl.multiple_of(r * NORM_ROWS, NORM_ROWS)
        for c in range(D_MODEL // LANES):
            cs = slice(c * LANES, (c + 1) * LANES)
            h = _normed_tile(x_ref, rs_scr, a_scr, s_scr, r0, c)
            hi = h.astype(BF16)
            hi_scr[pl.ds(r0, NORM_ROWS), cs] = hi
            lo_scr[pl.ds(r0, NORM_ROWS), cs] = (h - hi.astype(F32)).astype(BF16)
        return 0
    lax.fori_loop(0, N2_TM // NORM_ROWS, body, 0)
    a = jnp.dot(hi_scr[...], wcat_ref[...], preferred_element_type=F32)
    b = jnp.dot(lo_scr[...], whi_ref[...], preferred_element_type=F32)
    logits = a[:, :RT_PAD] + a[:, RT_PAD:] + b + b_ref[...]
    lt_ref[...] = logits.T[:N_EXPERTS, :]


def _norm2(x1, norm2_g, mod, wcat, whi, b_pad):
    row = lambda n: pl.BlockSpec((1, D_MODEL), lambda i, n=n: (0, n))
    return pl.pallas_call(
        _norm2_kernel,
        out_shape=jax.ShapeDtypeStruct((N_EXPERTS, SEQ), F32),
        grid=(SEQ // N2_TM,),
        in_specs=[
            pl.BlockSpec((N2_TM, D_MODEL), lambda i: (i, 0)),
            pl.BlockSpec((1, D_MODEL), lambda i: (0, 0)),
            row(MOD_B_SC2), row(MOD_B_SH2),
            pl.BlockSpec((D_MODEL, 2 * RT_PAD), lambda i: (0, 0)),
            pl.BlockSpec((D_MODEL, RT_PAD), lambda i: (0, 0)),
            pl.BlockSpec((1, RT_PAD), lambda i: (0, 0)),
        ],
        out_specs=pl.BlockSpec((N_EXPERTS, N2_TM), lambda i: (0, i)),
        scratch_shapes=[
            pltpu.VMEM((N2_TM, D_MODEL), BF16),
            pltpu.VMEM((N2_TM, D_MODEL), BF16),
            pltpu.VMEM((N2_TM, LANES), F32),
            pltpu.VMEM((NORM_ROWS, D_MODEL), F32),
            pltpu.VMEM((NORM_ROWS, D_MODEL), F32),
        ],
        compiler_params=_params(("arbitrary",)),
        name="norm2",
    )(x1, norm2_g, mod, mod, wcat, whi, b_pad)


RT_CH = 1024
RT_SB = 256


SCHED_W = RT_SB
SCHED_BE, SCHED_FIRST, SCHED_SEG, SCHED_NXT, SCHED_META, SCHED_HALF = range(6)


def _route_kernel(lt_ref, dest_ref, gate_ref, sched_ref, lastblk_ref, idx_scr, rank_scr):
    e_col = lax.broadcasted_iota(I32, (N_EXPERTS, RT_CH), 0).astype(F32)
    tri = (lax.broadcasted_iota(I32, (RT_SB, RT_SB), 0) < lax.broadcasted_iota(I32, (RT_SB, RT_SB), 1)).astype(BF16)
    carry = jnp.zeros((N_EXPERTS, 1), F32)
    for c in range(SEQ // RT_CH):
        cs = slice(c * RT_CH, (c + 1) * RT_CH)
        work = lt_ref[:, cs]
        vals = []
        mask = jnp.zeros((N_EXPERTS, RT_CH), F32)
        for k in range(TOP_K):
            m = work.max(axis=0, keepdims=True)
            idx = jnp.where(work == m, e_col, float(N_EXPERTS)).min(axis=0, keepdims=True)
            sel = e_col == idx
            vals.append(m)
            idx_scr[k:k + 1, cs] = idx
            mask = jnp.where(sel, 1.0, mask)
            work = jnp.where(sel, -jnp.inf, work)
        ex = [jnp.exp(v - vals[0]) for v in vals]
        den = ex[0] + ex[1] + ex[2] + ex[3]
        for k in range(TOP_K):
            gate_ref[k:k + 1, cs] = ex[k] / den
        for b in range(RT_CH // RT_SB):
            blk = mask[:, b * RT_SB:(b + 1) * RT_SB]
            pref = jnp.dot(blk.astype(BF16), tri, preferred_element_type=F32)
            rank_scr[:, c * RT_CH + b * RT_SB:c * RT_CH + (b + 1) * RT_SB] = pref + carry
            carry = carry + blk.sum(axis=1, keepdims=True)
    nblk = jnp.floor((carry + (MOE_TM - 1)) * (1.0 / MOE_TM))
    nblk_b = jnp.broadcast_to(nblk, (N_EXPERTS, LANES))
    lower = (lax.broadcasted_iota(I32, (N_EXPERTS, N_EXPERTS), 1) < lax.broadcasted_iota(I32, (N_EXPERTS, N_EXPERTS), 0)).astype(BF16)
    start_blk = jnp.dot(lower, nblk_b.astype(BF16), preferred_element_type=F32)
    start = start_blk[:, 0:1] * float(MOE_TM)

    end_blk = start_blk[:, 0:1] + nblk
    lastblk_ref[...] = jnp.broadcast_to(jnp.where(nblk > 0, end_blk - 1.0, -1.0), (N_EXPERTS, LANES)).astype(I32)
    e_blk = lax.broadcasted_iota(I32, (N_EXPERTS, SCHED_W), 0).astype(F32)
    b_blk = lax.broadcasted_iota(I32, (N_EXPERTS, SCHED_W), 1).astype(F32)
    lane = b_blk[0:1]
    n_used = jnp.sum(nblk, axis=0, keepdims=True)
    be = jnp.minimum(jnp.sum(jnp.where(end_blk <= b_blk, 1.0, 0.0), axis=0, keepdims=True), N_EXPERTS - 1.0)
    prev = jnp.where(lane == 0, -1.0, pltpu.roll(be, 1, 1))
    first = jnp.where((lane < n_used) & (be != prev), 1.0, 0.0)
    excl = jnp.dot(jnp.broadcast_to(first, (8, SCHED_W)).astype(BF16), tri, preferred_element_type=F32)[0:1]
    seg = excl + first - 1.0
    nxt = jnp.min(jnp.where((e_blk > be) & (nblk > 0), e_blk, float(N_EXPERTS)), axis=0, keepdims=True)
    nxt = jnp.where(nxt >= N_EXPERTS, -1.0, nxt)
    n_seg = jnp.sum(first, axis=1, keepdims=True)
    meta = jnp.where(lane == 0, n_used, jnp.where(lane == 1, n_seg, 0.0))
    rem = carry - (nblk - 1.0) * float(MOE_TM)
    half_e = (nblk > 0) & (rem <= float(MOE_TM // 2))
    half = jnp.sum(jnp.where((b_blk == end_blk - 1.0) & half_e, 1.0, 0.0), axis=0, keepdims=True)
    for r, v in enumerate((be, first, seg, nxt, meta, half)):
        sched_ref[r:r + 1, :] = v.astype(I32)
    sched_ref[6:8, :] = jnp.zeros((2, SCHED_W), I32)
    for c in range(SEQ // RT_CH):
        cs = slice(c * RT_CH, (c + 1) * RT_CH)
        slot = rank_scr[:, cs] + start
        for k in range(TOP_K):
            sel = e_col == idx_scr[k:k + 1, cs]
            dest_ref[k:k + 1, cs] = jnp.where(sel, slot, 0.0).sum(axis=0, keepdims=True).astype(I32)


def _route(logits_t):
    return pl.pallas_call(
        _route_kernel,
        out_shape=(
            jax.ShapeDtypeStruct((TOP_K, SEQ), I32),
            jax.ShapeDtypeStruct((TOP_K, SEQ), F32),
            jax.ShapeDtypeStruct((8, SCHED_W), I32),
            jax.ShapeDtypeStruct((N_EXPERTS, LANES), I32),
        ),
        scratch_shapes=[pltpu.VMEM((8, SEQ), F32), pltpu.VMEM((N_EXPERTS, SEQ), F32)],
        compiler_params=pltpu.CompilerParams(vmem_limit_bytes=VMEM_LIMIT),
        name="route",
    )(logits_t)


DP_TM = 512
XS_W = D_MODEL // 2
U32 = jnp.uint32


def _dispatch_kernel(dest_ref, lastblk_ref, x_ref, g_ref, sc_ref, sh_ref, xs_hbm, pk, zero_buf, rs_scr, a_scr,
                     s_scr, zsem, sem):
    i = pl.program_id(0)
    par = i % 2

    @pl.when(i == 0)
    def _():
        _prep_modulation(g_ref, sc_ref, sh_ref, a_scr, s_scr)
        zero_buf[...] = jnp.zeros_like(zero_buf)

        def zcopy(e):
            b = jnp.maximum(lastblk_ref[e, 0], 0)
            return pltpu.make_async_copy(zero_buf, xs_hbm.at[pl.ds(pl.multiple_of(b * MOE_TM, MOE_TM), MOE_TM)], zsem)

        def zstart(e, _):
            @pl.when(lastblk_ref[e, 0] >= 0)
            def _():
                zcopy(e).start()
            return 0

        def zwait(e, _):
            @pl.when(lastblk_ref[e, 0] >= 0)
            def _():
                zcopy(e).wait()
            return 0

        lax.fori_loop(0, N_EXPERTS, zstart, 0)
        lax.fori_loop(0, N_EXPERTS, zwait, 0)

    _row_rms(x_ref, rs_scr, DP_TM)

    n_tiles = XS_W // LANES
    assert n_tiles == NORM_ROWS

    def pack_tile(r0, c):
        lo = _normed_tile(x_ref, rs_scr, a_scr, s_scr, r0, c)
        hi = _normed_tile(x_ref, rs_scr, a_scr, s_scr, r0, c + n_tiles)
        lo = lax.bitcast_convert_type(lo.astype(BF16).astype(F32), U32)
        hi = lax.bitcast_convert_type(hi.astype(BF16).astype(F32), U32)
        pk[par, pl.ds(r0, NORM_ROWS), c * LANES:(c + 1) * LANES] = (lo >> 16) | (hi & jnp.uint32(0xFFFF0000))

    def issue_token(u):
        for k in range(TOP_K):
            d = dest_ref[k * SEQ + i * DP_TM + u]
            pltpu.make_async_copy(pk.at[par, pl.ds(u, 1)], xs_hbm.at[pl.ds(d, 1)], sem.at[par]).start(priority=k % 2)

    for c in range(n_tiles):
        pack_tile(0, c)

    def group(g, _):
        r0 = pl.multiple_of(g * NORM_ROWS, NORM_ROWS)
        for u in range(NORM_ROWS):
            pack_tile(r0, u)
            issue_token(r0 - NORM_ROWS + u)
        return 0

    lax.fori_loop(1, DP_TM // NORM_ROWS, group, 0)

    def tail(u, _):
        issue_token(DP_TM - NORM_ROWS + u)
        return 0

    lax.fori_loop(0, NORM_ROWS, tail, 0)

    def drain(p):
        for _ in range(TOP_K):
            pltpu.make_async_copy(pk.at[p], xs_hbm.at[pl.ds(0, DP_TM)], sem.at[p]).wait()

    @pl.when(i > 0)
    def _():
        drain(1 - par)

    @pl.when(i == pl.num_programs(0) - 1)
    def _():
        drain(par)


def _dispatch(dest, lastblk, x1, norm2_g, mod):
    row = lambda n: pl.BlockSpec((1, D_MODEL), lambda i, d, lb, n=n: (0, n))
    return pl.pallas_call(
        _dispatch_kernel,
        out_shape=jax.ShapeDtypeStruct((MOE_ROWS, XS_W), U32),
        grid_spec=pltpu.PrefetchScalarGridSpec(
            num_scalar_prefetch=2,
            grid=(SEQ // DP_TM,),
            in_specs=[
                pl.BlockSpec((DP_TM, D_MODEL), lambda i, d, lb: (i, 0)),
                pl.BlockSpec((1, D_MODEL), lambda i, d, lb: (0, 0)),
                row(MOD_B_SC2), row(MOD_B_SH2),
            ],
            out_specs=pl.BlockSpec(memory_space=pl.ANY),
            scratch_shapes=[
                pltpu.VMEM((2, DP_TM, XS_W), U32),
                pltpu.VMEM((MOE_TM, XS_W), U32),
                pltpu.VMEM((DP_TM, LANES), F32),
                pltpu.VMEM((NORM_ROWS, D_MODEL), F32),
                pltpu.VMEM((NORM_ROWS, D_MODEL), F32),
                pltpu.SemaphoreType.DMA,
                pltpu.SemaphoreType.DMA((2,)),
            ],
        ),
        compiler_params=_params(("arbitrary",)),
        name="dispatch",
    )(dest, lastblk, x1, norm2_g, mod, mod)


F2_TN = 4096
assert F2_TN == D_MODEL


CAST_ROWS = 128
WEIGHT_DMA_PRIORITY = 1


def _blocks_used(sched_ref):
    return sched_ref[SCHED_META, 0]


def _used_block(i, sched_ref):
    return jnp.minimum(i, _blocks_used(sched_ref) - 1)


def _stream_expert_weights(j, i, nj, sched_ref, tile_copies, stg, wbuf):
    @pl.when(sched_ref[SCHED_FIRST, i] == 1)
    def _():
        seq = j * sched_ref[SCHED_META, 1] + sched_ref[SCHED_SEG, i]
        slot = seq % 2

        @pl.when(seq == 0)
        def _():
            for cp in tile_copies(sched_ref[SCHED_BE, i], j, slot):
                cp.start(priority=WEIGHT_DMA_PRIORITY)

        for cp in tile_copies(sched_ref[SCHED_BE, i], j, slot):
            cp.wait()

        nxt = sched_ref[SCHED_NXT, i]

        @pl.when(nxt >= 0)
        def _():
            for cp in tile_copies(nxt, j, 1 - slot):
                cp.start(priority=WEIGHT_DMA_PRIORITY)

        @pl.when((nxt < 0) & (j + 1 < nj))
        def _():
            for cp in tile_copies(sched_ref[SCHED_BE, 0], j + 1, 1 - slot):
                cp.start(priority=WEIGHT_DMA_PRIORITY)

        if wbuf is not None:
            def cast(r, _):
                r0 = pl.multiple_of(r * CAST_ROWS, CAST_ROWS)
                wbuf[pl.ds(r0, CAST_ROWS), :] = stg[slot, pl.ds(r0, CAST_ROWS), :].astype(BF16)
                return 0

            lax.fori_loop(0, wbuf.shape[0] // CAST_ROWS, cast, 0)


F1_NH = D_MODEL // XS_W


def _ffn1_kernel(sched_ref, x_ref, bg_ref, bl_ref, w1_hbm, o_ref, stg, wbuf, sem):
    i = pl.program_id(0)

    def slab_copies(e, h):
        return (pltpu.make_async_copy(w1_hbm.at[e, h * XS_W:(h + 1) * XS_W, :], stg.at[h], sem.at[h]),)

    def compute(weights, n_rows=MOE_TM):
        xp = x_ref[0:n_rows, :]
        x_lo = lax.bitcast_convert_type(xp << 16, F32).astype(BF16)
        x_hi = lax.bitcast_convert_type(xp & jnp.uint32(0xFFFF0000), F32).astype(BF16)
        y = jnp.dot(x_lo, weights(0), preferred_element_type=F32)
        y = y + jnp.dot(x_hi, weights(1), preferred_element_type=F32)
        glu = jnp.minimum(y[:, :D_FF] + bg_ref[...], SWIGLU_LIMIT)
        lin = jnp.clip(y[:, D_FF:] + bl_ref[...], -SWIGLU_LIMIT, SWIGLU_LIMIT)
        o_ref[0:n_rows, :] = (glu * jax.nn.sigmoid(SWIGLU_ALPHA * glu) * (lin + 1.0)).astype(BF16)

    used = i < _blocks_used(sched_ref)
    first = sched_ref[SCHED_FIRST, i] == 1

    @pl.when(used & first)
    def _():
        @pl.when(i == 0)
        def _():
            for h in range(F1_NH):
                for cp in slab_copies(sched_ref[SCHED_BE, 0], h):
                    cp.start(priority=WEIGHT_DMA_PRIORITY)

        for h in range(F1_NH):
            for cp in slab_copies(sched_ref[SCHED_BE, i], h):
                cp.wait()

        def convert(h):
            w = stg[h].astype(BF16)
            wbuf[h] = w
            return w

        compute(convert)

        nxt = sched_ref[SCHED_NXT, i]

        @pl.when(nxt >= 0)
        def _():
            for h in range(F1_NH):
                for cp in slab_copies(nxt, h):
                    cp.start(priority=WEIGHT_DMA_PRIORITY)

    half = sched_ref[SCHED_HALF, i] == 1

    @pl.when(used & jnp.logical_not(first) & jnp.logical_not(half))
    def _():
        compute(lambda h: wbuf[h])

    @pl.when(used & jnp.logical_not(first) & half)
    def _():
        compute(lambda h: wbuf[h], MOE_TM // 2)


def _ffn1(sched, xs, w1, b1_3):
    expert = lambda i, s: s[SCHED_BE, _used_block(i, s)]
    return pl.pallas_call(
        _ffn1_kernel,
        out_shape=jax.ShapeDtypeStruct((MOE_ROWS, D_FF), BF16),
        grid_spec=pltpu.PrefetchScalarGridSpec(
            num_scalar_prefetch=1,
            grid=(MOE_NB,),
            in_specs=[
                pl.BlockSpec((MOE_TM, XS_W), lambda i, s: (_used_block(i, s), 0)),
                pl.BlockSpec((None, 1, D_FF), lambda i, s: (expert(i, s), 0, 0)),
                pl.BlockSpec((None, 1, D_FF), lambda i, s: (expert(i, s), 0, 1)),
                pl.BlockSpec(memory_space=pl.ANY),
            ],
            out_specs=pl.BlockSpec((MOE_TM, D_FF), lambda i, s: (_used_block(i, s), 0)),
            scratch_shapes=[
                pltpu.VMEM((F1_NH, XS_W, 2 * D_FF), F32),
                pltpu.VMEM((F1_NH, XS_W, 2 * D_FF), BF16),
                pltpu.SemaphoreType.DMA((F1_NH,)),
            ],
        ),
        compiler_params=_params(("arbitrary",), vmem=BIG_VMEM_LIMIT),
        name="ffn1",
    )(sched, xs, b1_3, b1_3, w1)


def _ffn2_kernel(sched_ref, a_ref, b_ref, w2_hbm, o_ref, stg, sem):
    j = pl.program_id(0)
    i = pl.program_id(1)
    nj = pl.num_programs(0)

    def tile_copies(e, jj, slot):
        c0 = pl.multiple_of(jj * F2_TN, F2_TN)
        return (pltpu.make_async_copy(w2_hbm.at[e, :, pl.ds(c0, F2_TN)], stg.at[slot], sem.at[slot]),)

    @pl.when(i < _blocks_used(sched_ref))
    def _():
        _stream_expert_weights(j, i, nj, sched_ref, tile_copies, stg, None)
        slot = (j * sched_ref[SCHED_META, 1] + sched_ref[SCHED_SEG, i]) % 2

        def compute(n_rows):
            y = jnp.dot(a_ref[0:n_rows, :], stg[slot].astype(BF16), preferred_element_type=F32) + b_ref[...]
            lo = lax.bitcast_convert_type(y[:, :XS_W].astype(BF16).astype(F32), U32)
            hi = lax.bitcast_convert_type(y[:, XS_W:].astype(BF16).astype(F32), U32)
            o_ref[0:n_rows, :] = (lo >> 16) | (hi & jnp.uint32(0xFFFF0000))

        half = sched_ref[SCHED_HALF, i] == 1

        @pl.when(jnp.logical_not(half))
        def _():
            compute(MOE_TM)

        @pl.when(half)
        def _():
            compute(MOE_TM // 2)


def _ffn2(sched, act, w2, b2_3):
    return pl.pallas_call(
        _ffn2_kernel,
        out_shape=jax.ShapeDtypeStruct((MOE_ROWS, XS_W), U32),
        grid_spec=pltpu.PrefetchScalarGridSpec(
            num_scalar_prefetch=1,
            grid=(D_MODEL // F2_TN, MOE_NB),
            in_specs=[
                pl.BlockSpec((MOE_TM, D_FF), lambda j, i, s: (_used_block(i, s), 0)),
                pl.BlockSpec((None, 1, F2_TN), lambda j, i, s: (s[SCHED_BE, _used_block(i, s)], 0, j)),
                pl.BlockSpec(memory_space=pl.ANY),
            ],
            out_specs=pl.BlockSpec((MOE_TM, XS_W), lambda j, i, s: (_used_block(i, s), j)),
            scratch_shapes=[
                pltpu.VMEM((2, D_FF, F2_TN), F32),
                pltpu.SemaphoreType.DMA((2,)),
            ],
        ),
        compiler_params=_params(("arbitrary", "arbitrary")),
        name="ffn2",
    )(sched, act, b2_3, w2)


CB_TM = 256
CB_RING = 3


def _combine_kernel(dest_ref, x_ref, gate_ref, g2_ref, ys_hbm, o_ref, buf, gate_scr, g2_scr, sem):
    i = pl.program_id(0)
    last = pl.num_programs(0) - 1
    par = i % CB_RING
    ahead = (i + CB_RING - 1) % CB_RING
    n_tiles = XS_W // LANES
    copies_per_tile = 8 * TOP_K // n_tiles
    assert copies_per_tile * n_tiles == 8 * TOP_K

    def row_start(tile, p, u, k):
        d = dest_ref[k * SEQ + tile * CB_TM + u]
        pltpu.make_async_copy(ys_hbm.at[pl.ds(d, 1)], buf.at[p, k, pl.ds(u, 1)], sem.at[p]).start(priority=k % 2)

    def drain(p):
        for k in range(TOP_K):
            pltpu.make_async_copy(ys_hbm.at[pl.ds(0, CB_TM)], buf.at[p, k], sem.at[p]).wait()

    @pl.when(i == 0)
    def _():
        for t in range(CB_RING - 1):
            def body(u, _, t=t):
                for k in range(TOP_K):
                    row_start(t, t, u, k)
                return 0
            lax.fori_loop(0, CB_TM, body, 0)
        g2_scr[...] = jnp.broadcast_to(g2_ref[...], g2_scr.shape)

    drain(par)

    for k in range(TOP_K):
        gate_scr[k] = jnp.broadcast_to(gate_ref[:, k:k + 1], (CB_TM, LANES))

    nxt = jnp.minimum(i + CB_RING - 1, last)

    def body(r, _):
        r0 = pl.multiple_of(r * 8, 8)
        rows = pl.ds(r0, 8)
        gk = [gate_scr[k, rows, :] for k in range(TOP_K)]
        for c in range(n_tiles):
            cs_lo = slice(c * LANES, (c + 1) * LANES)
            cs_hi = slice(XS_W + c * LANES, XS_W + (c + 1) * LANES)
            y_lo = y_hi = None
            for k in range(TOP_K):
                w = buf[par, k, rows, cs_lo]
                lo = lax.bitcast_convert_type(w << 16, F32) * gk[k]
                hi = lax.bitcast_convert_type(w & jnp.uint32(0xFFFF0000), F32) * gk[k]
                y_lo = lo if y_lo is None else y_lo + lo
                y_hi = hi if y_hi is None else y_hi + hi
            o_ref[rows, cs_lo] = x_ref[rows, cs_lo] + g2_scr[:, cs_lo] * y_lo
            o_ref[rows, cs_hi] = x_ref[rows, cs_hi] + g2_scr[:, cs_hi] * y_hi
            for q in range(copies_per_tile):
                n = c * copies_per_tile + q
                row_start(nxt, ahead, r0 + n // TOP_K, n % TOP_K)
        return 0

    lax.fori_loop(0, CB_TM // 8, body, 0)

    @pl.when(i == last)
    def _():
        drain((i + 1) % CB_RING)
        drain(ahead)


def _combine(dest, x1, gates_t, mod, ys):
    return pl.pallas_call(
        _combine_kernel,
        out_shape=jax.ShapeDtypeStruct((SEQ, D_MODEL), F32),
        grid_spec=pltpu.PrefetchScalarGridSpec(
            num_scalar_prefetch=1,
            grid=(SEQ // CB_TM,),
            in_specs=[
                pl.BlockSpec((CB_TM, D_MODEL), lambda i, d: (i, 0)),
                pl.BlockSpec((CB_TM, TOP_K), lambda i, d: (i, 0)),
                pl.BlockSpec((1, D_MODEL), lambda i, d: (0, MOD_B_G2)),
                pl.BlockSpec(memory_space=pl.ANY),
            ],
            out_specs=pl.BlockSpec((CB_TM, D_MODEL), lambda i, d: (i, 0)),
            scratch_shapes=[
                pltpu.VMEM((CB_RING, TOP_K, CB_TM, XS_W), U32),
                pltpu.VMEM((TOP_K, CB_TM, LANES), F32),
                pltpu.VMEM((8, D_MODEL), F32),
                pltpu.SemaphoreType.DMA((CB_RING,)),
            ],
        ),
        compiler_params=_params(("arbitrary",)),
        name="combine",
    )(dest, x1, gates_t, mod, ys)


def _rope_tables():
    t = jnp.arange(SEQ, dtype=I32)
    row = (t // GRID_W).astype(F32)
    col = (t % GRID_W).astype(F32)
    inv_freq = ROPE_THETA ** (-jnp.arange(0, ROPE_AXIS_DIM, 2, dtype=F32) / ROPE_AXIS_DIM)
    ang_r = inv_freq[:, None] * row[None, :]
    ang_c = inv_freq[:, None] * col[None, :]
    cos_t = jnp.concatenate([jnp.cos(ang_r), jnp.cos(ang_r), jnp.cos(ang_c), jnp.cos(ang_c)], axis=0)
    sin_t = jnp.concatenate([-jnp.sin(ang_r), jnp.sin(ang_r), -jnp.sin(ang_c), jnp.sin(ang_c)], axis=0)
    return cos_t, sin_t


def kernel(x, c, w_mod, b_mod, norm1_g, w_in, q_norm_g, k_norm_g, w_pool, pool_scale, w_out, norm2_g,
           w_router, b_router, w1, b1, w2, b2):
    assert x.shape == (1, SEQ, D_MODEL) and w_mod.shape[0] == 1
    x2 = x[0]
    cos_t, sin_t = _rope_tables()

    c_col = c.reshape(D_MODEL, 1)
    mod_a = _mod(c_col, w_mod[0], b_mod)

    pool_in, qt, k, vt, mod = _inproj(x2, mod_a, norm1_g, w_in[0].astype(BF16), cos_t, sin_t,
                                      q_norm_g.reshape(HEAD_DIM, 1), k_norm_g.reshape(HEAD_DIM, 1),
                                      c.reshape(D_MODEL // LANES, LANES), w_mod[0], b_mod)
    attn = _attention(qt, k, vt)
    pool = _pool(pool_in, w_pool[0].astype(BF16), pool_scale)
    x1 = _outproj(attn, pool, w_out[0], x2, mod)

    wr = w_router[0]
    wr_hi = wr.astype(BF16)
    wr_lo = (wr - wr_hi.astype(F32)).astype(BF16)
    pad = lambda a: jnp.pad(a, ((0, 0), (0, RT_PAD - N_EXPERTS)))
    wcat = jnp.concatenate([pad(wr_hi), pad(wr_lo)], axis=1)
    logits_t = _norm2(x1, norm2_g, mod, wcat, pad(wr_hi), pad(b_router))

    dest, gates, sched, lastblk = _route(logits_t)
    dest = dest.reshape(TOP_K * SEQ)

    xs = _dispatch(dest, lastblk, x1, norm2_g, mod)
    act = _ffn1(sched, xs, w1[0], b1[0].reshape(N_EXPERTS, 1, 2 * D_FF))
    ys = _ffn2(sched, act, w2[0], b2[0].reshape(N_EXPERTS, 1, D_MODEL))
    out = _combine(dest, x1, gates.T, mod, ys)
    return out[None]
```
